```python
import jax, jax.numpy as jnp
from jax import lax
import numpy as np

D_MODEL = 1024
BATCH = 16
SEQ = 256
DEPTH = 4
DEC_BATCH = 8
DEC_SEQ = 4096
PAST_LEN = 512

GRID_W = 64
ML_HEADS = 4
ML_DK = 128
ML_DV = 128
ML_WIDTH = ML_HEADS * ML_DV
HG_HEADS = 4
HG_DK = 128
HG_DV = 128
HG_WIDTH = HG_HEADS * HG_DV
AT_HEADS = 8
AT_KV_HEADS = 2
AT_GROUP = AT_HEADS // AT_KV_HEADS
AT_HD = 64
AT_WIDTH = AT_HEADS * AT_HD
N_BRANCH = 3
BRANCH_W = ML_WIDTH
FFN_HIDDEN = -(-8 * D_MODEL // (3 * 256)) * 256
CHUNK = 64
Q_BLOCK = 128
ROPE_THETA = 10000.0
EPS = 1e-6
IN_SIZES = (ML_HEADS * ML_DK, ML_HEADS * ML_DK, ML_WIDTH, ML_WIDTH, 2 * 2 * ML_HEADS,
            HG_HEADS * HG_DK, 2 * HG_HEADS * HG_DK, HG_WIDTH, HG_WIDTH,
            AT_HEADS * AT_HD, AT_KV_HEADS * AT_HD, AT_KV_HEADS * AT_HD, N_BRANCH * D_MODEL)
IN_COLS = 2 * ML_HEADS * ML_DK + 2 * ML_WIDTH + 4 * ML_HEADS + 3 * HG_HEADS * HG_DK + 2 * HG_WIDTH + (AT_HEADS + 2 * AT_KV_HEADS) * AT_HD + N_BRANCH * D_MODEL

kernel_name = 'hybrid_mlstm_hgrn2_gqa_diffusion_step'


def _rmsnorm(x, g):
    xf = x.astype(jnp.float32)
    xf = xf * lax.rsqrt(jnp.mean(xf * xf, axis=-1, keepdims=True) + EPS)
    return (xf * g.astype(jnp.float32)).astype(x.dtype)


def _headnorm(y, g):
    B, L = y.shape[:2]
    y = y * lax.rsqrt(jnp.mean(y * y, axis=-1, keepdims=True) + EPS)
    return y.reshape(B, L, -1) * g.astype(jnp.float32)


def _split(a, sizes):
    outs, off = [], 0
    for s in sizes:
        outs.append(a[..., off:off + s])
        off += s
    return outs


def _flip(a):
    return jnp.flip(a, axis=1)


def _to_chunks(a):
    B, L = a.shape[:2]
    a = a.reshape((B, L // CHUNK, CHUNK) + a.shape[2:])
    return jnp.swapaxes(jnp.moveaxis(a, 1, 0), 2, 3)


def _from_chunks(a):
    nc, B, H, C = a.shape[:4]
    a = jnp.moveaxis(jnp.swapaxes(a, 2, 3), 0, 1)
    return a.reshape((B, nc * C, H) + a.shape[4:])


def _mlstm_scan(q, k, v, ig, fg, C0, n0, m0):
    f32 = jnp.float32
    scale = ML_DK ** -0.5
    causal = jnp.tril(jnp.ones((CHUNK, CHUNK), dtype=bool))
    xs = tuple(_to_chunks(a.astype(f32)) for a in (q, k, v, ig, fg))

    def step(carry, inp):
        C, n, m = carry
        qc, kc, vc, ic, fc = inp
        qc = qc * scale
        b = jnp.cumsum(jax.nn.log_sigmoid(fc), axis=-1)
        d = jnp.where(causal, b[..., :, None] - b[..., None, :] + ic[..., None, :], -jnp.inf)
        m_state = b + m[..., None]
        m_t = jnp.maximum(m_state, jnp.max(d, axis=-1))
        w_intra = jnp.exp(d - m_t[..., None])
        w_state = jnp.exp(m_state - m_t)
        s = jnp.einsum('bhtd,bhsd->bhts', qc, kc) * w_intra
        num = jnp.einsum('bhts,bhsv->bhtv', s, vc) + w_state[..., None] * jnp.einsum('bhtd,bhdv->bhtv', qc, C)
        den = jnp.sum(s, axis=-1) + w_state * jnp.einsum('bhtd,bhd->bht', qc, n)
        h = num / jnp.maximum(jnp.abs(den), jnp.exp(-m_t))[..., None]
        b_end = b[..., -1]
        g = b_end[..., None] - b + ic
        m_new = jnp.maximum(b_end + m, jnp.max(g, axis=-1))
        w_k = jnp.exp(g - m_new[..., None])
        decay = jnp.exp(b_end + m - m_new)
        C_new = decay[..., None, None] * C + jnp.einsum('bhs,bhsd,bhsv->bhdv', w_k, kc, vc)
        n_new = decay[..., None] * n + jnp.einsum('bhs,bhsd->bhd', w_k, kc)
        return (C_new, n_new, m_new), h

    (C, n, m), h = lax.scan(step, (C0.astype(f32), n0.astype(f32), m0.astype(f32)), xs)
    return _from_chunks(h), (C, n, m)


def _hgrn_scan(q, k, logf, v, S0):
    f32 = jnp.float32
    causal = jnp.tril(jnp.ones((CHUNK, CHUNK), dtype=bool))[..., None]
    xs = tuple(_to_chunks(a.astype(f32)) for a in (q, k, logf, v))

    def step(S, inp):
        qc, kc, lf, vc = inp
        b = jnp.cumsum(lf, axis=2)
        d = jnp.where(causal, b[:, :, :, None, :] - b[:, :, None, :, :], -jnp.inf)
        a = jnp.einsum('bhtc,bhsc,bhtsc->bhts', qc, kc, jnp.exp(d))
        o = jnp.einsum('bhts,bhsv->bhtv', a, vc) + jnp.einsum('bhtc,bhcv->bhtv', qc * jnp.exp(b), S)
        b_end = b[:, :, -1:, :]
        S_new = jnp.exp(b_end[:, :, 0, :])[..., None] * S + jnp.einsum('bhsc,bhsv->bhcv', kc * jnp.exp(b_end - b), vc)
        return S_new, o

    S, o = lax.scan(step, S0.astype(f32), xs)
    return _from_chunks(o), S


def _axial_rope_tables(L):
    rows = L // GRID_W
    row = jnp.repeat(jnp.arange(rows, dtype=jnp.float32), GRID_W)
    col = jnp.tile(jnp.arange(GRID_W, dtype=jnp.float32), rows)
    n_freq = AT_HD // 4
    inv = ROPE_THETA ** (-jnp.arange(n_freq, dtype=jnp.float32) / n_freq)
    ang = jnp.concatenate([row[:, None] * inv, col[:, None] * inv], axis=-1)
    return jnp.cos(ang), jnp.sin(ang)


def _apply_rope(x, cos, sin):
    xf = x.astype(jnp.float32).reshape(x.shape[:-1] + (AT_HD // 2, 2))
    x1, x2 = xf[..., 0], xf[..., 1]
    c = cos[None, :, None, :]
    s = sin[None, :, None, :]
    out = jnp.stack([x1 * c - x2 * s, x1 * s + x2 * c], axis=-1).reshape(x.shape)
    return out.astype(x.dtype)


def _block_attention(q, k, v):
    B, Lq, H, D = q.shape
    nb = Lq // Q_BLOCK
    qb = jnp.moveaxis(q.reshape(B, nb, Q_BLOCK, AT_KV_HEADS, AT_GROUP, D), 1, 0)
    scale = D ** -0.5

    def one(qblk):
        s = jnp.einsum('bqkgd,bskd->bkgqs', qblk, k).astype(jnp.float32) * scale
        p = jax.nn.softmax(s, axis=-1).astype(v.dtype)
        return jnp.einsum('bkgqs,bskd->bqkgd', p, v)

    o = lax.map(one, qb)
    return jnp.moveaxis(o, 0, 1).reshape(B, Lq, H * D)


def _trunk_layer(x, ada, p, lb, rope, cache):
    f32 = jnp.float32
    B, L, _ = x.shape
    sh1, sc1, g1, sh2, sc2, g2 = jnp.split(ada, 6, axis=-1)
    h = _rmsnorm(x, p['norm1_g']) * (1 + sc1) + sh1
    proj = h @ p['w_in']
    (mq, mk, mv, mo, mg, hq, hf, hi, hg, aq, ak, av, bg) = _split(proj, IN_SIZES)

    mq = mq.reshape(B, L, ML_HEADS, ML_DK)
    mk = mk.reshape(B, L, ML_HEADS, ML_DK)
    mv = mv.reshape(B, L, ML_HEADS, ML_DV)
    gates = mg.reshape(B, L, 2, 2, ML_HEADS).astype(f32) + p['ml_gate_b'].astype(f32)
    if cache is None:
        z = (jnp.zeros((B, ML_HEADS, ML_DK, ML_DV), f32), jnp.zeros((B, ML_HEADS, ML_DK), f32), jnp.zeros((B, ML_HEADS), f32))
        init_f, init_b = z, z
    else:
        init_f = (cache['ml_C'][:, 0], cache['ml_n'][:, 0], cache['ml_m'][:, 0])
        init_b = (cache['ml_C'][:, 1], cache['ml_n'][:, 1], cache['ml_m'][:, 1])
    yf, (Cf, nf, mf) = _mlstm_scan(mq, mk, mv, gates[:, :, 0, 0], gates[:, :, 0, 1], *init_f)
    yb, (Cb, nb, mb) = _mlstm_scan(_flip(mq), _flip(mk), _flip(mv), _flip(gates[:, :, 1, 0]), _flip(gates[:, :, 1, 1]), *init_b)
    y_ml = (_headnorm(yf + _flip(yb), p['ml_norm_g']) * jax.nn.sigmoid(mo.astype(f32))).astype(x.dtype)

    hq = hq.reshape(B, L, HG_HEADS, HG_DK)
    hi = hi.reshape(B, L, HG_HEADS, HG_DV)
    hf = hf.reshape(B, L, 2, HG_HEADS, HG_DK).astype(f32)
    lbh = lb.reshape(2, HG_HEADS, HG_DK).astype(f32)
    logf = jnp.logaddexp(jnp.log(lbh), jnp.log1p(-lbh) + jax.nn.log_sigmoid(hf))
    kk = (1 - lbh) * jax.nn.sigmoid(-hf)
    if cache is None:
        zs = jnp.zeros((B, HG_HEADS, HG_DK, HG_DV), f32)
        s_f, s_b = zs, zs
    else:
        s_f, s_b = cache['hg_S'][:, 0], cache['hg_S'][:, 1]
    of, Sf = _hgrn_scan(hq, kk[:, :, 0], logf[:, :, 0], hi, s_f)
    ob, Sb = _hgrn_scan(_flip(hq), _flip(kk[:, :, 1]), _flip(logf[:, :, 1]), _flip(hi), s_b)
    y_hg = (_headnorm(of + _flip(ob), p['hg_norm_g']) * jax.nn.silu(hg.astype(f32))).astype(x.dtype)

    aq = _rmsnorm(aq.reshape(B, L, AT_HEADS, AT_HD), p['q_norm_g'])
    ak = _rmsnorm(ak.reshape(B, L, AT_KV_HEADS, AT_HD), p['k_norm_g'])
    av = av.reshape(B, L, AT_KV_HEADS, AT_HD)
    if cache is None:
        y_at = _block_attention(aq, ak, av)
    else:
        cos, sin = rope
        keys = jnp.concatenate([_apply_rope(ak, cos, sin), cache['k'].astype(ak.dtype)], axis=1)
        vals = jnp.concatenate([av, cache['v'].astype(av.dtype)], axis=1)
        y_at = _block_attention(_apply_rope(aq, cos, sin), keys, vals)

    branches = jnp.stack([y_ml, y_hg, y_at], axis=2)
    proj_b = jnp.einsum('blnw,nwd->blnd', branches, p['w_branch'])
    gates_b = jax.nn.sigmoid(bg.reshape(B, L, N_BRANCH, D_MODEL))
    merged = jnp.sum(gates_b * proj_b, axis=2)
    x = x + g1 * (merged @ p['w_out'])

    h2 = _rmsnorm(x, p['norm2_g']) * (1 + sc2) + sh2
    gu = h2 @ p['w_ffn_in']
    gt, up = gu[..., :FFN_HIDDEN], gu[..., FFN_HIDDEN:]
    x = x + g2 * ((jax.nn.silu(gt) * up) @ p['w_ffn_out'])

    ctx = None
    if cache is None:
        ctx = dict(k=ak, v=av, ml_C=jnp.stack([Cf, Cb], axis=1), ml_n=jnp.stack([nf, nb], axis=1),
                   ml_m=jnp.stack([mf, mb], axis=1), hg_S=jnp.stack([Sf, Sb], axis=1))
    return x, ctx


def setup_inputs(seed: int = 0) -> dict:
    key = jax.random.key(seed)
    ks = jax.random.split(key, 32)
    f32 = jnp.float32
    D = D_MODEL

    def nrm(k, shape, scale):
        return jax.random.normal(k, shape, f32) * scale

    ig_b = nrm(ks[14], (DEPTH, 2, ML_HEADS), 0.1)
    fg_b = jnp.linspace(3.0, 6.0, ML_HEADS, dtype=f32)[None, None, :] + nrm(ks[15], (DEPTH, 2, ML_HEADS), 0.1)
    return {
        'x_prompt': nrm(ks[0], (BATCH, SEQ, D), 1.0),
        'x_sample': nrm(ks[1], (DEC_BATCH, DEC_SEQ, D), 1.0),
        'c': nrm(ks[2], (DEC_BATCH, D), 1.0),
        'cache_k': nrm(ks[3], (DEC_BATCH, DEPTH, PAST_LEN, AT_KV_HEADS, AT_HD), 1.0),
        'cache_v': nrm(ks[4], (DEC_BATCH, DEPTH, PAST_LEN, AT_KV_HEADS, AT_HD), 0.5),
        'state_ml_C': nrm(ks[5], (DEC_BATCH, DEPTH, 2, ML_HEADS, ML_DK, ML_DV), 0.1),
        'state_ml_n': nrm(ks[6], (DEC_BATCH, DEPTH, 2, ML_HEADS, ML_DK), 1.0),
        'state_ml_m': nrm(ks[7], (DEC_BATCH, DEPTH, 2, ML_HEADS), 1.0),
        'state_hg_S': nrm(ks[8], (DEC_BATCH, DEPTH, 2, HG_HEADS, HG_DK, HG_DV), 0.5),
        'c_ctx': nrm(ks[9], (D,), 1.0),
        'w_ada': nrm(ks[10], (DEPTH, D, 6 * D), 0.5 * D ** -0.5),
        'b_ada': nrm(ks[11], (DEPTH, 6 * D), 0.02),
        'norm1_g': 1.0 + nrm(ks[12], (DEPTH, D), 0.05),
        'norm2_g': 1.0 + nrm(ks[13], (DEPTH, D), 0.05),
        'w_in': nrm(ks[16], (DEPTH, D, IN_COLS), D ** -0.5),
        'ml_gate_b': jnp.stack([ig_b, fg_b], axis=2),
        'ml_norm_g': 1.0 + nrm(ks[17], (DEPTH, ML_WIDTH), 0.05),
        'hg_lb_logits': nrm(ks[18], (DEPTH, 2, HG_HEADS * HG_DK), 0.5),
        'hg_norm_g': 1.0 + nrm(ks[19], (DEPTH, HG_WIDTH), 0.05),
        'q_norm_g': 1.0 + nrm(ks[20], (DEPTH, AT_HD), 0.05),
        'k_norm_g': 1.0 + nrm(ks[21], (DEPTH, AT_HD), 0.05),
        'w_branch': nrm(ks[22], (DEPTH, N_BRANCH, BRANCH_W, D), BRANCH_W ** -0.5),
        'w_out': nrm(ks[23], (DEPTH, D, D), D ** -0.5),
        'w_ffn_in': nrm(ks[24], (DEPTH, D, 2 * FFN_HIDDEN), D ** -0.5),
        'w_ffn_out': nrm(ks[25], (DEPTH, FFN_HIDDEN, D), FFN_HIDDEN ** -0.5),
        'final_g': 1.0 + nrm(ks[26], (D,), 0.05),
    }


def reference(x_prompt, x_sample, c, cache_k, cache_v, state_ml_C, state_ml_n, state_ml_m, state_hg_S,
              c_ctx, w_ada, b_ada, norm1_g, norm2_g, w_in, ml_gate_b, ml_norm_g, hg_lb_logits, hg_norm_g,
              q_norm_g, k_norm_g, w_branch, w_out, w_ffn_in, w_ffn_out, final_g):
    lb_all = jnp.cumsum(jax.nn.softmax(hg_lb_logits.astype(jnp.float32), axis=0), axis=0)
    lb_all = lb_all - lb_all[0]
    rope = _axial_rope_tables(x_sample.shape[1])
    xp, xs = x_prompt, x_sample
    nk, nv, nC, nn_, nm, nS = [], [], [], [], [], []
    for l in range(DEPTH):
        p = dict(norm1_g=norm1_g[l], norm2_g=norm2_g[l], w_in=w_in[l], ml_gate_b=ml_gate_b[l],
                 ml_norm_g=ml_norm_g[l], hg_norm_g=hg_norm_g[l], q_norm_g=q_norm_g[l], k_norm_g=k_norm_g[l],
                 w_branch=w_branch[l], w_out=w_out[l], w_ffn_in=w_ffn_in[l], w_ffn_out=w_ffn_out[l])
        ada_ctx = (jax.nn.silu(c_ctx) @ w_ada[l] + b_ada[l])[None, None, :]
        ada_lat = (jax.nn.silu(c) @ w_ada[l] + b_ada[l])[:, None, :]
        xp, ctx = _trunk_layer(xp, ada_ctx, p, lb_all[l], None, None)
        nk.append(ctx['k']); nv.append(ctx['v']); nC.append(ctx['ml_C'])
        nn_.append(ctx['ml_n']); nm.append(ctx['ml_m']); nS.append(ctx['hg_S'])
        cache = dict(k=cache_k[:, l], v=cache_v[:, l], ml_C=state_ml_C[:, l], ml_n=state_ml_n[:, l],
                     ml_m=state_ml_m[:, l], hg_S=state_hg_S[:, l])
        xs, _ = _trunk_layer(xs, ada_lat, p, lb_all[l], rope, cache)
    y_prompt = _rmsnorm(xp, final_g)
    y_sample = _rmsnorm(xs, final_g)
    new_k = jnp.stack(nk, axis=1)
    new_v = jnp.stack(nv, axis=1)
    new_ml_C = jnp.stack(nC, axis=1)
    new_ml_n = jnp.stack(nn_, axis=1)
    new_ml_m = jnp.stack(nm, axis=1)
    new_hg_S = jnp.stack(nS, axis=1)
    return (y_prompt, y_sample, new_k, new_v, new_ml_C, new_ml_n, new_ml_m, new_hg_S)
```

```python
import functools
import math

import jax
import jax.numpy as jnp
from jax import lax
from jax.experimental import pallas as pl
from jax.experimental.pallas import tpu as pltpu

f32 = jnp.float32
bf16 = jnp.bfloat16

D_MODEL = 1024
DEPTH = 4
N_HEADS = 4
D_HEAD = 128
AT_HEADS = 8
AT_KV_HEADS = 2
AT_GROUP = AT_HEADS // AT_KV_HEADS
AT_HD = 64
BRANCH_W = 512
FFN_HIDDEN = 2816
GRID_W = 64
ROPE_THETA = 10000.0
EPS = 1e-6

LANES = 128
SUBLANES = 8
VMEM_LIMIT = 52 * 1024 * 1024

CB_BG = 0
CB_MQ = 24
CB_MK = 28
CB_MV = 32
CB_MO = 36
CB_HQ = 40
CB_HF = 44
CB_HI = 52
CB_HG = 56
CB_AQ = 60
CB_AK = 64
CB_AV = 65
CB_MG = 66
N_COL_BLOCKS = 68
N_COLS = N_COL_BLOCKS * LANES

ML_CHUNK = 128
HG_CHUNK = 64
HG_SUB = 8
NEG_BIG = -1e30


def _cparams(sem):
    return pltpu.CompilerParams(dimension_semantics=sem, vmem_limit_bytes=VMEM_LIMIT)


def _dot(a, b):
    return jnp.dot(a, b, preferred_element_type=f32)


def _dot_nt(a, b):
    return lax.dot_general(a, b, (((1,), (1,)), ((), ())), preferred_element_type=f32)


def _dot_tn(a, b):
    return lax.dot_general(a, b, (((0,), (0,)), ((), ())), preferred_element_type=f32)


def _sigmoid(x):
    return 1.0 / (1.0 + jnp.exp(-x))


def _silu(x):
    return x * _sigmoid(x)


def _log_sigmoid(x):
    return jnp.minimum(x, 0.0) - jnp.log1p(jnp.exp(-jnp.abs(x)))


def _split3(x):
    hi = x.astype(bf16)
    r1 = x - hi.astype(f32)
    mid = r1.astype(bf16)
    lo = (r1 - mid.astype(f32)).astype(bf16)
    return hi, mid, lo


def _rms_rows(x, g_row):
    ms = jnp.mean(x * x, axis=-1, keepdims=True)
    return x * lax.rsqrt(ms + EPS) * g_row


def _ada_kernel(c_ref, w_ref, b_ref, o_ref):
    s = _silu(c_ref[...]).astype(bf16)
    o_ref[0] = _dot(s, w_ref[0].astype(bf16)) + b_ref[0]


def ada_call(cvec, w_ada, b_ada):
    rows = cvec.shape[0]
    tn = 1536
    n6 = 6 * D_MODEL
    return pl.pallas_call(
        _ada_kernel,
        grid=(DEPTH, n6 // tn),
        in_specs=[pl.BlockSpec((rows, D_MODEL), lambda l, n: (0, 0)),
                  pl.BlockSpec((1, D_MODEL, tn), lambda l, n: (l, 0, n)),
                  pl.BlockSpec((1, 1, tn), lambda l, n: (l, 0, n))],
        out_specs=pl.BlockSpec((1, rows, tn), lambda l, n: (l, 0, n)),
        out_shape=jax.ShapeDtypeStruct((DEPTH, rows, n6), f32),
        compiler_params=_cparams(("arbitrary", "arbitrary")),
        name="ada",
    )(cvec, w_ada, b_ada.reshape(DEPTH, 1, n6))


def _lb_kernel(x_ref, o_ref):
    xs = [x_ref[l] for l in range(DEPTH)]
    mx = xs[0]
    for l in range(1, DEPTH):
        mx = jnp.maximum(mx, xs[l])
    es = [jnp.exp(x - mx) for x in xs]
    tot = es[0]
    for l in range(1, DEPTH):
        tot = tot + es[l]
    sm = [e / tot for e in es]
    run = sm[0]
    o_ref[0] = run - sm[0]
    for l in range(1, DEPTH):
        run = run + sm[l]
        o_ref[l] = run - sm[0]


def lb_call(logits):
    x = logits.astype(f32).reshape(DEPTH, 2 * N_HEADS, D_HEAD)
    return pl.pallas_call(
        _lb_kernel,
        out_shape=jax.ShapeDtypeStruct((DEPTH, 2 * N_HEADS, D_HEAD), f32),
        name="hg_lower_bounds",
    )(x)


def _in_proj_kernel(x_ref, mod_ref, g_ref, w_ref, o_ref, h_s):
    @pl.when(pl.program_id(2) == 0)
    def _():
        x = x_ref[0]
        sh = mod_ref[0, 0:1, :]
        sc = mod_ref[0, 1:2, :]
        h = _rms_rows(x, g_ref[...]) * (1.0 + sc) + sh
        h_s[...] = h.astype(bf16)

    o_ref[0] = _dot(h_s[...], w_ref[...])


def in_proj_call(x, mod, g_row, w, shared_mod):
    B, L, _ = x.shape
    tm = min(L, 512)
    tn = N_COLS // 4
    mod_map = (lambda b, m, n: (0, 0, 0)) if shared_mod else (lambda b, m, n: (b, 0, 0))
    return pl.pallas_call(
        _in_proj_kernel,
        grid=(B, L // tm, N_COLS // tn),
        in_specs=[pl.BlockSpec((1, tm, D_MODEL), lambda b, m, n: (b, m, 0)),
                  pl.BlockSpec((1, 8, D_MODEL), mod_map),
                  pl.BlockSpec((1, D_MODEL), lambda b, m, n: (0, 0)),
                  pl.BlockSpec((D_MODEL, tn), lambda b, m, n: (0, n))],
        out_specs=pl.BlockSpec((1, tm, tn), lambda b, m, n: (b, m, n)),
        out_shape=jax.ShapeDtypeStruct((B, L, N_COLS), f32),
        scratch_shapes=[pltpu.VMEM((tm, D_MODEL), bf16)],
        compiler_params=_cparams(("arbitrary", "arbitrary", "arbitrary")),
        name="in_proj",
    )(x, mod, g_row, w)


def _mlstm_kernel(bias_ref, m0_ref, q_ref, k_ref, v_ref, mo_ref, g_ref, ng_ref, c0_ref, n0_ref,
                  y_ref, cout_ref, nout_ref, mout_ref,
                  gt_s, rw_s, cl_s, h_s, c_s, n_s, m_s, *, seq_len):
    T = ML_CHUNK
    nc = seq_len // T
    b_idx = pl.program_id(0)
    h = pl.program_id(1)
    scale = D_HEAD ** -0.5

    def tr_body(j, carry):
        r0 = pl.multiple_of(j * T, T)
        gt_s[:, pl.ds(r0, T)] = g_ref[0, pl.ds(r0, T), :].T
        return carry
    lax.fori_loop(0, nc, tr_body, 0)

    i_f = gt_s[pl.ds(h, 1), :] + bias_ref[h]
    f_f = _log_sigmoid(gt_s[pl.ds(4 + h, 1), :] + bias_ref[4 + h])
    i_b = gt_s[pl.ds(8 + h, 1), :] + bias_ref[8 + h]
    f_b = _log_sigmoid(gt_s[pl.ds(12 + h, 1), :] + bias_ref[12 + h])
    zrow = jnp.zeros_like(i_f)
    rw_s[...] = jnp.concatenate([f_f, i_f, f_b, i_b, zrow, zrow, zrow, zrow], axis=0)

    ui = lax.broadcasted_iota(jnp.int32, (T, T), 0)
    si = lax.broadcasted_iota(jnp.int32, (T, T), 1)
    tri = jnp.concatenate([(ui <= si).astype(bf16), (ui >= si).astype(bf16)], axis=1)
    causal_f = si <= ui
    causal_b = si >= ui

    def cs_body(j, carry):
        r0 = pl.multiple_of(j * T, T)
        rows = rw_s[:, pl.ds(r0, T)]
        hi, mid, lo = _split3(rows)
        cs = _dot(hi, tri) + _dot(mid, tri) + _dot(lo, tri)
        new_rows = jnp.concatenate([cs[0:1, 0:T], rows[1:2], cs[2:3, T:2 * T], rows[3:8]], axis=0)
        rw_s[:, pl.ds(r0, T)] = new_rows
        cl_s[pl.ds(r0, T), :] = jnp.concatenate([new_rows, jnp.zeros((T - 8, T), f32)], axis=0).T
        return carry
    lax.fori_loop(0, nc, cs_body, 0)

    for d in range(2):
        c_s[d] = c0_ref[0, 0, d, 0]
        n_s[d] = n0_ref[0, 0, d, 0]
        m_s[d] = jnp.full((1, LANES), m0_ref[b_idx * 2 * N_HEADS + d * N_HEADS + h], f32)

    def step(d, j):
        r0 = pl.multiple_of(j * T, T)
        rows = rw_s[:, pl.ds(r0, T)]
        cols = cl_s[pl.ds(r0, T), :]
        b_row = rows[2 * d:2 * d + 1]
        i_row = rows[2 * d + 1:2 * d + 2]
        b_col = cols[:, 2 * d:2 * d + 1]
        i_col = cols[:, 2 * d + 1:2 * d + 2]
        b_end = b_row[:, T - 1:T] if d == 0 else b_row[:, 0:1]

        q = q_ref[0, pl.ds(r0, T), :] * scale
        k = k_ref[0, pl.ds(r0, T), :]
        qb = q.astype(bf16)
        kb = k.astype(bf16)
        vb = v_ref[0, pl.ds(r0, T), :].astype(bf16)
        c_prev = c_s[d]
        n_prev = n_s[d]
        m_prev = m_s[d][:, 0:1]

        dmat = jnp.where(causal_f if d == 0 else causal_b, b_col - b_row + i_row, NEG_BIG)
        m_state = b_col + m_prev
        m_t = jnp.maximum(m_state, jnp.max(dmat, axis=-1, keepdims=True))
        w_intra = jnp.exp(dmat - m_t)
        w_state = jnp.exp(m_state - m_t)
        s = _dot_nt(qb, kb) * w_intra
        num = _dot(s.astype(bf16), vb) + w_state * _dot(qb, c_prev.astype(bf16))
        den = jnp.sum(s, axis=-1, keepdims=True) + w_state * jnp.sum(q * n_prev, axis=-1, keepdims=True)
        hh = num * (1.0 / jnp.maximum(jnp.abs(den), jnp.exp(-m_t)))
        h_s[d, pl.ds(r0, T), :] = hh

        g = b_end - b_col + i_col
        m_new = jnp.maximum(b_end + m_prev, jnp.max(g, axis=0, keepdims=True))
        w_k = jnp.exp(g - m_new)
        decay = jnp.exp(b_end + m_prev - m_new)
        kw = k * w_k
        c_s[d] = decay * c_prev + _dot_tn(kw.astype(bf16), vb)
        n_s[d] = decay * n_prev + jnp.sum(kw, axis=0, keepdims=True)
        m_s[d] = jnp.broadcast_to(m_new, (1, LANES))

    def loop_body(j, carry):
        step(0, j)
        step(1, nc - 1 - j)
        return carry
    lax.fori_loop(0, nc, loop_body, 0)

    ng = ng_ref[...]

    def out_body(j, carry):
        r0 = pl.multiple_of(j * T, T)
        y = h_s[0, pl.ds(r0, T), :] + h_s[1, pl.ds(r0, T), :]
        y = _rms_rows(y, ng) * _sigmoid(mo_ref[0, pl.ds(r0, T), :])
        y_ref[0, pl.ds(r0, T), :] = y.astype(y_ref.dtype)
        return carry
    lax.fori_loop(0, nc, out_body, 0)

    for d in range(2):
        cout_ref[0, d, 0] = c_s[d]
        nout_ref[0, d, 0] = n_s[d]
        mout_ref[0, d, 0] = m_s[d]


def mlstm_call(proj, gate_bias, norm_g, c0, n0, m0, layer):
    B, L, _ = proj.shape

    def col(cb):
        return pl.BlockSpec((1, L, LANES), lambda b, h, cb=cb: (b, 0, cb + h))

    smem = pl.BlockSpec(memory_space=pltpu.SMEM)
    kern = functools.partial(_mlstm_kernel, seq_len=L)
    return pl.pallas_call(
        kern,
        grid=(B, N_HEADS),
        in_specs=[smem, smem,
                  col(CB_MQ), col(CB_MK), col(CB_MV), col(CB_MO),
                  pl.BlockSpec((1, L, LANES), lambda b, h: (b, 0, CB_MG)),
                  pl.BlockSpec((1, LANES), lambda b, h: (0, h)),
                  pl.BlockSpec((1, 1, 2, 1, D_HEAD, D_HEAD), lambda b, h: (b, layer, 0, h, 0, 0)),
                  pl.BlockSpec((1, 1, 2, 1, 1, D_HEAD), lambda b, h: (b, layer, 0, h, 0, 0))],
        out_specs=[pl.BlockSpec((1, L, LANES), lambda b, h: (b, 0, h)),
                   pl.BlockSpec((1, 2, 1, D_HEAD, D_HEAD), lambda b, h: (b, 0, h, 0, 0)),
                   pl.BlockSpec((1, 2, 1, 1, D_HEAD), lambda b, h: (b, 0, h, 0, 0)),
                   pl.BlockSpec((1, 2, 1, 1, LANES), lambda b, h: (b, 0, h, 0, 0))],
        out_shape=[jax.ShapeDtypeStruct((B, L, BRANCH_W), bf16),
                   jax.ShapeDtypeStruct((B, 2, N_HEADS, D_HEAD, D_HEAD), f32),
                   jax.ShapeDtypeStruct((B, 2, N_HEADS, 1, D_HEAD), f32),
                   jax.ShapeDtypeStruct((B, 2, N_HEADS, 1, LANES), f32)],
        scratch_shapes=[pltpu.VMEM((LANES, L), f32),
                        pltpu.VMEM((8, L), f32),
                        pltpu.VMEM((L, LANES), f32),
                        pltpu.VMEM((2, L, D_HEAD), f32),
                        pltpu.VMEM((2, D_HEAD, D_HEAD), f32),
                        pltpu.VMEM((2, 1, D_HEAD), f32),
                        pltpu.VMEM((2, 1, LANES), f32)],
        compiler_params=_cparams(("arbitrary", "arbitrary")),
        name="mlstm",
    )(gate_bias, m0, proj, proj, proj, proj, proj, norm_g, c0, n0)


def _hgrn_kernel(q_ref, f0_ref, f1_ref, i_ref, og_ref, lb_ref, ng_ref, s0_ref,
                 y_ref, sout_ref, o_s, st_s, *, seq_len):
    T = HG_CHUNK
    nc = seq_len // T
    h = pl.program_id(1)
    f_refs = (f0_ref, f1_ref)

    lbs, loglb, log1mlb = [], [], []
    for d in range(2):
        lb = lb_ref[pl.ds(d * N_HEADS + h, 1), :]
        lbs.append(lb)
        loglb.append(jnp.log(lb))
        log1mlb.append(jnp.log1p(-lb))
        st_s[d] = s0_ref[0, 0, d, 0].T

    ti = lax.broadcasted_iota(jnp.int32, (T, T), 0)
    ui = lax.broadcasted_iota(jnp.int32, (T, T), 1)
    tri = ((ui <= ti).astype(bf16), (ui >= ti).astype(bf16))
    sub_t = lax.broadcasted_iota(jnp.int32, (HG_SUB, D_HEAD), 0)

    def step(d, j):
        r0 = pl.multiple_of(j * T, T)
        q = q_ref[0, pl.ds(r0, T), :]
        hf = f_refs[d][0, pl.ds(r0, T), :]
        v = i_ref[0, pl.ds(r0, T), :]
        vb = v.astype(bf16)

        e = jnp.exp(-jnp.abs(hf))
        sig_neg = jnp.where(hf >= 0.0, e, 1.0) / (1.0 + e)
        logsig = jnp.minimum(hf, 0.0) - jnp.log1p(e)
        cterm = log1mlb[d] + logsig
        amax = jnp.maximum(loglb[d], cterm)
        logf = amax + jnp.log1p(jnp.exp(-jnp.abs(loglb[d] - cterm)))
        k = (1.0 - lbs[d]) * sig_neg

        hi, mid, lo = _split3(logf)
        b = _dot(tri[d], hi) + _dot(tri[d], mid) + _dot(tri[d], lo)

        nsub = T // HG_SUB
        o_blk = [None] * nsub

        def add(idx, val):
            o_blk[idx] = val if o_blk[idx] is None else o_blk[idx] + val

        def diag(lo_r):
            bs = b[lo_r:lo_r + HG_SUB]
            qs = q[lo_r:lo_r + HG_SUB]
            ks = k[lo_r:lo_r + HG_SUB]
            vs = v[lo_r:lo_r + HG_SUB]
            acc = None
            for s in range(HG_SUB):
                valid = (sub_t >= s) if d == 0 else (sub_t <= s)
                ee = jnp.where(valid, jnp.exp(jnp.minimum(bs - bs[s:s + 1], 0.0)), 0.0)
                a_col = jnp.sum(qs * ee * ks[s:s + 1], axis=-1, keepdims=True)
                term = a_col * vs[s:s + 1]
                acc = term if acc is None else acc + term
            add(lo_r // HG_SUB, acc)

        def block(lo_r, hi_r):
            if hi_r - lo_r == HG_SUB:
                diag(lo_r)
                return
            mid_r = (lo_r + hi_r) // 2
            if d == 0:
                ref = b[mid_r - 1:mid_r]
                qx = q[mid_r:hi_r] * jnp.exp(b[mid_r:hi_r] - ref)
                kx = k[lo_r:mid_r] * jnp.exp(ref - b[lo_r:mid_r])
                a = _dot_nt(qx.astype(bf16), kx.astype(bf16))
                ov = _dot(a.astype(bf16), vb[lo_r:mid_r])
                base = mid_r
            else:
                ref = b[mid_r:mid_r + 1]
                qx = q[lo_r:mid_r] * jnp.exp(b[lo_r:mid_r] - ref)
                kx = k[mid_r:hi_r] * jnp.exp(ref - b[mid_r:hi_r])
                a = _dot_nt(qx.astype(bf16), kx.astype(bf16))
                ov = _dot(a.astype(bf16), vb[mid_r:hi_r])
                base = lo_r
            for p in range((hi_r - lo_r) // 2 // HG_SUB):
                add(base // HG_SUB + p, ov[p * HG_SUB:(p + 1) * HG_SUB])
            block(lo_r, mid_r)
            block(mid_r, hi_r)

        block(0, T)
        o_intra = jnp.concatenate(o_blk, axis=0)

        st = st_s[d]
        b_end = b[T - 1:T] if d == 0 else b[0:1]
        o_inter = _dot_nt((q * jnp.exp(b)).astype(bf16), st.astype(bf16))
        o_s[d, pl.ds(r0, T), :] = o_intra + o_inter
        kx_end = (k * jnp.exp(b_end - b)).astype(bf16)
        st_s[d] = st * jnp.exp(b_end) + _dot_tn(vb, kx_end)

    def loop_body(j, carry):
        step(0, j)
        step(1, nc - 1 - j)
        return carry
    lax.fori_loop(0, nc, loop_body, 0)

    ng = ng_ref[...]

    def out_body(j, carry):
        r0 = pl.multiple_of(j * T, T)
        y = o_s[0, pl.ds(r0, T), :] + o_s[1, pl.ds(r0, T), :]
        y = _rms_rows(y, ng) * _silu(og_ref[0, pl.ds(r0, T), :])
        y_ref[0, pl.ds(r0, T), :] = y.astype(y_ref.dtype)
        return carry
    lax.fori_loop(0, nc, out_body, 0)

    for d in range(2):
        sout_ref[0, d, 0] = st_s[d].T


def hgrn_call(proj, lb_rows, norm_g, s0, layer):
    B, L, _ = proj.shape

    def col(cb):
        return pl.BlockSpec((1, L, LANES), lambda b, h, cb=cb: (b, 0, cb + h))

    kern = functools.partial(_hgrn_kernel, seq_len=L)
    return pl.pallas_call(
        kern,
        grid=(B, N_HEADS),
        in_specs=[col(CB_HQ), col(CB_HF), col(CB_HF + N_HEADS), col(CB_HI), col(CB_HG),
                  pl.BlockSpec((2 * N_HEADS, D_HEAD), lambda b, h: (0, 0)),
                  pl.BlockSpec((1, LANES), lambda b, h: (0, h)),
                  pl.BlockSpec((1, 1, 2, 1, D_HEAD, D_HEAD), lambda b, h: (b, layer, 0, h, 0, 0))],
        out_specs=[pl.BlockSpec((1, L, LANES), lambda b, h: (b, 0, h)),
                   pl.BlockSpec((1, 2, 1, D_HEAD, D_HEAD), lambda b, h: (b, 0, h, 0, 0))],
        out_shape=[jax.ShapeDtypeStruct((B, L, BRANCH_W), bf16),
                   jax.ShapeDtypeStruct((B, 2, N_HEADS, D_HEAD, D_HEAD), f32)],
        scratch_shapes=[pltpu.VMEM((2, L, D_HEAD), f32),
                        pltpu.VMEM((2, D_HEAD, D_HEAD), f32)],
        compiler_params=_cparams(("arbitrary", "arbitrary")),
        name="hgrn",
    )(proj, proj, proj, proj, proj, lb_rows, norm_g, s0)


def _pair_swap(x):
    lane = lax.broadcasted_iota(jnp.int32, x.shape, 1)
    return jnp.where((lane % 2) == 0, pltpu.roll(x, LANES - 1, 1), pltpu.roll(x, 1, 1))


def _head_norm_pair(x, g_row):
    lane = lax.broadcasted_iota(jnp.int32, x.shape, 1)
    low = lane < AT_HD
    sq = x * x
    s_lo = jnp.sum(jnp.where(low, sq, 0.0), axis=-1, keepdims=True)
    s_hi = jnp.sum(jnp.where(low, 0.0, sq), axis=-1, keepdims=True)
    ms = jnp.where(low, s_lo, s_hi) * (1.0 / AT_HD)
    return x * lax.rsqrt(ms + EPS) * g_row


def _qkv_prep_kernel(aq_ref, ak_ref, av_ref, qg_ref, kg_ref, cos_ref, sin_ref,
                     q_ref, k_ref, v_ref, kn_ref, *, use_rope):
    qg = qg_ref[...]
    kg = kg_ref[...]
    if use_rope:
        cos = cos_ref[...]
        sin = sin_ref[...]
    for p in range(AT_HEADS // 2):
        x = _head_norm_pair(aq_ref[0, :, p * LANES:(p + 1) * LANES], qg)
        if use_rope:
            x = x * cos + _pair_swap(x) * sin
        x = (x * (AT_HD ** -0.5)).astype(bf16)
        q_ref[0, 2 * p] = x[:, :AT_HD]
        q_ref[0, 2 * p + 1] = x[:, AT_HD:]
    kn = _head_norm_pair(ak_ref[0], kg)
    kn_ref[0] = kn
    if use_rope:
        kn = kn * cos + _pair_swap(kn) * sin
    kb = kn.astype(bf16)
    vb = av_ref[0].astype(bf16)
    for g in range(AT_KV_HEADS):
        k_ref[0, g] = kb[:, g * AT_HD:(g + 1) * AT_HD]
        v_ref[0, g] = vb[:, g * AT_HD:(g + 1) * AT_HD]


def qkv_prep_call(proj, qg_row, kg_row, cos_t, sin_t, use_rope):
    B, L, _ = proj.shape
    tm = min(L, 512)
    kern = functools.partial(_qkv_prep_kernel, use_rope=use_rope)
    return pl.pallas_call(
        kern,
        grid=(B, L // tm),
        in_specs=[pl.BlockSpec((1, tm, 4 * LANES), lambda b, m: (b, m, CB_AQ // 4)),
                  pl.BlockSpec((1, tm, LANES), lambda b, m: (b, m, CB_AK)),
                  pl.BlockSpec((1, tm, LANES), lambda b, m: (b, m, CB_AV)),
                  pl.BlockSpec((1, LANES), lambda b, m: (0, 0)),
                  pl.BlockSpec((1, LANES), lambda b, m: (0, 0)),
                  pl.BlockSpec((tm, LANES), lambda b, m: (m, 0)),
                  pl.BlockSpec((tm, LANES), lambda b, m: (m, 0))],
        out_specs=[pl.BlockSpec((1, AT_HEADS, tm, AT_HD), lambda b, m: (b, 0, m, 0)),
                   pl.BlockSpec((1, AT_KV_HEADS, tm, AT_HD), lambda b, m: (b, 0, m, 0)),
                   pl.BlockSpec((1, AT_KV_HEADS, tm, AT_HD), lambda b, m: (b, 0, m, 0)),
                   pl.BlockSpec((1, tm, LANES), lambda b, m: (b, m, 0))],
        out_shape=[jax.ShapeDtypeStruct((B, AT_HEADS, L, AT_HD), bf16),
                   jax.ShapeDtypeStruct((B, AT_KV_HEADS, L, AT_HD), bf16),
                   jax.ShapeDtypeStruct((B, AT_KV_HEADS, L, AT_HD), bf16),
                   jax.ShapeDtypeStruct((B, L, LANES), f32)],
        compiler_params=_cparams(("arbitrary", "arbitrary")),
        name="qkv_prep",
    )(proj, proj, proj, qg_row, kg_row, cos_t, sin_t)


def _attn_kernel(q_ref, k_ref, v_ref, o_ref, m_s, l_s, acc_s, *, tq):
    kv = pl.program_id(3)

    @pl.when(kv == 0)
    def _():
        m_s[...] = jnp.full(m_s.shape, NEG_BIG, f32)
        l_s[...] = jnp.zeros(l_s.shape, f32)
        acc_s[...] = jnp.zeros(acc_s.shape, f32)

    q = q_ref[0].reshape(AT_GROUP * tq, AT_HD)
    s = _dot_nt(q, k_ref[0, 0])
    m_prev = m_s[...]
    m_new = jnp.maximum(m_prev, jnp.max(s, axis=-1, keepdims=True))
    alpha = jnp.exp(m_prev - m_new)
    p = jnp.exp(s - m_new)
    l_s[...] = alpha * l_s[...] + jnp.sum(p, axis=-1, keepdims=True)
    acc_s[...] = alpha * acc_s[...] + _dot(p.astype(bf16), v_ref[0, 0])
    m_s[...] = m_new

    @pl.when(kv == pl.num_programs(3) - 1)
    def _():
        o = acc_s[...] * (1.0 / l_s[...])
        for g in range(AT_GROUP):
            o_ref[0, :, g * AT_HD:(g + 1) * AT_HD] = o[g * tq:(g + 1) * tq].astype(o_ref.dtype)


def attn_call(q, k, v):
    B, _, Lq, _ = q.shape
    Lk = k.shape[2]
    tq = 128
    tk = 512 if Lk % 512 == 0 else Lk
    kern = functools.partial(_attn_kernel, tq=tq)
    rows = AT_GROUP * tq
    return pl.pallas_call(
        kern,
        grid=(B, AT_KV_HEADS, Lq // tq, Lk // tk),
        in_specs=[pl.BlockSpec((1, AT_GROUP, tq, AT_HD), lambda b, g, i, j: (b, g, i, 0)),
                  pl.BlockSpec((1, 1, tk, AT_HD), lambda b, g, i, j: (b, g, j, 0)),
                  pl.BlockSpec((1, 1, tk, AT_HD), lambda b, g, i, j: (b, g, j, 0))],
        out_specs=pl.BlockSpec((1, tq, AT_GROUP * AT_HD), lambda b, g, i, j: (b, i, g)),
        out_shape=jax.ShapeDtypeStruct((B, Lq, AT_HEADS * AT_HD), bf16),
        scratch_shapes=[pltpu.VMEM((rows, 1), f32),
                        pltpu.VMEM((rows, 1), f32),
                        pltpu.VMEM((rows, AT_HD), f32)],
        compiler_params=_cparams(("arbitrary", "arbitrary", "arbitrary", "arbitrary")),
        name="attention",
    )(q, k, v)


def _merge_kernel(x_ref, yml_ref, yhg_ref, yat_ref, bg_ref, mod_ref, wb_ref, wo_ref, o_ref):
    merged = None
    for n, y_ref in enumerate((yml_ref, yhg_ref, yat_ref)):
        pn = _dot(y_ref[0], wb_ref[n])
        gn = _sigmoid(bg_ref[0, :, n * D_MODEL:(n + 1) * D_MODEL])
        merged = gn * pn if merged is None else merged + gn * pn
    out = _dot(merged.astype(bf16), wo_ref[...])
    o_ref[0] = x_ref[0] + mod_ref[0, 2:3, :] * out


def merge_call(x, y_ml, y_hg, y_at, proj, mod, wb, wo, shared_mod):
    B, L, _ = x.shape
    tm = min(L, 256)
    mod_map = (lambda b, m: (0, 0, 0)) if shared_mod else (lambda b, m: (b, 0, 0))
    yspec = pl.BlockSpec((1, tm, BRANCH_W), lambda b, m: (b, m, 0))
    return pl.pallas_call(
        _merge_kernel,
        grid=(B, L // tm),
        in_specs=[pl.BlockSpec((1, tm, D_MODEL), lambda b, m: (b, m, 0)),
                  yspec, yspec, yspec,
                  pl.BlockSpec((1, tm, 3 * D_MODEL), lambda b, m: (b, m, CB_BG)),
                  pl.BlockSpec((1, 8, D_MODEL), mod_map),
                  pl.BlockSpec((3, BRANCH_W, D_MODEL), lambda b, m: (0, 0, 0)),
                  pl.BlockSpec((D_MODEL, D_MODEL), lambda b, m: (0, 0))],
        out_specs=pl.BlockSpec((1, tm, D_MODEL), lambda b, m: (b, m, 0)),
        out_shape=jax.ShapeDtypeStruct((B, L, D_MODEL), f32),
        compiler_params=_cparams(("arbitrary", "arbitrary")),
        name="merge_out",
    )(x, y_ml, y_hg, y_at, proj, mod, wb, wo)


def _ffn_kernel(x_ref, mod_ref, g_ref, wg_ref, wu_ref, wd_ref, fg_ref, o_ref, h_s, acc_s, *, final_norm):
    j = pl.program_id(2)

    @pl.when(j == 0)
    def _():
        h = _rms_rows(x_ref[0], g_ref[...]) * (1.0 + mod_ref[0, 4:5, :]) + mod_ref[0, 3:4, :]
        h_s[...] = h.astype(bf16)

    hb = h_s[...]
    act = _silu(_dot(hb, wg_ref[...])) * _dot(hb, wu_ref[...])
    part = _dot(act.astype(bf16), wd_ref[...])

    @pl.when(j == 0)
    def _():
        acc_s[...] = part

    @pl.when(j != 0)
    def _():
        acc_s[...] = acc_s[...] + part

    @pl.when(j == pl.num_programs(2) - 1)
    def _():
        y = x_ref[0] + mod_ref[0, 5:6, :] * acc_s[...]
        if final_norm:
            y = _rms_rows(y, fg_ref[...])
        o_ref[0] = y


def ffn_call(x, mod, g_row, w_in, w_out, fg_row, shared_mod, final_norm):
    B, L, _ = x.shape
    tm = min(L, 512)
    th = FFN_HIDDEN // 2
    nh = FFN_HIDDEN // th
    mod_map = (lambda b, m, j: (0, 0, 0)) if shared_mod else (lambda b, m, j: (b, 0, 0))
    kern = functools.partial(_ffn_kernel, final_norm=final_norm)
    return pl.pallas_call(
        kern,
        grid=(B, L // tm, nh),
        in_specs=[pl.BlockSpec((1, tm, D_MODEL), lambda b, m, j: (b, m, 0)),
                  pl.BlockSpec((1, 8, D_MODEL), mod_map),
                  pl.BlockSpec((1, D_MODEL), lambda b, m, j: (0, 0)),
                  pl.BlockSpec((D_MODEL, th), lambda b, m, j: (0, j)),
                  pl.BlockSpec((D_MODEL, th), lambda b, m, j: (0, nh + j)),
                  pl.BlockSpec((th, D_MODEL), lambda b, m, j: (j, 0)),
                  pl.BlockSpec((1, D_MODEL), lambda b, m, j: (0, 0))],
        out_specs=pl.BlockSpec((1, tm, D_MODEL), lambda b, m, j: (b, m, 0)),
        out_shape=jax.ShapeDtypeStruct((B, L, D_MODEL), f32),
        scratch_shapes=[pltpu.VMEM((tm, D_MODEL), bf16),
                        pltpu.VMEM((tm, D_MODEL), f32)],
        compiler_params=_cparams(("arbitrary", "arbitrary", "arbitrary")),
        name="ffn",
    )(x, mod, g_row, w_in, w_in, w_out, fg_row)


def _reorder_w_in(w):
    o_mg = 2048
    o_hq = 2064
    o_bg = 5392
    pad = jnp.zeros((w.shape[0], N_COLS - 8464), w.dtype)
    return jnp.concatenate([w[:, o_bg:o_bg + 3 * D_MODEL], w[:, :o_mg], w[:, o_hq:o_bg],
                            w[:, o_mg:o_hq], pad], axis=1).astype(bf16)


def _rope_tables(L):
    rows = L // GRID_W
    row = jnp.repeat(jnp.arange(rows, dtype=f32), GRID_W)
    colp = jnp.tile(jnp.arange(GRID_W, dtype=f32), rows)
    n_freq = AT_HD // 4
    inv = ROPE_THETA ** (-jnp.arange(n_freq, dtype=f32) / n_freq)
    ang = jnp.concatenate([row[:, None] * inv, colp[:, None] * inv], axis=-1)
    cos = jnp.repeat(jnp.cos(ang), 2, axis=-1)
    sin = jnp.repeat(jnp.sin(ang), 2, axis=-1)
    sign = jnp.tile(jnp.array([-1.0, 1.0], f32), AT_HD // 2)
    sin = sin * sign
    return jnp.tile(cos, (1, 2)), jnp.tile(sin, (1, 2))


def _mod_rows(ada_rows):
    r = ada_rows.shape[0]
    m = ada_rows.reshape(r, 6, D_MODEL)
    return jnp.concatenate([m, jnp.zeros((r, 2, D_MODEL), f32)], axis=1)


def kernel(x_prompt, x_sample, c, cache_k, cache_v, state_ml_C, state_ml_n, state_ml_m, state_hg_S, c_ctx, w_ada, b_ada, norm1_g, norm2_g, w_in, ml_gate_b, ml_norm_g, hg_lb_logits, hg_norm_g, q_norm_g, k_norm_g, w_branch, w_out, w_ffn_in, w_ffn_out, final_g):
    Bp, Lp, _ = x_prompt.shape
    Bs, Ls, _ = x_sample.shape

    n_rows = 16
    cvec = jnp.concatenate([c_ctx[None, :], c, jnp.zeros((n_rows - 1 - Bs, D_MODEL), f32)], axis=0)
    ada = ada_call(cvec, w_ada, b_ada)
    lb_all = lb_call(hg_lb_logits)
    cos_t, sin_t = _rope_tables(Ls)

    zeros_c = jnp.zeros((Bp, 1, 2, N_HEADS, D_HEAD, D_HEAD), f32)
    zeros_n = jnp.zeros((Bp, 1, 2, N_HEADS, 1, D_HEAD), f32)
    zeros_m = jnp.zeros((Bp * 2 * N_HEADS,), f32)
    n_state = state_ml_n.reshape(Bs, DEPTH, 2, N_HEADS, 1, D_HEAD)

    xp, xs = x_prompt, x_sample
    nk, nv, nC, nn_, nm, nS = [], [], [], [], [], []
    for l in range(DEPTH):
        w_l = _reorder_w_in(w_in[l])
        wb_l = w_branch[l].astype(bf16)
        wo_l = w_out[l].astype(bf16)
        wfi_l = w_ffn_in[l].astype(bf16)
        wfo_l = w_ffn_out[l].astype(bf16)
        g1 = norm1_g[l][None, :]
        g2 = norm2_g[l][None, :]
        mlg = ml_norm_g[l][None, :]
        hgg = hg_norm_g[l][None, :]
        qg = jnp.tile(q_norm_g[l], 2)[None, :]
        kg = jnp.tile(k_norm_g[l], 2)[None, :]
        gate_b = ml_gate_b[l].reshape(-1).astype(f32)
        fg = final_g[None, :]
        last = l == DEPTH - 1

        mod_ctx = _mod_rows(ada[l, 0:1])
        mod_lat = _mod_rows(ada[l, 1:1 + Bs])

        proj = in_proj_call(xp, mod_ctx, g1, w_l, True)
        y_ml, c_f, n_f, m_f = mlstm_call(proj, gate_b, mlg, zeros_c, zeros_n, zeros_m, 0)
        y_hg, s_f = hgrn_call(proj, lb_all[l], hgg, zeros_c, 0)
        q_h, k_h, v_h, k_n = qkv_prep_call(proj, qg, kg, cos_t[:Lp], sin_t[:Lp], False)
        y_at = attn_call(q_h, k_h, v_h)
        xp = merge_call(xp, y_ml, y_hg, y_at, proj, mod_ctx, wb_l, wo_l, True)
        xp = ffn_call(xp, mod_ctx, g2, wfi_l, wfo_l, fg, True, last)
        nk.append(k_n.reshape(Bp, Lp, AT_KV_HEADS, AT_HD))
        nv.append(proj[:, :, CB_AV * LANES:(CB_AV + 1) * LANES].reshape(Bp, Lp, AT_KV_HEADS, AT_HD))
        nC.append(c_f)
        nn_.append(n_f[:, :, :, 0, :])
        nm.append(m_f[:, :, :, 0, 0])
        nS.append(s_f)

        proj = in_proj_call(xs, mod_lat, g1, w_l, False)
        y_ml, _, _, _ = mlstm_call(proj, gate_b, mlg, state_ml_C, n_state,
                                   state_ml_m[:, l].reshape(-1).astype(f32), l)
        y_hg, _ = hgrn_call(proj, lb_all[l], hgg, state_hg_S, l)
        q_h, k_h, v_h, _ = qkv_prep_call(proj, qg, kg, cos_t, sin_t, True)
        k_all = jnp.concatenate([k_h, jnp.swapaxes(cache_k[:, l], 1, 2).astype(bf16)], axis=2)
        v_all = jnp.concatenate([v_h, jnp.swapaxes(cache_v[:, l], 1, 2).astype(bf16)], axis=2)
        y_at = attn_call(q_h, k_all, v_all)
        xs = merge_call(xs, y_ml, y_hg, y_at, proj, mod_lat, wb_l, wo_l, False)
        xs = ffn_call(xs, mod_lat, g2, wfi_l, wfo_l, fg, False, last)

    return (xp, xs, jnp.stack(nk, axis=1), jnp.stack(nv, axis=1), jnp.stack(nC, axis=1),
            jnp.stack(nn_, axis=1), jnp.stack(nm, axis=1), jnp.stack(nS, axis=1))
```

```python
import functools
import math

import jax
import jax.numpy as jnp
from jax import lax
from jax.experimental import pallas as pl
from jax.experimental.pallas import tpu as pltpu

f32 = jnp.float32
bf16 = jnp.bfloat16

D_MODEL = 1024
DEPTH = 4
N_HEADS = 4
D_HEAD = 128
AT_HEADS = 8
AT_KV_HEADS = 2
AT_GROUP = AT_HEADS // AT_KV_HEADS
AT_HD = 64
BRANCH_W = 512
FFN_HIDDEN = 2816
GRID_W = 64
ROPE_THETA = 10000.0
EPS = 1e-6

LANES = 128
SUBLANES = 8
VMEM_LIMIT = 52 * 1024 * 1024

CB_BG = 0
CB_MQ = 24
CB_MK = 28
CB_MV = 32
CB_MO = 36
CB_HQ = 40
CB_HF = 44
CB_HI = 52
CB_HG = 56
CB_AQ = 60
CB_AK = 64
CB_AV = 65
CB_MG = 66
N_COL_BLOCKS = 68
N_COLS = N_COL_BLOCKS * LANES

ML_CHUNK = 128
HG_CHUNK = 64
HG_SUB = 8
NEG_BIG = -1e30


def _cparams(sem):
    return pltpu.CompilerParams(dimension_semantics=sem, vmem_limit_bytes=VMEM_LIMIT)


def _dot(a, b):
    return jnp.dot(a, b, preferred_element_type=f32)


def _dot_nt(a, b):
    return lax.dot_general(a, b, (((1,), (1,)), ((), ())), preferred_element_type=f32)


def _dot_tn(a, b):
    return lax.dot_general(a, b, (((0,), (0,)), ((), ())), preferred_element_type=f32)


def _sigmoid(x):
    return 1.0 / (1.0 + jnp.exp(-x))


def _silu(x):
    return x * _sigmoid(x)


def _log_sigmoid(x):
    return jnp.minimum(x, 0.0) - jnp.log1p(jnp.exp(-jnp.abs(x)))


def _split3(x):
    hi = x.astype(bf16)
    r1 = x - hi.astype(f32)
    mid = r1.astype(bf16)
    lo = (r1 - mid.astype(f32)).astype(bf16)
    return hi, mid, lo


def _rms_rows(x, g_row):
    ms = jnp.mean(x * x, axis=-1, keepdims=True)
    return x * lax.rsqrt(ms + EPS) * g_row


def _ada_kernel(c_ref, w_ref, b_ref, o_ref):
    s = _silu(c_ref[...]).astype(bf16)
    o_ref[0] = _dot(s, w_ref[0].astype(bf16)) + b_ref[0]


def ada_call(cvec, w_ada, b_ada):
    rows = cvec.shape[0]
    tn = 1536
    n6 = 6 * D_MODEL
    return pl.pallas_call(
        _ada_kernel,
        grid=(DEPTH, n6 // tn),
        in_specs=[pl.BlockSpec((rows, D_MODEL), lambda l, n: (0, 0)),
                  pl.BlockSpec((1, D_MODEL, tn), lambda l, n: (l, 0, n)),
                  pl.BlockSpec((1, 1, tn), lambda l, n: (l, 0, n))],
        out_specs=pl.BlockSpec((1, rows, tn), lambda l, n: (l, 0, n)),
        out_shape=jax.ShapeDtypeStruct((DEPTH, rows, n6), f32),
        compiler_params=_cparams(("arbitrary", "arbitrary")),
        name="ada",
    )(cvec, w_ada, b_ada.reshape(DEPTH, 1, n6))


def _lb_kernel(x_ref, o_ref):
    xs = [x_ref[l] for l in range(DEPTH)]
    mx = xs[0]
    for l in range(1, DEPTH):
        mx = jnp.maximum(mx, xs[l])
    es = [jnp.exp(x - mx) for x in xs]
    tot = es[0]
    for l in range(1, DEPTH):
        tot = tot + es[l]
    sm = [e / tot for e in es]
    run = sm[0]
    o_ref[0] = run - sm[0]
    for l in range(1, DEPTH):
        run = run + sm[l]
        o_ref[l] = run - sm[0]


def lb_call(logits):
    x = logits.astype(f32).reshape(DEPTH, 2 * N_HEADS, D_HEAD)
    return pl.pallas_call(
        _lb_kernel,
        out_shape=jax.ShapeDtypeStruct((DEPTH, 2 * N_HEADS, D_HEAD), f32),
        name="hg_lower_bounds",
    )(x)


def _in_proj_kernel(x_ref, mod_ref, g_ref, w_ref, o_ref, h_s):
    @pl.when(pl.program_id(2) == 0)
    def _():
        x = x_ref[0]
        sh = mod_ref[0, 0:1, :]
        sc = mod_ref[0, 1:2, :]
        h = _rms_rows(x, g_ref[...]) * (1.0 + sc) + sh
        h_s[...] = h.astype(bf16)

    o_ref[0] = _dot(h_s[...], w_ref[...])


def in_proj_call(x, mod, g_row, w, shared_mod):
    B, L, _ = x.shape
    tm = min(L, 512)
    tn = N_COLS // 4
    mod_map = (lambda b, m, n: (0, 0, 0)) if shared_mod else (lambda b, m, n: (b, 0, 0))
    return pl.pallas_call(
        _in_proj_kernel,
        grid=(B, L // tm, N_COLS // tn),
        in_specs=[pl.BlockSpec((1, tm, D_MODEL), lambda b, m, n: (b, m, 0)),
                  pl.BlockSpec((1, 8, D_MODEL), mod_map),
                  pl.BlockSpec((1, D_MODEL), lambda b, m, n: (0, 0)),
                  pl.BlockSpec((D_MODEL, tn), lambda b, m, n: (0, n))],
        out_specs=pl.BlockSpec((1, tm, tn), lambda b, m, n: (b, m, n)),
        out_shape=jax.ShapeDtypeStruct((B, L, N_COLS), f32),
        scratch_shapes=[pltpu.VMEM((tm, D_MODEL), bf16)],
        compiler_params=_cparams(("arbitrary", "arbitrary", "arbitrary")),
        name="in_proj",
    )(x, mod, g_row, w)


def _mlstm_kernel(bias_ref, m0_ref, q_ref, k_ref, v_ref, mo_ref, g_ref, ng_ref, c0_ref, n0_ref,
                  y_ref, cout_ref, nout_ref, mout_ref,
                  gt_s, rw_s, cl_s, h_s, c_s, n_s, m_s, *, seq_len):
    T = ML_CHUNK
    nc = seq_len // T
    b_idx = pl.program_id(0)
    h = pl.program_id(1)
    scale = D_HEAD ** -0.5

    def tr_body(j, carry):
        r0 = pl.multiple_of(j * T, T)
        gt_s[:, pl.ds(r0, T)] = g_ref[0, pl.ds(r0, T), :].T
        return carry
    lax.fori_loop(0, nc, tr_body, 0)

    i_f = gt_s[pl.ds(h, 1), :] + bias_ref[h]
    f_f = _log_sigmoid(gt_s[pl.ds(4 + h, 1), :] + bias_ref[4 + h])
    i_b = gt_s[pl.ds(8 + h, 1), :] + bias_ref[8 + h]
    f_b = _log_sigmoid(gt_s[pl.ds(12 + h, 1), :] + bias_ref[12 + h])
    zrow = jnp.zeros_like(i_f)
    rw_s[...] = jnp.concatenate([f_f, i_f, f_b, i_b, zrow, zrow, zrow, zrow], axis=0)

    ui = lax.broadcasted_iota(jnp.int32, (T, T), 0)
    si = lax.broadcasted_iota(jnp.int32, (T, T), 1)
    tri = jnp.concatenate([(ui <= si).astype(bf16), (ui >= si).astype(bf16)], axis=1)
    causal_f = si <= ui
    causal_b = si >= ui

    def cs_body(j, carry):
        r0 = pl.multiple_of(j * T, T)
        rows = rw_s[:, pl.ds(r0, T)]
        hi, mid, lo = _split3(rows)
        cs = _dot(hi, tri) + _dot(mid, tri) + _dot(lo, tri)
        new_rows = jnp.concatenate([cs[0:1, 0:T], rows[1:2], cs[2:3, T:2 * T], rows[3:8]], axis=0)
        rw_s[:, pl.ds(r0, T)] = new_rows
        cl_s[pl.ds(r0, T), :] = jnp.concatenate([new_rows, jnp.zeros((T - 8, T), f32)], axis=0).T
        return carry
    lax.fori_loop(0, nc, cs_body, 0)

    for d in range(2):
        c_s[d] = c0_ref[0, 0, d, 0]
        n_s[d] = n0_ref[0, 0, d, 0]
        m_s[d] = jnp.full((1, LANES), m0_ref[b_idx * 2 * N_HEADS + d * N_HEADS + h], f32)

    def step(d, j):
        r0 = pl.multiple_of(j * T, T)
        rows = rw_s[:, pl.ds(r0, T)]
        cols = cl_s[pl.ds(r0, T), :]
        b_row = rows[2 * d:2 * d + 1]
        i_row = rows[2 * d + 1:2 * d + 2]
        b_col = cols[:, 2 * d:2 * d + 1]
        i_col = cols[:, 2 * d + 1:2 * d + 2]
        b_end = b_row[:, T - 1:T] if d == 0 else b_row[:, 0:1]

        q = q_ref[0, pl.ds(r0, T), :] * scale
        k = k_ref[0, pl.ds(r0, T), :]
        qb = q.astype(bf16)
        kb = k.astype(bf16)
        vb = v_ref[0, pl.ds(r0, T), :].astype(bf16)
        c_prev = c_s[d]
        n_prev = n_s[d]
        m_prev = m_s[d][:, 0:1]

        dmat = jnp.where(causal_f if d == 0 else causal_b, b_col - b_row + i_row, NEG_BIG)
        m_state = b_col + m_prev
        m_t = jnp.maximum(m_state, jnp.max(dmat, axis=-1, keepdims=True))
        w_intra = jnp.exp(dmat - m_t)
        w_state = jnp.exp(m_state - m_t)
        s = _dot_nt(qb, kb) * w_intra
        num = _dot(s.astype(bf16), vb) + w_state * _dot(qb, c_prev.astype(bf16))
        den = jnp.sum(s, axis=-1, keepdims=True) + w_state * jnp.sum(q * n_prev, axis=-1, keepdims=True)
        hh = num * (1.0 / jnp.maximum(jnp.abs(den), jnp.exp(-m_t)))
        h_s[d, pl.ds(r0, T), :] = hh

        g = b_end - b_col + i_col
        m_new = jnp.maximum(b_end + m_prev, jnp.max(g, axis=0, keepdims=True))
        w_k = jnp.exp(g - m_new)
        decay = jnp.exp(b_end + m_prev - m_new)
        kw = k * w_k
        c_s[d] = decay * c_prev + _dot_tn(kw.astype(bf16), vb)
        n_s[d] = decay * n_prev + jnp.sum(kw, axis=0, keepdims=True)
        m_s[d] = jnp.broadcast_to(m_new, (1, LANES))

    def loop_body(j, carry):
        step(0, j)
        step(1, nc - 1 - j)
        return carry
    lax.fori_loop(0, nc, loop_body, 0)

    ng = ng_ref[...]

    def out_body(j, carry):
        r0 = pl.multiple_of(j * T, T)
        y = h_s[0, pl.ds(r0, T), :] + h_s[1, pl.ds(r0, T), :]
        y = _rms_rows(y, ng) * _sigmoid(mo_ref[0, pl.ds(r0, T), :])
        y_ref[0, pl.ds(r0, T), :] = y.astype(y_ref.dtype)
        return carry
    lax.fori_loop(0, nc, out_body, 0)

    for d in range(2):
        cout_ref[0, d, 0] = c_s[d]
        nout_ref[0, d, 0] = n_s[d]
        mout_ref[0, d, 0] = m_s[d]


def mlstm_call(proj, gate_bias, norm_g, c0, n0, m0, layer):
    B, L, _ = proj.shape

    def col(cb):
        return pl.BlockSpec((1, L, LANES), lambda b, h, cb=cb: (b, 0, cb + h))

    smem = pl.BlockSpec(memory_space=pltpu.SMEM)
    kern = functools.partial(_mlstm_kernel, seq_len=L)
    return pl.pallas_call(
        kern,
        grid=(B, N_HEADS),
        in_specs=[smem, smem,
                  col(CB_MQ), col(CB_MK), col(CB_MV), col(CB_MO),
                  pl.BlockSpec((1, L, LANES), lambda b, h: (b, 0, CB_MG)),
                  pl.BlockSpec((1, LANES), lambda b, h: (0, h)),
                  pl.BlockSpec((1, 1, 2, 1, D_HEAD, D_HEAD), lambda b, h: (b, layer, 0, h, 0, 0)),
                  pl.BlockSpec((1, 1, 2, 1, 1, D_HEAD), lambda b, h: (b, layer, 0, h, 0, 0))],
        out_specs=[pl.BlockSpec((1, L, LANES), lambda b, h: (b, 0, h)),
                   pl.BlockSpec((1, 2, 1, D_HEAD, D_HEAD), lambda b, h: (b, 0, h, 0, 0)),
                   pl.BlockSpec((1, 2, 1, 1, D_HEAD), lambda b, h: (b, 0, h, 0, 0)),
                   pl.BlockSpec((1, 2, 1, 1, LANES), lambda b, h: (b, 0, h, 0, 0))],
        out_shape=[jax.ShapeDtypeStruct((B, L, BRANCH_W), bf16),
                   jax.ShapeDtypeStruct((B, 2, N_HEADS, D_HEAD, D_HEAD), f32),
                   jax.ShapeDtypeStruct((B, 2, N_HEADS, 1, D_HEAD), f32),
                   jax.ShapeDtypeStruct((B, 2, N_HEADS, 1, LANES), f32)],
        scratch_shapes=[pltpu.VMEM((LANES, L), f32),
                        pltpu.VMEM((8, L), f32),
                        pltpu.VMEM((L, LANES), f32),
                        pltpu.VMEM((2, L, D_HEAD), f32),
                        pltpu.VMEM((2, D_HEAD, D_HEAD), f32),
                        pltpu.VMEM((2, 1, D_HEAD), f32),
                        pltpu.VMEM((2, 1, LANES), f32)],
        compiler_params=_cparams(("arbitrary", "arbitrary")),
        name="mlstm",
    )(gate_bias, m0, proj, proj, proj, proj, proj, norm_g, c0, n0)


def _hgrn_kernel(q_ref, f0_ref, f1_ref, i_ref, og_ref, lb_ref, ng_ref, s0_ref,
                 y_ref, sout_ref, o_s, st_s, *, seq_len):
    T = HG_CHUNK
    nc = seq_len // T
    h = pl.program_id(1)
    f_refs = (f0_ref, f1_ref)

    lbs, loglb, log1mlb = [], [], []
    for d in range(2):
        lb = lb_ref[pl.ds(d * N_HEADS + h, 1), :]
        lbs.append(lb)
        loglb.append(jnp.log(lb))
        log1mlb.append(jnp.log1p(-lb))
        st_s[d] = s0_ref[0, 0, d, 0].T

    ti = lax.broadcasted_iota(jnp.int32, (T, T), 0)
    ui = lax.broadcasted_iota(jnp.int32, (T, T), 1)
    tri = ((ui <= ti).astype(bf16), (ui >= ti).astype(bf16))
    sub_t = lax.broadcasted_iota(jnp.int32, (HG_SUB, D_HEAD), 0)

    def step(d, j):
        r0 = pl.multiple_of(j * T, T)
        q = q_ref[0, pl.ds(r0, T), :]
        hf = f_refs[d][0, pl.ds(r0, T), :]
        v = i_ref[0, pl.ds(r0, T), :]
        vb = v.astype(bf16)

        e = jnp.exp(-jnp.abs(hf))
        sig_neg = jnp.where(hf >= 0.0, e, 1.0) / (1.0 + e)
        logsig = jnp.minimum(hf, 0.0) - jnp.log1p(e)
        cterm = log1mlb[d] + logsig
        amax = jnp.maximum(loglb[d], cterm)
        logf = amax + jnp.log1p(jnp.exp(-jnp.abs(loglb[d] - cterm)))
        k = (1.0 - lbs[d]) * sig_neg

        hi, mid, lo = _split3(logf)
        b = _dot(tri[d], hi) + _dot(tri[d], mid) + _dot(tri[d], lo)

        nsub = T // HG_SUB
        o_blk = [None] * nsub

        def add(idx, val):
            o_blk[idx] = val if o_blk[idx] is None else o_blk[idx] + val

        def diag(lo_r):
            bs = b[lo_r:lo_r + HG_SUB]
            qs = q[lo_r:lo_r + HG_SUB]
            ks = k[lo_r:lo_r + HG_SUB]
            vs = v[lo_r:lo_r + HG_SUB]
            acc = None
            for s in range(HG_SUB):
                valid = (sub_t >= s) if d == 0 else (sub_t <= s)
                ee = jnp.where(valid, jnp.exp(jnp.minimum(bs - bs[s:s + 1], 0.0)), 0.0)
                a_col = jnp.sum(qs * ee * ks[s:s + 1], axis=-1, keepdims=True)
                term = a_col * vs[s:s + 1]
                acc = term if acc is None else acc + term
            add(lo_r // HG_SUB, acc)

        def block(lo_r, hi_r):
            if hi_r - lo_r == HG_SUB:
                diag(lo_r)
                return
            mid_r = (lo_r + hi_r) // 2
            if d == 0:
                ref = b[mid_r - 1:mid_r]
                qx = q[mid_r:hi_r] * jnp.exp(b[mid_r:hi_r] - ref)
                kx = k[lo_r:mid_r] * jnp.exp(ref - b[lo_r:mid_r])
                a = _dot_nt(qx.astype(bf16), kx.astype(bf16))
                ov = _dot(a.astype(bf16), vb[lo_r:mid_r])
                base = mid_r
            else:
                ref = b[mid_r:mid_r + 1]
                qx = q[lo_r:mid_r] * jnp.exp(b[lo_r:mid_r] - ref)
                kx = k[mid_r:hi_r] * jnp.exp(ref - b[mid_r:hi_r])
                a = _dot_nt(qx.astype(bf16), kx.astype(bf16))
                ov = _dot(a.astype(bf16), vb[mid_r:hi_r])
                base = lo_r
            for p in range((hi_r - lo_r) // 2 // HG_SUB):
                add(base // HG_SUB + p, ov[p * HG_SUB:(p + 1) * HG_SUB])
            block(lo_r, mid_r)
            block(mid_r, hi_r)

        block(0, T)
        o_intra = jnp.concatenate(o_blk, axis=0)

        st = st_s[d]
        b_end = b[T - 1:T] if d == 0 else b[0:1]
        o_inter = _dot_nt((q * jnp.exp(b)).astype(bf16), st.astype(bf16))
        o_s[d, pl.ds(r0, T), :] = o_intra + o_inter
        kx_end = (k * jnp.exp(b_end - b)).astype(bf16)
        st_s[d] = st * jnp.exp(b_end) + _dot_tn(vb, kx_end)

    def loop_body(j, carry):
        step(0, j)
        step(1, nc - 1 - j)
        return carry
    lax.fori_loop(0, nc, loop_body, 0, unroll=2)

    ng = ng_ref[...]

    def out_body(j, carry):
        r0 = pl.multiple_of(j * T, T)
        y = o_s[0, pl.ds(r0, T), :] + o_s[1, pl.ds(r0, T), :]
        y = _rms_rows(y, ng) * _silu(og_ref[0, pl.ds(r0, T), :])
        y_ref[0, pl.ds(r0, T), :] = y.astype(y_ref.dtype)
        return carry
    lax.fori_loop(0, nc, out_body, 0)

    for d in range(2):
        sout_ref[0, d, 0] = st_s[d].T


def hgrn_call(proj, lb_rows, norm_g, s0, layer):
    B, L, _ = proj.shape

    def col(cb):
        return pl.BlockSpec((1, L, LANES), lambda b, h, cb=cb: (b, 0, cb + h))

    kern = functools.partial(_hgrn_kernel, seq_len=L)
    return pl.pallas_call(
        kern,
        grid=(B, N_HEADS),
        in_specs=[col(CB_HQ), col(CB_HF), col(CB_HF + N_HEADS), col(CB_HI), col(CB_HG),
                  pl.BlockSpec((2 * N_HEADS, D_HEAD), lambda b, h: (0, 0)),
                  pl.BlockSpec((1, LANES), lambda b, h: (0, h)),
                  pl.BlockSpec((1, 1, 2, 1, D_HEAD, D_HEAD), lambda b, h: (b, layer, 0, h, 0, 0))],
        out_specs=[pl.BlockSpec((1, L, LANES), lambda b, h: (b, 0, h)),
                   pl.BlockSpec((1, 2, 1, D_HEAD, D_HEAD), lambda b, h: (b, 0, h, 0, 0))],
        out_shape=[jax.ShapeDtypeStruct((B, L, BRANCH_W), bf16),
                   jax.ShapeDtypeStruct((B, 2, N_HEADS, D_HEAD, D_HEAD), f32)],
        scratch_shapes=[pltpu.VMEM((2, L, D_HEAD), f32),
                        pltpu.VMEM((2, D_HEAD, D_HEAD), f32)],
        compiler_params=_cparams(("arbitrary", "arbitrary")),
        name="hgrn",
    )(proj, proj, proj, proj, proj, lb_rows, norm_g, s0)


def _pair_swap(x):
    lane = lax.broadcasted_iota(jnp.int32, x.shape, 1)
    return jnp.where((lane % 2) == 0, pltpu.roll(x, LANES - 1, 1), pltpu.roll(x, 1, 1))


def _head_norm_pair(x, g_row):
    lane = lax.broadcasted_iota(jnp.int32, x.shape, 1)
    low = lane < AT_HD
    sq = x * x
    s_lo = jnp.sum(jnp.where(low, sq, 0.0), axis=-1, keepdims=True)
    s_hi = jnp.sum(jnp.where(low, 0.0, sq), axis=-1, keepdims=True)
    ms = jnp.where(low, s_lo, s_hi) * (1.0 / AT_HD)
    return x * lax.rsqrt(ms + EPS) * g_row


def _qkv_prep_kernel(aq_ref, ak_ref, av_ref, qg_ref, kg_ref, cos_ref, sin_ref,
                     q_ref, k_ref, v_ref, kn_ref, *, use_rope):
    qg = qg_ref[...]
    kg = kg_ref[...]
    if use_rope:
        cos = cos_ref[...]
        sin = sin_ref[...]
    for p in range(AT_HEADS // 2):
        x = _head_norm_pair(aq_ref[0, :, p * LANES:(p + 1) * LANES], qg)
        if use_rope:
            x = x * cos + _pair_swap(x) * sin
        x = (x * (AT_HD ** -0.5)).astype(bf16)
        q_ref[0, 2 * p] = x[:, :AT_HD]
        q_ref[0, 2 * p + 1] = x[:, AT_HD:]
    kn = _head_norm_pair(ak_ref[0], kg)
    kn_ref[0] = kn
    if use_rope:
        kn = kn * cos + _pair_swap(kn) * sin
    kb = kn.astype(bf16)
    vb = av_ref[0].astype(bf16)
    for g in range(AT_KV_HEADS):
        k_ref[0, g] = kb[:, g * AT_HD:(g + 1) * AT_HD]
        v_ref[0, g] = vb[:, g * AT_HD:(g + 1) * AT_HD]


def qkv_prep_call(proj, qg_row, kg_row, cos_t, sin_t, use_rope):
    B, L, _ = proj.shape
    tm = min(L, 512)
    kern = functools.partial(_qkv_prep_kernel, use_rope=use_rope)
    return pl.pallas_call(
        kern,
        grid=(B, L // tm),
        in_specs=[pl.BlockSpec((1, tm, 4 * LANES), lambda b, m: (b, m, CB_AQ // 4)),
                  pl.BlockSpec((1, tm, LANES), lambda b, m: (b, m, CB_AK)),
                  pl.BlockSpec((1, tm, LANES), lambda b, m: (b, m, CB_AV)),
                  pl.BlockSpec((1, LANES), lambda b, m: (0, 0)),
                  pl.BlockSpec((1, LANES), lambda b, m: (0, 0)),
                  pl.BlockSpec((tm, LANES), lambda b, m: (m, 0)),
                  pl.BlockSpec((tm, LANES), lambda b, m: (m, 0))],
        out_specs=[pl.BlockSpec((1, AT_HEADS, tm, AT_HD), lambda b, m: (b, 0, m, 0)),
                   pl.BlockSpec((1, AT_KV_HEADS, tm, AT_HD), lambda b, m: (b, 0, m, 0)),
                   pl.BlockSpec((1, AT_KV_HEADS, tm, AT_HD), lambda b, m: (b, 0, m, 0)),
                   pl.BlockSpec((1, tm, LANES), lambda b, m: (b, m, 0))],
        out_shape=[jax.ShapeDtypeStruct((B, AT_HEADS, L, AT_HD), bf16),
                   jax.ShapeDtypeStruct((B, AT_KV_HEADS, L, AT_HD), bf16),
                   jax.ShapeDtypeStruct((B, AT_KV_HEADS, L, AT_HD), bf16),
                   jax.ShapeDtypeStruct((B, L, LANES), f32)],
        compiler_params=_cparams(("arbitrary", "arbitrary")),
        name="qkv_prep",
    )(proj, proj, proj, qg_row, kg_row, cos_t, sin_t)


def _attn_kernel(q_ref, k_ref, v_ref, o_ref, s_s, m_s, *, tq, tk, nkv):
    rows = AT_GROUP * tq
    ncb = tk // LANES
    always = pl.program_id(2) >= 0

    @pl.when(always)
    def _():
        q = q_ref[0].reshape(rows, AT_HD)
        mx = None
        for j in range(nkv):
            s = _dot_nt(q, k_ref[0, 0, j * tk:(j + 1) * tk, :])
            s_s[:, j * tk:(j + 1) * tk] = s
            for cb in range(ncb):
                blk = s[:, cb * LANES:(cb + 1) * LANES]
                mx = blk if mx is None else jnp.maximum(mx, blk)
        m_s[...] = jnp.broadcast_to(jnp.max(mx, axis=-1, keepdims=True), (rows, LANES))

    @pl.when(pl.program_id(1) >= 0)
    def _():
        m_b = m_s[...]
        acc = None
        for j in range(nkv):
            ps = []
            for cb in range(ncb):
                off = j * tk + cb * LANES
                ps.append(jnp.exp(s_s[:, off:off + LANES] - m_b).astype(bf16))
            pv = _dot(jnp.concatenate(ps, axis=1), v_ref[0, 0, j * tk:(j + 1) * tk, :])
            acc = pv if acc is None else acc + pv
        o = acc[:, :AT_HD] * (1.0 / acc[:, AT_HD:AT_HD + 1])
        for g in range(AT_GROUP):
            o_ref[0, :, g * AT_HD:(g + 1) * AT_HD] = o[g * tq:(g + 1) * tq].astype(o_ref.dtype)


def attn_call(q, k, v):
    B, _, Lq, _ = q.shape
    Lk = k.shape[2]
    tq = 128
    tk = 512 if Lk % 512 == 0 else Lk
    nkv = Lk // tk
    nq = Lq // tq
    kern = functools.partial(_attn_kernel, tq=tq, tk=tk, nkv=nkv)
    rows = AT_GROUP * tq
    return pl.pallas_call(
        kern,
        grid=(B, AT_KV_HEADS, nq),
        in_specs=[pl.BlockSpec((1, AT_GROUP, tq, AT_HD), lambda b, g, i: (b, g, i, 0)),
                  pl.BlockSpec((1, 1, Lk, AT_HD), lambda b, g, i: (b, g, 0, 0)),
                  pl.BlockSpec((1, 1, Lk, LANES), lambda b, g, i: (b, g, 0, 0))],
        out_specs=pl.BlockSpec((1, tq, AT_GROUP * AT_HD), lambda b, g, i: (b, i, g)),
        out_shape=jax.ShapeDtypeStruct((B, Lq, AT_HEADS * AT_HD), bf16),
        scratch_shapes=[pltpu.VMEM((rows, Lk), f32),
                        pltpu.VMEM((rows, LANES), f32)],
        compiler_params=_cparams(("arbitrary", "arbitrary", "arbitrary")),
        name="attention",
    )(q, k, v)


def _with_ones_column(v):
    ones = jnp.ones(v.shape[:-1] + (1,), v.dtype)
    zeros = jnp.zeros(v.shape[:-1] + (LANES - AT_HD - 1,), v.dtype)
    return jnp.concatenate([v, ones, zeros], axis=-1)


def _merge_kernel(x_ref, yml_ref, yhg_ref, yat_ref, bg_ref, mod_ref, wb_ref, wo_ref, o_ref):
    merged = None
    for n, y_ref in enumerate((yml_ref, yhg_ref, yat_ref)):
        pn = _dot(y_ref[0], wb_ref[n])
        gn = _sigmoid(bg_ref[0, :, n * D_MODEL:(n + 1) * D_MODEL])
        merged = gn * pn if merged is None else merged + gn * pn
    out = _dot(merged.astype(bf16), wo_ref[...])
    o_ref[0] = x_ref[0] + mod_ref[0, 2:3, :] * out


def merge_call(x, y_ml, y_hg, y_at, proj, mod, wb, wo, shared_mod):
    B, L, _ = x.shape
    tm = min(L, 256)
    mod_map = (lambda b, m: (0, 0, 0)) if shared_mod else (lambda b, m: (b, 0, 0))
    yspec = pl.BlockSpec((1, tm, BRANCH_W), lambda b, m: (b, m, 0))
    return pl.pallas_call(
        _merge_kernel,
        grid=(B, L // tm),
        in_specs=[pl.BlockSpec((1, tm, D_MODEL), lambda b, m: (b, m, 0)),
                  yspec, yspec, yspec,
                  pl.BlockSpec((1, tm, 3 * D_MODEL), lambda b, m: (b, m, CB_BG)),
                  pl.BlockSpec((1, 8, D_MODEL), mod_map),
                  pl.BlockSpec((3, BRANCH_W, D_MODEL), lambda b, m: (0, 0, 0)),
                  pl.BlockSpec((D_MODEL, D_MODEL), lambda b, m: (0, 0))],
        out_specs=pl.BlockSpec((1, tm, D_MODEL), lambda b, m: (b, m, 0)),
        out_shape=jax.ShapeDtypeStruct((B, L, D_MODEL), f32),
        compiler_params=_cparams(("arbitrary", "arbitrary")),
        name="merge_out",
    )(x, y_ml, y_hg, y_at, proj, mod, wb, wo)


def _ffn_kernel(x_ref, mod_ref, g_ref, wg_ref, wu_ref, wd_ref, fg_ref, o_ref, h_s, acc_s, *, final_norm):
    j = pl.program_id(2)

    @pl.when(j == 0)
    def _():
        h = _rms_rows(x_ref[0], g_ref[...]) * (1.0 + mod_ref[0, 4:5, :]) + mod_ref[0, 3:4, :]
        h_s[...] = h.astype(bf16)

    hb = h_s[...]
    act = _silu(_dot(hb, wg_ref[...])) * _dot(hb, wu_ref[...])
    part = _dot(act.astype(bf16), wd_ref[...])

    @pl.when(j == 0)
    def _():
        acc_s[...] = part

    @pl.when(j != 0)
    def _():
        acc_s[...] = acc_s[...] + part

    @pl.when(j == pl.num_programs(2) - 1)
    def _():
        y = x_ref[0] + mod_ref[0, 5:6, :] * acc_s[...]
        if final_norm:
            y = _rms_rows(y, fg_ref[...])
        o_ref[0] = y


def ffn_call(x, mod, g_row, w_in, w_out, fg_row, shared_mod, final_norm):
    B, L, _ = x.shape
    tm = min(L, 512)
    th = FFN_HIDDEN // 2
    nh = FFN_HIDDEN // th
    mod_map = (lambda b, m, j: (0, 0, 0)) if shared_mod else (lambda b, m, j: (b, 0, 0))
    kern = functools.partial(_ffn_kernel, final_norm=final_norm)
    return pl.pallas_call(
        kern,
        grid=(B, L // tm, nh),
        in_specs=[pl.BlockSpec((1, tm, D_MODEL), lambda b, m, j: (b, m, 0)),
                  pl.BlockSpec((1, 8, D_MODEL), mod_map),
                  pl.BlockSpec((1, D_MODEL), lambda b, m, j: (0, 0)),
                  pl.BlockSpec((D_MODEL, th), lambda b, m, j: (0, j)),
                  pl.BlockSpec((D_MODEL, th), lambda b, m, j: (0, nh + j)),
                  pl.BlockSpec((th, D_MODEL), lambda b, m, j: (j, 0)),
                  pl.BlockSpec((1, D_MODEL), lambda b, m, j: (0, 0))],
        out_specs=pl.BlockSpec((1, tm, D_MODEL), lambda b, m, j: (b, m, 0)),
        out_shape=jax.ShapeDtypeStruct((B, L, D_MODEL), f32),
        scratch_shapes=[pltpu.VMEM((tm, D_MODEL), bf16),
                        pltpu.VMEM((tm, D_MODEL), f32)],
        compiler_params=_cparams(("arbitrary", "arbitrary", "arbitrary")),
        name="ffn",
    )(x, mod, g_row, w_in, w_in, w_out, fg_row)


def _reorder_w_in(w):
    o_mg = 2048
    o_hq = 2064
    o_bg = 5392
    pad = jnp.zeros((w.shape[0], N_COLS - 8464), w.dtype)
    return jnp.concatenate([w[:, o_bg:o_bg + 3 * D_MODEL], w[:, :o_mg], w[:, o_hq:o_bg],
                            w[:, o_mg:o_hq], pad], axis=1).astype(bf16)


def _rope_tables(L):
    rows = L // GRID_W
    row = jnp.repeat(jnp.arange(rows, dtype=f32), GRID_W)
    colp = jnp.tile(jnp.arange(GRID_W, dtype=f32), rows)
    n_freq = AT_HD // 4
    inv = ROPE_THETA ** (-jnp.arange(n_freq, dtype=f32) / n_freq)
    ang = jnp.concatenate([row[:, None] * inv, colp[:, None] * inv], axis=-1)
    cos = jnp.repeat(jnp.cos(ang), 2, axis=-1)
    sin = jnp.repeat(jnp.sin(ang), 2, axis=-1)
    sign = jnp.tile(jnp.array([-1.0, 1.0], f32), AT_HD // 2)
    sin = sin * sign
    return jnp.tile(cos, (1, 2)), jnp.tile(sin, (1, 2))


def _mod_rows(ada_rows):
    r = ada_rows.shape[0]
    m = ada_rows.reshape(r, 6, D_MODEL)
    return jnp.concatenate([m, jnp.zeros((r, 2, D_MODEL), f32)], axis=1)


def kernel(x_prompt, x_sample, c, cache_k, cache_v, state_ml_C, state_ml_n, state_ml_m, state_hg_S, c_ctx, w_ada, b_ada, norm1_g, norm2_g, w_in, ml_gate_b, ml_norm_g, hg_lb_logits, hg_norm_g, q_norm_g, k_norm_g, w_branch, w_out, w_ffn_in, w_ffn_out, final_g):
    Bp, Lp, _ = x_prompt.shape
    Bs, Ls, _ = x_sample.shape

    n_rows = 16
    cvec = jnp.concatenate([c_ctx[None, :], c, jnp.zeros((n_rows - 1 - Bs, D_MODEL), f32)], axis=0)
    ada = ada_call(cvec, w_ada, b_ada)
    lb_all = lb_call(hg_lb_logits)
    cos_t, sin_t = _rope_tables(Ls)

    zeros_c = jnp.zeros((Bp, 1, 2, N_HEADS, D_HEAD, D_HEAD), f32)
    zeros_n = jnp.zeros((Bp, 1, 2, N_HEADS, 1, D_HEAD), f32)
    zeros_m = jnp.zeros((Bp * 2 * N_HEADS,), f32)
    n_state = state_ml_n.reshape(Bs, DEPTH, 2, N_HEADS, 1, D_HEAD)

    xp, xs = x_prompt, x_sample
    nk, nv, nC, nn_, nm, nS = [], [], [], [], [], []
    for l in range(DEPTH):
        w_l = _reorder_w_in(w_in[l])
        wb_l = w_branch[l].astype(bf16)
        wo_l = w_out[l].astype(bf16)
        wfi_l = w_ffn_in[l].astype(bf16)
        wfo_l = w_ffn_out[l].astype(bf16)
        g1 = norm1_g[l][None, :]
        g2 = norm2_g[l][None, :]
        mlg = ml_norm_g[l][None, :]
        hgg = hg_norm_g[l][None, :]
        qg = jnp.tile(q_norm_g[l], 2)[None, :]
        kg = jnp.tile(k_norm_g[l], 2)[None, :]
        gate_b = ml_gate_b[l].reshape(-1).astype(f32)
        fg = final_g[None, :]
        last = l == DEPTH - 1

        mod_ctx = _mod_rows(ada[l, 0:1])
        mod_lat = _mod_rows(ada[l, 1:1 + Bs])

        proj = in_proj_call(xp, mod_ctx, g1, w_l, True)
        y_ml, c_f, n_f, m_f = mlstm_call(proj, gate_b, mlg, zeros_c, zeros_n, zeros_m, 0)
        y_hg, s_f = hgrn_call(proj, lb_all[l], hgg, zeros_c, 0)
        q_h, k_h, v_h, k_n = qkv_prep_call(proj, qg, kg, cos_t[:Lp], sin_t[:Lp], False)
        y_at = attn_call(q_h, k_h, _with_ones_column(v_h))
        xp = merge_call(xp, y_ml, y_hg, y_at, proj, mod_ctx, wb_l, wo_l, True)
        xp = ffn_call(xp, mod_ctx, g2, wfi_l, wfo_l, fg, True, last)
        nk.append(k_n.reshape(Bp, Lp, AT_KV_HEADS, AT_HD))
        nv.append(proj[:, :, CB_AV * LANES:(CB_AV + 1) * LANES].reshape(Bp, Lp, AT_KV_HEADS, AT_HD))
        nC.append(c_f)
        nn_.append(n_f[:, :, :, 0, :])
        nm.append(m_f[:, :, :, 0, 0])
        nS.append(s_f)

        proj = in_proj_call(xs, mod_lat, g1, w_l, False)
        y_ml, _, _, _ = mlstm_call(proj, gate_b, mlg, state_ml_C, n_state,
                                   state_ml_m[:, l].reshape(-1).astype(f32), l)
        y_hg, _ = hgrn_call(proj, lb_all[l], hgg, state_hg_S, l)
        q_h, k_h, v_h, _ = qkv_prep_call(proj, qg, kg, cos_t, sin_t, True)
        k_all = jnp.concatenate([k_h, jnp.swapaxes(cache_k[:, l], 1, 2).astype(bf16)], axis=2)
        v_all = jnp.concatenate([v_h, jnp.swapaxes(cache_v[:, l], 1, 2).astype(bf16)], axis=2)
        y_at = attn_call(q_h, k_all, _with_ones_column(v_all))
        xs = merge_call(xs, y_ml, y_hg, y_at, proj, mod_lat, wb_l, wo_l, False)
        xs = ffn_call(xs, mod_lat, g2, wfi_l, wfo_l, fg, False, last)

    return (xp, xs, jnp.stack(nk, axis=1), jnp.stack(nv, axis=1), jnp.stack(nC, axis=1),
            jnp.stack(nn_, axis=1), jnp.stack(nm, axis=1), jnp.stack(nS, axis=1))
```

```python
import functools
import math

import jax
import jax.numpy as jnp
from jax import lax
from jax.experimental import pallas as pl
from jax.experimental.pallas import tpu as pltpu

f32 = jnp.float32
bf16 = jnp.bfloat16

D_MODEL = 1024
DEPTH = 4
N_HEADS = 4
D_HEAD = 128
AT_HEADS = 8
AT_KV_HEADS = 2
AT_GROUP = AT_HEADS // AT_KV_HEADS
AT_HD = 64
BRANCH_W = 512
FFN_HIDDEN = 2816
GRID_W = 64
ROPE_THETA = 10000.0
EPS = 1e-6

LANES = 128
SUBLANES = 8
VMEM_LIMIT = 52 * 1024 * 1024

CB_BG = 0
CB_MQ = 24
CB_MK = 28
CB_MV = 32
CB_MO = 36
CB_HQ = 40
CB_HF = 44
CB_HI = 52
CB_HG = 56
CB_AQ = 60
CB_AK = 64
CB_AV = 65
CB_MG = 66
N_COL_BLOCKS = 68
N_COLS = N_COL_BLOCKS * LANES

ML_CHUNK = 128
ML_EXT = 144
HG_CHUNK = 64
HG_SUB = 8
NEG_BIG = -1e30


def _cparams(sem):
    return pltpu.CompilerParams(dimension_semantics=sem, vmem_limit_bytes=VMEM_LIMIT)


def _dot(a, b):
    return jnp.dot(a, b, preferred_element_type=f32)


def _dot_nt(a, b):
    return lax.dot_general(a, b, (((1,), (1,)), ((), ())), preferred_element_type=f32)


def _dot_tn(a, b):
    return lax.dot_general(a, b, (((0,), (0,)), ((), ())), preferred_element_type=f32)


def _sigmoid(x):
    return 1.0 / (1.0 + jnp.exp(-x))


def _silu(x):
    return x * _sigmoid(x)


def _log_sigmoid(x):
    return jnp.minimum(x, 0.0) - jnp.log1p(jnp.exp(-jnp.abs(x)))


def _split3(x):
    hi = x.astype(bf16)
    r1 = x - hi.astype(f32)
    mid = r1.astype(bf16)
    lo = (r1 - mid.astype(f32)).astype(bf16)
    return hi, mid, lo


def _rms_rows(x, g_row):
    ms = jnp.mean(x * x, axis=-1, keepdims=True)
    return x * lax.rsqrt(ms + EPS) * g_row


def _ada_kernel(c_ref, w_ref, b_ref, o_ref):
    s = _silu(c_ref[...]).astype(bf16)
    o_ref[0] = _dot(s, w_ref[0].astype(bf16)) + b_ref[0]


def ada_call(cvec, w_ada, b_ada):
    rows = cvec.shape[0]
    tn = 1536
    n6 = 6 * D_MODEL
    return pl.pallas_call(
        _ada_kernel,
        grid=(DEPTH, n6 // tn),
        in_specs=[pl.BlockSpec((rows, D_MODEL), lambda l, n: (0, 0)),
                  pl.BlockSpec((1, D_MODEL, tn), lambda l, n: (l, 0, n)),
                  pl.BlockSpec((1, 1, tn), lambda l, n: (l, 0, n))],
        out_specs=pl.BlockSpec((1, rows, tn), lambda l, n: (l, 0, n)),
        out_shape=jax.ShapeDtypeStruct((DEPTH, rows, n6), f32),
        compiler_params=_cparams(("arbitrary", "arbitrary")),
        name="ada",
    )(cvec, w_ada, b_ada.reshape(DEPTH, 1, n6))


def _lb_kernel(x_ref, o_ref):
    xs = [x_ref[l] for l in range(DEPTH)]
    mx = xs[0]
    for l in range(1, DEPTH):
        mx = jnp.maximum(mx, xs[l])
    es = [jnp.exp(x - mx) for x in xs]
    tot = es[0]
    for l in range(1, DEPTH):
        tot = tot + es[l]
    sm = [e / tot for e in es]
    run = sm[0]
    o_ref[0] = run - sm[0]
    for l in range(1, DEPTH):
        run = run + sm[l]
        o_ref[l] = run - sm[0]


def lb_call(logits):
    x = logits.astype(f32).reshape(DEPTH, 2 * N_HEADS, D_HEAD)
    return pl.pallas_call(
        _lb_kernel,
        out_shape=jax.ShapeDtypeStruct((DEPTH, 2 * N_HEADS, D_HEAD), f32),
        name="hg_lower_bounds",
    )(x)


def _in_proj_kernel(x_ref, mod_ref, g_ref, w_ref, o_ref, h_s):
    @pl.when(pl.program_id(2) == 0)
    def _():
        x = x_ref[0]
        sh = mod_ref[0, 0:1, :]
        sc = mod_ref[0, 1:2, :]
        h = _rms_rows(x, g_ref[...]) * (1.0 + sc) + sh
        h_s[...] = h.astype(bf16)

    o_ref[0] = _dot(h_s[...], w_ref[...])


def in_proj_call(x, mod, g_row, w, shared_mod):
    B, L, _ = x.shape
    tm = min(L, 512)
    tn = N_COLS // 4
    mod_map = (lambda b, m, n: (0, 0, 0)) if shared_mod else (lambda b, m, n: (b, 0, 0))
    return pl.pallas_call(
        _in_proj_kernel,
        grid=(B, L // tm, N_COLS // tn),
        in_specs=[pl.BlockSpec((1, tm, D_MODEL), lambda b, m, n: (b, m, 0)),
                  pl.BlockSpec((1, 8, D_MODEL), mod_map),
                  pl.BlockSpec((1, D_MODEL), lambda b, m, n: (0, 0)),
                  pl.BlockSpec((D_MODEL, tn), lambda b, m, n: (0, n))],
        out_specs=pl.BlockSpec((1, tm, tn), lambda b, m, n: (b, m, n)),
        out_shape=jax.ShapeDtypeStruct((B, L, N_COLS), f32),
        scratch_shapes=[pltpu.VMEM((tm, D_MODEL), bf16)],
        compiler_params=_cparams(("arbitrary", "arbitrary", "arbitrary")),
        name="in_proj",
    )(x, mod, g_row, w)


def _mlstm_kernel(bias_ref, m0_ref, q_ref, k_ref, v_ref, mo_ref, g_ref, ng_ref, c0_ref, n0_ref,
                  y_ref, cout_ref, nout_ref, mout_ref,
                  gt_s, rw_s, ucb_s, vt_s, ht_s, ct_s, m_s, *, seq_len):
    T = ML_CHUNK
    assert T == LANES
    nc = seq_len // T
    b_idx = pl.program_id(0)
    h = pl.program_id(1)
    scale = D_HEAD ** -0.5
    ext_rows = jnp.concatenate([jnp.ones((1, T), f32), jnp.zeros((ML_EXT - D_HEAD - 1, T), f32)], axis=0)

    def tr_body(j, carry):
        r0 = pl.multiple_of(j * T, T)
        gt_s[:, pl.ds(r0, T)] = g_ref[0, pl.ds(r0, T), :].T
        vt = v_ref[0, pl.ds(r0, T), :].T
        vt_s[:, pl.ds(r0, T)] = jnp.concatenate([vt, ext_rows], axis=0).astype(bf16)
        return carry
    lax.fori_loop(0, nc, tr_body, 0, unroll=2)

    i_f = gt_s[pl.ds(h, 1), :] + bias_ref[h]
    f_f = _log_sigmoid(gt_s[pl.ds(4 + h, 1), :] + bias_ref[4 + h])
    i_b = gt_s[pl.ds(8 + h, 1), :] + bias_ref[8 + h]
    f_b = _log_sigmoid(gt_s[pl.ds(12 + h, 1), :] + bias_ref[12 + h])
    zrow = jnp.zeros_like(i_f)
    rw_s[...] = jnp.concatenate([f_f, i_f, f_b, i_b, zrow, zrow, zrow, zrow], axis=0)

    ui = lax.broadcasted_iota(jnp.int32, (T, T), 0)
    si = lax.broadcasted_iota(jnp.int32, (T, T), 1)
    tri = jnp.concatenate([(ui <= si).astype(bf16), (ui >= si).astype(bf16)], axis=1)
    valid_f = ui <= si
    valid_b = ui >= si

    def cs_body(j, carry):
        r0 = pl.multiple_of(j * T, T)
        rows = rw_s[:, pl.ds(r0, T)]
        hi, mid, lo = _split3(rows)
        cs3 = _dot(jnp.concatenate([hi, mid, lo, jnp.zeros_like(hi)], axis=0), tri)
        cs = cs3[0:8] + cs3[8:16] + cs3[16:24]
        b_f = cs[0:1, 0:T]
        b_b = cs[2:3, T:2 * T]
        rw_s[:, pl.ds(r0, T)] = jnp.concatenate([b_f, rows[1:2], b_b, rows[3:8]], axis=0)
        ucb_s[0, pl.ds(r0, T), :] = jnp.broadcast_to(rows[1:2] - b_f, (T, T)).T
        ucb_s[1, pl.ds(r0, T), :] = jnp.broadcast_to(rows[3:4] - b_b, (T, T)).T
        return carry
    lax.fori_loop(0, nc, cs_body, 0, unroll=min(4, nc))

    for d in range(2):
        ct_s[d] = jnp.concatenate([c0_ref[0, 0, d, 0].T, n0_ref[0, 0, d, 0],
                                   jnp.zeros((ML_EXT - D_HEAD - 1, D_HEAD), f32)], axis=0)
        m_s[d] = jnp.full((1, LANES), m0_ref[b_idx * 2 * N_HEADS + d * N_HEADS + h], f32)

    def step(d, j):
        r0 = pl.multiple_of(j * T, T)
        b_row = rw_s[pl.ds(2 * d, 1), pl.ds(r0, T)]
        b_end = b_row[:, T - 1:T] if d == 0 else b_row[:, 0:1]
        b_end_l = jnp.broadcast_to(b_end, (1, LANES))
        ucb = ucb_s[d, pl.ds(r0, T), :]

        q = q_ref[0, pl.ds(r0, T), :] * scale
        k = k_ref[0, pl.ds(r0, T), :]
        qb = q.astype(bf16)
        kb = k.astype(bf16)
        vt = vt_s[:, pl.ds(r0, T)]
        ct = ct_s[d]
        m_prev = m_s[d]

        d_t = jnp.where(valid_f if d == 0 else valid_b, ucb + b_row, NEG_BIG)
        m_state = b_row + m_prev
        m_t = jnp.maximum(m_state, jnp.max(d_t, axis=0, keepdims=True))
        w_state = jnp.exp(m_state - m_t)
        s_t = _dot_nt(kb, qb) * jnp.exp(d_t - m_t)
        tot = _dot(vt, s_t.astype(bf16)) + w_state * _dot_nt(ct.astype(bf16), qb)
        den = tot[D_HEAD:D_HEAD + 1]
        ht_s[d, :, pl.ds(r0, T)] = tot[:D_HEAD] * (1.0 / jnp.maximum(jnp.abs(den), jnp.exp(-m_t)))

        gcb = ucb + b_end_l
        m_new = jnp.maximum(b_end_l + m_prev, jnp.max(gcb, axis=0, keepdims=True))
        kw = k * jnp.exp(gcb - m_new)
        decay = jnp.exp(b_end_l + m_prev - m_new)
        ct_s[d] = decay * ct + _dot(vt, kw.astype(bf16))
        m_s[d] = m_new

    def loop_body(j, carry):
        step(0, j)
        step(1, nc - 1 - j)
        return carry
    lax.fori_loop(0, nc, loop_body, 0, unroll=min(4, nc))

    ng = ng_ref[...]

    def out_body(j, carry):
        r0 = pl.multiple_of(j * T, T)
        y_t = ht_s[0, :, pl.ds(r0, T)] + ht_s[1, :, pl.ds(r0, T)]
        ms = jnp.mean(y_t * y_t, axis=0, keepdims=True)
        y = (y_t * lax.rsqrt(ms + EPS)).T * ng * _sigmoid(mo_ref[0, pl.ds(r0, T), :])
        y_ref[0, pl.ds(r0, T), :] = y.astype(y_ref.dtype)
        return carry
    lax.fori_loop(0, nc, out_body, 0, unroll=2)

    for d in range(2):
        cout_ref[0, d, 0] = ct_s[d, 0:D_HEAD, :].T
        nout_ref[0, d, 0] = ct_s[d, D_HEAD:D_HEAD + 1, :]
        mout_ref[0, d, 0] = m_s[d]


def mlstm_call(proj, gate_bias, norm_g, c0, n0, m0, layer):
    B, L, _ = proj.shape

    def col(cb):
        return pl.BlockSpec((1, L, LANES), lambda b, h, cb=cb: (b, 0, cb + h))

    smem = pl.BlockSpec(memory_space=pltpu.SMEM)
    kern = functools.partial(_mlstm_kernel, seq_len=L)
    return pl.pallas_call(
        kern,
        grid=(B, N_HEADS),
        in_specs=[smem, smem,
                  col(CB_MQ), col(CB_MK), col(CB_MV), col(CB_MO),
                  pl.BlockSpec((1, L, LANES), lambda b, h: (b, 0, CB_MG)),
                  pl.BlockSpec((1, LANES), lambda b, h: (0, h)),
                  pl.BlockSpec((1, 1, 2, 1, D_HEAD, D_HEAD), lambda b, h: (b, layer, 0, h, 0, 0)),
                  pl.BlockSpec((1, 1, 2, 1, 1, D_HEAD), lambda b, h: (b, layer, 0, h, 0, 0))],
        out_specs=[pl.BlockSpec((1, L, LANES), lambda b, h: (b, 0, h)),
                   pl.BlockSpec((1, 2, 1, D_HEAD, D_HEAD), lambda b, h: (b, 0, h, 0, 0)),
                   pl.BlockSpec((1, 2, 1, 1, D_HEAD), lambda b, h: (b, 0, h, 0, 0)),
                   pl.BlockSpec((1, 2, 1, 1, LANES), lambda b, h: (b, 0, h, 0, 0))],
        out_shape=[jax.ShapeDtypeStruct((B, L, BRANCH_W), bf16),
                   jax.ShapeDtypeStruct((B, 2, N_HEADS, D_HEAD, D_HEAD), f32),
                   jax.ShapeDtypeStruct((B, 2, N_HEADS, 1, D_HEAD), f32),
                   jax.ShapeDtypeStruct((B, 2, N_HEADS, 1, LANES), f32)],
        scratch_shapes=[pltpu.VMEM((LANES, L), f32),
                        pltpu.VMEM((8, L), f32),
                        pltpu.VMEM((2, L, LANES), f32),
                        pltpu.VMEM((ML_EXT, L), bf16),
                        pltpu.VMEM((2, D_HEAD, L), f32),
                        pltpu.VMEM((2, ML_EXT, D_HEAD), f32),
                        pltpu.VMEM((2, 1, LANES), f32)],
        compiler_params=_cparams(("arbitrary", "arbitrary")),
        name="mlstm",
    )(gate_bias, m0, proj, proj, proj, proj, proj, norm_g, c0, n0)


def _hgrn_kernel(q_ref, f0_ref, f1_ref, i_ref, og_ref, lb_ref, ng_ref, s0_ref,
                 y_ref, sout_ref, o_s, st_s, qx_s, kx_s, dec_s, vt_s, k_s, b_s, *, seq_len):
    T = HG_CHUNK
    nc = seq_len // T
    h = pl.program_id(1)
    f_refs = (f0_ref, f1_ref)

    lbs, loglb, log1mlb = [], [], []
    for d in range(2):
        lb = lb_ref[pl.ds(d * N_HEADS + h, 1), :]
        lbs.append(lb)
        loglb.append(jnp.log(lb))
        log1mlb.append(jnp.log1p(-lb))
        st_s[d] = s0_ref[0, 0, d, 0].T

    ti = lax.broadcasted_iota(jnp.int32, (T, T), 0)
    ui = lax.broadcasted_iota(jnp.int32, (T, T), 1)
    tri = ((ui <= ti).astype(bf16), (ui >= ti).astype(bf16))
    sub_t = lax.broadcasted_iota(jnp.int32, (HG_SUB, D_HEAD), 0)

    def gates(d, j):
        r0 = pl.multiple_of(j * T, T)
        hf = f_refs[d][0, pl.ds(r0, T), :]
        e = jnp.exp(-jnp.abs(hf))
        sig_neg = jnp.where(hf >= 0.0, e, 1.0) / (1.0 + e)
        logsig = jnp.minimum(hf, 0.0) - jnp.log1p(e)
        cterm = log1mlb[d] + logsig
        amax = jnp.maximum(loglb[d], cterm)
        logf = amax + jnp.log1p(jnp.exp(-jnp.abs(loglb[d] - cterm)))
        k = (1.0 - lbs[d]) * sig_neg
        hi, mid, lo = _split3(logf)
        b = _dot(tri[d], hi) + _dot(tri[d], mid) + _dot(tri[d], lo)
        b_end = b[T - 1:T] if d == 0 else b[0:1]
        k_s[d, pl.ds(r0, T), :] = k
        b_s[d, pl.ds(r0, T), :] = b
        q = q_ref[0, pl.ds(r0, T), :]
        qx_s[d, pl.ds(r0, T), :] = (q * jnp.exp(b)).astype(bf16)
        kx_s[d, pl.ds(r0, T), :] = (k * jnp.exp(b_end - b)).astype(bf16)
        dec_s[d, pl.ds(j, 1), :] = jnp.exp(b_end)

    def gates_body(j, carry):
        r0 = pl.multiple_of(j * T, T)
        vt_s[j] = i_ref[0, pl.ds(r0, T), :].T.astype(bf16)
        gates(0, j)
        gates(1, j)
        return carry
    lax.fori_loop(0, nc, gates_body, 0, unroll=2)

    def intra(d, j):
        r0 = pl.multiple_of(j * T, T)

        def rows(ref3, lo_r, n):
            return ref3[0, pl.ds(r0 + lo_r, n), :]

        def srows(ref3, lo_r, n):
            return ref3[d, pl.ds(r0 + lo_r, n), :]

        nsub = T // HG_SUB
        o_blk = [None] * nsub

        def add(idx, val):
            o_blk[idx] = val if o_blk[idx] is None else o_blk[idx] + val

        def diag(lo_r):
            bs = srows(b_s, lo_r, HG_SUB)
            qs = rows(q_ref, lo_r, HG_SUB)
            acc = None
            for s in range(HG_SUB):
                valid = (sub_t >= s) if d == 0 else (sub_t <= s)
                ee = jnp.exp(jnp.where(valid, bs - srows(b_s, lo_r + s, 1), NEG_BIG))
                a_col = jnp.sum(qs * ee * srows(k_s, lo_r + s, 1), axis=-1, keepdims=True)
                term = a_col * rows(i_ref, lo_r + s, 1)
                acc = term if acc is None else acc + term
            add(lo_r // HG_SUB, acc)

        def block(lo_r, hi_r):
            if hi_r - lo_r == HG_SUB:
                diag(lo_r)
                return
            mid_r = (lo_r + hi_r) // 2
            n = mid_r - lo_r
            if d == 0:
                ref = srows(b_s, mid_r - 1, 1)
                q_lo, k_lo = mid_r, lo_r
            else:
                ref = srows(b_s, mid_r, 1)
                q_lo, k_lo = lo_r, mid_r
            qx = rows(q_ref, q_lo, n) * jnp.exp(srows(b_s, q_lo, n) - ref)
            kx = srows(k_s, k_lo, n) * jnp.exp(ref - srows(b_s, k_lo, n))
            a = _dot_nt(qx.astype(bf16), kx.astype(bf16))
            ov = _dot(a.astype(bf16), rows(i_ref, k_lo, n).astype(bf16))
            for p in range(n // HG_SUB):
                add(q_lo // HG_SUB + p, ov[p * HG_SUB:(p + 1) * HG_SUB])
            block(lo_r, mid_r)
            block(mid_r, hi_r)

        block(0, T)
        o_s[d, pl.ds(r0, T), :] = jnp.concatenate(o_blk, axis=0)

    def intra_body(j, carry):
        intra(0, j)
        intra(1, j)
        return carry
    lax.fori_loop(0, nc, intra_body, 0, unroll=4)

    def state_step(d, j):
        r0 = pl.multiple_of(j * T, T)
        st = st_s[d]
        o_s[d, pl.ds(r0, T), :] = o_s[d, pl.ds(r0, T), :] + _dot_nt(qx_s[d, pl.ds(r0, T), :], st.astype(bf16))
        st_s[d] = st * dec_s[d, pl.ds(j, 1), :] + _dot(vt_s[j], kx_s[d, pl.ds(r0, T), :])

    def state_body(j, carry):
        state_step(0, j)
        state_step(1, nc - 1 - j)
        return carry
    lax.fori_loop(0, nc, state_body, 0, unroll=4)

    ng = ng_ref[...]

    def out_body(j, carry):
        r0 = pl.multiple_of(j * T, T)
        y = o_s[0, pl.ds(r0, T), :] + o_s[1, pl.ds(r0, T), :]
        y = _rms_rows(y, ng) * _silu(og_ref[0, pl.ds(r0, T), :])
        y_ref[0, pl.ds(r0, T), :] = y.astype(y_ref.dtype)
        return carry
    lax.fori_loop(0, nc, out_body, 0)

    for d in range(2):
        sout_ref[0, d, 0] = st_s[d].T


def hgrn_call(proj, lb_rows, norm_g, s0, layer):
    B, L, _ = proj.shape

    def col(cb):
        return pl.BlockSpec((1, L, LANES), lambda b, h, cb=cb: (b, 0, cb + h))

    kern = functools.partial(_hgrn_kernel, seq_len=L)
    return pl.pallas_call(
        kern,
        grid=(B, N_HEADS),
        in_specs=[col(CB_HQ), col(CB_HF), col(CB_HF + N_HEADS), col(CB_HI), col(CB_HG),
                  pl.BlockSpec((2 * N_HEADS, D_HEAD), lambda b, h: (0, 0)),
                  pl.BlockSpec((1, LANES), lambda b, h: (0, h)),
                  pl.BlockSpec((1, 1, 2, 1, D_HEAD, D_HEAD), lambda b, h: (b, layer, 0, h, 0, 0))],
        out_specs=[pl.BlockSpec((1, L, LANES), lambda b, h: (b, 0, h)),
                   pl.BlockSpec((1, 2, 1, D_HEAD, D_HEAD), lambda b, h: (b, 0, h, 0, 0))],
        out_shape=[jax.ShapeDtypeStruct((B, L, BRANCH_W), bf16),
                   jax.ShapeDtypeStruct((B, 2, N_HEADS, D_HEAD, D_HEAD), f32)],
        scratch_shapes=[pltpu.VMEM((2, L, D_HEAD), f32),
                        pltpu.VMEM((2, D_HEAD, D_HEAD), f32),
                        pltpu.VMEM((2, L, D_HEAD), bf16),
                        pltpu.VMEM((2, L, D_HEAD), bf16),
                        pltpu.VMEM((2, L // HG_CHUNK, D_HEAD), f32),
                        pltpu.VMEM((L // HG_CHUNK, D_HEAD, HG_CHUNK), bf16),
                        pltpu.VMEM((2, L, D_HEAD), f32),
                        pltpu.VMEM((2, L, D_HEAD), f32)],
        compiler_params=_cparams(("arbitrary", "arbitrary")),
        name="hgrn",
    )(proj, proj, proj, proj, proj, lb_rows, norm_g, s0)


def _pair_swap(x):
    lane = lax.broadcasted_iota(jnp.int32, x.shape, 1)
    return jnp.where((lane % 2) == 0, pltpu.roll(x, LANES - 1, 1), pltpu.roll(x, 1, 1))


def _head_norm_pair(x, g_row):
    lane = lax.broadcasted_iota(jnp.int32, x.shape, 1)
    low = lane < AT_HD
    sq = x * x
    s_lo = jnp.sum(jnp.where(low, sq, 0.0), axis=-1, keepdims=True)
    s_hi = jnp.sum(jnp.where(low, 0.0, sq), axis=-1, keepdims=True)
    ms = jnp.where(low, s_lo, s_hi) * (1.0 / AT_HD)
    return x * lax.rsqrt(ms + EPS) * g_row


def _qkv_prep_kernel(aq_ref, ak_ref, av_ref, qg_ref, kg_ref, cos_ref, sin_ref,
                     q_ref, k_ref, v_ref, kn_ref, *, use_rope):
    qg = qg_ref[...]
    kg = kg_ref[...]
    if use_rope:
        cos = cos_ref[...]
        sin = sin_ref[...]
    for p in range(AT_HEADS // 2):
        x = _head_norm_pair(aq_ref[0, :, p * LANES:(p + 1) * LANES], qg)
        if use_rope:
            x = x * cos + _pair_swap(x) * sin
        x = (x * (AT_HD ** -0.5)).astype(bf16)
        q_ref[0, 2 * p] = x[:, :AT_HD]
        q_ref[0, 2 * p + 1] = x[:, AT_HD:]
    kn = _head_norm_pair(ak_ref[0], kg)
    kn_ref[0] = kn
    if use_rope:
        kn = kn * cos + _pair_swap(kn) * sin
    kb = kn.astype(bf16)
    vb = av_ref[0].astype(bf16)
    for g in range(AT_KV_HEADS):
        k_ref[0, g] = kb[:, g * AT_HD:(g + 1) * AT_HD]
        v_ref[0, g] = vb[:, g * AT_HD:(g + 1) * AT_HD]


def qkv_prep_call(proj, qg_row, kg_row, cos_t, sin_t, use_rope):
    B, L, _ = proj.shape
    tm = min(L, 512)
    kern = functools.partial(_qkv_prep_kernel, use_rope=use_rope)
    return pl.pallas_call(
        kern,
        grid=(B, L // tm),
        in_specs=[pl.BlockSpec((1, tm, 4 * LANES), lambda b, m: (b, m, CB_AQ // 4)),
                  pl.BlockSpec((1, tm, LANES), lambda b, m: (b, m, CB_AK)),
                  pl.BlockSpec((1, tm, LANES), lambda b, m: (b, m, CB_AV)),
                  pl.BlockSpec((1, LANES), lambda b, m: (0, 0)),
                  pl.BlockSpec((1, LANES), lambda b, m: (0, 0)),
                  pl.BlockSpec((tm, LANES), lambda b, m: (m, 0)),
                  pl.BlockSpec((tm, LANES), lambda b, m: (m, 0))],
        out_specs=[pl.BlockSpec((1, AT_HEADS, tm, AT_HD), lambda b, m: (b, 0, m, 0)),
                   pl.BlockSpec((1, AT_KV_HEADS, tm, AT_HD), lambda b, m: (b, 0, m, 0)),
                   pl.BlockSpec((1, AT_KV_HEADS, tm, AT_HD), lambda b, m: (b, 0, m, 0)),
                   pl.BlockSpec((1, tm, LANES), lambda b, m: (b, m, 0))],
        out_shape=[jax.ShapeDtypeStruct((B, AT_HEADS, L, AT_HD), bf16),
                   jax.ShapeDtypeStruct((B, AT_KV_HEADS, L, AT_HD), bf16),
                   jax.ShapeDtypeStruct((B, AT_KV_HEADS, L, AT_HD), bf16),
                   jax.ShapeDtypeStruct((B, L, LANES), f32)],
        compiler_params=_cparams(("arbitrary", "arbitrary")),
        name="qkv_prep",
    )(proj, proj, proj, qg_row, kg_row, cos_t, sin_t)


def _attn_kernel(q_ref, k_ref, v_ref, o_ref, s_s, m_s, *, tq, tk, nkv):
    rows = AT_GROUP * tq
    ncb = tk // LANES
    always = pl.program_id(2) >= 0

    @pl.when(always)
    def _():
        q = q_ref[0].reshape(rows, AT_HD)
        mx = None
        for j in range(nkv):
            s = _dot_nt(q, k_ref[0, 0, j * tk:(j + 1) * tk, :])
            s_s[:, j * tk:(j + 1) * tk] = s
            for cb in range(ncb):
                blk = s[:, cb * LANES:(cb + 1) * LANES]
                mx = blk if mx is None else jnp.maximum(mx, blk)
        m_s[...] = jnp.broadcast_to(jnp.max(mx, axis=-1, keepdims=True), (rows, LANES))

    @pl.when(pl.program_id(1) >= 0)
    def _():
        m_b = m_s[...]
        acc = None
        for j in range(nkv):
            ps = []
            for cb in range(ncb):
                off = j * tk + cb * LANES
                ps.append(jnp.exp(s_s[:, off:off + LANES] - m_b).astype(bf16))
            pv = _dot(jnp.concatenate(ps, axis=1), v_ref[0, 0, j * tk:(j + 1) * tk, :])
            acc = pv if acc is None else acc + pv
        o = acc[:, :AT_HD] * (1.0 / acc[:, AT_HD:AT_HD + 1])
        for g in range(AT_GROUP):
            o_ref[0, :, g * AT_HD:(g + 1) * AT_HD] = o[g * tq:(g + 1) * tq].astype(o_ref.dtype)


def attn_call(q, k, v):
    B, _, Lq, _ = q.shape
    Lk = k.shape[2]
    tq = 128
    tk = 512 if Lk % 512 == 0 else Lk
    nkv = Lk // tk
    nq = Lq // tq
    kern = functools.partial(_attn_kernel, tq=tq, tk=tk, nkv=nkv)
    rows = AT_GROUP * tq
    return pl.pallas_call(
        kern,
        grid=(B, AT_KV_HEADS, nq),
        in_specs=[pl.BlockSpec((1, AT_GROUP, tq, AT_HD), lambda b, g, i: (b, g, i, 0)),
                  pl.BlockSpec((1, 1, Lk, AT_HD), lambda b, g, i: (b, g, 0, 0)),
                  pl.BlockSpec((1, 1, Lk, LANES), lambda b, g, i: (b, g, 0, 0))],
        out_specs=pl.BlockSpec((1, tq, AT_GROUP * AT_HD), lambda b, g, i: (b, i, g)),
        out_shape=jax.ShapeDtypeStruct((B, Lq, AT_HEADS * AT_HD), bf16),
        scratch_shapes=[pltpu.VMEM((rows, Lk), f32),
                        pltpu.VMEM((rows, LANES), f32)],
        compiler_params=_cparams(("arbitrary", "arbitrary", "arbitrary")),
        name="attention",
    )(q, k, v)


def _with_ones_column(v):
    ones = jnp.ones(v.shape[:-1] + (1,), v.dtype)
    zeros = jnp.zeros(v.shape[:-1] + (LANES - AT_HD - 1,), v.dtype)
    return jnp.concatenate([v, ones, zeros], axis=-1)


def _merge_kernel(x_ref, yml_ref, yhg_ref, yat_ref, bg_ref, mod_ref, wb_ref, wo_ref, o_ref):
    merged = None
    for n, y_ref in enumerate((yml_ref, yhg_ref, yat_ref)):
        pn = _dot(y_ref[0], wb_ref[n])
        gn = _sigmoid(bg_ref[0, :, n * D_MODEL:(n + 1) * D_MODEL])
        merged = gn * pn if merged is None else merged + gn * pn
    out = _dot(merged.astype(bf16), wo_ref[...])
    o_ref[0] = x_ref[0] + mod_ref[0, 2:3, :] * out


def merge_call(x, y_ml, y_hg, y_at, proj, mod, wb, wo, shared_mod):
    B, L, _ = x.shape
    tm = min(L, 256)
    mod_map = (lambda b, m: (0, 0, 0)) if shared_mod else (lambda b, m: (b, 0, 0))
    yspec = pl.BlockSpec((1, tm, BRANCH_W), lambda b, m: (b, m, 0))
    return pl.pallas_call(
        _merge_kernel,
        grid=(B, L // tm),
        in_specs=[pl.BlockSpec((1, tm, D_MODEL), lambda b, m: (b, m, 0)),
                  yspec, yspec, yspec,
                  pl.BlockSpec((1, tm, 3 * D_MODEL), lambda b, m: (b, m, CB_BG)),
                  pl.BlockSpec((1, 8, D_MODEL), mod_map),
                  pl.BlockSpec((3, BRANCH_W, D_MODEL), lambda b, m: (0, 0, 0)),
                  pl.BlockSpec((D_MODEL, D_MODEL), lambda b, m: (0, 0))],
        out_specs=pl.BlockSpec((1, tm, D_MODEL), lambda b, m: (b, m, 0)),
        out_shape=jax.ShapeDtypeStruct((B, L, D_MODEL), f32),
        compiler_params=_cparams(("arbitrary", "arbitrary")),
        name="merge_out",
    )(x, y_ml, y_hg, y_at, proj, mod, wb, wo)


def _ffn_kernel(x_ref, mod_ref, g_ref, wg_ref, wu_ref, wd_ref, fg_ref, o_ref, h_s, acc_s, *, final_norm):
    j = pl.program_id(2)

    @pl.when(j == 0)
    def _():
        h = _rms_rows(x_ref[0], g_ref[...]) * (1.0 + mod_ref[0, 4:5, :]) + mod_ref[0, 3:4, :]
        h_s[...] = h.astype(bf16)

    hb = h_s[...]
    act = _silu(_dot(hb, wg_ref[...])) * _dot(hb, wu_ref[...])
    part = _dot(act.astype(bf16), wd_ref[...])

    @pl.when(j == 0)
    def _():
        acc_s[...] = part

    @pl.when(j != 0)
    def _():
        acc_s[...] = acc_s[...] + part

    @pl.when(j == pl.num_programs(2) - 1)
    def _():
        y = x_ref[0] + mod_ref[0, 5:6, :] * acc_s[...]
        if final_norm:
            y = _rms_rows(y, fg_ref[...])
        o_ref[0] = y


def ffn_call(x, mod, g_row, w_in, w_out, fg_row, shared_mod, final_norm):
    B, L, _ = x.shape
    tm = min(L, 512)
    th = FFN_HIDDEN // 2
    nh = FFN_HIDDEN // th
    mod_map = (lambda b, m, j: (0, 0, 0)) if shared_mod else (lambda b, m, j: (b, 0, 0))
    kern = functools.partial(_ffn_kernel, final_norm=final_norm)
    return pl.pallas_call(
        kern,
        grid=(B, L // tm, nh),
        in_specs=[pl.BlockSpec((1, tm, D_MODEL), lambda b, m, j: (b, m, 0)),
                  pl.BlockSpec((1, 8, D_MODEL), mod_map),
                  pl.BlockSpec((1, D_MODEL), lambda b, m, j: (0, 0)),
                  pl.BlockSpec((D_MODEL, th), lambda b, m, j: (0, j)),
                  pl.BlockSpec((D_MODEL, th), lambda b, m, j: (0, nh + j)),
                  pl.BlockSpec((th, D_MODEL), lambda b, m, j: (j, 0)),
                  pl.BlockSpec((1, D_MODEL), lambda b, m, j: (0, 0))],
        out_specs=pl.BlockSpec((1, tm, D_MODEL), lambda b, m, j: (b, m, 0)),
        out_shape=jax.ShapeDtypeStruct((B, L, D_MODEL), f32),
        scratch_shapes=[pltpu.VMEM((tm, D_MODEL), bf16),
                        pltpu.VMEM((tm, D_MODEL), f32)],
        compiler_params=_cparams(("arbitrary", "arbitrary", "arbitrary")),
        name="ffn",
    )(x, mod, g_row, w_in, w_in, w_out, fg_row)


def _reorder_w_in(w):
    o_mg = 2048
    o_hq = 2064
    o_bg = 5392
    pad = jnp.zeros((w.shape[0], N_COLS - 8464), w.dtype)
    return jnp.concatenate([w[:, o_bg:o_bg + 3 * D_MODEL], w[:, :o_mg], w[:, o_hq:o_bg],
                            w[:, o_mg:o_hq], pad], axis=1).astype(bf16)


def _rope_tables(L):
    rows = L // GRID_W
    row = jnp.repeat(jnp.arange(rows, dtype=f32), GRID_W)
    colp = jnp.tile(jnp.arange(GRID_W, dtype=f32), rows)
    n_freq = AT_HD // 4
    inv = ROPE_THETA ** (-jnp.arange(n_freq, dtype=f32) / n_freq)
    ang = jnp.concatenate([row[:, None] * inv, colp[:, None] * inv], axis=-1)
    cos = jnp.repeat(jnp.cos(ang), 2, axis=-1)
    sin = jnp.repeat(jnp.sin(ang), 2, axis=-1)
    sign = jnp.tile(jnp.array([-1.0, 1.0], f32), AT_HD // 2)
    sin = sin * sign
    return jnp.tile(cos, (1, 2)), jnp.tile(sin, (1, 2))


def _mod_rows(ada_rows):
    r = ada_rows.shape[0]
    m = ada_rows.reshape(r, 6, D_MODEL)
    return jnp.concatenate([m, jnp.zeros((r, 2, D_MODEL), f32)], axis=1)


def kernel(x_prompt, x_sample, c, cache_k, cache_v, state_ml_C, state_ml_n, state_ml_m, state_hg_S, c_ctx, w_ada, b_ada, norm1_g, norm2_g, w_in, ml_gate_b, ml_norm_g, hg_lb_logits, hg_norm_g, q_norm_g, k_norm_g, w_branch, w_out, w_ffn_in, w_ffn_out, final_g):
    Bp, Lp, _ = x_prompt.shape
    Bs, Ls, _ = x_sample.shape

    n_rows = 16
    cvec = jnp.concatenate([c_ctx[None, :], c, jnp.zeros((n_rows - 1 - Bs, D_MODEL), f32)], axis=0)
    ada = ada_call(cvec, w_ada, b_ada)
    lb_all = lb_call(hg_lb_logits)
    cos_t, sin_t = _rope_tables(Ls)

    zeros_c = jnp.zeros((Bp, 1, 2, N_HEADS, D_HEAD, D_HEAD), f32)
    zeros_n = jnp.zeros((Bp, 1, 2, N_HEADS, 1, D_HEAD), f32)
    zeros_m = jnp.zeros((Bp * 2 * N_HEADS,), f32)
    n_state = state_ml_n.reshape(Bs, DEPTH, 2, N_HEADS, 1, D_HEAD)

    xp, xs = x_prompt, x_sample
    nk, nv, nC, nn_, nm, nS = [], [], [], [], [], []
    for l in range(DEPTH):
        w_l = _reorder_w_in(w_in[l])
        wb_l = w_branch[l].astype(bf16)
        wo_l = w_out[l].astype(bf16)
        wfi_l = w_ffn_in[l].astype(bf16)
        wfo_l = w_ffn_out[l].astype(bf16)
        g1 = norm1_g[l][None, :]
        g2 = norm2_g[l][None, :]
        mlg = ml_norm_g[l][None, :]
        hgg = hg_norm_g[l][None, :]
        qg = jnp.tile(q_norm_g[l], 2)[None, :]
        kg = jnp.tile(k_norm_g[l], 2)[None, :]
        gate_b = ml_gate_b[l].reshape(-1).astype(f32)
        fg = final_g[None, :]
        last = l == DEPTH - 1

        mod_ctx = _mod_rows(ada[l, 0:1])
        mod_lat = _mod_rows(ada[l, 1:1 + Bs])

        proj = in_proj_call(xp, mod_ctx, g1, w_l, True)
        y_ml, c_f, n_f, m_f = mlstm_call(proj, gate_b, mlg, zeros_c, zeros_n, zeros_m, 0)
        y_hg, s_f = hgrn_call(proj, lb_all[l], hgg, zeros_c, 0)
        q_h, k_h, v_h, k_n = qkv_prep_call(proj, qg, kg, cos_t[:Lp], sin_t[:Lp], False)
        y_at = attn_call(q_h, k_h, _with_ones_column(v_h))
        xp = merge_call(xp, y_ml, y_hg, y_at, proj, mod_ctx, wb_l, wo_l, True)
        xp = ffn_call(xp, mod_ctx, g2, wfi_l, wfo_l, fg, True, last)
        nk.append(k_n.reshape(Bp, Lp, AT_KV_HEADS, AT_HD))
        nv.append(proj[:, :, CB_AV * LANES:(CB_AV + 1) * LANES].reshape(Bp, Lp, AT_KV_HEADS, AT_HD))
        nC.append(c_f)
        nn_.append(n_f[:, :, :, 0, :])
        nm.append(m_f[:, :, :, 0, 0])
        nS.append(s_f)

        proj = in_proj_call(xs, mod_lat, g1, w_l, False)
        y_ml, _, _, _ = mlstm_call(proj, gate_b, mlg, state_ml_C, n_state,
                                   state_ml_m[:, l].reshape(-1).astype(f32), l)
        y_hg, _ = hgrn_call(proj, lb_all[l], hgg, state_hg_S, l)
        q_h, k_h, v_h, _ = qkv_prep_call(proj, qg, kg, cos_t, sin_t, True)
        k_all = jnp.concatenate([k_h, jnp.swapaxes(cache_k[:, l], 1, 2).astype(bf16)], axis=2)
        v_all = jnp.concatenate([v_h, jnp.swapaxes(cache_v[:, l], 1, 2).astype(bf16)], axis=2)
        y_at = attn_call(q_h, k_all, _with_ones_column(v_all))
        xs = merge_call(xs, y_ml, y_hg, y_at, proj, mod_lat, wb_l, wo_l, False)
        xs = ffn_call(xs, mod_lat, g2, wfi_l, wfo_l, fg, False, last)

    return (xp, xs, jnp.stack(nk, axis=1), jnp.stack(nv, axis=1), jnp.stack(nC, axis=1),
            jnp.stack(nn_, axis=1), jnp.stack(nm, axis=1), jnp.stack(nS, axis=1))
```

```python
import functools
import math

import jax
import jax.numpy as jnp
from jax import lax
from jax.experimental import pallas as pl
from jax.experimental.pallas import tpu as pltpu

f32 = jnp.float32
bf16 = jnp.bfloat16

D_MODEL = 1024
DEPTH = 4
N_HEADS = 4
D_HEAD = 128
AT_HEADS = 8
AT_KV_HEADS = 2
AT_GROUP = AT_HEADS // AT_KV_HEADS
AT_HD = 64
BRANCH_W = 512
FFN_HIDDEN = 2816
GRID_W = 64
ROPE_THETA = 10000.0
EPS = 1e-6

LANES = 128
SUBLANES = 8
VMEM_LIMIT = 52 * 1024 * 1024

CB_BG = 0
CB_MQ = 24
CB_MK = 28
CB_MV = 32
CB_MO = 36
CB_HQ = 40
CB_HI = 44
CB_HG = 48
CB_AQ = 52
CB_AK = 56
CB_AV = 57
N16_BLOCKS = 60
N16_COLS = N16_BLOCKS * LANES
CF_HF = 0
CF_MG = 8
N32_BLOCKS = 10
N32_COLS = N32_BLOCKS * LANES
PROJ_TN = N32_COLS
N16_TILES = N16_COLS // PROJ_TN

ML_CHUNK = 128
ML_EXT = 144
HG_CHUNK = 64
HG_SUB = 8
NEG_BIG = -1e30


def _cparams(sem):
    return pltpu.CompilerParams(dimension_semantics=sem, vmem_limit_bytes=VMEM_LIMIT)


def _dot(a, b):
    return jnp.dot(a, b, preferred_element_type=f32)


def _dot_nt(a, b):
    return lax.dot_general(a, b, (((1,), (1,)), ((), ())), preferred_element_type=f32)


def _dot_tn(a, b):
    return lax.dot_general(a, b, (((0,), (0,)), ((), ())), preferred_element_type=f32)


def _sigmoid(x):
    return 1.0 / (1.0 + jnp.exp(-x))


def _silu(x):
    return x * _sigmoid(x)


def _log_sigmoid(x):
    return jnp.minimum(x, 0.0) - jnp.log1p(jnp.exp(-jnp.abs(x)))


def _split3(x):
    hi = x.astype(bf16)
    r1 = x - hi.astype(f32)
    mid = r1.astype(bf16)
    lo = (r1 - mid.astype(f32)).astype(bf16)
    return hi, mid, lo


def _rms_rows(x, g_row):
    ms = jnp.mean(x * x, axis=-1, keepdims=True)
    return x * lax.rsqrt(ms + EPS) * g_row


def _ada_kernel(c_ref, w_ref, b_ref, o_ref):
    s = _silu(c_ref[...]).astype(bf16)
    o_ref[0] = _dot(s, w_ref[0].astype(bf16)) + b_ref[0]


def ada_call(cvec, w_ada, b_ada):
    rows = cvec.shape[0]
    tn = 1536
    n6 = 6 * D_MODEL
    return pl.pallas_call(
        _ada_kernel,
        grid=(DEPTH, n6 // tn),
        in_specs=[pl.BlockSpec((rows, D_MODEL), lambda l, n: (0, 0)),
                  pl.BlockSpec((1, D_MODEL, tn), lambda l, n: (l, 0, n)),
                  pl.BlockSpec((1, 1, tn), lambda l, n: (l, 0, n))],
        out_specs=pl.BlockSpec((1, rows, tn), lambda l, n: (l, 0, n)),
        out_shape=jax.ShapeDtypeStruct((DEPTH, rows, n6), f32),
        compiler_params=_cparams(("arbitrary", "arbitrary")),
        name="ada",
    )(cvec, w_ada, b_ada.reshape(DEPTH, 1, n6))


def _lb_kernel(x_ref, o_ref):
    xs = [x_ref[l] for l in range(DEPTH)]
    mx = xs[0]
    for l in range(1, DEPTH):
        mx = jnp.maximum(mx, xs[l])
    es = [jnp.exp(x - mx) for x in xs]
    tot = es[0]
    for l in range(1, DEPTH):
        tot = tot + es[l]
    sm = [e / tot for e in es]
    run = sm[0]
    o_ref[0] = run - sm[0]
    for l in range(1, DEPTH):
        run = run + sm[l]
        o_ref[l] = run - sm[0]


def lb_call(logits):
    x = logits.astype(f32).reshape(DEPTH, 2 * N_HEADS, D_HEAD)
    return pl.pallas_call(
        _lb_kernel,
        out_shape=jax.ShapeDtypeStruct((DEPTH, 2 * N_HEADS, D_HEAD), f32),
        name="hg_lower_bounds",
    )(x)


def _in_proj_kernel(x_ref, mod_ref, g_ref, w_ref, o16_ref, o32_ref, h_s):
    n = pl.program_id(2)

    @pl.when(n == 0)
    def _():
        x = x_ref[0]
        sh = mod_ref[0, 0:1, :]
        sc = mod_ref[0, 1:2, :]
        h = _rms_rows(x, g_ref[...]) * (1.0 + sc) + sh
        h_s[...] = h.astype(bf16)

    @pl.when(n < N16_TILES)
    def _():
        o16_ref[0] = _dot(h_s[...], w_ref[...]).astype(bf16)

    @pl.when(n == N16_TILES)
    def _():
        o32_ref[0] = _dot(h_s[...], w_ref[...])


def in_proj_call(x, mod, g_row, w, shared_mod):
    B, L, _ = x.shape
    tm = min(L, 1024)
    tn = PROJ_TN
    mod_map = (lambda b, m, n: (0, 0, 0)) if shared_mod else (lambda b, m, n: (b, 0, 0))
    return pl.pallas_call(
        _in_proj_kernel,
        grid=(B, L // tm, N16_TILES + 1),
        in_specs=[pl.BlockSpec((1, tm, D_MODEL), lambda b, m, n: (b, m, 0)),
                  pl.BlockSpec((1, 8, D_MODEL), mod_map),
                  pl.BlockSpec((1, D_MODEL), lambda b, m, n: (0, 0)),
                  pl.BlockSpec((D_MODEL, tn), lambda b, m, n: (0, n))],
        out_specs=[pl.BlockSpec((1, tm, tn), lambda b, m, n: (b, m, jnp.minimum(n, N16_TILES - 1))),
                   pl.BlockSpec((1, tm, tn), lambda b, m, n: (b, m, 0))],
        out_shape=[jax.ShapeDtypeStruct((B, L, N16_COLS), bf16),
                   jax.ShapeDtypeStruct((B, L, N32_COLS), f32)],
        scratch_shapes=[pltpu.VMEM((tm, D_MODEL), bf16)],
        compiler_params=_cparams(("arbitrary", "arbitrary", "arbitrary")),
        name="in_proj",
    )(x, mod, g_row, w)


def _mlstm_kernel(bias_ref, m0_ref, q_ref, k_ref, v_ref, mo_ref, g_ref, ng_ref, c0_ref, n0_ref,
                  y_ref, cout_ref, nout_ref, mout_ref,
                  gt_s, rw_s, ucb_s, vt_s, ht_s, ct_s, m_s, *, seq_len):
    T = ML_CHUNK
    assert T == LANES
    nc = seq_len // T
    b_idx = pl.program_id(0)
    h = pl.program_id(1)
    scale = D_HEAD ** -0.5
    ext_rows = jnp.concatenate([jnp.ones((1, T), f32), jnp.zeros((ML_EXT - D_HEAD - 1, T), f32)], axis=0)

    def tr_body(j, carry):
        r0 = pl.multiple_of(j * T, T)
        gt_s[:, pl.ds(r0, T)] = g_ref[0, pl.ds(r0, T), :].T
        vt = v_ref[0, pl.ds(r0, T), :].astype(f32).T
        vt_s[:, pl.ds(r0, T)] = jnp.concatenate([vt, ext_rows], axis=0).astype(bf16)
        return carry
    lax.fori_loop(0, nc, tr_body, 0, unroll=2)

    i_f = gt_s[pl.ds(h, 1), :] + bias_ref[h]
    f_f = _log_sigmoid(gt_s[pl.ds(4 + h, 1), :] + bias_ref[4 + h])
    i_b = gt_s[pl.ds(8 + h, 1), :] + bias_ref[8 + h]
    f_b = _log_sigmoid(gt_s[pl.ds(12 + h, 1), :] + bias_ref[12 + h])
    zrow = jnp.zeros_like(i_f)
    rw_s[...] = jnp.concatenate([f_f, i_f, f_b, i_b, zrow, zrow, zrow, zrow], axis=0)

    ui = lax.broadcasted_iota(jnp.int32, (T, T), 0)
    si = lax.broadcasted_iota(jnp.int32, (T, T), 1)
    tri = jnp.concatenate([(ui <= si).astype(bf16), (ui >= si).astype(bf16)], axis=1)
    valid_f = ui <= si
    valid_b = ui >= si

    def cs_body(j, carry):
        r0 = pl.multiple_of(j * T, T)
        rows = rw_s[:, pl.ds(r0, T)]
        hi, mid, lo = _split3(rows)
        cs3 = _dot(jnp.concatenate([hi, mid, lo, jnp.zeros_like(hi)], axis=0), tri)
        cs = cs3[0:8] + cs3[8:16] + cs3[16:24]
        b_f = cs[0:1, 0:T]
        b_b = cs[2:3, T:2 * T]
        rw_s[:, pl.ds(r0, T)] = jnp.concatenate([b_f, rows[1:2], b_b, rows[3:8]], axis=0)
        ucb_s[0, pl.ds(r0, T), :] = jnp.broadcast_to(rows[1:2] - b_f, (T, T)).T
        ucb_s[1, pl.ds(r0, T), :] = jnp.broadcast_to(rows[3:4] - b_b, (T, T)).T
        return carry
    lax.fori_loop(0, nc, cs_body, 0, unroll=min(4, nc))

    for d in range(2):
        ct_s[d] = jnp.concatenate([c0_ref[0, 0, d, 0].T, n0_ref[0, 0, d, 0],
                                   jnp.zeros((ML_EXT - D_HEAD - 1, D_HEAD), f32)], axis=0)
        m_s[d] = jnp.full((1, LANES), m0_ref[b_idx * 2 * N_HEADS + d * N_HEADS + h], f32)

    def step(d, j):
        r0 = pl.multiple_of(j * T, T)
        b_row = rw_s[pl.ds(2 * d, 1), pl.ds(r0, T)]
        b_end = b_row[:, T - 1:T] if d == 0 else b_row[:, 0:1]
        b_end_l = jnp.broadcast_to(b_end, (1, LANES))
        ucb = ucb_s[d, pl.ds(r0, T), :]

        q = q_ref[0, pl.ds(r0, T), :].astype(f32) * scale
        kb = k_ref[0, pl.ds(r0, T), :]
        k = kb.astype(f32)
        qb = q.astype(bf16)
        vt = vt_s[:, pl.ds(r0, T)]
        ct = ct_s[d]
        m_prev = m_s[d]

        d_t = jnp.where(valid_f if d == 0 else valid_b, ucb + b_row, NEG_BIG)
        m_state = b_row + m_prev
        m_t = jnp.maximum(m_state, jnp.max(d_t, axis=0, keepdims=True))
        w_state = jnp.exp(m_state - m_t)
        s_t = _dot_nt(kb, qb) * jnp.exp(d_t - m_t)
        tot = _dot(vt, s_t.astype(bf16)) + w_state * _dot_nt(ct.astype(bf16), qb)
        den = tot[D_HEAD:D_HEAD + 1]
        ht_s[d, :, pl.ds(r0, T)] = tot[:D_HEAD] * (1.0 / jnp.maximum(jnp.abs(den), jnp.exp(-m_t)))

        gcb = ucb + b_end_l
        m_new = jnp.maximum(b_end_l + m_prev, jnp.max(gcb, axis=0, keepdims=True))
        kw = k * jnp.exp(gcb - m_new)
        decay = jnp.exp(b_end_l + m_prev - m_new)
        ct_s[d] = decay * ct + _dot(vt, kw.astype(bf16))
        m_s[d] = m_new

    def loop_body(j, carry):
        step(0, j)
        step(1, nc - 1 - j)
        return carry
    lax.fori_loop(0, nc, loop_body, 0, unroll=min(4, nc))

    ng = ng_ref[...]

    def out_body(j, carry):
        r0 = pl.multiple_of(j * T, T)
        y_t = ht_s[0, :, pl.ds(r0, T)] + ht_s[1, :, pl.ds(r0, T)]
        ms = jnp.mean(y_t * y_t, axis=0, keepdims=True)
        y = (y_t * lax.rsqrt(ms + EPS)).T * ng * _sigmoid(mo_ref[0, pl.ds(r0, T), :].astype(f32))
        y_ref[0, pl.ds(r0, T), :] = y.astype(y_ref.dtype)
        return carry
    lax.fori_loop(0, nc, out_body, 0, unroll=2)

    for d in range(2):
        cout_ref[0, d, 0] = ct_s[d, 0:D_HEAD, :].T
        nout_ref[0, d, 0] = ct_s[d, D_HEAD:D_HEAD + 1, :]
        mout_ref[0, d, 0] = m_s[d]


def mlstm_call(proj, proj32, gate_bias, norm_g, c0, n0, m0, layer):
    B, L, _ = proj.shape

    def col(cb):
        return pl.BlockSpec((1, L, LANES), lambda b, h, cb=cb: (b, 0, cb + h))

    smem = pl.BlockSpec(memory_space=pltpu.SMEM)
    kern = functools.partial(_mlstm_kernel, seq_len=L)
    return pl.pallas_call(
        kern,
        grid=(B, N_HEADS),
        in_specs=[smem, smem,
                  col(CB_MQ), col(CB_MK), col(CB_MV), col(CB_MO),
                  pl.BlockSpec((1, L, LANES), lambda b, h: (b, 0, CF_MG)),
                  pl.BlockSpec((1, LANES), lambda b, h: (0, h)),
                  pl.BlockSpec((1, 1, 2, 1, D_HEAD, D_HEAD), lambda b, h: (b, layer, 0, h, 0, 0)),
                  pl.BlockSpec((1, 1, 2, 1, 1, D_HEAD), lambda b, h: (b, layer, 0, h, 0, 0))],
        out_specs=[pl.BlockSpec((1, L, LANES), lambda b, h: (b, 0, h)),
                   pl.BlockSpec((1, 2, 1, D_HEAD, D_HEAD), lambda b, h: (b, 0, h, 0, 0)),
                   pl.BlockSpec((1, 2, 1, 1, D_HEAD), lambda b, h: (b, 0, h, 0, 0)),
                   pl.BlockSpec((1, 2, 1, 1, LANES), lambda b, h: (b, 0, h, 0, 0))],
        out_shape=[jax.ShapeDtypeStruct((B, L, BRANCH_W), bf16),
                   jax.ShapeDtypeStruct((B, 2, N_HEADS, D_HEAD, D_HEAD), f32),
                   jax.ShapeDtypeStruct((B, 2, N_HEADS, 1, D_HEAD), f32),
                   jax.ShapeDtypeStruct((B, 2, N_HEADS, 1, LANES), f32)],
        scratch_shapes=[pltpu.VMEM((LANES, L), f32),
                        pltpu.VMEM((8, L), f32),
                        pltpu.VMEM((2, L, LANES), f32),
                        pltpu.VMEM((ML_EXT, L), bf16),
                        pltpu.VMEM((2, D_HEAD, L), f32),
                        pltpu.VMEM((2, ML_EXT, D_HEAD), f32),
                        pltpu.VMEM((2, 1, LANES), f32)],
        compiler_params=_cparams(("arbitrary", "arbitrary")),
        name="mlstm",
    )(gate_bias, m0, proj, proj, proj, proj, proj32, norm_g, c0, n0)


def _hgrn_kernel(q_ref, f0_ref, f1_ref, i_ref, og_ref, lb_ref, ng_ref, s0_ref,
                 y_ref, sout_ref, o_s, st_s, qx_s, kx_s, dec_s, vt_s, k_s, b_s, q32_s, v32_s, *, seq_len):
    T = HG_CHUNK
    nc = seq_len // T
    h = pl.program_id(1)
    f_refs = (f0_ref, f1_ref)

    lbs, loglb, log1mlb = [], [], []
    for d in range(2):
        lb = lb_ref[pl.ds(d * N_HEADS + h, 1), :]
        lbs.append(lb)
        loglb.append(jnp.log(lb))
        log1mlb.append(jnp.log1p(-lb))
        st_s[d] = s0_ref[0, 0, d, 0].T

    ti = lax.broadcasted_iota(jnp.int32, (T, T), 0)
    ui = lax.broadcasted_iota(jnp.int32, (T, T), 1)
    tri = ((ui <= ti).astype(bf16), (ui >= ti).astype(bf16))
    sub_t = lax.broadcasted_iota(jnp.int32, (HG_SUB, D_HEAD), 0)

    def gates(d, j, q):
        r0 = pl.multiple_of(j * T, T)
        hf = f_refs[d][0, pl.ds(r0, T), :]
        e = jnp.exp(-jnp.abs(hf))
        sig_neg = jnp.where(hf >= 0.0, e, 1.0) / (1.0 + e)
        logsig = jnp.minimum(hf, 0.0) - jnp.log1p(e)
        cterm = log1mlb[d] + logsig
        amax = jnp.maximum(loglb[d], cterm)
        logf = amax + jnp.log1p(jnp.exp(-jnp.abs(loglb[d] - cterm)))
        k = (1.0 - lbs[d]) * sig_neg
        hi, mid, lo = _split3(logf)
        b = _dot(tri[d], hi) + _dot(tri[d], mid) + _dot(tri[d], lo)
        b_end = b[T - 1:T] if d == 0 else b[0:1]
        k_s[d, pl.ds(r0, T), :] = k
        b_s[d, pl.ds(r0, T), :] = b
        qx_s[d, pl.ds(r0, T), :] = (q * jnp.exp(b)).astype(bf16)
        kx_s[d, pl.ds(r0, T), :] = (k * jnp.exp(b_end - b)).astype(bf16)
        dec_s[d, pl.ds(j, 1), :] = jnp.exp(b_end)

    def gates_body(j, carry):
        r0 = pl.multiple_of(j * T, T)
        q = q_ref[0, pl.ds(r0, T), :].astype(f32)
        v = i_ref[0, pl.ds(r0, T), :].astype(f32)
        q32_s[pl.ds(r0, T), :] = q
        v32_s[pl.ds(r0, T), :] = v
        vt_s[j] = v.T.astype(bf16)
        gates(0, j, q)
        gates(1, j, q)
        return carry
    lax.fori_loop(0, nc, gates_body, 0, unroll=2)

    def intra(d, j):
        r0 = pl.multiple_of(j * T, T)

        def rows(ref2, lo_r, n):
            return ref2[pl.ds(r0 + lo_r, n), :]

        def srows(ref3, lo_r, n):
            return ref3[d, pl.ds(r0 + lo_r, n), :]

        nsub = T // HG_SUB
        o_blk = [None] * nsub

        def add(idx, val):
            o_blk[idx] = val if o_blk[idx] is None else o_blk[idx] + val

        def diag(lo_r):
            bs = srows(b_s, lo_r, HG_SUB)
            qs = rows(q32_s, lo_r, HG_SUB)
            acc = None
            for s in range(HG_SUB):
                valid = (sub_t >= s) if d == 0 else (sub_t <= s)
                ee = jnp.exp(jnp.where(valid, bs - srows(b_s, lo_r + s, 1), NEG_BIG))
                a_col = jnp.sum(qs * ee * srows(k_s, lo_r + s, 1), axis=-1, keepdims=True)
                term = a_col * rows(v32_s, lo_r + s, 1)
                acc = term if acc is None else acc + term
            add(lo_r // HG_SUB, acc)

        def block(lo_r, hi_r):
            if hi_r - lo_r == HG_SUB:
                diag(lo_r)
                return
            mid_r = (lo_r + hi_r) // 2
            n = mid_r - lo_r
            if d == 0:
                ref = srows(b_s, mid_r - 1, 1)
                q_lo, k_lo = mid_r, lo_r
            else:
                ref = srows(b_s, mid_r, 1)
                q_lo, k_lo = lo_r, mid_r
            qx = rows(q32_s, q_lo, n) * jnp.exp(srows(b_s, q_lo, n) - ref)
            kx = srows(k_s, k_lo, n) * jnp.exp(ref - srows(b_s, k_lo, n))
            a = _dot_nt(qx.astype(bf16), kx.astype(bf16))
            ov = _dot(a.astype(bf16), rows(v32_s, k_lo, n).astype(bf16))
            for p in range(n // HG_SUB):
                add(q_lo // HG_SUB + p, ov[p * HG_SUB:(p + 1) * HG_SUB])
            block(lo_r, mid_r)
            block(mid_r, hi_r)

        block(0, T)
        o_s[d, pl.ds(r0, T), :] = jnp.concatenate(o_blk, axis=0)

    def intra_body(j, carry):
        intra(0, j)
        intra(1, j)
        return carry
    lax.fori_loop(0, nc, intra_body, 0, unroll=4)

    def state_step(d, j):
        r0 = pl.multiple_of(j * T, T)
        st = st_s[d]
        o_s[d, pl.ds(r0, T), :] = o_s[d, pl.ds(r0, T), :] + _dot_nt(qx_s[d, pl.ds(r0, T), :], st.astype(bf16))
        st_s[d] = st * dec_s[d, pl.ds(j, 1), :] + _dot(vt_s[j], kx_s[d, pl.ds(r0, T), :])

    def state_body(j, carry):
        state_step(0, j)
        state_step(1, nc - 1 - j)
        return carry
    lax.fori_loop(0, nc, state_body, 0, unroll=4)

    ng = ng_ref[...]

    def out_body(j, carry):
        r0 = pl.multiple_of(j * T, T)
        y = o_s[0, pl.ds(r0, T), :] + o_s[1, pl.ds(r0, T), :]
        y = _rms_rows(y, ng) * _silu(og_ref[0, pl.ds(r0, T), :].astype(f32))
        y_ref[0, pl.ds(r0, T), :] = y.astype(y_ref.dtype)
        return carry
    lax.fori_loop(0, nc, out_body, 0, unroll=4)

    for d in range(2):
        sout_ref[0, d, 0] = st_s[d].T


def hgrn_call(proj, proj32, lb_rows, norm_g, s0, layer):
    B, L, _ = proj.shape

    def col(cb):
        return pl.BlockSpec((1, L, LANES), lambda b, h, cb=cb: (b, 0, cb + h))

    kern = functools.partial(_hgrn_kernel, seq_len=L)
    return pl.pallas_call(
        kern,
        grid=(B, N_HEADS),
        in_specs=[col(CB_HQ), col(CF_HF), col(CF_HF + N_HEADS), col(CB_HI), col(CB_HG),
                  pl.BlockSpec((2 * N_HEADS, D_HEAD), lambda b, h: (0, 0)),
                  pl.BlockSpec((1, LANES), lambda b, h: (0, h)),
                  pl.BlockSpec((1, 1, 2, 1, D_HEAD, D_HEAD), lambda b, h: (b, layer, 0, h, 0, 0))],
        out_specs=[pl.BlockSpec((1, L, LANES), lambda b, h: (b, 0, h)),
                   pl.BlockSpec((1, 2, 1, D_HEAD, D_HEAD), lambda b, h: (b, 0, h, 0, 0))],
        out_shape=[jax.ShapeDtypeStruct((B, L, BRANCH_W), bf16),
                   jax.ShapeDtypeStruct((B, 2, N_HEADS, D_HEAD, D_HEAD), f32)],
        scratch_shapes=[pltpu.VMEM((2, L, D_HEAD), f32),
                        pltpu.VMEM((2, D_HEAD, D_HEAD), f32),
                        pltpu.VMEM((2, L, D_HEAD), bf16),
                        pltpu.VMEM((2, L, D_HEAD), bf16),
                        pltpu.VMEM((2, L // HG_CHUNK, D_HEAD), f32),
                        pltpu.VMEM((L // HG_CHUNK, D_HEAD, HG_CHUNK), bf16),
                        pltpu.VMEM((2, L, D_HEAD), f32),
                        pltpu.VMEM((2, L, D_HEAD), f32),
                        pltpu.VMEM((L, D_HEAD), f32),
                        pltpu.VMEM((L, D_HEAD), f32)],
        compiler_params=_cparams(("arbitrary", "arbitrary")),
        name="hgrn",
    )(proj, proj32, proj32, proj, proj, lb_rows, norm_g, s0)


def _pair_swap(x):
    lane = lax.broadcasted_iota(jnp.int32, x.shape, 1)
    return jnp.where((lane % 2) == 0, pltpu.roll(x, LANES - 1, 1), pltpu.roll(x, 1, 1))


def _head_norm_pair(x, g_row):
    lane = lax.broadcasted_iota(jnp.int32, x.shape, 1)
    low = lane < AT_HD
    sq = x * x
    s_lo = jnp.sum(jnp.where(low, sq, 0.0), axis=-1, keepdims=True)
    s_hi = jnp.sum(jnp.where(low, 0.0, sq), axis=-1, keepdims=True)
    ms = jnp.where(low, s_lo, s_hi) * (1.0 / AT_HD)
    return x * lax.rsqrt(ms + EPS) * g_row


def _qkv_prep_kernel(aq_ref, ak_ref, av_ref, qg_ref, kg_ref, cos_ref, sin_ref,
                     q_ref, k_ref, v_ref, kn_ref, *, use_rope):
    qg = qg_ref[...]
    kg = kg_ref[...]
    if use_rope:
        cos = cos_ref[...]
        sin = sin_ref[...]
    for p in range(AT_HEADS // 2):
        x = _head_norm_pair(aq_ref[0, :, p * LANES:(p + 1) * LANES].astype(f32), qg)
        if use_rope:
            x = x * cos + _pair_swap(x) * sin
        x = (x * (AT_HD ** -0.5)).astype(bf16)
        q_ref[0, 2 * p] = x[:, :AT_HD]
        q_ref[0, 2 * p + 1] = x[:, AT_HD:]
    kn = _head_norm_pair(ak_ref[0].astype(f32), kg)
    kn_ref[0] = kn
    if use_rope:
        kn = kn * cos + _pair_swap(kn) * sin
    kb = kn.astype(bf16)
    vb = av_ref[0]
    for g in range(AT_KV_HEADS):
        k_ref[0, g] = kb[:, g * AT_HD:(g + 1) * AT_HD]
        v_ref[0, g] = vb[:, g * AT_HD:(g + 1) * AT_HD]


def qkv_prep_call(proj, qg_row, kg_row, cos_t, sin_t, use_rope):
    B, L, _ = proj.shape
    tm = min(L, 512)
    kern = functools.partial(_qkv_prep_kernel, use_rope=use_rope)
    return pl.pallas_call(
        kern,
        grid=(B, L // tm),
        in_specs=[pl.BlockSpec((1, tm, 4 * LANES), lambda b, m: (b, m, CB_AQ // 4)),
                  pl.BlockSpec((1, tm, LANES), lambda b, m: (b, m, CB_AK)),
                  pl.BlockSpec((1, tm, LANES), lambda b, m: (b, m, CB_AV)),
                  pl.BlockSpec((1, LANES), lambda b, m: (0, 0)),
                  pl.BlockSpec((1, LANES), lambda b, m: (0, 0)),
                  pl.BlockSpec((tm, LANES), lambda b, m: (m, 0)),
                  pl.BlockSpec((tm, LANES), lambda b, m: (m, 0))],
        out_specs=[pl.BlockSpec((1, AT_HEADS, tm, AT_HD), lambda b, m: (b, 0, m, 0)),
                   pl.BlockSpec((1, AT_KV_HEADS, tm, AT_HD), lambda b, m: (b, 0, m, 0)),
                   pl.BlockSpec((1, AT_KV_HEADS, tm, AT_HD), lambda b, m: (b, 0, m, 0)),
                   pl.BlockSpec((1, tm, LANES), lambda b, m: (b, m, 0))],
        out_shape=[jax.ShapeDtypeStruct((B, AT_HEADS, L, AT_HD), bf16),
                   jax.ShapeDtypeStruct((B, AT_KV_HEADS, L, AT_HD), bf16),
                   jax.ShapeDtypeStruct((B, AT_KV_HEADS, L, AT_HD), bf16),
                   jax.ShapeDtypeStruct((B, L, LANES), f32)],
        compiler_params=_cparams(("arbitrary", "arbitrary")),
        name="qkv_prep",
    )(proj, proj, proj, qg_row, kg_row, cos_t, sin_t)


def _attn_kernel(q_ref, k_ref, v_ref, o_ref, s_s, m_s, *, tq, tk, nkv):
    rows = AT_GROUP * tq
    ncb = tk // LANES
    always = pl.program_id(2) >= 0

    @pl.when(always)
    def _():
        q = q_ref[0].reshape(rows, AT_HD)
        mx = None
        for j in range(nkv):
            s = _dot_nt(q, k_ref[0, 0, j * tk:(j + 1) * tk, :])
            s_s[:, j * tk:(j + 1) * tk] = s
            for cb in range(ncb):
                blk = s[:, cb * LANES:(cb + 1) * LANES]
                mx = blk if mx is None else jnp.maximum(mx, blk)
        m_s[...] = jnp.broadcast_to(jnp.max(mx, axis=-1, keepdims=True), (rows, LANES))

    @pl.when(pl.program_id(1) >= 0)
    def _():
        m_b = m_s[...]
        acc = None
        for j in range(nkv):
            ps = []
            for cb in range(ncb):
                off = j * tk + cb * LANES
                ps.append(jnp.exp(s_s[:, off:off + LANES] - m_b).astype(bf16))
            pv = _dot(jnp.concatenate(ps, axis=1), v_ref[0, 0, j * tk:(j + 1) * tk, :])
            acc = pv if acc is None else acc + pv
        o = acc[:, :AT_HD] * (1.0 / acc[:, AT_HD:AT_HD + 1])
        for g in range(AT_GROUP):
            o_ref[0, :, g * AT_HD:(g + 1) * AT_HD] = o[g * tq:(g + 1) * tq].astype(o_ref.dtype)


def attn_call(q, k, v):
    B, _, Lq, _ = q.shape
    Lk = k.shape[2]
    tq = 256
    tk = 512 if Lk % 512 == 0 else Lk
    nkv = Lk // tk
    nq = Lq // tq
    kern = functools.partial(_attn_kernel, tq=tq, tk=tk, nkv=nkv)
    rows = AT_GROUP * tq
    return pl.pallas_call(
        kern,
        grid=(B, AT_KV_HEADS, nq),
        in_specs=[pl.BlockSpec((1, AT_GROUP, tq, AT_HD), lambda b, g, i: (b, g, i, 0)),
                  pl.BlockSpec((1, 1, Lk, AT_HD), lambda b, g, i: (b, g, 0, 0)),
                  pl.BlockSpec((1, 1, Lk, LANES), lambda b, g, i: (b, g, 0, 0))],
        out_specs=pl.BlockSpec((1, tq, AT_GROUP * AT_HD), lambda b, g, i: (b, i, g)),
        out_shape=jax.ShapeDtypeStruct((B, Lq, AT_HEADS * AT_HD), bf16),
        scratch_shapes=[pltpu.VMEM((rows, Lk), f32),
                        pltpu.VMEM((rows, LANES), f32)],
        compiler_params=_cparams(("arbitrary", "arbitrary", "arbitrary")),
        name="attention",
    )(q, k, v)


def _with_ones_column(v):
    ones = jnp.ones(v.shape[:-1] + (1,), v.dtype)
    zeros = jnp.zeros(v.shape[:-1] + (LANES - AT_HD - 1,), v.dtype)
    return jnp.concatenate([v, ones, zeros], axis=-1)


def _merge_kernel(x_ref, yml_ref, yhg_ref, yat_ref, bg_ref, mod_ref, wb_ref, wo_ref, o_ref):
    merged = None
    for n, y_ref in enumerate((yml_ref, yhg_ref, yat_ref)):
        pn = _dot(y_ref[0], wb_ref[n])
        gn = _sigmoid(bg_ref[0, :, n * D_MODEL:(n + 1) * D_MODEL].astype(f32))
        merged = gn * pn if merged is None else merged + gn * pn
    out = _dot(merged.astype(bf16), wo_ref[...])
    o_ref[0] = x_ref[0] + mod_ref[0, 2:3, :] * out


def merge_call(x, y_ml, y_hg, y_at, proj, mod, wb, wo, shared_mod):
    B, L, _ = x.shape
    tm = min(L, 256)
    mod_map = (lambda b, m: (0, 0, 0)) if shared_mod else (lambda b, m: (b, 0, 0))
    yspec = pl.BlockSpec((1, tm, BRANCH_W), lambda b, m: (b, m, 0))
    return pl.pallas_call(
        _merge_kernel,
        grid=(B, L // tm),
        in_specs=[pl.BlockSpec((1, tm, D_MODEL), lambda b, m: (b, m, 0)),
                  yspec, yspec, yspec,
                  pl.BlockSpec((1, tm, 3 * D_MODEL), lambda b, m: (b, m, CB_BG)),
                  pl.BlockSpec((1, 8, D_MODEL), mod_map),
                  pl.BlockSpec((3, BRANCH_W, D_MODEL), lambda b, m: (0, 0, 0)),
                  pl.BlockSpec((D_MODEL, D_MODEL), lambda b, m: (0, 0))],
        out_specs=pl.BlockSpec((1, tm, D_MODEL), lambda b, m: (b, m, 0)),
        out_shape=jax.ShapeDtypeStruct((B, L, D_MODEL), f32),
        compiler_params=_cparams(("arbitrary", "arbitrary")),
        name="merge_out",
    )(x, y_ml, y_hg, y_at, proj, mod, wb, wo)


def _ffn_kernel(x_ref, mod_ref, g_ref, wg_ref, wu_ref, wd_ref, fg_ref, o_ref, h_s, acc_s, *, final_norm):
    j = pl.program_id(2)

    @pl.when(j == 0)
    def _():
        h = _rms_rows(x_ref[0], g_ref[...]) * (1.0 + mod_ref[0, 4:5, :]) + mod_ref[0, 3:4, :]
        h_s[...] = h.astype(bf16)

    hb = h_s[...]
    act = _silu(_dot(hb, wg_ref[...])) * _dot(hb, wu_ref[...])
    part = _dot(act.astype(bf16), wd_ref[...])

    @pl.when(j == 0)
    def _():
        acc_s[...] = part

    @pl.when(j != 0)
    def _():
        acc_s[...] = acc_s[...] + part

    @pl.when(j == pl.num_programs(2) - 1)
    def _():
        y = x_ref[0] + mod_ref[0, 5:6, :] * acc_s[...]
        if final_norm:
            y = _rms_rows(y, fg_ref[...])
        o_ref[0] = y


def ffn_call(x, mod, g_row, w_in, w_out, fg_row, shared_mod, final_norm):
    B, L, _ = x.shape
    tm = min(L, 512)
    th = FFN_HIDDEN // 2
    nh = FFN_HIDDEN // th
    mod_map = (lambda b, m, j: (0, 0, 0)) if shared_mod else (lambda b, m, j: (b, 0, 0))
    kern = functools.partial(_ffn_kernel, final_norm=final_norm)
    return pl.pallas_call(
        kern,
        grid=(B, L // tm, nh),
        in_specs=[pl.BlockSpec((1, tm, D_MODEL), lambda b, m, j: (b, m, 0)),
                  pl.BlockSpec((1, 8, D_MODEL), mod_map),
                  pl.BlockSpec((1, D_MODEL), lambda b, m, j: (0, 0)),
                  pl.BlockSpec((D_MODEL, th), lambda b, m, j: (0, j)),
                  pl.BlockSpec((D_MODEL, th), lambda b, m, j: (0, nh + j)),
                  pl.BlockSpec((th, D_MODEL), lambda b, m, j: (j, 0)),
                  pl.BlockSpec((1, D_MODEL), lambda b, m, j: (0, 0))],
        out_specs=pl.BlockSpec((1, tm, D_MODEL), lambda b, m, j: (b, m, 0)),
        out_shape=jax.ShapeDtypeStruct((B, L, D_MODEL), f32),
        scratch_shapes=[pltpu.VMEM((tm, D_MODEL), bf16),
                        pltpu.VMEM((tm, D_MODEL), f32)],
        compiler_params=_cparams(("arbitrary", "arbitrary", "arbitrary")),
        name="ffn",
    )(x, mod, g_row, w_in, w_in, w_out, fg_row)


def _reorder_w_in(w):
    o_mg, o_hq, o_hf, o_hi, o_bg, o_end = 2048, 2064, 2576, 3600, 5392, 8464
    d = w.shape[0]
    sec16 = [w[:, o_bg:o_end], w[:, :o_mg], w[:, o_hq:o_hf], w[:, o_hi:o_bg]]
    n16 = (o_end - o_bg) + o_mg + (o_hf - o_hq) + (o_bg - o_hi)
    sec32 = [w[:, o_hf:o_hi], w[:, o_mg:o_hq]]
    n32 = (o_hi - o_hf) + (o_hq - o_mg)
    return jnp.concatenate(sec16 + [jnp.zeros((d, N16_COLS - n16), w.dtype)] + sec32
                           + [jnp.zeros((d, N32_COLS - n32), w.dtype)], axis=1).astype(bf16)


def _rope_tables(L):
    rows = L // GRID_W
    row = jnp.repeat(jnp.arange(rows, dtype=f32), GRID_W)
    colp = jnp.tile(jnp.arange(GRID_W, dtype=f32), rows)
    n_freq = AT_HD // 4
    inv = ROPE_THETA ** (-jnp.arange(n_freq, dtype=f32) / n_freq)
    ang = jnp.concatenate([row[:, None] * inv, colp[:, None] * inv], axis=-1)
    cos = jnp.repeat(jnp.cos(ang), 2, axis=-1)
    sin = jnp.repeat(jnp.sin(ang), 2, axis=-1)
    sign = jnp.tile(jnp.array([-1.0, 1.0], f32), AT_HD // 2)
    sin = sin * sign
    return jnp.tile(cos, (1, 2)), jnp.tile(sin, (1, 2))


def _mod_rows(ada_rows):
    r = ada_rows.shape[0]
    m = ada_rows.reshape(r, 6, D_MODEL)
    return jnp.concatenate([m, jnp.zeros((r, 2, D_MODEL), f32)], axis=1)


def kernel(x_prompt, x_sample, c, cache_k, cache_v, state_ml_C, state_ml_n, state_ml_m, state_hg_S, c_ctx, w_ada, b_ada, norm1_g, norm2_g, w_in, ml_gate_b, ml_norm_g, hg_lb_logits, hg_norm_g, q_norm_g, k_norm_g, w_branch, w_out, w_ffn_in, w_ffn_out, final_g):
    Bp, Lp, _ = x_prompt.shape
    Bs, Ls, _ = x_sample.shape

    n_rows = 16
    cvec = jnp.concatenate([c_ctx[None, :], c, jnp.zeros((n_rows - 1 - Bs, D_MODEL), f32)], axis=0)
    ada = ada_call(cvec, w_ada, b_ada)
    lb_all = lb_call(hg_lb_logits)
    cos_t, sin_t = _rope_tables(Ls)

    zeros_c = jnp.zeros((Bp, 1, 2, N_HEADS, D_HEAD, D_HEAD), f32)
    zeros_n = jnp.zeros((Bp, 1, 2, N_HEADS, 1, D_HEAD), f32)
    zeros_m = jnp.zeros((Bp * 2 * N_HEADS,), f32)
    n_state = state_ml_n.reshape(Bs, DEPTH, 2, N_HEADS, 1, D_HEAD)

    xp, xs = x_prompt, x_sample
    nk, nv, nC, nn_, nm, nS = [], [], [], [], [], []
    for l in range(DEPTH):
        w_l = _reorder_w_in(w_in[l])
        wb_l = w_branch[l].astype(bf16)
        wo_l = w_out[l].astype(bf16)
        wfi_l = w_ffn_in[l].astype(bf16)
        wfo_l = w_ffn_out[l].astype(bf16)
        g1 = norm1_g[l][None, :]
        g2 = norm2_g[l][None, :]
        mlg = ml_norm_g[l][None, :]
        hgg = hg_norm_g[l][None, :]
        qg = jnp.tile(q_norm_g[l], 2)[None, :]
        kg = jnp.tile(k_norm_g[l], 2)[None, :]
        gate_b = ml_gate_b[l].reshape(-1).astype(f32)
        fg = final_g[None, :]
        last = l == DEPTH - 1

        mod_ctx = _mod_rows(ada[l, 0:1])
        mod_lat = _mod_rows(ada[l, 1:1 + Bs])

        proj, proj32 = in_proj_call(xp, mod_ctx, g1, w_l, True)
        y_ml, c_f, n_f, m_f = mlstm_call(proj, proj32, gate_b, mlg, zeros_c, zeros_n, zeros_m, 0)
        y_hg, s_f = hgrn_call(proj, proj32, lb_all[l], hgg, zeros_c, 0)
        q_h, k_h, v_h, k_n = qkv_prep_call(proj, qg, kg, cos_t[:Lp], sin_t[:Lp], False)
        y_at = attn_call(q_h, k_h, _with_ones_column(v_h))
        xp = merge_call(xp, y_ml, y_hg, y_at, proj, mod_ctx, wb_l, wo_l, True)
        xp = ffn_call(xp, mod_ctx, g2, wfi_l, wfo_l, fg, True, last)
        nk.append(k_n.reshape(Bp, Lp, AT_KV_HEADS, AT_HD))
        nv.append(proj[:, :, CB_AV * LANES:(CB_AV + 1) * LANES].astype(f32).reshape(Bp, Lp, AT_KV_HEADS, AT_HD))
        nC.append(c_f)
        nn_.append(n_f[:, :, :, 0, :])
        nm.append(m_f[:, :, :, 0, 0])
        nS.append(s_f)

        proj, proj32 = in_proj_call(xs, mod_lat, g1, w_l, False)
        y_ml, _, _, _ = mlstm_call(proj, proj32, gate_b, mlg, state_ml_C, n_state,
                                   state_ml_m[:, l].reshape(-1).astype(f32), l)
        y_hg, _ = hgrn_call(proj, proj32, lb_all[l], hgg, state_hg_S, l)
        q_h, k_h, v_h, _ = qkv_prep_call(proj, qg, kg, cos_t, sin_t, True)
        k_all = jnp.concatenate([k_h, jnp.swapaxes(cache_k[:, l], 1, 2).astype(bf16)], axis=2)
        v_all = jnp.concatenate([v_h, jnp.swapaxes(cache_v[:, l], 1, 2).astype(bf16)], axis=2)
        y_at = attn_call(q_h, k_all, _with_ones_column(v_all))
        xs = merge_call(xs, y_ml, y_hg, y_at, proj, mod_lat, wb_l, wo_l, False)
        xs = ffn_call(xs, mod_lat, g2, wfi_l, wfo_l, fg, False, last)

    return (xp, xs, jnp.stack(nk, axis=1), jnp.stack(nv, axis=1), jnp.stack(nC, axis=1),
            jnp.stack(nn_, axis=1), jnp.stack(nm, axis=1), jnp.stack(nS, axis=1))
```

```python
import functools
import math

import jax
import jax.numpy as jnp
from jax import lax
from jax.experimental import pallas as pl
from jax.experimental.pallas import tpu as pltpu

f32 = jnp.float32
bf16 = jnp.bfloat16

D_MODEL = 1024
DEPTH = 4
N_HEADS = 4
D_HEAD = 128
AT_HEADS = 8
AT_KV_HEADS = 2
AT_GROUP = AT_HEADS // AT_KV_HEADS
AT_HD = 64
BRANCH_W = 512
FFN_HIDDEN = 2816
GRID_W = 64
ROPE_THETA = 10000.0
EPS = 1e-6

LANES = 128
SUBLANES = 8
VMEM_LIMIT = 52 * 1024 * 1024

CB_BG = 0
CB_MQ = 24
CB_MK = 28
CB_MV = 32
CB_MO = 36
CB_HQ = 40
CB_HI = 44
CB_HG = 48
CB_AQ = 52
CB_AK = 56
CB_AV = 57
N16_BLOCKS = 60
N16_COLS = N16_BLOCKS * LANES
CF_HF = 0
CF_MG = 8
N32_BLOCKS = 10
N32_COLS = N32_BLOCKS * LANES
PROJ_TN = N32_COLS
N16_TILES = N16_COLS // PROJ_TN

ML_CHUNK = 128
ML_EXT = 144
HG_CHUNK = 64
HG_SUB = 8
NEG_BIG = -1e30
LOG2E = 1.4426950408889634


def _cparams(sem):
    return pltpu.CompilerParams(dimension_semantics=sem, vmem_limit_bytes=VMEM_LIMIT)


def _dot(a, b):
    return jnp.dot(a, b, preferred_element_type=f32)


def _dot_nt(a, b):
    return lax.dot_general(a, b, (((1,), (1,)), ((), ())), preferred_element_type=f32)


def _dot_tn(a, b):
    return lax.dot_general(a, b, (((0,), (0,)), ((), ())), preferred_element_type=f32)


def _sigmoid(x):
    return 1.0 / (1.0 + jnp.exp(-x))


def _silu(x):
    return x * _sigmoid(x)


def _log_sigmoid(x):
    return jnp.minimum(x, 0.0) - jnp.log1p(jnp.exp(-jnp.abs(x)))


def _split3(x):
    hi = x.astype(bf16)
    r1 = x - hi.astype(f32)
    mid = r1.astype(bf16)
    lo = (r1 - mid.astype(f32)).astype(bf16)
    return hi, mid, lo


def _rms_rows(x, g_row):
    ms = jnp.mean(x * x, axis=-1, keepdims=True)
    return x * lax.rsqrt(ms + EPS) * g_row


def _ada_kernel(c_ref, w_ref, b_ref, o_ref):
    s = _silu(c_ref[...]).astype(bf16)
    o_ref[0] = _dot(s, w_ref[0].astype(bf16)) + b_ref[0]


def ada_call(cvec, w_ada, b_ada):
    rows = cvec.shape[0]
    tn = 1536
    n6 = 6 * D_MODEL
    return pl.pallas_call(
        _ada_kernel,
        grid=(DEPTH, n6 // tn),
        in_specs=[pl.BlockSpec((rows, D_MODEL), lambda l, n: (0, 0)),
                  pl.BlockSpec((1, D_MODEL, tn), lambda l, n: (l, 0, n)),
                  pl.BlockSpec((1, 1, tn), lambda l, n: (l, 0, n))],
        out_specs=pl.BlockSpec((1, rows, tn), lambda l, n: (l, 0, n)),
        out_shape=jax.ShapeDtypeStruct((DEPTH, rows, n6), f32),
        compiler_params=_cparams(("arbitrary", "arbitrary")),
        name="ada",
    )(cvec, w_ada, b_ada.reshape(DEPTH, 1, n6))


def _lb_kernel(x_ref, o_ref):
    xs = [x_ref[l] for l in range(DEPTH)]
    mx = xs[0]
    for l in range(1, DEPTH):
        mx = jnp.maximum(mx, xs[l])
    es = [jnp.exp(x - mx) for x in xs]
    tot = es[0]
    for l in range(1, DEPTH):
        tot = tot + es[l]
    sm = [e / tot for e in es]
    run = sm[0]
    o_ref[0] = run - sm[0]
    for l in range(1, DEPTH):
        run = run + sm[l]
        o_ref[l] = run - sm[0]


def lb_call(logits):
    x = logits.astype(f32).reshape(DEPTH, 2 * N_HEADS, D_HEAD)
    return pl.pallas_call(
        _lb_kernel,
        out_shape=jax.ShapeDtypeStruct((DEPTH, 2 * N_HEADS, D_HEAD), f32),
        name="hg_lower_bounds",
    )(x)


def _in_proj_kernel(x_ref, mod_ref, g_ref, w_ref, o16_ref, o32_ref, h_s):
    n = pl.program_id(2)

    @pl.when(n == 0)
    def _():
        x = x_ref[0]
        sh = mod_ref[0, 0:1, :]
        sc = mod_ref[0, 1:2, :]
        h = _rms_rows(x, g_ref[...]) * (1.0 + sc) + sh
        h_s[...] = h.astype(bf16)

    @pl.when(n < N16_TILES)
    def _():
        o16_ref[0] = _dot(h_s[...], w_ref[...]).astype(bf16)

    @pl.when(n == N16_TILES)
    def _():
        o32_ref[0] = _dot(h_s[...], w_ref[...])


def in_proj_call(x, mod, g_row, w, shared_mod):
    B, L, _ = x.shape
    tm = min(L, 1024)
    tn = PROJ_TN
    mod_map = (lambda b, m, n: (0, 0, 0)) if shared_mod else (lambda b, m, n: (b, 0, 0))
    return pl.pallas_call(
        _in_proj_kernel,
        grid=(B, L // tm, N16_TILES + 1),
        in_specs=[pl.BlockSpec((1, tm, D_MODEL), lambda b, m, n: (b, m, 0)),
                  pl.BlockSpec((1, 8, D_MODEL), mod_map),
                  pl.BlockSpec((1, D_MODEL), lambda b, m, n: (0, 0)),
                  pl.BlockSpec((D_MODEL, tn), lambda b, m, n: (0, n))],
        out_specs=[pl.BlockSpec((1, tm, tn), lambda b, m, n: (b, m, jnp.minimum(n, N16_TILES - 1))),
                   pl.BlockSpec((1, tm, tn), lambda b, m, n: (b, m, 0))],
        out_shape=[jax.ShapeDtypeStruct((B, L, N16_COLS), bf16),
                   jax.ShapeDtypeStruct((B, L, N32_COLS), f32)],
        scratch_shapes=[pltpu.VMEM((tm, D_MODEL), bf16)],
        compiler_params=_cparams(("arbitrary", "arbitrary", "arbitrary")),
        name="in_proj",
    )(x, mod, g_row, w)


def _mlstm_kernel(bias_ref, m0_ref, q_ref, k_ref, v_ref, mo_ref, g_ref, ng_ref, c0_ref, n0_ref,
                  y_ref, cout_ref, nout_ref, mout_ref,
                  gt_s, rw_s, ucb_s, vt_s, ht_s, ct_s, m_s, *, seq_len):
    T = ML_CHUNK
    assert T == LANES
    nc = seq_len // T
    b_idx = pl.program_id(0)
    h = pl.program_id(1)
    scale = D_HEAD ** -0.5
    ext_rows = jnp.concatenate([jnp.ones((1, T), f32), jnp.zeros((ML_EXT - D_HEAD - 1, T), f32)], axis=0)

    def tr_body(j, carry):
        r0 = pl.multiple_of(j * T, T)
        gt_s[:, pl.ds(r0, T)] = g_ref[0, pl.ds(r0, T), :].T
        vt = v_ref[0, pl.ds(r0, T), :].astype(f32).T
        vt_s[:, pl.ds(r0, T)] = jnp.concatenate([vt, ext_rows], axis=0).astype(bf16)
        return carry
    lax.fori_loop(0, nc, tr_body, 0, unroll=2)

    i_f = gt_s[pl.ds(h, 1), :] + bias_ref[h]
    f_f = _log_sigmoid(gt_s[pl.ds(4 + h, 1), :] + bias_ref[4 + h])
    i_b = gt_s[pl.ds(8 + h, 1), :] + bias_ref[8 + h]
    f_b = _log_sigmoid(gt_s[pl.ds(12 + h, 1), :] + bias_ref[12 + h])
    zrow = jnp.zeros_like(i_f)
    rw_s[...] = jnp.concatenate([f_f, i_f, f_b, i_b, zrow, zrow, zrow, zrow], axis=0)

    ui = lax.broadcasted_iota(jnp.int32, (T, T), 0)
    si = lax.broadcasted_iota(jnp.int32, (T, T), 1)
    tri = jnp.concatenate([(ui <= si).astype(bf16), (ui >= si).astype(bf16)], axis=1)
    valid_f = ui <= si
    valid_b = ui >= si

    def cs_body(j, carry):
        r0 = pl.multiple_of(j * T, T)
        rows = rw_s[:, pl.ds(r0, T)]
        hi, mid, lo = _split3(rows)
        cs3 = _dot(jnp.concatenate([hi, mid, lo, jnp.zeros_like(hi)], axis=0), tri)
        cs = cs3[0:8] + cs3[8:16] + cs3[16:24]
        b_f = cs[0:1, 0:T]
        b_b = cs[2:3, T:2 * T]
        rw_s[:, pl.ds(r0, T)] = jnp.concatenate([b_f, rows[1:2], b_b, rows[3:8]], axis=0)
        ucb_s[0, pl.ds(r0, T), :] = jnp.broadcast_to(rows[1:2] - b_f, (T, T)).T
        ucb_s[1, pl.ds(r0, T), :] = jnp.broadcast_to(rows[3:4] - b_b, (T, T)).T
        return carry
    lax.fori_loop(0, nc, cs_body, 0, unroll=min(4, nc))

    for d in range(2):
        ct_s[d] = jnp.concatenate([c0_ref[0, 0, d, 0].T, n0_ref[0, 0, d, 0],
                                   jnp.zeros((ML_EXT - D_HEAD - 1, D_HEAD), f32)], axis=0)
        m_s[d] = jnp.full((1, LANES), m0_ref[b_idx * 2 * N_HEADS + d * N_HEADS + h], f32)

    def step(d, j):
        r0 = pl.multiple_of(j * T, T)
        b_row = rw_s[pl.ds(2 * d, 1), pl.ds(r0, T)]
        b_end = b_row[:, T - 1:T] if d == 0 else b_row[:, 0:1]
        b_end_l = jnp.broadcast_to(b_end, (1, LANES))
        ucb = ucb_s[d, pl.ds(r0, T), :]

        q = q_ref[0, pl.ds(r0, T), :].astype(f32) * scale
        kb = k_ref[0, pl.ds(r0, T), :]
        k = kb.astype(f32)
        qb = q.astype(bf16)
        vt = vt_s[:, pl.ds(r0, T)]
        ct = ct_s[d]
        m_prev = m_s[d]

        d_t = jnp.where(valid_f if d == 0 else valid_b, ucb + b_row, NEG_BIG)
        m_state = b_row + m_prev
        m_t = jnp.maximum(m_state, jnp.max(d_t, axis=0, keepdims=True))
        w_state = jnp.exp(m_state - m_t)
        s_t = _dot_nt(kb, qb) * jnp.exp(d_t - m_t)
        tot = _dot(vt, s_t.astype(bf16)) + w_state * _dot_nt(ct.astype(bf16), qb)
        den = tot[D_HEAD:D_HEAD + 1]
        ht_s[d, :, pl.ds(r0, T)] = tot[:D_HEAD] * (1.0 / jnp.maximum(jnp.abs(den), jnp.exp(-m_t)))

        gcb = ucb + b_end_l
        m_new = jnp.maximum(b_end_l + m_prev, jnp.max(gcb, axis=0, keepdims=True))
        kw = k * jnp.exp(gcb - m_new)
        decay = jnp.exp(b_end_l + m_prev - m_new)
        ct_s[d] = decay * ct + _dot(vt, kw.astype(bf16))
        m_s[d] = m_new

    def loop_body(j, carry):
        step(0, j)
        step(1, nc - 1 - j)
        return carry
    lax.fori_loop(0, nc, loop_body, 0, unroll=min(4, nc))

    ng = ng_ref[...]

    def out_body(j, carry):
        r0 = pl.multiple_of(j * T, T)
        y_t = ht_s[0, :, pl.ds(r0, T)] + ht_s[1, :, pl.ds(r0, T)]
        ms = jnp.mean(y_t * y_t, axis=0, keepdims=True)
        y = (y_t * lax.rsqrt(ms + EPS)).T * ng * _sigmoid(mo_ref[0, pl.ds(r0, T), :].astype(f32))
        y_ref[0, pl.ds(r0, T), :] = y.astype(y_ref.dtype)
        return carry
    lax.fori_loop(0, nc, out_body, 0, unroll=2)

    for d in range(2):
        cout_ref[0, d, 0] = ct_s[d, 0:D_HEAD, :].T
        nout_ref[0, d, 0] = ct_s[d, D_HEAD:D_HEAD + 1, :]
        mout_ref[0, d, 0] = m_s[d]


def mlstm_call(proj, proj32, gate_bias, norm_g, c0, n0, m0, layer):
    B, L, _ = proj.shape

    def col(cb):
        return pl.BlockSpec((1, L, LANES), lambda b, h, cb=cb: (b, 0, cb + h))

    smem = pl.BlockSpec(memory_space=pltpu.SMEM)
    kern = functools.partial(_mlstm_kernel, seq_len=L)
    return pl.pallas_call(
        kern,
        grid=(B, N_HEADS),
        in_specs=[smem, smem,
                  col(CB_MQ), col(CB_MK), col(CB_MV), col(CB_MO),
                  pl.BlockSpec((1, L, LANES), lambda b, h: (b, 0, CF_MG)),
                  pl.BlockSpec((1, LANES), lambda b, h: (0, h)),
                  pl.BlockSpec((1, 1, 2, 1, D_HEAD, D_HEAD), lambda b, h: (b, layer, 0, h, 0, 0)),
                  pl.BlockSpec((1, 1, 2, 1, 1, D_HEAD), lambda b, h: (b, layer, 0, h, 0, 0))],
        out_specs=[pl.BlockSpec((1, L, LANES), lambda b, h: (b, 0, h)),
                   pl.BlockSpec((1, 2, 1, D_HEAD, D_HEAD), lambda b, h: (b, 0, h, 0, 0)),
                   pl.BlockSpec((1, 2, 1, 1, D_HEAD), lambda b, h: (b, 0, h, 0, 0)),
                   pl.BlockSpec((1, 2, 1, 1, LANES), lambda b, h: (b, 0, h, 0, 0))],
        out_shape=[jax.ShapeDtypeStruct((B, L, BRANCH_W), bf16),
                   jax.ShapeDtypeStruct((B, 2, N_HEADS, D_HEAD, D_HEAD), f32),
                   jax.ShapeDtypeStruct((B, 2, N_HEADS, 1, D_HEAD), f32),
                   jax.ShapeDtypeStruct((B, 2, N_HEADS, 1, LANES), f32)],
        scratch_shapes=[pltpu.VMEM((LANES, L), f32),
                        pltpu.VMEM((8, L), f32),
                        pltpu.VMEM((2, L, LANES), f32),
                        pltpu.VMEM((ML_EXT, L), bf16),
                        pltpu.VMEM((2, D_HEAD, L), f32),
                        pltpu.VMEM((2, ML_EXT, D_HEAD), f32),
                        pltpu.VMEM((2, 1, LANES), f32)],
        compiler_params=_cparams(("arbitrary", "arbitrary")),
        name="mlstm",
    )(gate_bias, m0, proj, proj, proj, proj, proj32, norm_g, c0, n0)


def _hgrn_kernel(q_ref, f0_ref, f1_ref, i_ref, og_ref, lb_ref, ng_ref, s0_ref,
                 y_ref, sout_ref, o_s, st_s, qx_s, dec_s, u_s, vt_s, k_s, b_s, q32_s, v32_s, *, seq_len):
    T = HG_CHUNK
    nc = seq_len // T
    h = pl.program_id(1)
    f_refs = (f0_ref, f1_ref)

    lbs, loglb, log1mlb = [], [], []
    for d in range(2):
        lb = lb_ref[pl.ds(d * N_HEADS + h, 1), :]
        lbs.append(lb)
        loglb.append(jnp.log(lb))
        log1mlb.append(jnp.log1p(-lb))
        st_s[d] = s0_ref[0, 0, d, 0].T

    ti = lax.broadcasted_iota(jnp.int32, (T, T), 0)
    ui = lax.broadcasted_iota(jnp.int32, (T, T), 1)
    tri = ((ui <= ti).astype(bf16), (ui >= ti).astype(bf16))
    sub_t = lax.broadcasted_iota(jnp.int32, (HG_SUB, D_HEAD), 0)

    def gates(d, j, q):
        r0 = pl.multiple_of(j * T, T)
        hf = f_refs[d][0, pl.ds(r0, T), :]
        e = jnp.exp(-jnp.abs(hf))
        sig_neg = jnp.where(hf >= 0.0, e, 1.0) / (1.0 + e)
        logsig = jnp.minimum(hf, 0.0) - jnp.log(1.0 + e)
        cterm = log1mlb[d] + logsig
        amax = jnp.maximum(loglb[d], cterm)
        logf = amax + jnp.log(1.0 + jnp.exp(-jnp.abs(loglb[d] - cterm)))
        k = (1.0 - lbs[d]) * sig_neg
        hi, mid, lo = _split3(logf * LOG2E)
        b = _dot(tri[d], hi) + _dot(tri[d], mid) + _dot(tri[d], lo)
        b_end = b[T - 1:T] if d == 0 else b[0:1]
        k_s[d, pl.ds(r0, T), :] = k
        b_s[d, pl.ds(r0, T), :] = b
        qx_s[d, pl.ds(r0, T), :] = (q * jnp.exp2(b)).astype(bf16)
        dec_s[d, pl.ds(j, 1), :] = jnp.exp2(b_end)

    def gates_body(j, carry):
        r0 = pl.multiple_of(j * T, T)
        q = q_ref[0, pl.ds(r0, T), :].astype(f32)
        v = i_ref[0, pl.ds(r0, T), :].astype(f32)
        q32_s[pl.ds(r0, T), :] = q
        v32_s[pl.ds(r0, T), :] = v
        vt_s[j] = v.T.astype(bf16)
        gates(0, j, q)
        gates(1, j, q)
        return carry
    lax.fori_loop(0, nc, gates_body, 0, unroll=2)

    def intra(d, j):
        r0 = pl.multiple_of(j * T, T)

        def rows(ref2, lo_r, n):
            return ref2[pl.ds(r0 + lo_r, n), :]

        def srows(ref3, lo_r, n):
            return ref3[d, pl.ds(r0 + lo_r, n), :]

        nsub = T // HG_SUB
        o_blk = [None] * nsub

        def add(idx, val):
            o_blk[idx] = val if o_blk[idx] is None else o_blk[idx] + val

        def diag(lo_r):
            bs = srows(b_s, lo_r, HG_SUB)
            qs = rows(q32_s, lo_r, HG_SUB)
            acc = None
            for s in range(HG_SUB):
                valid = (sub_t >= s) if d == 0 else (sub_t <= s)
                ee = jnp.exp2(jnp.where(valid, bs - srows(b_s, lo_r + s, 1), NEG_BIG))
                a_col = jnp.sum(qs * ee * srows(k_s, lo_r + s, 1), axis=-1, keepdims=True)
                term = a_col * rows(v32_s, lo_r + s, 1)
                acc = term if acc is None else acc + term
            add(lo_r // HG_SUB, acc)

        def block(lo_r, hi_r):
            if hi_r - lo_r == HG_SUB:
                diag(lo_r)
                return
            mid_r = (lo_r + hi_r) // 2
            n = mid_r - lo_r
            if d == 0:
                ref = srows(b_s, mid_r - 1, 1)
                q_lo, k_lo = mid_r, lo_r
            else:
                ref = srows(b_s, mid_r, 1)
                q_lo, k_lo = lo_r, mid_r
            qx = rows(q32_s, q_lo, n) * jnp.exp2(srows(b_s, q_lo, n) - ref)
            kx = srows(k_s, k_lo, n) * jnp.exp2(ref - srows(b_s, k_lo, n))
            a = _dot_nt(qx.astype(bf16), kx.astype(bf16))
            ov = _dot(a.astype(bf16), rows(v32_s, k_lo, n).astype(bf16))
            for p in range(n // HG_SUB):
                add(q_lo // HG_SUB + p, ov[p * HG_SUB:(p + 1) * HG_SUB])
            block(lo_r, mid_r)
            block(mid_r, hi_r)

        block(0, T)
        o_s[d, pl.ds(r0, T), :] = jnp.concatenate(o_blk, axis=0)

    def intra_body(j, carry):
        intra(0, j)
        intra(1, j)
        return carry
    lax.fori_loop(0, nc, intra_body, 0, unroll=4)

    def incr_body(j, carry):
        r0 = pl.multiple_of(j * T, T)
        vt = vt_s[j]
        for d in range(2):
            b = b_s[d, pl.ds(r0, T), :]
            b_end = b[T - 1:T] if d == 0 else b[0:1]
            u_s[d, j] = _dot(vt, (k_s[d, pl.ds(r0, T), :] * jnp.exp2(b_end - b)).astype(bf16))
        return carry
    lax.fori_loop(0, nc, incr_body, 0, unroll=4)

    def state_step(d, j):
        r0 = pl.multiple_of(j * T, T)
        st = st_s[d]
        o_s[d, pl.ds(r0, T), :] = o_s[d, pl.ds(r0, T), :] + _dot_nt(qx_s[d, pl.ds(r0, T), :], st.astype(bf16))
        st_s[d] = st * dec_s[d, pl.ds(j, 1), :] + u_s[d, j]

    def state_body(j, carry):
        state_step(0, j)
        state_step(1, nc - 1 - j)
        return carry
    lax.fori_loop(0, nc, state_body, 0, unroll=4)

    ng = ng_ref[...]

    def out_body(j, carry):
        r0 = pl.multiple_of(j * T, T)
        y = o_s[0, pl.ds(r0, T), :] + o_s[1, pl.ds(r0, T), :]
        y = _rms_rows(y, ng) * _silu(og_ref[0, pl.ds(r0, T), :].astype(f32))
        y_ref[0, pl.ds(r0, T), :] = y.astype(y_ref.dtype)
        return carry
    lax.fori_loop(0, nc, out_body, 0, unroll=4)

    for d in range(2):
        sout_ref[0, d, 0] = st_s[d].T


def hgrn_call(proj, proj32, lb_rows, norm_g, s0, layer):
    B, L, _ = proj.shape

    def col(cb):
        return pl.BlockSpec((1, L, LANES), lambda b, h, cb=cb: (b, 0, cb + h))

    kern = functools.partial(_hgrn_kernel, seq_len=L)
    return pl.pallas_call(
        kern,
        grid=(B, N_HEADS),
        in_specs=[col(CB_HQ), col(CF_HF), col(CF_HF + N_HEADS), col(CB_HI), col(CB_HG),
                  pl.BlockSpec((2 * N_HEADS, D_HEAD), lambda b, h: (0, 0)),
                  pl.BlockSpec((1, LANES), lambda b, h: (0, h)),
                  pl.BlockSpec((1, 1, 2, 1, D_HEAD, D_HEAD), lambda b, h: (b, layer, 0, h, 0, 0))],
        out_specs=[pl.BlockSpec((1, L, LANES), lambda b, h: (b, 0, h)),
                   pl.BlockSpec((1, 2, 1, D_HEAD, D_HEAD), lambda b, h: (b, 0, h, 0, 0))],
        out_shape=[jax.ShapeDtypeStruct((B, L, BRANCH_W), bf16),
                   jax.ShapeDtypeStruct((B, 2, N_HEADS, D_HEAD, D_HEAD), f32)],
        scratch_shapes=[pltpu.VMEM((2, L, D_HEAD), f32),
                        pltpu.VMEM((2, D_HEAD, D_HEAD), f32),
                        pltpu.VMEM((2, L, D_HEAD), bf16),
                        pltpu.VMEM((2, L // HG_CHUNK, D_HEAD), f32),
                        pltpu.VMEM((2, L // HG_CHUNK, D_HEAD, D_HEAD), f32),
                        pltpu.VMEM((L // HG_CHUNK, D_HEAD, HG_CHUNK), bf16),
                        pltpu.VMEM((2, L, D_HEAD), f32),
                        pltpu.VMEM((2, L, D_HEAD), f32),
                        pltpu.VMEM((L, D_HEAD), f32),
                        pltpu.VMEM((L, D_HEAD), f32)],
        compiler_params=_cparams(("arbitrary", "arbitrary")),
        name="hgrn",
    )(proj, proj32, proj32, proj, proj, lb_rows, norm_g, s0)


def _pair_swap(x):
    lane = lax.broadcasted_iota(jnp.int32, x.shape, 1)
    return jnp.where((lane % 2) == 0, pltpu.roll(x, LANES - 1, 1), pltpu.roll(x, 1, 1))


def _head_norm_pair(x, g_row):
    lane = lax.broadcasted_iota(jnp.int32, x.shape, 1)
    low = lane < AT_HD
    sq = x * x
    s_lo = jnp.sum(jnp.where(low, sq, 0.0), axis=-1, keepdims=True)
    s_hi = jnp.sum(jnp.where(low, 0.0, sq), axis=-1, keepdims=True)
    ms = jnp.where(low, s_lo, s_hi) * (1.0 / AT_HD)
    return x * lax.rsqrt(ms + EPS) * g_row


def _qkv_prep_kernel(aq_ref, ak_ref, av_ref, qg_ref, kg_ref, cos_ref, sin_ref,
                     q_ref, k_ref, v_ref, kn_ref, *, use_rope):
    qg = qg_ref[...]
    kg = kg_ref[...]
    if use_rope:
        cos = cos_ref[...]
        sin = sin_ref[...]
    for p in range(AT_HEADS // 2):
        x = _head_norm_pair(aq_ref[0, :, p * LANES:(p + 1) * LANES].astype(f32), qg)
        if use_rope:
            x = x * cos + _pair_swap(x) * sin
        x = (x * (AT_HD ** -0.5)).astype(bf16)
        q_ref[0, 2 * p] = x[:, :AT_HD]
        q_ref[0, 2 * p + 1] = x[:, AT_HD:]
    kn = _head_norm_pair(ak_ref[0].astype(f32), kg)
    kn_ref[0] = kn
    if use_rope:
        kn = kn * cos + _pair_swap(kn) * sin
    kb = kn.astype(bf16)
    vb = av_ref[0]
    for g in range(AT_KV_HEADS):
        k_ref[0, g] = kb[:, g * AT_HD:(g + 1) * AT_HD]
        v_ref[0, g] = vb[:, g * AT_HD:(g + 1) * AT_HD]


def qkv_prep_call(proj, qg_row, kg_row, cos_t, sin_t, use_rope):
    B, L, _ = proj.shape
    tm = min(L, 512)
    kern = functools.partial(_qkv_prep_kernel, use_rope=use_rope)
    return pl.pallas_call(
        kern,
        grid=(B, L // tm),
        in_specs=[pl.BlockSpec((1, tm, 4 * LANES), lambda b, m: (b, m, CB_AQ // 4)),
                  pl.BlockSpec((1, tm, LANES), lambda b, m: (b, m, CB_AK)),
                  pl.BlockSpec((1, tm, LANES), lambda b, m: (b, m, CB_AV)),
                  pl.BlockSpec((1, LANES), lambda b, m: (0, 0)),
                  pl.BlockSpec((1, LANES), lambda b, m: (0, 0)),
                  pl.BlockSpec((tm, LANES), lambda b, m: (m, 0)),
                  pl.BlockSpec((tm, LANES), lambda b, m: (m, 0))],
        out_specs=[pl.BlockSpec((1, AT_HEADS, tm, AT_HD), lambda b, m: (b, 0, m, 0)),
                   pl.BlockSpec((1, AT_KV_HEADS, tm, AT_HD), lambda b, m: (b, 0, m, 0)),
                   pl.BlockSpec((1, AT_KV_HEADS, tm, AT_HD), lambda b, m: (b, 0, m, 0)),
                   pl.BlockSpec((1, tm, LANES), lambda b, m: (b, m, 0))],
        out_shape=[jax.ShapeDtypeStruct((B, AT_HEADS, L, AT_HD), bf16),
                   jax.ShapeDtypeStruct((B, AT_KV_HEADS, L, AT_HD), bf16),
                   jax.ShapeDtypeStruct((B, AT_KV_HEADS, L, AT_HD), bf16),
                   jax.ShapeDtypeStruct((B, L, LANES), f32)],
        compiler_params=_cparams(("arbitrary", "arbitrary")),
        name="qkv_prep",
    )(proj, proj, proj, qg_row, kg_row, cos_t, sin_t)


def _attn_kernel(q_ref, k_ref, v_ref, o_ref, s_s, m_s, *, tq, tk, nkv):
    rows = AT_GROUP * tq
    ncb = tk // LANES
    always = pl.program_id(2) >= 0

    @pl.when(always)
    def _():
        q = q_ref[0].reshape(rows, AT_HD)
        mx = None
        for j in range(nkv):
            s = _dot_nt(q, k_ref[0, 0, j * tk:(j + 1) * tk, :])
            s_s[:, j * tk:(j + 1) * tk] = s
            cm = s[:, 0:LANES]
            for cb in range(1, ncb):
                cm = jnp.maximum(cm, s[:, cb * LANES:(cb + 1) * LANES])
            mx = cm if mx is None else jnp.maximum(mx, cm)
        m_s[...] = jnp.broadcast_to(jnp.max(mx, axis=-1, keepdims=True), (rows, LANES))

    @pl.when(pl.program_id(1) >= 0)
    def _():
        m_b = m_s[...]
        acc = None
        for j in range(nkv):
            ps = []
            for cb in range(ncb):
                off = j * tk + cb * LANES
                ps.append(jnp.exp(s_s[:, off:off + LANES] - m_b).astype(bf16))
            pv = _dot(jnp.concatenate(ps, axis=1), v_ref[0, 0, j * tk:(j + 1) * tk, :])
            acc = pv if acc is None else acc + pv
        o = acc[:, :AT_HD] * (1.0 / acc[:, AT_HD:AT_HD + 1])
        for g in range(AT_GROUP):
            o_ref[0, :, g * AT_HD:(g + 1) * AT_HD] = o[g * tq:(g + 1) * tq].astype(o_ref.dtype)


def attn_call(q, k, v):
    B, _, Lq, _ = q.shape
    Lk = k.shape[2]
    tq = 256
    tk = 512 if Lk % 512 == 0 else Lk
    nkv = Lk // tk
    nq = Lq // tq
    kern = functools.partial(_attn_kernel, tq=tq, tk=tk, nkv=nkv)
    rows = AT_GROUP * tq
    return pl.pallas_call(
        kern,
        grid=(B, AT_KV_HEADS, nq),
        in_specs=[pl.BlockSpec((1, AT_GROUP, tq, AT_HD), lambda b, g, i: (b, g, i, 0)),
                  pl.BlockSpec((1, 1, Lk, AT_HD), lambda b, g, i: (b, g, 0, 0)),
                  pl.BlockSpec((1, 1, Lk, LANES), lambda b, g, i: (b, g, 0, 0))],
        out_specs=pl.BlockSpec((1, tq, AT_GROUP * AT_HD), lambda b, g, i: (b, i, g)),
        out_shape=jax.ShapeDtypeStruct((B, Lq, AT_HEADS * AT_HD), bf16),
        scratch_shapes=[pltpu.VMEM((rows, Lk), f32),
                        pltpu.VMEM((rows, LANES), f32)],
        compiler_params=_cparams(("arbitrary", "arbitrary", "arbitrary")),
        name="attention",
    )(q, k, v)


def _with_ones_column(v):
    ones = jnp.ones(v.shape[:-1] + (1,), v.dtype)
    zeros = jnp.zeros(v.shape[:-1] + (LANES - AT_HD - 1,), v.dtype)
    return jnp.concatenate([v, ones, zeros], axis=-1)


def _merge_kernel(x_ref, yml_ref, yhg_ref, yat_ref, bg_ref, mod_ref, wb_ref, wo_ref, o_ref):
    merged = None
    for n, y_ref in enumerate((yml_ref, yhg_ref, yat_ref)):
        pn = _dot(y_ref[0], wb_ref[n])
        gn = _sigmoid(bg_ref[0, :, n * D_MODEL:(n + 1) * D_MODEL].astype(f32))
        merged = gn * pn if merged is None else merged + gn * pn
    out = _dot(merged.astype(bf16), wo_ref[...])
    o_ref[0] = x_ref[0] + mod_ref[0, 2:3, :] * out


def merge_call(x, y_ml, y_hg, y_at, proj, mod, wb, wo, shared_mod):
    B, L, _ = x.shape
    tm = min(L, 256)
    mod_map = (lambda b, m: (0, 0, 0)) if shared_mod else (lambda b, m: (b, 0, 0))
    yspec = pl.BlockSpec((1, tm, BRANCH_W), lambda b, m: (b, m, 0))
    return pl.pallas_call(
        _merge_kernel,
        grid=(B, L // tm),
        in_specs=[pl.BlockSpec((1, tm, D_MODEL), lambda b, m: (b, m, 0)),
                  yspec, yspec, yspec,
                  pl.BlockSpec((1, tm, 3 * D_MODEL), lambda b, m: (b, m, CB_BG)),
                  pl.BlockSpec((1, 8, D_MODEL), mod_map),
                  pl.BlockSpec((3, BRANCH_W, D_MODEL), lambda b, m: (0, 0, 0)),
                  pl.BlockSpec((D_MODEL, D_MODEL), lambda b, m: (0, 0))],
        out_specs=pl.BlockSpec((1, tm, D_MODEL), lambda b, m: (b, m, 0)),
        out_shape=jax.ShapeDtypeStruct((B, L, D_MODEL), f32),
        compiler_params=_cparams(("arbitrary", "arbitrary")),
        name="merge_out",
    )(x, y_ml, y_hg, y_at, proj, mod, wb, wo)


FFN_SPLITS = ((0, 1536), (1536, FFN_HIDDEN))


def _ffn_kernel(x_ref, mod_ref, g_ref, wi_ref, wd_ref, fg_ref, o_ref, *, final_norm):
    x = x_ref[0]
    hb = (_rms_rows(x, g_ref[...]) * (1.0 + mod_ref[0, 4:5, :]) + mod_ref[0, 3:4, :]).astype(bf16)
    acc = None
    for lo, hi in FFN_SPLITS:
        gate = _dot(hb, wi_ref[:, lo:hi])
        up = _dot(hb, wi_ref[:, FFN_HIDDEN + lo:FFN_HIDDEN + hi])
        part = _dot((_silu(gate) * up).astype(bf16), wd_ref[lo:hi, :])
        acc = part if acc is None else acc + part
    y = x + mod_ref[0, 5:6, :] * acc
    if final_norm:
        y = _rms_rows(y, fg_ref[...])
    o_ref[0] = y


def ffn_call(x, mod, g_row, w_in, w_out, fg_row, shared_mod, final_norm):
    B, L, _ = x.shape
    tm = min(L, 512)
    mod_map = (lambda b, m: (0, 0, 0)) if shared_mod else (lambda b, m: (b, 0, 0))
    kern = functools.partial(_ffn_kernel, final_norm=final_norm)
    resident = pl.Buffered(1)
    return pl.pallas_call(
        kern,
        grid=(B, L // tm),
        in_specs=[pl.BlockSpec((1, tm, D_MODEL), lambda b, m: (b, m, 0)),
                  pl.BlockSpec((1, 8, D_MODEL), mod_map),
                  pl.BlockSpec((1, D_MODEL), lambda b, m: (0, 0)),
                  pl.BlockSpec((D_MODEL, 2 * FFN_HIDDEN), lambda b, m: (0, 0), pipeline_mode=resident),
                  pl.BlockSpec((FFN_HIDDEN, D_MODEL), lambda b, m: (0, 0), pipeline_mode=resident),
                  pl.BlockSpec((1, D_MODEL), lambda b, m: (0, 0))],
        out_specs=pl.BlockSpec((1, tm, D_MODEL), lambda b, m: (b, m, 0)),
        out_shape=jax.ShapeDtypeStruct((B, L, D_MODEL), f32),
        compiler_params=_cparams(("arbitrary", "arbitrary")),
        name="ffn",
    )(x, mod, g_row, w_in, w_out, fg_row)


def _reorder_w_in(w):
    o_mg, o_hq, o_hf, o_hi, o_bg, o_end = 2048, 2064, 2576, 3600, 5392, 8464
    d = w.shape[0]
    sec16 = [w[:, o_bg:o_end], w[:, :o_mg], w[:, o_hq:o_hf], w[:, o_hi:o_bg]]
    n16 = (o_end - o_bg) + o_mg + (o_hf - o_hq) + (o_bg - o_hi)
    sec32 = [w[:, o_hf:o_hi], w[:, o_mg:o_hq]]
    n32 = (o_hi - o_hf) + (o_hq - o_mg)
    return jnp.concatenate(sec16 + [jnp.zeros((d, N16_COLS - n16), w.dtype)] + sec32
                           + [jnp.zeros((d, N32_COLS - n32), w.dtype)], axis=1).astype(bf16)


def _rope_tables(L):
    rows = L // GRID_W
    row = jnp.repeat(jnp.arange(rows, dtype=f32), GRID_W)
    colp = jnp.tile(jnp.arange(GRID_W, dtype=f32), rows)
    n_freq = AT_HD // 4
    inv = ROPE_THETA ** (-jnp.arange(n_freq, dtype=f32) / n_freq)
    ang = jnp.concatenate([row[:, None] * inv, colp[:, None] * inv], axis=-1)
    cos = jnp.repeat(jnp.cos(ang), 2, axis=-1)
    sin = jnp.repeat(jnp.sin(ang), 2, axis=-1)
    sign = jnp.tile(jnp.array([-1.0, 1.0], f32), AT_HD // 2)
    sin = sin * sign
    return jnp.tile(cos, (1, 2)), jnp.tile(sin, (1, 2))


def _mod_rows(ada_rows):
    r = ada_rows.shape[0]
    m = ada_rows.reshape(r, 6, D_MODEL)
    return jnp.concatenate([m, jnp.zeros((r, 2, D_MODEL), f32)], axis=1)


def kernel(x_prompt, x_sample, c, cache_k, cache_v, state_ml_C, state_ml_n, state_ml_m, state_hg_S, c_ctx, w_ada, b_ada, norm1_g, norm2_g, w_in, ml_gate_b, ml_norm_g, hg_lb_logits, hg_norm_g, q_norm_g, k_norm_g, w_branch, w_out, w_ffn_in, w_ffn_out, final_g):
    Bp, Lp, _ = x_prompt.shape
    Bs, Ls, _ = x_sample.shape

    n_rows = 16
    cvec = jnp.concatenate([c_ctx[None, :], c, jnp.zeros((n_rows - 1 - Bs, D_MODEL), f32)], axis=0)
    ada = ada_call(cvec, w_ada, b_ada)
    lb_all = lb_call(hg_lb_logits)
    cos_t, sin_t = _rope_tables(Ls)

    zeros_c = jnp.zeros((Bp, 1, 2, N_HEADS, D_HEAD, D_HEAD), f32)
    zeros_n = jnp.zeros((Bp, 1, 2, N_HEADS, 1, D_HEAD), f32)
    zeros_m = jnp.zeros((Bp * 2 * N_HEADS,), f32)
    n_state = state_ml_n.reshape(Bs, DEPTH, 2, N_HEADS, 1, D_HEAD)

    xp, xs = x_prompt, x_sample
    nk, nv, nC, nn_, nm, nS = [], [], [], [], [], []
    for l in range(DEPTH):
        w_l = _reorder_w_in(w_in[l])
        wb_l = w_branch[l].astype(bf16)
        wo_l = w_out[l].astype(bf16)
        wfi_l = w_ffn_in[l].astype(bf16)
        wfo_l = w_ffn_out[l].astype(bf16)
        g1 = norm1_g[l][None, :]
        g2 = norm2_g[l][None, :]
        mlg = ml_norm_g[l][None, :]
        hgg = hg_norm_g[l][None, :]
        qg = jnp.tile(q_norm_g[l], 2)[None, :]
        kg = jnp.tile(k_norm_g[l], 2)[None, :]
        gate_b = ml_gate_b[l].reshape(-1).astype(f32)
        fg = final_g[None, :]
        last = l == DEPTH - 1

        mod_ctx = _mod_rows(ada[l, 0:1])
        mod_lat = _mod_rows(ada[l, 1:1 + Bs])

        proj, proj32 = in_proj_call(xp, mod_ctx, g1, w_l, True)
        y_ml, c_f, n_f, m_f = mlstm_call(proj, proj32, gate_b, mlg, zeros_c, zeros_n, zeros_m, 0)
        y_hg, s_f = hgrn_call(proj, proj32, lb_all[l], hgg, zeros_c, 0)
        q_h, k_h, v_h, k_n = qkv_prep_call(proj, qg, kg, cos_t[:Lp], sin_t[:Lp], False)
        y_at = attn_call(q_h, k_h, _with_ones_column(v_h))
        xp = merge_call(xp, y_ml, y_hg, y_at, proj, mod_ctx, wb_l, wo_l, True)
        xp = ffn_call(xp, mod_ctx, g2, wfi_l, wfo_l, fg, True, last)
        nk.append(k_n.reshape(Bp, Lp, AT_KV_HEADS, AT_HD))
        nv.append(proj[:, :, CB_AV * LANES:(CB_AV + 1) * LANES].astype(f32).reshape(Bp, Lp, AT_KV_HEADS, AT_HD))
        nC.append(c_f)
        nn_.append(n_f[:, :, :, 0, :])
        nm.append(m_f[:, :, :, 0, 0])
        nS.append(s_f)

        proj, proj32 = in_proj_call(xs, mod_lat, g1, w_l, False)
        y_ml, _, _, _ = mlstm_call(proj, proj32, gate_b, mlg, state_ml_C, n_state,
                                   state_ml_m[:, l].reshape(-1).astype(f32), l)
        y_hg, _ = hgrn_call(proj, proj32, lb_all[l], hgg, state_hg_S, l)
        q_h, k_h, v_h, _ = qkv_prep_call(proj, qg, kg, cos_t, sin_t, True)
        k_all = jnp.concatenate([k_h, jnp.swapaxes(cache_k[:, l], 1, 2).astype(bf16)], axis=2)
        v_all = jnp.concatenate([v_h, jnp.swapaxes(cache_v[:, l], 1, 2).astype(bf16)], axis=2)
        y_at = attn_call(q_h, k_all, _with_ones_column(v_all))
        xs = merge_call(xs, y_ml, y_hg, y_at, proj, mod_lat, wb_l, wo_l, False)
        xs = ffn_call(xs, mod_lat, g2, wfi_l, wfo_l, fg, False, last)

    return (xp, xs, jnp.stack(nk, axis=1), jnp.stack(nv, axis=1), jnp.stack(nC, axis=1),
            jnp.stack(nn_, axis=1), jnp.stack(nm, axis=1), jnp.stack(nS, axis=1))
```

```python
import functools
import math

import jax
import jax.numpy as jnp
from jax import lax
from jax.experimental import pallas as pl
from jax.experimental.pallas import tpu as pltpu

f32 = jnp.float32
bf16 = jnp.bfloat16

D_MODEL = 1024
DEPTH = 4
N_HEADS = 4
D_HEAD = 128
AT_HEADS = 8
AT_KV_HEADS = 2
AT_GROUP = AT_HEADS // AT_KV_HEADS
AT_HD = 64
BRANCH_W = 512
FFN_HIDDEN = 2816
GRID_W = 64
ROPE_THETA = 10000.0
EPS = 1e-6

LANES = 128
SUBLANES = 8
VMEM_LIMIT = 52 * 1024 * 1024

CB_BG = 0
CB_MQ = 24
CB_MK = 28
CB_MV = 32
CB_MO = 36
CB_HQ = 40
CB_HI = 44
CB_HG = 48
CB_AQ = 52
CB_AK = 56
CB_AV = 57
N16_BLOCKS = 60
N16_COLS = N16_BLOCKS * LANES
CF_HF = 0
CF_MG = 8
N32_BLOCKS = 10
N32_COLS = N32_BLOCKS * LANES
PROJ_TN = N32_COLS
N16_TILES = N16_COLS // PROJ_TN

ML_CHUNK = 128
ML_EXT = 144
HG_CHUNK = 64
HG_SUB = 8
HG_HALF = 4
HG_LEVELS = (32, 16, 8, 4)
NEG_BIG = -1e30
LOG2E = 1.4426950408889634


def _cparams(sem):
    return pltpu.CompilerParams(dimension_semantics=sem, vmem_limit_bytes=VMEM_LIMIT)


def _dot(a, b):
    return jnp.dot(a, b, preferred_element_type=f32)


def _dot_nt(a, b):
    return lax.dot_general(a, b, (((1,), (1,)), ((), ())), preferred_element_type=f32)


def _dot_tn(a, b):
    return lax.dot_general(a, b, (((0,), (0,)), ((), ())), preferred_element_type=f32)


def _sigmoid(x):
    return 1.0 / (1.0 + jnp.exp(-x))


def _silu(x):
    return x * _sigmoid(x)


def _log_sigmoid(x):
    return jnp.minimum(x, 0.0) - jnp.log1p(jnp.exp(-jnp.abs(x)))


def _split3(x):
    hi = x.astype(bf16)
    r1 = x - hi.astype(f32)
    mid = r1.astype(bf16)
    lo = (r1 - mid.astype(f32)).astype(bf16)
    return hi, mid, lo


def _rms_rows(x, g_row):
    ms = jnp.mean(x * x, axis=-1, keepdims=True)
    return x * lax.rsqrt(ms + EPS) * g_row


def _ada_kernel(c_ref, w_ref, b_ref, o_ref):
    s = _silu(c_ref[...]).astype(bf16)
    o_ref[0] = _dot(s, w_ref[0].astype(bf16)) + b_ref[0]


def ada_call(cvec, w_ada, b_ada):
    rows = cvec.shape[0]
    tn = 1536
    n6 = 6 * D_MODEL
    return pl.pallas_call(
        _ada_kernel,
        grid=(DEPTH, n6 // tn),
        in_specs=[pl.BlockSpec((rows, D_MODEL), lambda l, n: (0, 0)),
                  pl.BlockSpec((1, D_MODEL, tn), lambda l, n: (l, 0, n)),
                  pl.BlockSpec((1, 1, tn), lambda l, n: (l, 0, n))],
        out_specs=pl.BlockSpec((1, rows, tn), lambda l, n: (l, 0, n)),
        out_shape=jax.ShapeDtypeStruct((DEPTH, rows, n6), f32),
        compiler_params=_cparams(("arbitrary", "arbitrary")),
        name="ada",
    )(cvec, w_ada, b_ada.reshape(DEPTH, 1, n6))


def _lb_kernel(x_ref, o_ref):
    xs = [x_ref[l] for l in range(DEPTH)]
    mx = xs[0]
    for l in range(1, DEPTH):
        mx = jnp.maximum(mx, xs[l])
    es = [jnp.exp(x - mx) for x in xs]
    tot = es[0]
    for l in range(1, DEPTH):
        tot = tot + es[l]
    sm = [e / tot for e in es]
    run = sm[0]
    o_ref[0] = run - sm[0]
    for l in range(1, DEPTH):
        run = run + sm[l]
        o_ref[l] = run - sm[0]


def lb_call(logits):
    x = logits.astype(f32).reshape(DEPTH, 2 * N_HEADS, D_HEAD)
    return pl.pallas_call(
        _lb_kernel,
        out_shape=jax.ShapeDtypeStruct((DEPTH, 2 * N_HEADS, D_HEAD), f32),
        name="hg_lower_bounds",
    )(x)


def _in_proj_kernel(x_ref, mod_ref, g_ref, w_ref, o16_ref, o32_ref, h_s):
    n = pl.program_id(2)

    @pl.when(n == 0)
    def _():
        x = x_ref[0]
        sh = mod_ref[0, 0:1, :]
        sc = mod_ref[0, 1:2, :]
        h = _rms_rows(x, g_ref[...]) * (1.0 + sc) + sh
        h_s[...] = h.astype(bf16)

    @pl.when(n < N16_TILES)
    def _():
        o16_ref[0] = _dot(h_s[...], w_ref[...]).astype(bf16)

    @pl.when(n == N16_TILES)
    def _():
        o32_ref[0] = _dot(h_s[...], w_ref[...])


def in_proj_call(x, mod, g_row, w, shared_mod):
    B, L, _ = x.shape
    tm = min(L, 1024)
    tn = PROJ_TN
    mod_map = (lambda b, m, n: (0, 0, 0)) if shared_mod else (lambda b, m, n: (b, 0, 0))
    return pl.pallas_call(
        _in_proj_kernel,
        grid=(B, L // tm, N16_TILES + 1),
        in_specs=[pl.BlockSpec((1, tm, D_MODEL), lambda b, m, n: (b, m, 0)),
                  pl.BlockSpec((1, 8, D_MODEL), mod_map),
                  pl.BlockSpec((1, D_MODEL), lambda b, m, n: (0, 0)),
                  pl.BlockSpec((D_MODEL, tn), lambda b, m, n: (0, n))],
        out_specs=[pl.BlockSpec((1, tm, tn), lambda b, m, n: (b, m, jnp.minimum(n, N16_TILES - 1))),
                   pl.BlockSpec((1, tm, tn), lambda b, m, n: (b, m, 0))],
        out_shape=[jax.ShapeDtypeStruct((B, L, N16_COLS), bf16),
                   jax.ShapeDtypeStruct((B, L, N32_COLS), f32)],
        scratch_shapes=[pltpu.VMEM((tm, D_MODEL), bf16)],
        compiler_params=_cparams(("arbitrary", "arbitrary", "arbitrary")),
        name="in_proj",
    )(x, mod, g_row, w)


def _mlstm_kernel(bias_ref, m0_ref, q_ref, k_ref, v_ref, mo_ref, g_ref, ng_ref, c0_ref, n0_ref,
                  y_ref, cout_ref, nout_ref, mout_ref,
                  gt_s, rw_s, ucb_s, vt_s, ht_s, ct_s, m_s, *, seq_len):
    T = ML_CHUNK
    assert T == LANES
    nc = seq_len // T
    b_idx = pl.program_id(0)
    h = pl.program_id(1)
    scale = D_HEAD ** -0.5
    ext_rows = jnp.concatenate([jnp.ones((1, T), f32), jnp.zeros((ML_EXT - D_HEAD - 1, T), f32)], axis=0)

    def tr_body(j, carry):
        r0 = pl.multiple_of(j * T, T)
        gt_s[:, pl.ds(r0, T)] = g_ref[0, pl.ds(r0, T), :].T
        vt = v_ref[0, pl.ds(r0, T), :].astype(f32).T
        vt_s[:, pl.ds(r0, T)] = jnp.concatenate([vt, ext_rows], axis=0).astype(bf16)
        return carry
    lax.fori_loop(0, nc, tr_body, 0, unroll=2)

    i_f = gt_s[pl.ds(h, 1), :] + bias_ref[h]
    f_f = _log_sigmoid(gt_s[pl.ds(4 + h, 1), :] + bias_ref[4 + h])
    i_b = gt_s[pl.ds(8 + h, 1), :] + bias_ref[8 + h]
    f_b = _log_sigmoid(gt_s[pl.ds(12 + h, 1), :] + bias_ref[12 + h])
    zrow = jnp.zeros_like(i_f)
    rw_s[...] = jnp.concatenate([f_f, i_f, f_b, i_b, zrow, zrow, zrow, zrow], axis=0)

    ui = lax.broadcasted_iota(jnp.int32, (T, T), 0)
    si = lax.broadcasted_iota(jnp.int32, (T, T), 1)
    tri = jnp.concatenate([(ui <= si).astype(bf16), (ui >= si).astype(bf16)], axis=1)
    valid_f = ui <= si
    valid_b = ui >= si

    def cs_body(j, carry):
        r0 = pl.multiple_of(j * T, T)
        rows = rw_s[:, pl.ds(r0, T)]
        hi, mid, lo = _split3(rows)
        cs3 = _dot(jnp.concatenate([hi, mid, lo, jnp.zeros_like(hi)], axis=0), tri)
        cs = cs3[0:8] + cs3[8:16] + cs3[16:24]
        b_f = cs[0:1, 0:T]
        b_b = cs[2:3, T:2 * T]
        rw_s[:, pl.ds(r0, T)] = jnp.concatenate([b_f, rows[1:2], b_b, rows[3:8]], axis=0)
        ucb_s[0, pl.ds(r0, T), :] = jnp.broadcast_to(rows[1:2] - b_f, (T, T)).T
        ucb_s[1, pl.ds(r0, T), :] = jnp.broadcast_to(rows[3:4] - b_b, (T, T)).T
        return carry
    lax.fori_loop(0, nc, cs_body, 0, unroll=min(8, nc))

    for d in range(2):
        ct_s[d] = jnp.concatenate([c0_ref[0, 0, d, 0].T, n0_ref[0, 0, d, 0],
                                   jnp.zeros((ML_EXT - D_HEAD - 1, D_HEAD), f32)], axis=0)
        m_s[d] = jnp.full((1, LANES), m0_ref[b_idx * 2 * N_HEADS + d * N_HEADS + h], f32)

    def step(d, j):
        r0 = pl.multiple_of(j * T, T)
        b_row = rw_s[pl.ds(2 * d, 1), pl.ds(r0, T)]
        b_end = b_row[:, T - 1:T] if d == 0 else b_row[:, 0:1]
        b_end_l = jnp.broadcast_to(b_end, (1, LANES))
        ucb = ucb_s[d, pl.ds(r0, T), :]

        q = q_ref[0, pl.ds(r0, T), :].astype(f32) * scale
        kb = k_ref[0, pl.ds(r0, T), :]
        k = kb.astype(f32)
        qb = q.astype(bf16)
        vt = vt_s[:, pl.ds(r0, T)]
        ct = ct_s[d]
        m_prev = m_s[d]

        d_t = jnp.where(valid_f if d == 0 else valid_b, ucb + b_row, NEG_BIG)
        m_state = b_row + m_prev
        m_t = jnp.maximum(m_state, jnp.max(d_t, axis=0, keepdims=True))
        w_state = jnp.exp(m_state - m_t)
        s_t = _dot_nt(kb, qb) * jnp.exp(d_t - m_t)
        tot = _dot(vt, s_t.astype(bf16)) + w_state * _dot_nt(ct.astype(bf16), qb)
        den = tot[D_HEAD:D_HEAD + 1]
        ht_s[d, :, pl.ds(r0, T)] = tot[:D_HEAD] * (1.0 / jnp.maximum(jnp.abs(den), jnp.exp(-m_t)))

        gcb = ucb + b_end_l
        m_new = jnp.maximum(b_end_l + m_prev, jnp.max(gcb, axis=0, keepdims=True))
        kw = k * jnp.exp(gcb - m_new)
        decay = jnp.exp(b_end_l + m_prev - m_new)
        ct_s[d] = decay * ct + _dot(vt, kw.astype(bf16))
        m_s[d] = m_new

    def loop_body(j, carry):
        step(0, j)
        step(1, nc - 1 - j)
        return carry
    lax.fori_loop(0, nc, loop_body, 0, unroll=min(4, nc))

    ng = ng_ref[...]

    def out_body(j, carry):
        r0 = pl.multiple_of(j * T, T)
        y_t = ht_s[0, :, pl.ds(r0, T)] + ht_s[1, :, pl.ds(r0, T)]
        ms = jnp.mean(y_t * y_t, axis=0, keepdims=True)
        y = (y_t * lax.rsqrt(ms + EPS)).T * ng * _sigmoid(mo_ref[0, pl.ds(r0, T), :].astype(f32))
        y_ref[0, pl.ds(r0, T), :] = y.astype(y_ref.dtype)
        return carry
    lax.fori_loop(0, nc, out_body, 0, unroll=2)

    for d in range(2):
        cout_ref[0, d, 0] = ct_s[d, 0:D_HEAD, :].T
        nout_ref[0, d, 0] = ct_s[d, D_HEAD:D_HEAD + 1, :]
        mout_ref[0, d, 0] = m_s[d]


def mlstm_call(proj, proj32, gate_bias, norm_g, c0, n0, m0, layer):
    B, L, _ = proj.shape

    def col(cb):
        return pl.BlockSpec((1, L, LANES), lambda b, h, cb=cb: (b, 0, cb + h))

    smem = pl.BlockSpec(memory_space=pltpu.SMEM)
    kern = functools.partial(_mlstm_kernel, seq_len=L)
    return pl.pallas_call(
        kern,
        grid=(B, N_HEADS),
        in_specs=[smem, smem,
                  col(CB_MQ), col(CB_MK), col(CB_MV), col(CB_MO),
                  pl.BlockSpec((1, L, LANES), lambda b, h: (b, 0, CF_MG)),
                  pl.BlockSpec((1, LANES), lambda b, h: (0, h)),
                  pl.BlockSpec((1, 1, 2, 1, D_HEAD, D_HEAD), lambda b, h: (b, layer, 0, h, 0, 0)),
                  pl.BlockSpec((1, 1, 2, 1, 1, D_HEAD), lambda b, h: (b, layer, 0, h, 0, 0))],
        out_specs=[pl.BlockSpec((1, L, LANES), lambda b, h: (b, 0, h)),
                   pl.BlockSpec((1, 2, 1, D_HEAD, D_HEAD), lambda b, h: (b, 0, h, 0, 0)),
                   pl.BlockSpec((1, 2, 1, 1, D_HEAD), lambda b, h: (b, 0, h, 0, 0)),
                   pl.BlockSpec((1, 2, 1, 1, LANES), lambda b, h: (b, 0, h, 0, 0))],
        out_shape=[jax.ShapeDtypeStruct((B, L, BRANCH_W), bf16),
                   jax.ShapeDtypeStruct((B, 2, N_HEADS, D_HEAD, D_HEAD), f32),
                   jax.ShapeDtypeStruct((B, 2, N_HEADS, 1, D_HEAD), f32),
                   jax.ShapeDtypeStruct((B, 2, N_HEADS, 1, LANES), f32)],
        scratch_shapes=[pltpu.VMEM((LANES, L), f32),
                        pltpu.VMEM((8, L), f32),
                        pltpu.VMEM((2, L, LANES), f32),
                        pltpu.VMEM((ML_EXT, L), bf16),
                        pltpu.VMEM((2, D_HEAD, L), f32),
                        pltpu.VMEM((2, ML_EXT, D_HEAD), f32),
                        pltpu.VMEM((2, 1, LANES), f32)],
        compiler_params=_cparams(("arbitrary", "arbitrary")),
        name="mlstm",
    )(gate_bias, m0, proj, proj, proj, proj, proj32, norm_g, c0, n0)


def _hgrn_kernel(q_ref, f0_ref, f1_ref, i_ref, og_ref, lb_ref, ng_ref, s0_ref,
                 y_ref, sout_ref, o_s, st_s, qx_s, dec_s, u_s, vt_s, k_s, b_s, q32_s, v32_s, *, seq_len):
    T = HG_CHUNK
    nc = seq_len // T
    h = pl.program_id(1)
    f_refs = (f0_ref, f1_ref)

    lbs, loglb, log1mlb = [], [], []
    for d in range(2):
        lb = lb_ref[pl.ds(d * N_HEADS + h, 1), :]
        lbs.append(lb)
        loglb.append(jnp.log(lb))
        log1mlb.append(jnp.log1p(-lb))
        st_s[d] = s0_ref[0, 0, d, 0].T

    ti = lax.broadcasted_iota(jnp.int32, (T, T), 0)
    ui = lax.broadcasted_iota(jnp.int32, (T, T), 1)
    tri = ((ui <= ti).astype(bf16), (ui >= ti).astype(bf16))
    sub_t = lax.broadcasted_iota(jnp.int32, (HG_SUB, D_HEAD), 0)
    sub_q = sub_t % HG_HALF
    low_half = sub_t < HG_HALF
    row_i = lax.broadcasted_iota(jnp.int32, (T, D_HEAD), 0)
    tt_r = lax.broadcasted_iota(jnp.int32, (T, T), 0)
    tt_c = lax.broadcasted_iota(jnp.int32, (T, T), 1)
    q_rows = {(hh, dd): ((row_i % (2 * hh)) >= hh) if dd == 0 else ((row_i % (2 * hh)) < hh)
              for hh in HG_LEVELS for dd in range(2)}
    same_blk = {hh: (tt_r // (2 * hh)) == (tt_c // (2 * hh)) for hh in HG_LEVELS}

    def gates(d, j, q):
        r0 = pl.multiple_of(j * T, T)
        hf = f_refs[d][0, pl.ds(r0, T), :]
        e = jnp.exp(-jnp.abs(hf))
        sig_neg = jnp.where(hf >= 0.0, e, 1.0) / (1.0 + e)
        logsig = jnp.minimum(hf, 0.0) - jnp.log(1.0 + e)
        cterm = log1mlb[d] + logsig
        amax = jnp.maximum(loglb[d], cterm)
        logf = amax + jnp.log(1.0 + jnp.exp(-jnp.abs(loglb[d] - cterm)))
        k = (1.0 - lbs[d]) * sig_neg
        hi, mid, lo = _split3(logf * LOG2E)
        b = _dot(tri[d], hi) + _dot(tri[d], mid) + _dot(tri[d], lo)
        b_end = b[T - 1:T] if d == 0 else b[0:1]
        k_s[d, pl.ds(r0, T), :] = k
        b_s[d, pl.ds(r0, T), :] = b
        qx_s[d, pl.ds(r0, T), :] = (q * jnp.exp2(b)).astype(bf16)
        dec_s[d, pl.ds(j, 1), :] = jnp.exp2(b_end)

    def gates_body(j, carry):
        r0 = pl.multiple_of(j * T, T)
        q = q_ref[0, pl.ds(r0, T), :].astype(f32)
        v = i_ref[0, pl.ds(r0, T), :].astype(f32)
        q32_s[pl.ds(r0, T), :] = q
        v32_s[pl.ds(r0, T), :] = v
        vt_s[j] = v.T.astype(bf16)
        gates(0, j, q)
        gates(1, j, q)
        return carry
    lax.fori_loop(0, nc, gates_body, 0, unroll=2)

    def intra(d, j):
        r0 = pl.multiple_of(j * T, T)

        def rows(ref2, lo_r, n):
            return ref2[pl.ds(r0 + lo_r, n), :]

        def srows(ref3, lo_r, n):
            return ref3[d, pl.ds(r0 + lo_r, n), :]

        nsub = T // HG_SUB
        o_blk = [None] * nsub

        def add(idx, val):
            o_blk[idx] = val if o_blk[idx] is None else o_blk[idx] + val

        def levels():
            b_all = srows(b_s, 0, T)
            q_all = rows(q32_s, 0, T)
            k_all = srows(k_s, 0, T)
            a_tot = None
            for h in HG_LEVELS:
                blk = 2 * h
                ref_off = h - 1 if d == 0 else h
                ref = jnp.concatenate([jnp.broadcast_to(srows(b_s, base + ref_off, 1), (blk, D_HEAD))
                                       for base in range(0, T, blk)], axis=0)
                is_q = q_rows[(h, d)]
                qx = q_all * jnp.exp2(jnp.where(is_q, b_all - ref, NEG_BIG))
                kx = k_all * jnp.exp2(jnp.where(is_q, NEG_BIG, ref - b_all))
                a = _dot_nt(qx.astype(bf16), kx.astype(bf16))
                if blk < T:
                    a = jnp.where(same_blk[h], a, 0.0)
                a_tot = a if a_tot is None else a_tot + a
            ov = _dot(a_tot.astype(bf16), rows(v32_s, 0, T).astype(bf16))
            for p in range(nsub):
                add(p, ov[p * HG_SUB:(p + 1) * HG_SUB])

        def diag(lo_r):
            bs = srows(b_s, lo_r, HG_SUB)
            qs = rows(q32_s, lo_r, HG_SUB)

            def keyrow(read, s):
                return jnp.where(low_half, read(lo_r + s, 1), read(lo_r + HG_HALF + s, 1))

            acc = None
            for s in range(HG_HALF):
                valid = (sub_q >= s) if d == 0 else (sub_q <= s)
                b_key = keyrow(lambda r, n: srows(b_s, r, n), s)
                ee = jnp.exp2(jnp.where(valid, bs - b_key, NEG_BIG))
                a_col = jnp.sum(qs * ee * keyrow(lambda r, n: srows(k_s, r, n), s), axis=-1, keepdims=True)
                term = a_col * keyrow(lambda r, n: rows(v32_s, r, n), s)
                acc = term if acc is None else acc + term
            add(lo_r // HG_SUB, acc)

        levels()
        for i in range(nsub):
            diag(i * HG_SUB)
        o_s[d, pl.ds(r0, T), :] = jnp.concatenate(o_blk, axis=0)

    def intra_body(j, carry):
        intra(0, j)
        intra(1, j)
        return carry
    lax.fori_loop(0, nc, intra_body, 0, unroll=4)

    def incr_body(j, carry):
        r0 = pl.multiple_of(j * T, T)
        vt = vt_s[j]
        for d in range(2):
            b = b_s[d, pl.ds(r0, T), :]
            b_end = b[T - 1:T] if d == 0 else b[0:1]
            u_s[d, j] = _dot(vt, (k_s[d, pl.ds(r0, T), :] * jnp.exp2(b_end - b)).astype(bf16))
        return carry
    lax.fori_loop(0, nc, incr_body, 0, unroll=4)

    def state_step(d, j):
        r0 = pl.multiple_of(j * T, T)
        st = st_s[d]
        o_s[d, pl.ds(r0, T), :] = o_s[d, pl.ds(r0, T), :] + _dot_nt(qx_s[d, pl.ds(r0, T), :], st.astype(bf16))
        st_s[d] = st * dec_s[d, pl.ds(j, 1), :] + u_s[d, j]

    def state_body(j, carry):
        state_step(0, j)
        state_step(1, nc - 1 - j)
        return carry
    lax.fori_loop(0, nc, state_body, 0, unroll=4)

    ng = ng_ref[...]

    def out_body(j, carry):
        r0 = pl.multiple_of(j * T, T)
        y = o_s[0, pl.ds(r0, T), :] + o_s[1, pl.ds(r0, T), :]
        y = _rms_rows(y, ng) * _silu(og_ref[0, pl.ds(r0, T), :].astype(f32))
        y_ref[0, pl.ds(r0, T), :] = y.astype(y_ref.dtype)
        return carry
    lax.fori_loop(0, nc, out_body, 0, unroll=4)

    for d in range(2):
        sout_ref[0, d, 0] = st_s[d].T


def hgrn_call(proj, proj32, lb_rows, norm_g, s0, layer):
    B, L, _ = proj.shape

    def col(cb):
        return pl.BlockSpec((1, L, LANES), lambda b, h, cb=cb: (b, 0, cb + h))

    kern = functools.partial(_hgrn_kernel, seq_len=L)
    return pl.pallas_call(
        kern,
        grid=(B, N_HEADS),
        in_specs=[col(CB_HQ), col(CF_HF), col(CF_HF + N_HEADS), col(CB_HI), col(CB_HG),
                  pl.BlockSpec((2 * N_HEADS, D_HEAD), lambda b, h: (0, 0)),
                  pl.BlockSpec((1, LANES), lambda b, h: (0, h)),
                  pl.BlockSpec((1, 1, 2, 1, D_HEAD, D_HEAD), lambda b, h: (b, layer, 0, h, 0, 0))],
        out_specs=[pl.BlockSpec((1, L, LANES), lambda b, h: (b, 0, h)),
                   pl.BlockSpec((1, 2, 1, D_HEAD, D_HEAD), lambda b, h: (b, 0, h, 0, 0))],
        out_shape=[jax.ShapeDtypeStruct((B, L, BRANCH_W), bf16),
                   jax.ShapeDtypeStruct((B, 2, N_HEADS, D_HEAD, D_HEAD), f32)],
        scratch_shapes=[pltpu.VMEM((2, L, D_HEAD), f32),
                        pltpu.VMEM((2, D_HEAD, D_HEAD), f32),
                        pltpu.VMEM((2, L, D_HEAD), bf16),
                        pltpu.VMEM((2, L // HG_CHUNK, D_HEAD), f32),
                        pltpu.VMEM((2, L // HG_CHUNK, D_HEAD, D_HEAD), f32),
                        pltpu.VMEM((L // HG_CHUNK, D_HEAD, HG_CHUNK), bf16),
                        pltpu.VMEM((2, L, D_HEAD), f32),
                        pltpu.VMEM((2, L, D_HEAD), f32),
                        pltpu.VMEM((L, D_HEAD), f32),
                        pltpu.VMEM((L, D_HEAD), f32)],
        compiler_params=_cparams(("arbitrary", "arbitrary")),
        name="hgrn",
    )(proj, proj32, proj32, proj, proj, lb_rows, norm_g, s0)


def _pair_swap(x):
    lane = lax.broadcasted_iota(jnp.int32, x.shape, 1)
    return jnp.where((lane % 2) == 0, pltpu.roll(x, LANES - 1, 1), pltpu.roll(x, 1, 1))


def _head_norm_pair(x, g_row):
    lane = lax.broadcasted_iota(jnp.int32, x.shape, 1)
    low = lane < AT_HD
    sq = x * x
    s_lo = jnp.sum(jnp.where(low, sq, 0.0), axis=-1, keepdims=True)
    s_hi = jnp.sum(jnp.where(low, 0.0, sq), axis=-1, keepdims=True)
    ms = jnp.where(low, s_lo, s_hi) * (1.0 / AT_HD)
    return x * lax.rsqrt(ms + EPS) * g_row


def _qkv_prep_kernel(aq_ref, ak_ref, av_ref, qg_ref, kg_ref, cos_ref, sin_ref,
                     q_ref, k_ref, v_ref, kn_ref, *, use_rope):
    qg = qg_ref[...]
    kg = kg_ref[...]
    if use_rope:
        cos = cos_ref[...]
        sin = sin_ref[...]
    for p in range(AT_HEADS // 2):
        x = _head_norm_pair(aq_ref[0, :, p * LANES:(p + 1) * LANES].astype(f32), qg)
        if use_rope:
            x = x * cos + _pair_swap(x) * sin
        x = (x * (AT_HD ** -0.5)).astype(bf16)
        q_ref[0, 2 * p] = x[:, :AT_HD]
        q_ref[0, 2 * p + 1] = x[:, AT_HD:]
    kn = _head_norm_pair(ak_ref[0].astype(f32), kg)
    kn_ref[0] = kn
    if use_rope:
        kn = kn * cos + _pair_swap(kn) * sin
    kb = kn.astype(bf16)
    vb = av_ref[0]
    for g in range(AT_KV_HEADS):
        k_ref[0, g] = kb[:, g * AT_HD:(g + 1) * AT_HD]
        v_ref[0, g] = vb[:, g * AT_HD:(g + 1) * AT_HD]


def qkv_prep_call(proj, qg_row, kg_row, cos_t, sin_t, use_rope):
    B, L, _ = proj.shape
    tm = min(L, 512)
    kern = functools.partial(_qkv_prep_kernel, use_rope=use_rope)
    return pl.pallas_call(
        kern,
        grid=(B, L // tm),
        in_specs=[pl.BlockSpec((1, tm, 4 * LANES), lambda b, m: (b, m, CB_AQ // 4)),
                  pl.BlockSpec((1, tm, LANES), lambda b, m: (b, m, CB_AK)),
                  pl.BlockSpec((1, tm, LANES), lambda b, m: (b, m, CB_AV)),
                  pl.BlockSpec((1, LANES), lambda b, m: (0, 0)),
                  pl.BlockSpec((1, LANES), lambda b, m: (0, 0)),
                  pl.BlockSpec((tm, LANES), lambda b, m: (m, 0)),
                  pl.BlockSpec((tm, LANES), lambda b, m: (m, 0))],
        out_specs=[pl.BlockSpec((1, AT_HEADS, tm, AT_HD), lambda b, m: (b, 0, m, 0)),
                   pl.BlockSpec((1, AT_KV_HEADS, tm, AT_HD), lambda b, m: (b, 0, m, 0)),
                   pl.BlockSpec((1, AT_KV_HEADS, tm, AT_HD), lambda b, m: (b, 0, m, 0)),
                   pl.BlockSpec((1, tm, LANES), lambda b, m: (b, m, 0))],
        out_shape=[jax.ShapeDtypeStruct((B, AT_HEADS, L, AT_HD), bf16),
                   jax.ShapeDtypeStruct((B, AT_KV_HEADS, L, AT_HD), bf16),
                   jax.ShapeDtypeStruct((B, AT_KV_HEADS, L, AT_HD), bf16),
                   jax.ShapeDtypeStruct((B, L, LANES), f32)],
        compiler_params=_cparams(("arbitrary", "arbitrary")),
        name="qkv_prep",
    )(proj, proj, proj, qg_row, kg_row, cos_t, sin_t)


def _attn_kernel(q_ref, k_ref, v_ref, o_ref, s_s, m_s, *, tq, tk, nkv):
    rows = AT_GROUP * tq
    ncb = tk // LANES
    always = pl.program_id(2) >= 0

    @pl.when(always)
    def _():
        q = q_ref[0].reshape(rows, AT_HD)
        mx = None
        for j in range(nkv):
            s = _dot_nt(q, k_ref[0, 0, j * tk:(j + 1) * tk, :])
            s_s[:, j * tk:(j + 1) * tk] = s
            cm = s[:, 0:LANES]
            for cb in range(1, ncb):
                cm = jnp.maximum(cm, s[:, cb * LANES:(cb + 1) * LANES])
            mx = cm if mx is None else jnp.maximum(mx, cm)
        m_s[...] = jnp.broadcast_to(jnp.max(mx, axis=-1, keepdims=True), (rows, LANES))

    @pl.when(pl.program_id(1) >= 0)
    def _():
        m_b = m_s[...]
        acc = None
        for j in range(nkv):
            ps = []
            for cb in range(ncb):
                off = j * tk + cb * LANES
                ps.append(jnp.exp(s_s[:, off:off + LANES] - m_b).astype(bf16))
            pv = _dot(jnp.concatenate(ps, axis=1), v_ref[0, 0, j * tk:(j + 1) * tk, :])
            acc = pv if acc is None else acc + pv
        o = acc[:, :AT_HD] * (1.0 / acc[:, AT_HD:AT_HD + 1])
        for g in range(AT_GROUP):
            o_ref[0, :, g * AT_HD:(g + 1) * AT_HD] = o[g * tq:(g + 1) * tq].astype(o_ref.dtype)


def attn_call(q, k, v):
    B, _, Lq, _ = q.shape
    Lk = k.shape[2]
    tq = 256
    tk = 512 if Lk % 512 == 0 else Lk
    nkv = Lk // tk
    nq = Lq // tq
    kern = functools.partial(_attn_kernel, tq=tq, tk=tk, nkv=nkv)
    rows = AT_GROUP * tq
    return pl.pallas_call(
        kern,
        grid=(B, AT_KV_HEADS, nq),
        in_specs=[pl.BlockSpec((1, AT_GROUP, tq, AT_HD), lambda b, g, i: (b, g, i, 0)),
                  pl.BlockSpec((1, 1, Lk, AT_HD), lambda b, g, i: (b, g, 0, 0)),
                  pl.BlockSpec((1, 1, Lk, LANES), lambda b, g, i: (b, g, 0, 0))],
        out_specs=pl.BlockSpec((1, tq, AT_GROUP * AT_HD), lambda b, g, i: (b, i, g)),
        out_shape=jax.ShapeDtypeStruct((B, Lq, AT_HEADS * AT_HD), bf16),
        scratch_shapes=[pltpu.VMEM((rows, Lk), f32),
                        pltpu.VMEM((rows, LANES), f32)],
        compiler_params=_cparams(("arbitrary", "arbitrary", "arbitrary")),
        name="attention",
    )(q, k, v)


def _with_ones_column(v):
    ones = jnp.ones(v.shape[:-1] + (1,), v.dtype)
    zeros = jnp.zeros(v.shape[:-1] + (LANES - AT_HD - 1,), v.dtype)
    return jnp.concatenate([v, ones, zeros], axis=-1)


def _merge_kernel(x_ref, yml_ref, yhg_ref, yat_ref, bg_ref, mod_ref, wb_ref, wo_ref, o_ref):
    merged = None
    for n, y_ref in enumerate((yml_ref, yhg_ref, yat_ref)):
        pn = _dot(y_ref[0], wb_ref[n])
        gn = _sigmoid(bg_ref[0, :, n * D_MODEL:(n + 1) * D_MODEL].astype(f32))
        merged = gn * pn if merged is None else merged + gn * pn
    out = _dot(merged.astype(bf16), wo_ref[...])
    o_ref[0] = x_ref[0] + mod_ref[0, 2:3, :] * out


def merge_call(x, y_ml, y_hg, y_at, proj, mod, wb, wo, shared_mod):
    B, L, _ = x.shape
    tm = min(L, 256)
    mod_map = (lambda b, m: (0, 0, 0)) if shared_mod else (lambda b, m: (b, 0, 0))
    yspec = pl.BlockSpec((1, tm, BRANCH_W), lambda b, m: (b, m, 0))
    return pl.pallas_call(
        _merge_kernel,
        grid=(B, L // tm),
        in_specs=[pl.BlockSpec((1, tm, D_MODEL), lambda b, m: (b, m, 0)),
                  yspec, yspec, yspec,
                  pl.BlockSpec((1, tm, 3 * D_MODEL), lambda b, m: (b, m, CB_BG)),
                  pl.BlockSpec((1, 8, D_MODEL), mod_map),
                  pl.BlockSpec((3, BRANCH_W, D_MODEL), lambda b, m: (0, 0, 0)),
                  pl.BlockSpec((D_MODEL, D_MODEL), lambda b, m: (0, 0))],
        out_specs=pl.BlockSpec((1, tm, D_MODEL), lambda b, m: (b, m, 0)),
        out_shape=jax.ShapeDtypeStruct((B, L, D_MODEL), f32),
        compiler_params=_cparams(("arbitrary", "arbitrary")),
        name="merge_out",
    )(x, y_ml, y_hg, y_at, proj, mod, wb, wo)


FFN_SPLITS = ((0, 1536), (1536, FFN_HIDDEN))


def _ffn_kernel(x_ref, mod_ref, g_ref, wi_ref, wd_ref, fg_ref, o_ref, *, final_norm):
    x = x_ref[0]
    hb = (_rms_rows(x, g_ref[...]) * (1.0 + mod_ref[0, 4:5, :]) + mod_ref[0, 3:4, :]).astype(bf16)
    acc = None
    for lo, hi in FFN_SPLITS:
        gate = _dot(hb, wi_ref[:, lo:hi])
        up = _dot(hb, wi_ref[:, FFN_HIDDEN + lo:FFN_HIDDEN + hi])
        part = _dot((_silu(gate) * up).astype(bf16), wd_ref[lo:hi, :])
        acc = part if acc is None else acc + part
    y = x + mod_ref[0, 5:6, :] * acc
    if final_norm:
        y = _rms_rows(y, fg_ref[...])
    o_ref[0] = y


def ffn_call(x, mod, g_row, w_in, w_out, fg_row, shared_mod, final_norm):
    B, L, _ = x.shape
    tm = min(L, 512)
    mod_map = (lambda b, m: (0, 0, 0)) if shared_mod else (lambda b, m: (b, 0, 0))
    kern = functools.partial(_ffn_kernel, final_norm=final_norm)
    resident = pl.Buffered(1)
    return pl.pallas_call(
        kern,
        grid=(B, L // tm),
        in_specs=[pl.BlockSpec((1, tm, D_MODEL), lambda b, m: (b, m, 0)),
                  pl.BlockSpec((1, 8, D_MODEL), mod_map),
                  pl.BlockSpec((1, D_MODEL), lambda b, m: (0, 0)),
                  pl.BlockSpec((D_MODEL, 2 * FFN_HIDDEN), lambda b, m: (0, 0), pipeline_mode=resident),
                  pl.BlockSpec((FFN_HIDDEN, D_MODEL), lambda b, m: (0, 0), pipeline_mode=resident),
                  pl.BlockSpec((1, D_MODEL), lambda b, m: (0, 0))],
        out_specs=pl.BlockSpec((1, tm, D_MODEL), lambda b, m: (b, m, 0)),
        out_shape=jax.ShapeDtypeStruct((B, L, D_MODEL), f32),
        compiler_params=_cparams(("arbitrary", "arbitrary")),
        name="ffn",
    )(x, mod, g_row, w_in, w_out, fg_row)


def _reorder_w_in(w):
    o_mg, o_hq, o_hf, o_hi, o_bg, o_end = 2048, 2064, 2576, 3600, 5392, 8464
    d = w.shape[0]
    sec16 = [w[:, o_bg:o_end], w[:, :o_mg], w[:, o_hq:o_hf], w[:, o_hi:o_bg]]
    n16 = (o_end - o_bg) + o_mg + (o_hf - o_hq) + (o_bg - o_hi)
    sec32 = [w[:, o_hf:o_hi], w[:, o_mg:o_hq]]
    n32 = (o_hi - o_hf) + (o_hq - o_mg)
    return jnp.concatenate(sec16 + [jnp.zeros((d, N16_COLS - n16), w.dtype)] + sec32
                           + [jnp.zeros((d, N32_COLS - n32), w.dtype)], axis=1).astype(bf16)


def _rope_tables(L):
    rows = L // GRID_W
    row = jnp.repeat(jnp.arange(rows, dtype=f32), GRID_W)
    colp = jnp.tile(jnp.arange(GRID_W, dtype=f32), rows)
    n_freq = AT_HD // 4
    inv = ROPE_THETA ** (-jnp.arange(n_freq, dtype=f32) / n_freq)
    ang = jnp.concatenate([row[:, None] * inv, colp[:, None] * inv], axis=-1)
    cos = jnp.repeat(jnp.cos(ang), 2, axis=-1)
    sin = jnp.repeat(jnp.sin(ang), 2, axis=-1)
    sign = jnp.tile(jnp.array([-1.0, 1.0], f32), AT_HD // 2)
    sin = sin * sign
    return jnp.tile(cos, (1, 2)), jnp.tile(sin, (1, 2))


def _mod_rows(ada_rows):
    r = ada_rows.shape[0]
    m = ada_rows.reshape(r, 6, D_MODEL)
    return jnp.concatenate([m, jnp.zeros((r, 2, D_MODEL), f32)], axis=1)


def kernel(x_prompt, x_sample, c, cache_k, cache_v, state_ml_C, state_ml_n, state_ml_m, state_hg_S, c_ctx, w_ada, b_ada, norm1_g, norm2_g, w_in, ml_gate_b, ml_norm_g, hg_lb_logits, hg_norm_g, q_norm_g, k_norm_g, w_branch, w_out, w_ffn_in, w_ffn_out, final_g):
    Bp, Lp, _ = x_prompt.shape
    Bs, Ls, _ = x_sample.shape

    n_rows = 16
    cvec = jnp.concatenate([c_ctx[None, :], c, jnp.zeros((n_rows - 1 - Bs, D_MODEL), f32)], axis=0)
    ada = ada_call(cvec, w_ada, b_ada)
    lb_all = lb_call(hg_lb_logits)
    cos_t, sin_t = _rope_tables(Ls)

    zeros_c = jnp.zeros((Bp, 1, 2, N_HEADS, D_HEAD, D_HEAD), f32)
    zeros_n = jnp.zeros((Bp, 1, 2, N_HEADS, 1, D_HEAD), f32)
    zeros_m = jnp.zeros((Bp * 2 * N_HEADS,), f32)
    n_state = state_ml_n.reshape(Bs, DEPTH, 2, N_HEADS, 1, D_HEAD)

    xp, xs = x_prompt, x_sample
    nk, nv, nC, nn_, nm, nS = [], [], [], [], [], []
    for l in range(DEPTH):
        w_l = _reorder_w_in(w_in[l])
        wb_l = w_branch[l].astype(bf16)
        wo_l = w_out[l].astype(bf16)
        wfi_l = w_ffn_in[l].astype(bf16)
        wfo_l = w_ffn_out[l].astype(bf16)
        g1 = norm1_g[l][None, :]
        g2 = norm2_g[l][None, :]
        mlg = ml_norm_g[l][None, :]
        hgg = hg_norm_g[l][None, :]
        qg = jnp.tile(q_norm_g[l], 2)[None, :]
        kg = jnp.tile(k_norm_g[l], 2)[None, :]
        gate_b = ml_gate_b[l].reshape(-1).astype(f32)
        fg = final_g[None, :]
        last = l == DEPTH - 1

        mod_ctx = _mod_rows(ada[l, 0:1])
        mod_lat = _mod_rows(ada[l, 1:1 + Bs])

        proj, proj32 = in_proj_call(xp, mod_ctx, g1, w_l, True)
        y_ml, c_f, n_f, m_f = mlstm_call(proj, proj32, gate_b, mlg, zeros_c, zeros_n, zeros_m, 0)
        y_hg, s_f = hgrn_call(proj, proj32, lb_all[l], hgg, zeros_c, 0)
        q_h, k_h, v_h, k_n = qkv_prep_call(proj, qg, kg, cos_t[:Lp], sin_t[:Lp], False)
        y_at = attn_call(q_h, k_h, _with_ones_column(v_h))
        xp = merge_call(xp, y_ml, y_hg, y_at, proj, mod_ctx, wb_l, wo_l, True)
        xp = ffn_call(xp, mod_ctx, g2, wfi_l, wfo_l, fg, True, last)
        nk.append(k_n.reshape(Bp, Lp, AT_KV_HEADS, AT_HD))
        nv.append(proj[:, :, CB_AV * LANES:(CB_AV + 1) * LANES].astype(f32).reshape(Bp, Lp, AT_KV_HEADS, AT_HD))
        nC.append(c_f)
        nn_.append(n_f[:, :, :, 0, :])
        nm.append(m_f[:, :, :, 0, 0])
        nS.append(s_f)

        proj, proj32 = in_proj_call(xs, mod_lat, g1, w_l, False)
        y_ml, _, _, _ = mlstm_call(proj, proj32, gate_b, mlg, state_ml_C, n_state,
                                   state_ml_m[:, l].reshape(-1).astype(f32), l)
        y_hg, _ = hgrn_call(proj, proj32, lb_all[l], hgg, state_hg_S, l)
        q_h, k_h, v_h, _ = qkv_prep_call(proj, qg, kg, cos_t, sin_t, True)
        k_all = jnp.concatenate([k_h, jnp.swapaxes(cache_k[:, l], 1, 2).astype(bf16)], axis=2)
        v_all = jnp.concatenate([v_h, jnp.swapaxes(cache_v[:, l], 1, 2).astype(bf16)], axis=2)
        y_at = attn_call(q_h, k_all, _with_ones_column(v_all))
        xs = merge_call(xs, y_ml, y_hg, y_at, proj, mod_lat, wb_l, wo_l, False)
        xs = ffn_call(xs, mod_lat, g2, wfi_l, wfo_l, fg, False, last)

    return (xp, xs, jnp.stack(nk, axis=1), jnp.stack(nv, axis=1), jnp.stack(nC, axis=1),
            jnp.stack(nn_, axis=1), jnp.stack(nm, axis=1), jnp.stack(nS, axis=1))
```

```python
import functools
import math

import jax
import jax.numpy as jnp
from jax import lax
from jax.experimental import pallas as pl
from jax.experimental.pallas import tpu as pltpu

f32 = jnp.float32
bf16 = jnp.bfloat16

D_MODEL = 1024
DEPTH = 4
N_HEADS = 4
D_HEAD = 128
AT_HEADS = 8
AT_KV_HEADS = 2
AT_GROUP = AT_HEADS // AT_KV_HEADS
AT_HD = 64
BRANCH_W = 512
FFN_HIDDEN = 2816
GRID_W = 64
ROPE_THETA = 10000.0
EPS = 1e-6

LANES = 128
SUBLANES = 8
VMEM_LIMIT = 52 * 1024 * 1024

CB_BG = 0
CB_MQ = 24
CB_MK = 28
CB_MV = 32
CB_MO = 36
CB_HQ = 40
CB_HI = 44
CB_HG = 48
CB_AQ = 52
CB_AK = 56
CB_AV = 57
N16_BLOCKS = 60
N16_COLS = N16_BLOCKS * LANES
CF_HF = 0
CF_MG = 8
N32_BLOCKS = 10
N32_COLS = N32_BLOCKS * LANES
PROJ_TN = N32_COLS
N16_TILES = N16_COLS // PROJ_TN

ML_CHUNK = 128
ML_EXT = 144
HG_CHUNK = 64
HG_SUB = 8
HG_HALF = 4
HG_LEVELS = (32, 16, 8, 4)
NEG_BIG = -1e30
LOG2E = 1.4426950408889634


def _cparams(sem):
    return pltpu.CompilerParams(dimension_semantics=sem, vmem_limit_bytes=VMEM_LIMIT)


def _dot(a, b):
    return jnp.dot(a, b, preferred_element_type=f32)


def _dot_nt(a, b):
    return lax.dot_general(a, b, (((1,), (1,)), ((), ())), preferred_element_type=f32)


def _dot_tn(a, b):
    return lax.dot_general(a, b, (((0,), (0,)), ((), ())), preferred_element_type=f32)


def _sigmoid(x):
    return 1.0 / (1.0 + jnp.exp(-x))


def _silu(x):
    return x * _sigmoid(x)


def _log_sigmoid(x):
    return jnp.minimum(x, 0.0) - jnp.log1p(jnp.exp(-jnp.abs(x)))


def _split3(x):
    hi = x.astype(bf16)
    r1 = x - hi.astype(f32)
    mid = r1.astype(bf16)
    lo = (r1 - mid.astype(f32)).astype(bf16)
    return hi, mid, lo


def _rms_rows(x, g_row):
    ms = jnp.mean(x * x, axis=-1, keepdims=True)
    return x * lax.rsqrt(ms + EPS) * g_row


def _ada_kernel(c_ref, w_ref, b_ref, o_ref):
    s = _silu(c_ref[...]).astype(bf16)
    o_ref[0] = _dot(s, w_ref[0].astype(bf16)) + b_ref[0]


def ada_call(cvec, w_ada, b_ada):
    rows = cvec.shape[0]
    tn = 1536
    n6 = 6 * D_MODEL
    return pl.pallas_call(
        _ada_kernel,
        grid=(DEPTH, n6 // tn),
        in_specs=[pl.BlockSpec((rows, D_MODEL), lambda l, n: (0, 0)),
                  pl.BlockSpec((1, D_MODEL, tn), lambda l, n: (l, 0, n)),
                  pl.BlockSpec((1, 1, tn), lambda l, n: (l, 0, n))],
        out_specs=pl.BlockSpec((1, rows, tn), lambda l, n: (l, 0, n)),
        out_shape=jax.ShapeDtypeStruct((DEPTH, rows, n6), f32),
        compiler_params=_cparams(("arbitrary", "arbitrary")),
        name="ada",
    )(cvec, w_ada, b_ada.reshape(DEPTH, 1, n6))


def _lb_kernel(x_ref, o_ref):
    xs = [x_ref[l] for l in range(DEPTH)]
    mx = xs[0]
    for l in range(1, DEPTH):
        mx = jnp.maximum(mx, xs[l])
    es = [jnp.exp(x - mx) for x in xs]
    tot = es[0]
    for l in range(1, DEPTH):
        tot = tot + es[l]
    sm = [e / tot for e in es]
    run = sm[0]
    o_ref[0] = run - sm[0]
    for l in range(1, DEPTH):
        run = run + sm[l]
        o_ref[l] = run - sm[0]


def lb_call(logits):
    x = logits.astype(f32).reshape(DEPTH, 2 * N_HEADS, D_HEAD)
    return pl.pallas_call(
        _lb_kernel,
        out_shape=jax.ShapeDtypeStruct((DEPTH, 2 * N_HEADS, D_HEAD), f32),
        name="hg_lower_bounds",
    )(x)


def _in_proj_kernel(x_ref, mod_ref, g_ref, w_ref, o16_ref, o32_ref, h_s):
    n = pl.program_id(2)

    @pl.when(n == 0)
    def _():
        x = x_ref[0]
        sh = mod_ref[0, 0:1, :]
        sc = mod_ref[0, 1:2, :]
        h = _rms_rows(x, g_ref[...]) * (1.0 + sc) + sh
        h_s[...] = h.astype(bf16)

    @pl.when(n < N16_TILES)
    def _():
        o16_ref[0] = _dot(h_s[...], w_ref[...]).astype(bf16)

    @pl.when(n == N16_TILES)
    def _():
        o32_ref[0] = _dot(h_s[...], w_ref[...])


def in_proj_call(x, mod, g_row, w, shared_mod):
    B, L, _ = x.shape
    tm = min(L, 1024)
    tn = PROJ_TN
    mod_map = (lambda b, m, n: (0, 0, 0)) if shared_mod else (lambda b, m, n: (b, 0, 0))
    return pl.pallas_call(
        _in_proj_kernel,
        grid=(B, L // tm, N16_TILES + 1),
        in_specs=[pl.BlockSpec((1, tm, D_MODEL), lambda b, m, n: (b, m, 0)),
                  pl.BlockSpec((1, 8, D_MODEL), mod_map),
                  pl.BlockSpec((1, D_MODEL), lambda b, m, n: (0, 0)),
                  pl.BlockSpec((D_MODEL, tn), lambda b, m, n: (0, n))],
        out_specs=[pl.BlockSpec((1, tm, tn), lambda b, m, n: (b, m, jnp.minimum(n, N16_TILES - 1))),
                   pl.BlockSpec((1, tm, tn), lambda b, m, n: (b, m, 0))],
        out_shape=[jax.ShapeDtypeStruct((B, L, N16_COLS), bf16),
                   jax.ShapeDtypeStruct((B, L, N32_COLS), f32)],
        scratch_shapes=[pltpu.VMEM((tm, D_MODEL), bf16)],
        compiler_params=_cparams(("arbitrary", "arbitrary", "arbitrary")),
        name="in_proj",
    )(x, mod, g_row, w)


def _mlstm_kernel(bias_ref, m0_ref, q_ref, k_ref, v_ref, mo_ref, g_ref, ng_ref, c0_ref, n0_ref,
                  y_ref, cout_ref, nout_ref, mout_ref,
                  gt_s, rw_s, ucb_s, vt_s, ht_s, ct_s, m_s, *, seq_len):
    T = ML_CHUNK
    assert T == LANES
    nc = seq_len // T
    b_idx = pl.program_id(0)
    h = pl.program_id(1)
    scale = D_HEAD ** -0.5
    ext_rows = jnp.concatenate([jnp.ones((1, T), f32), jnp.zeros((ML_EXT - D_HEAD - 1, T), f32)], axis=0)

    def tr_body(j, carry):
        r0 = pl.multiple_of(j * T, T)
        gt_s[:, pl.ds(r0, T)] = g_ref[0, pl.ds(r0, T), :].T
        vt = v_ref[0, pl.ds(r0, T), :].astype(f32).T
        vt_s[:, pl.ds(r0, T)] = jnp.concatenate([vt, ext_rows], axis=0).astype(bf16)
        return carry
    lax.fori_loop(0, nc, tr_body, 0, unroll=2)

    i_f = gt_s[pl.ds(h, 1), :] + bias_ref[h]
    f_f = _log_sigmoid(gt_s[pl.ds(4 + h, 1), :] + bias_ref[4 + h])
    i_b = gt_s[pl.ds(8 + h, 1), :] + bias_ref[8 + h]
    f_b = _log_sigmoid(gt_s[pl.ds(12 + h, 1), :] + bias_ref[12 + h])
    zrow = jnp.zeros_like(i_f)
    rw_s[...] = jnp.concatenate([f_f, i_f, f_b, i_b, zrow, zrow, zrow, zrow], axis=0)

    ui = lax.broadcasted_iota(jnp.int32, (T, T), 0)
    si = lax.broadcasted_iota(jnp.int32, (T, T), 1)
    tri = jnp.concatenate([(ui <= si).astype(bf16), (ui >= si).astype(bf16)], axis=1)
    valid_f = ui <= si
    valid_b = ui >= si

    def cs_body(j, carry):
        r0 = pl.multiple_of(j * T, T)
        rows = rw_s[:, pl.ds(r0, T)]
        hi, mid, lo = _split3(rows)
        cs3 = _dot(jnp.concatenate([hi, mid, lo, jnp.zeros_like(hi)], axis=0), tri)
        cs = cs3[0:8] + cs3[8:16] + cs3[16:24]
        b_f = cs[0:1, 0:T]
        b_b = cs[2:3, T:2 * T]
        rw_s[:, pl.ds(r0, T)] = jnp.concatenate([b_f, rows[1:2], b_b, rows[3:8]], axis=0)
        ucb_s[0, pl.ds(r0, T), :] = jnp.broadcast_to(rows[1:2] - b_f, (T, T)).T
        ucb_s[1, pl.ds(r0, T), :] = jnp.broadcast_to(rows[3:4] - b_b, (T, T)).T
        return carry
    lax.fori_loop(0, nc, cs_body, 0, unroll=min(8, nc))

    for d in range(2):
        ct_s[d] = jnp.concatenate([c0_ref[0, 0, d, 0].T, n0_ref[0, 0, d, 0],
                                   jnp.zeros((ML_EXT - D_HEAD - 1, D_HEAD), f32)], axis=0)
        m_s[d] = jnp.full((1, LANES), m0_ref[b_idx * 2 * N_HEADS + d * N_HEADS + h], f32)

    def step(d, j):
        r0 = pl.multiple_of(j * T, T)
        b_row = rw_s[pl.ds(2 * d, 1), pl.ds(r0, T)]
        b_end = b_row[:, T - 1:T] if d == 0 else b_row[:, 0:1]
        b_end_l = jnp.broadcast_to(b_end, (1, LANES))
        ucb = ucb_s[d, pl.ds(r0, T), :]

        q = q_ref[0, pl.ds(r0, T), :].astype(f32) * scale
        kb = k_ref[0, pl.ds(r0, T), :]
        k = kb.astype(f32)
        qb = q.astype(bf16)
        vt = vt_s[:, pl.ds(r0, T)]
        ct = ct_s[d]
        m_prev = m_s[d]

        d_t = jnp.where(valid_f if d == 0 else valid_b, ucb + b_row, NEG_BIG)
        m_state = b_row + m_prev
        m_t = jnp.maximum(m_state, jnp.max(d_t, axis=0, keepdims=True))
        w_state = jnp.exp(m_state - m_t)
        s_t = _dot_nt(kb, qb) * jnp.exp(d_t - m_t)
        tot = _dot(vt, s_t.astype(bf16)) + w_state * _dot_nt(ct.astype(bf16), qb)
        den = tot[D_HEAD:D_HEAD + 1]
        ht_s[d, :, pl.ds(r0, T)] = tot[:D_HEAD] * (1.0 / jnp.maximum(jnp.abs(den), jnp.exp(-m_t)))

        gcb = ucb + b_end_l
        m_new = jnp.maximum(b_end_l + m_prev, jnp.max(gcb, axis=0, keepdims=True))
        kw = k * jnp.exp(gcb - m_new)
        decay = jnp.exp(b_end_l + m_prev - m_new)
        ct_s[d] = decay * ct + _dot(vt, kw.astype(bf16))
        m_s[d] = m_new

    def loop_body(j, carry):
        step(0, j)
        step(1, nc - 1 - j)
        return carry
    lax.fori_loop(0, nc, loop_body, 0, unroll=min(4, nc))

    ng = ng_ref[...]

    def out_body(j, carry):
        r0 = pl.multiple_of(j * T, T)
        y_t = ht_s[0, :, pl.ds(r0, T)] + ht_s[1, :, pl.ds(r0, T)]
        ms = jnp.mean(y_t * y_t, axis=0, keepdims=True)
        y = (y_t * lax.rsqrt(ms + EPS)).T * ng * _sigmoid(mo_ref[0, pl.ds(r0, T), :].astype(f32))
        y_ref[0, pl.ds(r0, T), :] = y.astype(y_ref.dtype)
        return carry
    lax.fori_loop(0, nc, out_body, 0, unroll=2)

    for d in range(2):
        cout_ref[0, d, 0] = ct_s[d, 0:D_HEAD, :].T
        nout_ref[0, d, 0] = ct_s[d, D_HEAD:D_HEAD + 1, :]
        mout_ref[0, d, 0] = m_s[d]


def mlstm_call(proj, proj32, gate_bias, norm_g, c0, n0, m0, layer):
    B, L, _ = proj.shape

    def col(cb):
        return pl.BlockSpec((1, L, LANES), lambda b, h, cb=cb: (b, 0, cb + h))

    smem = pl.BlockSpec(memory_space=pltpu.SMEM)
    kern = functools.partial(_mlstm_kernel, seq_len=L)
    return pl.pallas_call(
        kern,
        grid=(B, N_HEADS),
        in_specs=[smem, smem,
                  col(CB_MQ), col(CB_MK), col(CB_MV), col(CB_MO),
                  pl.BlockSpec((1, L, LANES), lambda b, h: (b, 0, CF_MG)),
                  pl.BlockSpec((1, LANES), lambda b, h: (0, h)),
                  pl.BlockSpec((1, 1, 2, 1, D_HEAD, D_HEAD), lambda b, h: (b, layer, 0, h, 0, 0)),
                  pl.BlockSpec((1, 1, 2, 1, 1, D_HEAD), lambda b, h: (b, layer, 0, h, 0, 0))],
        out_specs=[pl.BlockSpec((1, L, LANES), lambda b, h: (b, 0, h)),
                   pl.BlockSpec((1, 2, 1, D_HEAD, D_HEAD), lambda b, h: (b, 0, h, 0, 0)),
                   pl.BlockSpec((1, 2, 1, 1, D_HEAD), lambda b, h: (b, 0, h, 0, 0)),
                   pl.BlockSpec((1, 2, 1, 1, LANES), lambda b, h: (b, 0, h, 0, 0))],
        out_shape=[jax.ShapeDtypeStruct((B, L, BRANCH_W), bf16),
                   jax.ShapeDtypeStruct((B, 2, N_HEADS, D_HEAD, D_HEAD), f32),
                   jax.ShapeDtypeStruct((B, 2, N_HEADS, 1, D_HEAD), f32),
                   jax.ShapeDtypeStruct((B, 2, N_HEADS, 1, LANES), f32)],
        scratch_shapes=[pltpu.VMEM((LANES, L), f32),
                        pltpu.VMEM((8, L), f32),
                        pltpu.VMEM((2, L, LANES), f32),
                        pltpu.VMEM((ML_EXT, L), bf16),
                        pltpu.VMEM((2, D_HEAD, L), f32),
                        pltpu.VMEM((2, ML_EXT, D_HEAD), f32),
                        pltpu.VMEM((2, 1, LANES), f32)],
        compiler_params=_cparams(("arbitrary", "arbitrary")),
        name="mlstm",
    )(gate_bias, m0, proj, proj, proj, proj, proj32, norm_g, c0, n0)


def _hgrn_kernel(q_ref, f0_ref, f1_ref, i_ref, og_ref, lb_ref, ng_ref, s0_ref,
                 y_ref, sout_ref, o_s, st_s, qx_s, dec_s, u_s, vt_s, k_s, b_s, q32_s, v32_s, *, seq_len):
    T = HG_CHUNK
    nc = seq_len // T
    h = pl.program_id(1)
    f_refs = (f0_ref, f1_ref)

    lbs, loglb, log1mlb = [], [], []
    for d in range(2):
        lb = lb_ref[pl.ds(d * N_HEADS + h, 1), :]
        lbs.append(lb)
        loglb.append(jnp.log(lb))
        log1mlb.append(jnp.log1p(-lb))
        st_s[d] = s0_ref[0, 0, d, 0].T

    ti = lax.broadcasted_iota(jnp.int32, (T, T), 0)
    ui = lax.broadcasted_iota(jnp.int32, (T, T), 1)
    tri = ((ui <= ti).astype(bf16), (ui >= ti).astype(bf16))
    sub_t = lax.broadcasted_iota(jnp.int32, (HG_SUB, D_HEAD), 0)
    sub_q = sub_t % HG_HALF
    low_half = sub_t < HG_HALF
    row_i = lax.broadcasted_iota(jnp.int32, (T, D_HEAD), 0)
    tt_r = lax.broadcasted_iota(jnp.int32, (T, T), 0)
    tt_c = lax.broadcasted_iota(jnp.int32, (T, T), 1)
    q_rows = {(hh, dd): ((row_i % (2 * hh)) >= hh) if dd == 0 else ((row_i % (2 * hh)) < hh)
              for hh in HG_LEVELS for dd in range(2)}
    same_blk = {hh: (tt_r // (2 * hh)) == (tt_c // (2 * hh)) for hh in HG_LEVELS}

    def gates(d, j, q):
        r0 = pl.multiple_of(j * T, T)
        hf = f_refs[d][0, pl.ds(r0, T), :]
        e = jnp.exp(-jnp.abs(hf))
        sig_neg = jnp.where(hf >= 0.0, e, 1.0) / (1.0 + e)
        logsig = jnp.minimum(hf, 0.0) - jnp.log(1.0 + e)
        cterm = log1mlb[d] + logsig
        amax = jnp.maximum(loglb[d], cterm)
        logf = amax + jnp.log(1.0 + jnp.exp(-jnp.abs(loglb[d] - cterm)))
        k = (1.0 - lbs[d]) * sig_neg
        hi, mid, lo = _split3(logf * LOG2E)
        b = _dot(tri[d], hi) + _dot(tri[d], mid) + _dot(tri[d], lo)
        b_end = b[T - 1:T] if d == 0 else b[0:1]
        k_s[d, pl.ds(r0, T), :] = k
        b_s[d, pl.ds(r0, T), :] = b
        qx_s[d, pl.ds(r0, T), :] = (q * jnp.exp2(b)).astype(bf16)
        dec_s[d, pl.ds(j, 1), :] = jnp.exp2(b_end)

    def gates_body(j, carry):
        r0 = pl.multiple_of(j * T, T)
        q = q_ref[0, pl.ds(r0, T), :].astype(f32)
        v = i_ref[0, pl.ds(r0, T), :].astype(f32)
        q32_s[pl.ds(r0, T), :] = q
        v32_s[pl.ds(r0, T), :] = v
        vt_s[j] = v.T.astype(bf16)
        gates(0, j, q)
        gates(1, j, q)
        return carry
    lax.fori_loop(0, nc, gates_body, 0, unroll=2)

    def intra(d, j):
        r0 = pl.multiple_of(j * T, T)

        def rows(ref2, lo_r, n):
            return ref2[pl.ds(r0 + lo_r, n), :]

        def srows(ref3, lo_r, n):
            return ref3[d, pl.ds(r0 + lo_r, n), :]

        nsub = T // HG_SUB
        o_blk = [None] * nsub

        def add(idx, val):
            o_blk[idx] = val if o_blk[idx] is None else o_blk[idx] + val

        def levels():
            b_all = srows(b_s, 0, T)
            q_all = rows(q32_s, 0, T)
            k_all = srows(k_s, 0, T)
            a_tot = None
            for h in HG_LEVELS:
                blk = 2 * h
                ref_off = h - 1 if d == 0 else h
                ref = jnp.concatenate([jnp.broadcast_to(srows(b_s, base + ref_off, 1), (blk, D_HEAD))
                                       for base in range(0, T, blk)], axis=0)
                is_q = q_rows[(h, d)]
                qx = q_all * jnp.exp2(jnp.where(is_q, b_all - ref, NEG_BIG))
                kx = k_all * jnp.exp2(jnp.where(is_q, NEG_BIG, ref - b_all))
                a = _dot_nt(qx.astype(bf16), kx.astype(bf16))
                if blk < T:
                    a = jnp.where(same_blk[h], a, 0.0)
                a_tot = a if a_tot is None else a_tot + a
            ov = _dot(a_tot.astype(bf16), rows(v32_s, 0, T).astype(bf16))
            for p in range(nsub):
                add(p, ov[p * HG_SUB:(p + 1) * HG_SUB])

        def diag(lo_r):
            bs = srows(b_s, lo_r, HG_SUB)
            qs = rows(q32_s, lo_r, HG_SUB)

            def keyrow(read, s):
                return jnp.where(low_half, read(lo_r + s, 1), read(lo_r + HG_HALF + s, 1))

            acc = None
            for s in range(HG_HALF):
                valid = (sub_q >= s) if d == 0 else (sub_q <= s)
                b_key = keyrow(lambda r, n: srows(b_s, r, n), s)
                ee = jnp.exp2(jnp.where(valid, bs - b_key, NEG_BIG))
                a_col = jnp.sum(qs * ee * keyrow(lambda r, n: srows(k_s, r, n), s), axis=-1, keepdims=True)
                term = a_col * keyrow(lambda r, n: rows(v32_s, r, n), s)
                acc = term if acc is None else acc + term
            add(lo_r // HG_SUB, acc)

        levels()
        for i in range(nsub):
            diag(i * HG_SUB)
        o_s[d, pl.ds(r0, T), :] = jnp.concatenate(o_blk, axis=0)

    def intra_body(j, carry):
        intra(0, j)
        intra(1, j)
        return carry
    lax.fori_loop(0, nc, intra_body, 0, unroll=4)

    def incr_body(j, carry):
        r0 = pl.multiple_of(j * T, T)
        vt = vt_s[j]
        for d in range(2):
            b = b_s[d, pl.ds(r0, T), :]
            b_end = b[T - 1:T] if d == 0 else b[0:1]
            u_s[d, j] = _dot(vt, (k_s[d, pl.ds(r0, T), :] * jnp.exp2(b_end - b)).astype(bf16))
        return carry
    lax.fori_loop(0, nc, incr_body, 0, unroll=4)

    def state_step(d, j):
        r0 = pl.multiple_of(j * T, T)
        st = st_s[d]
        o_s[d, pl.ds(r0, T), :] = o_s[d, pl.ds(r0, T), :] + _dot_nt(qx_s[d, pl.ds(r0, T), :], st.astype(bf16))
        st_s[d] = st * dec_s[d, pl.ds(j, 1), :] + u_s[d, j]

    def state_body(j, carry):
        state_step(0, j)
        state_step(1, nc - 1 - j)
        return carry
    lax.fori_loop(0, nc, state_body, 0, unroll=4)

    ng = ng_ref[...]

    def out_body(j, carry):
        r0 = pl.multiple_of(j * T, T)
        y = o_s[0, pl.ds(r0, T), :] + o_s[1, pl.ds(r0, T), :]
        y = _rms_rows(y, ng) * _silu(og_ref[0, pl.ds(r0, T), :].astype(f32))
        y_ref[0, pl.ds(r0, T), :] = y.astype(y_ref.dtype)
        return carry
    lax.fori_loop(0, nc, out_body, 0, unroll=4)

    for d in range(2):
        sout_ref[0, d, 0] = st_s[d].T


def hgrn_call(proj, proj32, lb_rows, norm_g, s0, layer):
    B, L, _ = proj.shape

    def col(cb):
        return pl.BlockSpec((1, L, LANES), lambda b, h, cb=cb: (b, 0, cb + h))

    kern = functools.partial(_hgrn_kernel, seq_len=L)
    return pl.pallas_call(
        kern,
        grid=(B, N_HEADS),
        in_specs=[col(CB_HQ), col(CF_HF), col(CF_HF + N_HEADS), col(CB_HI), col(CB_HG),
                  pl.BlockSpec((2 * N_HEADS, D_HEAD), lambda b, h: (0, 0)),
                  pl.BlockSpec((1, LANES), lambda b, h: (0, h)),
                  pl.BlockSpec((1, 1, 2, 1, D_HEAD, D_HEAD), lambda b, h: (b, layer, 0, h, 0, 0))],
        out_specs=[pl.BlockSpec((1, L, LANES), lambda b, h: (b, 0, h)),
                   pl.BlockSpec((1, 2, 1, D_HEAD, D_HEAD), lambda b, h: (b, 0, h, 0, 0))],
        out_shape=[jax.ShapeDtypeStruct((B, L, BRANCH_W), bf16),
                   jax.ShapeDtypeStruct((B, 2, N_HEADS, D_HEAD, D_HEAD), f32)],
        scratch_shapes=[pltpu.VMEM((2, L, D_HEAD), f32),
                        pltpu.VMEM((2, D_HEAD, D_HEAD), f32),
                        pltpu.VMEM((2, L, D_HEAD), bf16),
                        pltpu.VMEM((2, L // HG_CHUNK, D_HEAD), f32),
                        pltpu.VMEM((2, L // HG_CHUNK, D_HEAD, D_HEAD), f32),
                        pltpu.VMEM((L // HG_CHUNK, D_HEAD, HG_CHUNK), bf16),
                        pltpu.VMEM((2, L, D_HEAD), f32),
                        pltpu.VMEM((2, L, D_HEAD), f32),
                        pltpu.VMEM((L, D_HEAD), f32),
                        pltpu.VMEM((L, D_HEAD), f32)],
        compiler_params=_cparams(("arbitrary", "arbitrary")),
        name="hgrn",
    )(proj, proj32, proj32, proj, proj, lb_rows, norm_g, s0)


def _pair_swap(x):
    lane = lax.broadcasted_iota(jnp.int32, x.shape, 1)
    return jnp.where((lane % 2) == 0, pltpu.roll(x, LANES - 1, 1), pltpu.roll(x, 1, 1))


def _head_norm_pair(x, g_row):
    lane = lax.broadcasted_iota(jnp.int32, x.shape, 1)
    low = lane < AT_HD
    sq = x * x
    s_lo = jnp.sum(jnp.where(low, sq, 0.0), axis=-1, keepdims=True)
    s_hi = jnp.sum(jnp.where(low, 0.0, sq), axis=-1, keepdims=True)
    ms = jnp.where(low, s_lo, s_hi) * (1.0 / AT_HD)
    return x * lax.rsqrt(ms + EPS) * g_row


def _qkv_prep_kernel(aq_ref, ak_ref, av_ref, qg_ref, kg_ref, cos_ref, sin_ref,
                     q_ref, k_ref, v_ref, kn_ref, *, use_rope):
    qg = qg_ref[...]
    kg = kg_ref[...]
    if use_rope:
        cos = cos_ref[...]
        sin = sin_ref[...]
    for p in range(AT_HEADS // 2):
        x = _head_norm_pair(aq_ref[0, :, p * LANES:(p + 1) * LANES].astype(f32), qg)
        if use_rope:
            x = x * cos + _pair_swap(x) * sin
        x = (x * (AT_HD ** -0.5)).astype(bf16)
        q_ref[0, 2 * p] = x[:, :AT_HD]
        q_ref[0, 2 * p + 1] = x[:, AT_HD:]
    kn = _head_norm_pair(ak_ref[0].astype(f32), kg)
    kn_ref[0] = kn
    if use_rope:
        kn = kn * cos + _pair_swap(kn) * sin
    kb = kn.astype(bf16)
    vb = av_ref[0]
    for g in range(AT_KV_HEADS):
        k_ref[0, g] = kb[:, g * AT_HD:(g + 1) * AT_HD]
        v_ref[0, g] = vb[:, g * AT_HD:(g + 1) * AT_HD]


def qkv_prep_call(proj, qg_row, kg_row, cos_t, sin_t, use_rope):
    B, L, _ = proj.shape
    tm = min(L, 512)
    kern = functools.partial(_qkv_prep_kernel, use_rope=use_rope)
    return pl.pallas_call(
        kern,
        grid=(B, L // tm),
        in_specs=[pl.BlockSpec((1, tm, 4 * LANES), lambda b, m: (b, m, CB_AQ // 4)),
                  pl.BlockSpec((1, tm, LANES), lambda b, m: (b, m, CB_AK)),
                  pl.BlockSpec((1, tm, LANES), lambda b, m: (b, m, CB_AV)),
                  pl.BlockSpec((1, LANES), lambda b, m: (0, 0)),
                  pl.BlockSpec((1, LANES), lambda b, m: (0, 0)),
                  pl.BlockSpec((tm, LANES), lambda b, m: (m, 0)),
                  pl.BlockSpec((tm, LANES), lambda b, m: (m, 0))],
        out_specs=[pl.BlockSpec((1, AT_HEADS, tm, AT_HD), lambda b, m: (b, 0, m, 0)),
                   pl.BlockSpec((1, AT_KV_HEADS, tm, AT_HD), lambda b, m: (b, 0, m, 0)),
                   pl.BlockSpec((1, AT_KV_HEADS, tm, AT_HD), lambda b, m: (b, 0, m, 0)),
                   pl.BlockSpec((1, tm, LANES), lambda b, m: (b, m, 0))],
        out_shape=[jax.ShapeDtypeStruct((B, AT_HEADS, L, AT_HD), bf16),
                   jax.ShapeDtypeStruct((B, AT_KV_HEADS, L, AT_HD), bf16),
                   jax.ShapeDtypeStruct((B, AT_KV_HEADS, L, AT_HD), bf16),
                   jax.ShapeDtypeStruct((B, L, LANES), f32)],
        compiler_params=_cparams(("arbitrary", "arbitrary")),
        name="qkv_prep",
    )(proj, proj, proj, qg_row, kg_row, cos_t, sin_t)


def _attn_kernel(q_ref, k_ref, v_ref, o_ref, s_s, m_s, *, tq, tk, nkv):
    rows = AT_GROUP * tq
    ncb = tk // LANES
    always = pl.program_id(2) >= 0

    @pl.when(always)
    def _():
        q = q_ref[0].reshape(rows, AT_HD)
        mx = None
        for j in range(nkv):
            s = _dot_nt(q, k_ref[0, 0, j * tk:(j + 1) * tk, :])
            s_s[:, j * tk:(j + 1) * tk] = s
            cm = s[:, 0:LANES]
            for cb in range(1, ncb):
                cm = jnp.maximum(cm, s[:, cb * LANES:(cb + 1) * LANES])
            mx = cm if mx is None else jnp.maximum(mx, cm)
        m_s[...] = jnp.broadcast_to(jnp.max(mx, axis=-1, keepdims=True), (rows, LANES))

    @pl.when(pl.program_id(1) >= 0)
    def _():
        m_b = m_s[...]
        acc = None
        for j in range(nkv):
            ps = []
            for cb in range(ncb):
                off = j * tk + cb * LANES
                ps.append(jnp.exp(s_s[:, off:off + LANES] - m_b).astype(bf16))
            pv = _dot(jnp.concatenate(ps, axis=1), v_ref[0, 0, j * tk:(j + 1) * tk, :])
            acc = pv if acc is None else acc + pv
        o = acc[:, :AT_HD] * (1.0 / acc[:, AT_HD:AT_HD + 1])
        for g in range(AT_GROUP):
            o_ref[0, :, g * AT_HD:(g + 1) * AT_HD] = o[g * tq:(g + 1) * tq].astype(o_ref.dtype)


def attn_call(q, k, v):
    B, _, Lq, _ = q.shape
    Lk = k.shape[2]
    tq = 256
    tk = 512 if Lk % 512 == 0 else Lk
    nkv = Lk // tk
    nq = Lq // tq
    kern = functools.partial(_attn_kernel, tq=tq, tk=tk, nkv=nkv)
    rows = AT_GROUP * tq
    return pl.pallas_call(
        kern,
        grid=(B, AT_KV_HEADS, nq),
        in_specs=[pl.BlockSpec((1, AT_GROUP, tq, AT_HD), lambda b, g, i: (b, g, i, 0)),
                  pl.BlockSpec((1, 1, Lk, AT_HD), lambda b, g, i: (b, g, 0, 0)),
                  pl.BlockSpec((1, 1, Lk, LANES), lambda b, g, i: (b, g, 0, 0))],
        out_specs=pl.BlockSpec((1, tq, AT_GROUP * AT_HD), lambda b, g, i: (b, i, g)),
        out_shape=jax.ShapeDtypeStruct((B, Lq, AT_HEADS * AT_HD), bf16),
        scratch_shapes=[pltpu.VMEM((rows, Lk), f32),
                        pltpu.VMEM((rows, LANES), f32)],
        compiler_params=_cparams(("arbitrary", "arbitrary", "arbitrary")),
        name="attention",
    )(q, k, v)


def _with_ones_column(v):
    ones = jnp.ones(v.shape[:-1] + (1,), v.dtype)
    zeros = jnp.zeros(v.shape[:-1] + (LANES - AT_HD - 1,), v.dtype)
    return jnp.concatenate([v, ones, zeros], axis=-1)


def _merge_kernel(x_ref, yml_ref, yhg_ref, yat_ref, bg_ref, mod_ref, wb_ref, wo_ref, o_ref):
    merged = None
    for n, y_ref in enumerate((yml_ref, yhg_ref, yat_ref)):
        pn = _dot(y_ref[0], wb_ref[n])
        gn = _sigmoid(bg_ref[0, :, n * D_MODEL:(n + 1) * D_MODEL].astype(f32))
        merged = gn * pn if merged is None else merged + gn * pn
    out = _dot(merged.astype(bf16), wo_ref[...])
    o_ref[0] = x_ref[0] + mod_ref[0, 2:3, :] * out


def merge_call(x, y_ml, y_hg, y_at, proj, mod, wb, wo, shared_mod):
    B, L, _ = x.shape
    tm = min(L, 512)
    mod_map = (lambda b, m: (0, 0, 0)) if shared_mod else (lambda b, m: (b, 0, 0))
    yspec = pl.BlockSpec((1, tm, BRANCH_W), lambda b, m: (b, m, 0))
    return pl.pallas_call(
        _merge_kernel,
        grid=(B, L // tm),
        in_specs=[pl.BlockSpec((1, tm, D_MODEL), lambda b, m: (b, m, 0)),
                  yspec, yspec, yspec,
                  pl.BlockSpec((1, tm, 3 * D_MODEL), lambda b, m: (b, m, CB_BG)),
                  pl.BlockSpec((1, 8, D_MODEL), mod_map),
                  pl.BlockSpec((3, BRANCH_W, D_MODEL), lambda b, m: (0, 0, 0)),
                  pl.BlockSpec((D_MODEL, D_MODEL), lambda b, m: (0, 0))],
        out_specs=pl.BlockSpec((1, tm, D_MODEL), lambda b, m: (b, m, 0)),
        out_shape=jax.ShapeDtypeStruct((B, L, D_MODEL), f32),
        compiler_params=_cparams(("arbitrary", "arbitrary")),
        name="merge_out",
    )(x, y_ml, y_hg, y_at, proj, mod, wb, wo)


FFN_SPLITS = ((0, 1536), (1536, FFN_HIDDEN))


def _ffn_kernel(x_ref, mod_ref, g_ref, wi_ref, wd_ref, fg_ref, o_ref, *, final_norm):
    x = x_ref[0]
    hb = (_rms_rows(x, g_ref[...]) * (1.0 + mod_ref[0, 4:5, :]) + mod_ref[0, 3:4, :]).astype(bf16)
    acc = None
    for lo, hi in FFN_SPLITS:
        gate = _dot(hb, wi_ref[:, lo:hi])
        up = _dot(hb, wi_ref[:, FFN_HIDDEN + lo:FFN_HIDDEN + hi])
        part = _dot((_silu(gate) * up).astype(bf16), wd_ref[lo:hi, :])
        acc = part if acc is None else acc + part
    y = x + mod_ref[0, 5:6, :] * acc
    if final_norm:
        y = _rms_rows(y, fg_ref[...])
    o_ref[0] = y


def ffn_call(x, mod, g_row, w_in, w_out, fg_row, shared_mod, final_norm):
    B, L, _ = x.shape
    tm = min(L, 512)
    mod_map = (lambda b, m: (0, 0, 0)) if shared_mod else (lambda b, m: (b, 0, 0))
    kern = functools.partial(_ffn_kernel, final_norm=final_norm)
    resident = pl.Buffered(1)
    return pl.pallas_call(
        kern,
        grid=(B, L // tm),
        in_specs=[pl.BlockSpec((1, tm, D_MODEL), lambda b, m: (b, m, 0)),
                  pl.BlockSpec((1, 8, D_MODEL), mod_map),
                  pl.BlockSpec((1, D_MODEL), lambda b, m: (0, 0)),
                  pl.BlockSpec((D_MODEL, 2 * FFN_HIDDEN), lambda b, m: (0, 0), pipeline_mode=resident),
                  pl.BlockSpec((FFN_HIDDEN, D_MODEL), lambda b, m: (0, 0), pipeline_mode=resident),
                  pl.BlockSpec((1, D_MODEL), lambda b, m: (0, 0))],
        out_specs=pl.BlockSpec((1, tm, D_MODEL), lambda b, m: (b, m, 0)),
        out_shape=jax.ShapeDtypeStruct((B, L, D_MODEL), f32),
        compiler_params=_cparams(("arbitrary", "arbitrary")),
        name="ffn",
    )(x, mod, g_row, w_in, w_out, fg_row)


def _reorder_w_in(w):
    o_mg, o_hq, o_hf, o_hi, o_bg, o_end = 2048, 2064, 2576, 3600, 5392, 8464
    d = w.shape[0]
    sec16 = [w[:, o_bg:o_end], w[:, :o_mg], w[:, o_hq:o_hf], w[:, o_hi:o_bg]]
    n16 = (o_end - o_bg) + o_mg + (o_hf - o_hq) + (o_bg - o_hi)
    sec32 = [w[:, o_hf:o_hi], w[:, o_mg:o_hq]]
    n32 = (o_hi - o_hf) + (o_hq - o_mg)
    return jnp.concatenate(sec16 + [jnp.zeros((d, N16_COLS - n16), w.dtype)] + sec32
                           + [jnp.zeros((d, N32_COLS - n32), w.dtype)], axis=1).astype(bf16)


def _rope_tables(L):
    rows = L // GRID_W
    row = jnp.repeat(jnp.arange(rows, dtype=f32), GRID_W)
    colp = jnp.tile(jnp.arange(GRID_W, dtype=f32), rows)
    n_freq = AT_HD // 4
    inv = ROPE_THETA ** (-jnp.arange(n_freq, dtype=f32) / n_freq)
    ang = jnp.concatenate([row[:, None] * inv, colp[:, None] * inv], axis=-1)
    cos = jnp.repeat(jnp.cos(ang), 2, axis=-1)
    sin = jnp.repeat(jnp.sin(ang), 2, axis=-1)
    sign = jnp.tile(jnp.array([-1.0, 1.0], f32), AT_HD // 2)
    sin = sin * sign
    return jnp.tile(cos, (1, 2)), jnp.tile(sin, (1, 2))


def _mod_rows(ada_rows):
    r = ada_rows.shape[0]
    m = ada_rows.reshape(r, 6, D_MODEL)
    return jnp.concatenate([m, jnp.zeros((r, 2, D_MODEL), f32)], axis=1)


def kernel(x_prompt, x_sample, c, cache_k, cache_v, state_ml_C, state_ml_n, state_ml_m, state_hg_S, c_ctx, w_ada, b_ada, norm1_g, norm2_g, w_in, ml_gate_b, ml_norm_g, hg_lb_logits, hg_norm_g, q_norm_g, k_norm_g, w_branch, w_out, w_ffn_in, w_ffn_out, final_g):
    Bp, Lp, _ = x_prompt.shape
    Bs, Ls, _ = x_sample.shape

    n_rows = 16
    cvec = jnp.concatenate([c_ctx[None, :], c, jnp.zeros((n_rows - 1 - Bs, D_MODEL), f32)], axis=0)
    ada = ada_call(cvec, w_ada, b_ada)
    lb_all = lb_call(hg_lb_logits)
    cos_t, sin_t = _rope_tables(Ls)

    zeros_c = jnp.zeros((Bp, 1, 2, N_HEADS, D_HEAD, D_HEAD), f32)
    zeros_n = jnp.zeros((Bp, 1, 2, N_HEADS, 1, D_HEAD), f32)
    zeros_m = jnp.zeros((Bp * 2 * N_HEADS,), f32)
    n_state = state_ml_n.reshape(Bs, DEPTH, 2, N_HEADS, 1, D_HEAD)

    xp, xs = x_prompt.reshape(1, Bp * Lp, D_MODEL), x_sample
    nk, nv, nC, nn_, nm, nS = [], [], [], [], [], []
    for l in range(DEPTH):
        w_l = _reorder_w_in(w_in[l])
        wb_l = w_branch[l].astype(bf16)
        wo_l = w_out[l].astype(bf16)
        wfi_l = w_ffn_in[l].astype(bf16)
        wfo_l = w_ffn_out[l].astype(bf16)
        g1 = norm1_g[l][None, :]
        g2 = norm2_g[l][None, :]
        mlg = ml_norm_g[l][None, :]
        hgg = hg_norm_g[l][None, :]
        qg = jnp.tile(q_norm_g[l], 2)[None, :]
        kg = jnp.tile(k_norm_g[l], 2)[None, :]
        gate_b = ml_gate_b[l].reshape(-1).astype(f32)
        fg = final_g[None, :]
        last = l == DEPTH - 1

        mod_ctx = _mod_rows(ada[l, 0:1])
        mod_lat = _mod_rows(ada[l, 1:1 + Bs])

        proj_flat, proj32 = in_proj_call(xp, mod_ctx, g1, w_l, True)
        proj = proj_flat.reshape(Bp, Lp, N16_COLS)
        proj32 = proj32.reshape(Bp, Lp, N32_COLS)
        y_ml, c_f, n_f, m_f = mlstm_call(proj, proj32, gate_b, mlg, zeros_c, zeros_n, zeros_m, 0)
        y_hg, s_f = hgrn_call(proj, proj32, lb_all[l], hgg, zeros_c, 0)
        q_h, k_h, v_h, k_n = qkv_prep_call(proj, qg, kg, cos_t[:Lp], sin_t[:Lp], False)
        y_at = attn_call(q_h, k_h, _with_ones_column(v_h))
        flat = (1, Bp * Lp, BRANCH_W)
        xp = merge_call(xp, y_ml.reshape(flat), y_hg.reshape(flat), y_at.reshape(flat), proj_flat,
                        mod_ctx, wb_l, wo_l, True)
        xp = ffn_call(xp, mod_ctx, g2, wfi_l, wfo_l, fg, True, last)
        nk.append(k_n.reshape(Bp, Lp, AT_KV_HEADS, AT_HD))
        nv.append(proj[:, :, CB_AV * LANES:(CB_AV + 1) * LANES].astype(f32).reshape(Bp, Lp, AT_KV_HEADS, AT_HD))
        nC.append(c_f)
        nn_.append(n_f[:, :, :, 0, :])
        nm.append(m_f[:, :, :, 0, 0])
        nS.append(s_f)

        proj, proj32 = in_proj_call(xs, mod_lat, g1, w_l, False)
        y_ml, _, _, _ = mlstm_call(proj, proj32, gate_b, mlg, state_ml_C, n_state,
                                   state_ml_m[:, l].reshape(-1).astype(f32), l)
        y_hg, _ = hgrn_call(proj, proj32, lb_all[l], hgg, state_hg_S, l)
        q_h, k_h, v_h, _ = qkv_prep_call(proj, qg, kg, cos_t, sin_t, True)
        k_all = jnp.concatenate([k_h, jnp.swapaxes(cache_k[:, l], 1, 2).astype(bf16)], axis=2)
        v_all = jnp.concatenate([v_h, jnp.swapaxes(cache_v[:, l], 1, 2).astype(bf16)], axis=2)
        y_at = attn_call(q_h, k_all, _with_ones_column(v_all))
        xs = merge_call(xs, y_ml, y_hg, y_at, proj, mod_lat, wb_l, wo_l, False)
        xs = ffn_call(xs, mod_lat, g2, wfi_l, wfo_l, fg, False, last)

    return (xp.reshape(Bp, Lp, D_MODEL), xs, jnp.stack(nk, axis=1), jnp.stack(nv, axis=1), jnp.stack(nC, axis=1),
            jnp.stack(nn_, axis=1), jnp.stack(nm, axis=1), jnp.stack(nS, axis=1))
```

```python
import functools
import math

import jax
import jax.numpy as jnp
from jax import lax
from jax.experimental import pallas as pl
from jax.experimental.pallas import tpu as pltpu

f32 = jnp.float32
bf16 = jnp.bfloat16

D_MODEL = 1024
DEPTH = 4
N_HEADS = 4
D_HEAD = 128
AT_HEADS = 8
AT_KV_HEADS = 2
AT_GROUP = AT_HEADS // AT_KV_HEADS
AT_HD = 64
BRANCH_W = 512
FFN_HIDDEN = 2816
GRID_W = 64
ROPE_THETA = 10000.0
EPS = 1e-6

LANES = 128
SUBLANES = 8
VMEM_LIMIT = 52 * 1024 * 1024

CB_BG = 0
CB_MQ = 24
CB_MK = 28
CB_MV = 32
CB_MO = 36
CB_HQ = 40
CB_HI = 44
CB_HG = 48
CB_AQ = 52
CB_AK = 56
CB_AV = 57
N16_BLOCKS = 60
N16_COLS = N16_BLOCKS * LANES
CF_HF = 0
CF_MG = 8
N32_BLOCKS = 10
N32_COLS = N32_BLOCKS * LANES
PROJ_TN = N32_COLS
N16_TILES = N16_COLS // PROJ_TN

ML_CHUNK = 128
ML_EXT = 144
HG_CHUNK = 64
HG_SUB = 8
HG_HALF = 4
HG_LEVELS = (32, 16, 8, 4)
NEG_BIG = -1e30
LOG2E = 1.4426950408889634


def _cparams(sem):
    return pltpu.CompilerParams(dimension_semantics=sem, vmem_limit_bytes=VMEM_LIMIT)


def _dot(a, b):
    return jnp.dot(a, b, preferred_element_type=f32)


def _dot_nt(a, b):
    return lax.dot_general(a, b, (((1,), (1,)), ((), ())), preferred_element_type=f32)


def _dot_tn(a, b):
    return lax.dot_general(a, b, (((0,), (0,)), ((), ())), preferred_element_type=f32)


def _sigmoid(x):
    return 1.0 / (1.0 + jnp.exp(-x))


def _silu(x):
    return x * _sigmoid(x)


def _log_sigmoid(x):
    return jnp.minimum(x, 0.0) - jnp.log1p(jnp.exp(-jnp.abs(x)))


def _split3(x):
    hi = x.astype(bf16)
    r1 = x - hi.astype(f32)
    mid = r1.astype(bf16)
    lo = (r1 - mid.astype(f32)).astype(bf16)
    return hi, mid, lo


def _rms_rows(x, g_row):
    ms = jnp.mean(x * x, axis=-1, keepdims=True)
    return x * lax.rsqrt(ms + EPS) * g_row


def _ada_kernel(c_ref, w_ref, b_ref, o_ref):
    s = _silu(c_ref[...]).astype(bf16)
    o_ref[0] = _dot(s, w_ref[0].astype(bf16)) + b_ref[0]


def ada_call(cvec, w_ada, b_ada):
    rows = cvec.shape[0]
    tn = 1536
    n6 = 6 * D_MODEL
    return pl.pallas_call(
        _ada_kernel,
        grid=(DEPTH, n6 // tn),
        in_specs=[pl.BlockSpec((rows, D_MODEL), lambda l, n: (0, 0)),
                  pl.BlockSpec((1, D_MODEL, tn), lambda l, n: (l, 0, n)),
                  pl.BlockSpec((1, 1, tn), lambda l, n: (l, 0, n))],
        out_specs=pl.BlockSpec((1, rows, tn), lambda l, n: (l, 0, n)),
        out_shape=jax.ShapeDtypeStruct((DEPTH, rows, n6), f32),
        compiler_params=_cparams(("arbitrary", "arbitrary")),
        name="ada",
    )(cvec, w_ada, b_ada.reshape(DEPTH, 1, n6))


def _lb_kernel(x_ref, o_ref):
    xs = [x_ref[l] for l in range(DEPTH)]
    mx = xs[0]
    for l in range(1, DEPTH):
        mx = jnp.maximum(mx, xs[l])
    es = [jnp.exp(x - mx) for x in xs]
    tot = es[0]
    for l in range(1, DEPTH):
        tot = tot + es[l]
    sm = [e / tot for e in es]
    run = sm[0]
    o_ref[0] = run - sm[0]
    for l in range(1, DEPTH):
        run = run + sm[l]
        o_ref[l] = run - sm[0]


def lb_call(logits):
    x = logits.astype(f32).reshape(DEPTH, 2 * N_HEADS, D_HEAD)
    return pl.pallas_call(
        _lb_kernel,
        out_shape=jax.ShapeDtypeStruct((DEPTH, 2 * N_HEADS, D_HEAD), f32),
        name="hg_lower_bounds",
    )(x)


def _in_proj_kernel(x_ref, mod_ref, g_ref, w_ref, o16_ref, o32_ref, h_s):
    n = pl.program_id(2)

    @pl.when(n == 0)
    def _():
        x = x_ref[0]
        sh = mod_ref[0, 0:1, :]
        sc = mod_ref[0, 1:2, :]
        h = _rms_rows(x, g_ref[...]) * (1.0 + sc) + sh
        h_s[...] = h.astype(bf16)

    @pl.when(n < N16_TILES)
    def _():
        o16_ref[0] = _dot(h_s[...], w_ref[...]).astype(bf16)

    @pl.when(n == N16_TILES)
    def _():
        o32_ref[0] = _dot(h_s[...], w_ref[...])


def in_proj_call(x, mod, g_row, w, shared_mod):
    B, L, _ = x.shape
    tm = min(L, 1024)
    tn = PROJ_TN
    mod_map = (lambda b, m, n: (0, 0, 0)) if shared_mod else (lambda b, m, n: (b, 0, 0))
    return pl.pallas_call(
        _in_proj_kernel,
        grid=(B, L // tm, N16_TILES + 1),
        in_specs=[pl.BlockSpec((1, tm, D_MODEL), lambda b, m, n: (b, m, 0)),
                  pl.BlockSpec((1, 8, D_MODEL), mod_map),
                  pl.BlockSpec((1, D_MODEL), lambda b, m, n: (0, 0)),
                  pl.BlockSpec((D_MODEL, tn), lambda b, m, n: (0, n))],
        out_specs=[pl.BlockSpec((1, tm, tn), lambda b, m, n: (b, m, jnp.minimum(n, N16_TILES - 1))),
                   pl.BlockSpec((1, tm, tn), lambda b, m, n: (b, m, 0))],
        out_shape=[jax.ShapeDtypeStruct((B, L, N16_COLS), bf16),
                   jax.ShapeDtypeStruct((B, L, N32_COLS), f32)],
        scratch_shapes=[pltpu.VMEM((tm, D_MODEL), bf16)],
        compiler_params=_cparams(("arbitrary", "arbitrary", "arbitrary")),
        name="in_proj",
    )(x, mod, g_row, w)


def _mlstm_kernel(bias_ref, m0_ref, q_ref, k_ref, v_ref, mo_ref, g_ref, ng_ref, c0_ref, n0_ref,
                  y_ref, cout_ref, nout_ref, mout_ref,
                  gt_s, rw_s, ucb_s, vt_s, ht_s, ct_s, m_s, *, seq_len):
    T = ML_CHUNK
    assert T == LANES
    nc = seq_len // T
    b_idx = pl.program_id(0)
    h = pl.program_id(1)
    scale = D_HEAD ** -0.5
    ext_rows = jnp.concatenate([jnp.ones((1, T), f32), jnp.zeros((ML_EXT - D_HEAD - 1, T), f32)], axis=0)

    def tr_body(j, carry):
        r0 = pl.multiple_of(j * T, T)
        gt_s[:, pl.ds(r0, T)] = g_ref[0, pl.ds(r0, T), :].T
        vt = v_ref[0, pl.ds(r0, T), :].astype(f32).T
        vt_s[:, pl.ds(r0, T)] = jnp.concatenate([vt, ext_rows], axis=0).astype(bf16)
        return carry
    lax.fori_loop(0, nc, tr_body, 0, unroll=2)

    i_f = gt_s[pl.ds(h, 1), :] + bias_ref[h]
    f_f = _log_sigmoid(gt_s[pl.ds(4 + h, 1), :] + bias_ref[4 + h])
    i_b = gt_s[pl.ds(8 + h, 1), :] + bias_ref[8 + h]
    f_b = _log_sigmoid(gt_s[pl.ds(12 + h, 1), :] + bias_ref[12 + h])
    zrow = jnp.zeros_like(i_f)
    rw_s[...] = jnp.concatenate([f_f, i_f, f_b, i_b, zrow, zrow, zrow, zrow], axis=0)

    ui = lax.broadcasted_iota(jnp.int32, (T, T), 0)
    si = lax.broadcasted_iota(jnp.int32, (T, T), 1)
    tri = jnp.concatenate([(ui <= si).astype(bf16), (ui >= si).astype(bf16)], axis=1)
    valid_f = ui <= si
    valid_b = ui >= si

    def cs_body(j, carry):
        r0 = pl.multiple_of(j * T, T)
        rows = rw_s[:, pl.ds(r0, T)]
        hi, mid, lo = _split3(rows)
        cs3 = _dot(jnp.concatenate([hi, mid, lo, jnp.zeros_like(hi)], axis=0), tri)
        cs = cs3[0:8] + cs3[8:16] + cs3[16:24]
        b_f = cs[0:1, 0:T]
        b_b = cs[2:3, T:2 * T]
        rw_s[:, pl.ds(r0, T)] = jnp.concatenate([b_f, rows[1:2], b_b, rows[3:8]], axis=0)
        ucb_s[0, pl.ds(r0, T), :] = jnp.broadcast_to(rows[1:2] - b_f, (T, T)).T
        ucb_s[1, pl.ds(r0, T), :] = jnp.broadcast_to(rows[3:4] - b_b, (T, T)).T
        return carry
    lax.fori_loop(0, nc, cs_body, 0, unroll=min(8, nc))

    for d in range(2):
        ct_s[d] = jnp.concatenate([c0_ref[0, 0, d, 0].T, n0_ref[0, 0, d, 0],
                                   jnp.zeros((ML_EXT - D_HEAD - 1, D_HEAD), f32)], axis=0)
        m_s[d] = jnp.full((1, LANES), m0_ref[b_idx * 2 * N_HEADS + d * N_HEADS + h], f32)

    def step(d, j):
        r0 = pl.multiple_of(j * T, T)
        b_row = rw_s[pl.ds(2 * d, 1), pl.ds(r0, T)]
        b_end = b_row[:, T - 1:T] if d == 0 else b_row[:, 0:1]
        b_end_l = jnp.broadcast_to(b_end, (1, LANES))
        ucb = ucb_s[d, pl.ds(r0, T), :]

        q = q_ref[0, pl.ds(r0, T), :].astype(f32) * scale
        kb = k_ref[0, pl.ds(r0, T), :]
        k = kb.astype(f32)
        qb = q.astype(bf16)
        vt = vt_s[:, pl.ds(r0, T)]
        ct = ct_s[d]
        m_prev = m_s[d]

        d_t = jnp.where(valid_f if d == 0 else valid_b, ucb + b_row, NEG_BIG)
        m_state = b_row + m_prev
        m_t = jnp.maximum(m_state, jnp.max(d_t, axis=0, keepdims=True))
        w_state = jnp.exp(m_state - m_t)
        s_t = _dot_nt(kb, qb) * jnp.exp(d_t - m_t)
        tot = _dot(vt, s_t.astype(bf16)) + w_state * _dot_nt(ct.astype(bf16), qb)
        den = tot[D_HEAD:D_HEAD + 1]
        ht_s[d, :, pl.ds(r0, T)] = tot[:D_HEAD] * (1.0 / jnp.maximum(jnp.abs(den), jnp.exp(-m_t)))

        gcb = ucb + b_end_l
        m_new = jnp.maximum(b_end_l + m_prev, jnp.max(gcb, axis=0, keepdims=True))
        kw = k * jnp.exp(gcb - m_new)
        decay = jnp.exp(b_end_l + m_prev - m_new)
        ct_s[d] = decay * ct + _dot(vt, kw.astype(bf16))
        m_s[d] = m_new

    def loop_body(j, carry):
        step(0, j)
        step(1, nc - 1 - j)
        return carry
    lax.fori_loop(0, nc, loop_body, 0, unroll=min(4, nc))

    ng = ng_ref[...]

    def out_body(j, carry):
        r0 = pl.multiple_of(j * T, T)
        y_t = ht_s[0, :, pl.ds(r0, T)] + ht_s[1, :, pl.ds(r0, T)]
        ms = jnp.mean(y_t * y_t, axis=0, keepdims=True)
        y = (y_t * lax.rsqrt(ms + EPS)).T * ng * _sigmoid(mo_ref[0, pl.ds(r0, T), :].astype(f32))
        y_ref[0, pl.ds(r0, T), :] = y.astype(y_ref.dtype)
        return carry
    lax.fori_loop(0, nc, out_body, 0, unroll=2)

    for d in range(2):
        cout_ref[0, d, 0] = ct_s[d, 0:D_HEAD, :].T
        nout_ref[0, d, 0] = ct_s[d, D_HEAD:D_HEAD + 1, :]
        mout_ref[0, d, 0] = m_s[d]


def mlstm_call(proj, proj32, gate_bias, norm_g, c0, n0, m0, layer):
    B, L, _ = proj.shape

    def col(cb):
        return pl.BlockSpec((1, L, LANES), lambda b, h, cb=cb: (b, 0, cb + h))

    smem = pl.BlockSpec(memory_space=pltpu.SMEM)
    kern = functools.partial(_mlstm_kernel, seq_len=L)
    return pl.pallas_call(
        kern,
        grid=(B, N_HEADS),
        in_specs=[smem, smem,
                  col(CB_MQ), col(CB_MK), col(CB_MV), col(CB_MO),
                  pl.BlockSpec((1, L, LANES), lambda b, h: (b, 0, CF_MG)),
                  pl.BlockSpec((1, LANES), lambda b, h: (0, h)),
                  pl.BlockSpec((1, 1, 2, 1, D_HEAD, D_HEAD), lambda b, h: (b, layer, 0, h, 0, 0)),
                  pl.BlockSpec((1, 1, 2, 1, 1, D_HEAD), lambda b, h: (b, layer, 0, h, 0, 0))],
        out_specs=[pl.BlockSpec((1, L, LANES), lambda b, h: (b, 0, h)),
                   pl.BlockSpec((1, 2, 1, D_HEAD, D_HEAD), lambda b, h: (b, 0, h, 0, 0)),
                   pl.BlockSpec((1, 2, 1, 1, D_HEAD), lambda b, h: (b, 0, h, 0, 0)),
                   pl.BlockSpec((1, 2, 1, 1, LANES), lambda b, h: (b, 0, h, 0, 0))],
        out_shape=[jax.ShapeDtypeStruct((B, L, BRANCH_W), bf16),
                   jax.ShapeDtypeStruct((B, 2, N_HEADS, D_HEAD, D_HEAD), f32),
                   jax.ShapeDtypeStruct((B, 2, N_HEADS, 1, D_HEAD), f32),
                   jax.ShapeDtypeStruct((B, 2, N_HEADS, 1, LANES), f32)],
        scratch_shapes=[pltpu.VMEM((LANES, L), f32),
                        pltpu.VMEM((8, L), f32),
                        pltpu.VMEM((2, L, LANES), f32),
                        pltpu.VMEM((ML_EXT, L), bf16),
                        pltpu.VMEM((2, D_HEAD, L), f32),
                        pltpu.VMEM((2, ML_EXT, D_HEAD), f32),
                        pltpu.VMEM((2, 1, LANES), f32)],
        compiler_params=_cparams(("arbitrary", "arbitrary")),
        name="mlstm",
    )(gate_bias, m0, proj, proj, proj, proj, proj32, norm_g, c0, n0)


def _hgrn_kernel(q_ref, f0_ref, f1_ref, i_ref, og_ref, lb_ref, ng_ref, s0_ref,
                 y_ref, sout_ref, o_s, st_s, qx_s, dec_s, u_s, vt_s, k_s, b_s, q32_s, v32_s, *, seq_len):
    T = HG_CHUNK
    nc = seq_len // T
    h = pl.program_id(1)
    f_refs = (f0_ref, f1_ref)

    lbs, loglb, log1mlb = [], [], []
    for d in range(2):
        lb = lb_ref[pl.ds(d * N_HEADS + h, 1), :]
        lbs.append(lb)
        loglb.append(jnp.log(lb))
        log1mlb.append(jnp.log1p(-lb))
        st_s[d] = s0_ref[0, 0, d, 0].T

    ti = lax.broadcasted_iota(jnp.int32, (T, T), 0)
    ui = lax.broadcasted_iota(jnp.int32, (T, T), 1)
    tri = ((ui <= ti).astype(bf16), (ui >= ti).astype(bf16))
    sub_t = lax.broadcasted_iota(jnp.int32, (HG_SUB, D_HEAD), 0)
    sub_q = sub_t % HG_HALF
    low_half = sub_t < HG_HALF
    row_i = lax.broadcasted_iota(jnp.int32, (T, D_HEAD), 0)
    tt_r = lax.broadcasted_iota(jnp.int32, (T, T), 0)
    tt_c = lax.broadcasted_iota(jnp.int32, (T, T), 1)
    q_rows = {(hh, dd): ((row_i % (2 * hh)) >= hh) if dd == 0 else ((row_i % (2 * hh)) < hh)
              for hh in HG_LEVELS for dd in range(2)}
    same_blk = {hh: (tt_r // (2 * hh)) == (tt_c // (2 * hh)) for hh in HG_LEVELS}

    def gates(d, j, q):
        r0 = pl.multiple_of(j * T, T)
        hf = f_refs[d][0, pl.ds(r0, T), :]
        e = jnp.exp(-jnp.abs(hf))
        sig_neg = jnp.where(hf >= 0.0, e, 1.0) / (1.0 + e)
        logsig = jnp.minimum(hf, 0.0) - jnp.log(1.0 + e)
        cterm = log1mlb[d] + logsig
        amax = jnp.maximum(loglb[d], cterm)
        logf = amax + jnp.log(1.0 + jnp.exp(-jnp.abs(loglb[d] - cterm)))
        k = (1.0 - lbs[d]) * sig_neg
        hi, mid, lo = _split3(logf * LOG2E)
        b = _dot(tri[d], hi) + _dot(tri[d], mid) + _dot(tri[d], lo)
        b_end = b[T - 1:T] if d == 0 else b[0:1]
        k_s[d, pl.ds(r0, T), :] = k
        b_s[d, pl.ds(r0, T), :] = b
        qx_s[d, pl.ds(r0, T), :] = (q * jnp.exp2(b)).astype(bf16)
        dec_s[d, pl.ds(j, 1), :] = jnp.exp2(b_end)

    def gates_body(j, carry):
        r0 = pl.multiple_of(j * T, T)
        q = q_ref[0, pl.ds(r0, T), :].astype(f32)
        v = i_ref[0, pl.ds(r0, T), :].astype(f32)
        q32_s[pl.ds(r0, T), :] = q
        v32_s[pl.ds(r0, T), :] = v
        vt_s[j] = v.T.astype(bf16)
        gates(0, j, q)
        gates(1, j, q)
        return carry
    lax.fori_loop(0, nc, gates_body, 0, unroll=2)

    def intra(d, j):
        r0 = pl.multiple_of(j * T, T)

        def rows(ref2, lo_r, n):
            return ref2[pl.ds(r0 + lo_r, n), :]

        def srows(ref3, lo_r, n):
            return ref3[d, pl.ds(r0 + lo_r, n), :]

        nsub = T // HG_SUB
        o_blk = [None] * nsub

        def add(idx, val):
            o_blk[idx] = val if o_blk[idx] is None else o_blk[idx] + val

        def levels():
            b_all = srows(b_s, 0, T)
            q_all = rows(q32_s, 0, T)
            k_all = srows(k_s, 0, T)
            a_tot = None
            for h in HG_LEVELS:
                blk = 2 * h
                ref_off = h - 1 if d == 0 else h
                ref = jnp.concatenate([jnp.broadcast_to(srows(b_s, base + ref_off, 1), (blk, D_HEAD))
                                       for base in range(0, T, blk)], axis=0)
                is_q = q_rows[(h, d)]
                qx = q_all * jnp.exp2(jnp.where(is_q, b_all - ref, NEG_BIG))
                kx = k_all * jnp.exp2(jnp.where(is_q, NEG_BIG, ref - b_all))
                a = _dot_nt(qx.astype(bf16), kx.astype(bf16))
                if blk < T:
                    a = jnp.where(same_blk[h], a, 0.0)
                a_tot = a if a_tot is None else a_tot + a
            ov = _dot(a_tot.astype(bf16), rows(v32_s, 0, T).astype(bf16))
            for p in range(nsub):
                add(p, ov[p * HG_SUB:(p + 1) * HG_SUB])

        def diag(lo_r):
            bs = srows(b_s, lo_r, HG_SUB)
            qs = rows(q32_s, lo_r, HG_SUB)

            def keyrow(read, s):
                return jnp.where(low_half, read(lo_r + s, 1), read(lo_r + HG_HALF + s, 1))

            acc = None
            for s in range(HG_HALF):
                valid = (sub_q >= s) if d == 0 else (sub_q <= s)
                b_key = keyrow(lambda r, n: srows(b_s, r, n), s)
                ee = jnp.exp2(jnp.where(valid, bs - b_key, NEG_BIG))
                a_col = jnp.sum(qs * ee * keyrow(lambda r, n: srows(k_s, r, n), s), axis=-1, keepdims=True)
                term = a_col * keyrow(lambda r, n: rows(v32_s, r, n), s)
                acc = term if acc is None else acc + term
            add(lo_r // HG_SUB, acc)

        levels()
        for i in range(nsub):
            diag(i * HG_SUB)
        o_s[d, pl.ds(r0, T), :] = jnp.concatenate(o_blk, axis=0)

    def intra_body(j, carry):
        intra(0, j)
        intra(1, j)
        return carry
    lax.fori_loop(0, nc, intra_body, 0, unroll=4)

    def incr_body(j, carry):
        r0 = pl.multiple_of(j * T, T)
        vt = vt_s[j]
        for d in range(2):
            b = b_s[d, pl.ds(r0, T), :]
            b_end = b[T - 1:T] if d == 0 else b[0:1]
            u_s[d, j] = _dot(vt, (k_s[d, pl.ds(r0, T), :] * jnp.exp2(b_end - b)).astype(bf16))
        return carry
    lax.fori_loop(0, nc, incr_body, 0, unroll=4)

    def state_step(d, j):
        r0 = pl.multiple_of(j * T, T)
        st = st_s[d]
        o_s[d, pl.ds(r0, T), :] = o_s[d, pl.ds(r0, T), :] + _dot_nt(qx_s[d, pl.ds(r0, T), :], st.astype(bf16))
        st_s[d] = st * dec_s[d, pl.ds(j, 1), :] + u_s[d, j]

    def state_body(j, carry):
        state_step(0, j)
        state_step(1, nc - 1 - j)
        return carry
    lax.fori_loop(0, nc, state_body, 0, unroll=4)

    ng = ng_ref[...]

    def out_body(j, carry):
        r0 = pl.multiple_of(j * T, T)
        y = o_s[0, pl.ds(r0, T), :] + o_s[1, pl.ds(r0, T), :]
        y = _rms_rows(y, ng) * _silu(og_ref[0, pl.ds(r0, T), :].astype(f32))
        y_ref[0, pl.ds(r0, T), :] = y.astype(y_ref.dtype)
        return carry
    lax.fori_loop(0, nc, out_body, 0, unroll=4)

    for d in range(2):
        sout_ref[0, d, 0] = st_s[d].T


def hgrn_call(proj, proj32, lb_rows, norm_g, s0, layer):
    B, L, _ = proj.shape

    def col(cb):
        return pl.BlockSpec((1, L, LANES), lambda b, h, cb=cb: (b, 0, cb + h))

    kern = functools.partial(_hgrn_kernel, seq_len=L)
    return pl.pallas_call(
        kern,
        grid=(B, N_HEADS),
        in_specs=[col(CB_HQ), col(CF_HF), col(CF_HF + N_HEADS), col(CB_HI), col(CB_HG),
                  pl.BlockSpec((2 * N_HEADS, D_HEAD), lambda b, h: (0, 0)),
                  pl.BlockSpec((1, LANES), lambda b, h: (0, h)),
                  pl.BlockSpec((1, 1, 2, 1, D_HEAD, D_HEAD), lambda b, h: (b, layer, 0, h, 0, 0))],
        out_specs=[pl.BlockSpec((1, L, LANES), lambda b, h: (b, 0, h)),
                   pl.BlockSpec((1, 2, 1, D_HEAD, D_HEAD), lambda b, h: (b, 0, h, 0, 0))],
        out_shape=[jax.ShapeDtypeStruct((B, L, BRANCH_W), bf16),
                   jax.ShapeDtypeStruct((B, 2, N_HEADS, D_HEAD, D_HEAD), f32)],
        scratch_shapes=[pltpu.VMEM((2, L, D_HEAD), f32),
                        pltpu.VMEM((2, D_HEAD, D_HEAD), f32),
                        pltpu.VMEM((2, L, D_HEAD), bf16),
                        pltpu.VMEM((2, L // HG_CHUNK, D_HEAD), f32),
                        pltpu.VMEM((2, L // HG_CHUNK, D_HEAD, D_HEAD), f32),
                        pltpu.VMEM((L // HG_CHUNK, D_HEAD, HG_CHUNK), bf16),
                        pltpu.VMEM((2, L, D_HEAD), f32),
                        pltpu.VMEM((2, L, D_HEAD), f32),
                        pltpu.VMEM((L, D_HEAD), f32),
                        pltpu.VMEM((L, D_HEAD), f32)],
        compiler_params=_cparams(("arbitrary", "arbitrary")),
        name="hgrn",
    )(proj, proj32, proj32, proj, proj, lb_rows, norm_g, s0)


def _pair_swap(x):
    lane = lax.broadcasted_iota(jnp.int32, x.shape, 1)
    return jnp.where((lane % 2) == 0, pltpu.roll(x, LANES - 1, 1), pltpu.roll(x, 1, 1))


def _head_norm_pair(x, g_row):
    gi = lax.broadcasted_iota(jnp.int32, (LANES, LANES), 0) // AT_HD
    gj = lax.broadcasted_iota(jnp.int32, (LANES, LANES), 1) // AT_HD
    same_head = (gi == gj).astype(bf16)
    sq = x * x
    hi = sq.astype(bf16)
    lo = (sq - hi.astype(f32)).astype(bf16)
    ms = (_dot(hi, same_head) + _dot(lo, same_head)) * (1.0 / AT_HD)
    return x * lax.rsqrt(ms + EPS) * g_row


def _qkv_prep_kernel(aq_ref, ak_ref, av_ref, qg_ref, kg_ref, cos_ref, sin_ref,
                     q_ref, k_ref, v_ref, kn_ref, *, use_rope):
    qg = qg_ref[...]
    kg = kg_ref[...]
    if use_rope:
        cos = cos_ref[...]
        sin = sin_ref[...]
    for p in range(AT_HEADS // 2):
        x = _head_norm_pair(aq_ref[0, :, p * LANES:(p + 1) * LANES].astype(f32), qg)
        if use_rope:
            x = x * cos + _pair_swap(x) * sin
        x = (x * (AT_HD ** -0.5)).astype(bf16)
        q_ref[0, 2 * p] = x[:, :AT_HD]
        q_ref[0, 2 * p + 1] = x[:, AT_HD:]
    kn = _head_norm_pair(ak_ref[0].astype(f32), kg)
    kn_ref[0] = kn
    if use_rope:
        kn = kn * cos + _pair_swap(kn) * sin
    kb = kn.astype(bf16)
    for g in range(AT_KV_HEADS):
        k_ref[0, g] = kb[:, g * AT_HD:(g + 1) * AT_HD]
    v = av_ref[0].astype(f32)
    lane = lax.broadcasted_iota(jnp.int32, v.shape, 1)
    tail = jnp.where(lane == AT_HD, 1.0, 0.0)
    v_ref[0, 0] = jnp.where(lane < AT_HD, v, tail).astype(bf16)
    v_ref[0, 1] = jnp.where(lane < AT_HD, pltpu.roll(v, AT_HD, 1), tail).astype(bf16)


def qkv_prep_call(proj, qg_row, kg_row, cos_t, sin_t, use_rope):
    B, L, _ = proj.shape
    tm = min(L, 512)
    kern = functools.partial(_qkv_prep_kernel, use_rope=use_rope)
    return pl.pallas_call(
        kern,
        grid=(B, L // tm),
        in_specs=[pl.BlockSpec((1, tm, 4 * LANES), lambda b, m: (b, m, CB_AQ // 4)),
                  pl.BlockSpec((1, tm, LANES), lambda b, m: (b, m, CB_AK)),
                  pl.BlockSpec((1, tm, LANES), lambda b, m: (b, m, CB_AV)),
                  pl.BlockSpec((1, LANES), lambda b, m: (0, 0)),
                  pl.BlockSpec((1, LANES), lambda b, m: (0, 0)),
                  pl.BlockSpec((tm, LANES), lambda b, m: (m, 0)),
                  pl.BlockSpec((tm, LANES), lambda b, m: (m, 0))],
        out_specs=[pl.BlockSpec((1, AT_HEADS, tm, AT_HD), lambda b, m: (b, 0, m, 0)),
                   pl.BlockSpec((1, AT_KV_HEADS, tm, AT_HD), lambda b, m: (b, 0, m, 0)),
                   pl.BlockSpec((1, AT_KV_HEADS, tm, LANES), lambda b, m: (b, 0, m, 0)),
                   pl.BlockSpec((1, tm, LANES), lambda b, m: (b, m, 0))],
        out_shape=[jax.ShapeDtypeStruct((B, AT_HEADS, L, AT_HD), bf16),
                   jax.ShapeDtypeStruct((B, AT_KV_HEADS, L, AT_HD), bf16),
                   jax.ShapeDtypeStruct((B, AT_KV_HEADS, L, LANES), bf16),
                   jax.ShapeDtypeStruct((B, L, LANES), f32)],
        compiler_params=_cparams(("arbitrary", "arbitrary")),
        name="qkv_prep",
    )(proj, proj, proj, qg_row, kg_row, cos_t, sin_t)


def _attn_kernel(*refs, tq, tk, n_new, n_cache):
    if n_cache:
        q_ref, k_ref, v_ref, kc_ref, vc_ref, o_ref, s_s, m_s = refs
    else:
        q_ref, k_ref, v_ref, o_ref, s_s, m_s = refs
    rows = AT_GROUP * tq
    ncb = tk // LANES
    nkv = n_new + n_cache

    def chunk(new_ref, cache_ref, j):
        if j < n_new:
            return new_ref[0, 0, j * tk:(j + 1) * tk, :]
        return cache_ref[0, 0, (j - n_new) * tk:(j - n_new + 1) * tk, :]

    always = pl.program_id(2) >= 0

    @pl.when(always)
    def _():
        q = q_ref[0].reshape(rows, AT_HD)
        mx = None
        for j in range(nkv):
            s = _dot_nt(q, chunk(k_ref, kc_ref if n_cache else None, j))
            s_s[:, j * tk:(j + 1) * tk] = s
            cm = s[:, 0:LANES]
            for cb in range(1, ncb):
                cm = jnp.maximum(cm, s[:, cb * LANES:(cb + 1) * LANES])
            mx = cm if mx is None else jnp.maximum(mx, cm)
        m_s[...] = jnp.broadcast_to(jnp.max(mx, axis=-1, keepdims=True), (rows, LANES))

    @pl.when(pl.program_id(1) >= 0)
    def _():
        m_b = m_s[...]
        acc = None
        for j in range(nkv):
            ps = []
            for cb in range(ncb):
                off = j * tk + cb * LANES
                ps.append(jnp.exp(s_s[:, off:off + LANES] - m_b).astype(bf16))
            pv = _dot(jnp.concatenate(ps, axis=1), chunk(v_ref, vc_ref if n_cache else None, j))
            acc = pv if acc is None else acc + pv
        o = acc[:, :AT_HD] * (1.0 / acc[:, AT_HD:AT_HD + 1])
        for g in range(AT_GROUP):
            o_ref[0, :, g * AT_HD:(g + 1) * AT_HD] = o[g * tq:(g + 1) * tq].astype(o_ref.dtype)


def attn_call(q, k, v, k_cache=None, v_cache=None):
    B, _, Lq, _ = q.shape
    L = k.shape[2]
    P = 0 if k_cache is None else k_cache.shape[2]
    tq = 256
    tk = 512 if (L % 512 == 0 and P % 512 == 0) else L
    assert L % tk == 0 and P % tk == 0
    n_new, n_cache = L // tk, P // tk
    nq = Lq // tq
    kern = functools.partial(_attn_kernel, tq=tq, tk=tk, n_new=n_new, n_cache=n_cache)
    rows = AT_GROUP * tq

    def kv_spec(length, width):
        return pl.BlockSpec((1, 1, length, width), lambda b, g, i: (b, g, 0, 0))

    in_specs = [pl.BlockSpec((1, AT_GROUP, tq, AT_HD), lambda b, g, i: (b, g, i, 0)),
                kv_spec(L, AT_HD), kv_spec(L, LANES)]
    args = [q, k, v]
    if n_cache:
        in_specs += [kv_spec(P, AT_HD), kv_spec(P, LANES)]
        args += [k_cache, v_cache]
    return pl.pallas_call(
        kern,
        grid=(B, AT_KV_HEADS, nq),
        in_specs=in_specs,
        out_specs=pl.BlockSpec((1, tq, AT_GROUP * AT_HD), lambda b, g, i: (b, i, g)),
        out_shape=jax.ShapeDtypeStruct((B, Lq, AT_HEADS * AT_HD), bf16),
        scratch_shapes=[pltpu.VMEM((rows, L + P), f32),
                        pltpu.VMEM((rows, LANES), f32)],
        compiler_params=_cparams(("arbitrary", "arbitrary", "arbitrary")),
        name="attention",
    )(*args)


def _cache_layouts(cache_k_l, cache_v_l):
    kc = jnp.swapaxes(cache_k_l, 1, 2).astype(bf16)
    vc = jnp.swapaxes(cache_v_l, 1, 2).astype(bf16)
    ones = jnp.ones(vc.shape[:-1] + (1,), bf16)
    zeros = jnp.zeros(vc.shape[:-1] + (LANES - AT_HD - 1,), bf16)
    return kc, jnp.concatenate([vc, ones, zeros], axis=-1)


def _merge_kernel(x_ref, yml_ref, yhg_ref, yat_ref, bg_ref, mod_ref, wb_ref, wo_ref, o_ref):
    merged = None
    for n, y_ref in enumerate((yml_ref, yhg_ref, yat_ref)):
        pn = _dot(y_ref[0], wb_ref[n])
        gn = _sigmoid(bg_ref[0, :, n * D_MODEL:(n + 1) * D_MODEL].astype(f32))
        merged = gn * pn if merged is None else merged + gn * pn
    out = _dot(merged.astype(bf16), wo_ref[...])
    o_ref[0] = x_ref[0] + mod_ref[0, 2:3, :] * out


def merge_call(x, y_ml, y_hg, y_at, proj, mod, wb, wo, shared_mod):
    B, L, _ = x.shape
    tm = min(L, 512)
    mod_map = (lambda b, m: (0, 0, 0)) if shared_mod else (lambda b, m: (b, 0, 0))
    yspec = pl.BlockSpec((1, tm, BRANCH_W), lambda b, m: (b, m, 0))
    return pl.pallas_call(
        _merge_kernel,
        grid=(B, L // tm),
        in_specs=[pl.BlockSpec((1, tm, D_MODEL), lambda b, m: (b, m, 0)),
                  yspec, yspec, yspec,
                  pl.BlockSpec((1, tm, 3 * D_MODEL), lambda b, m: (b, m, CB_BG)),
                  pl.BlockSpec((1, 8, D_MODEL), mod_map),
                  pl.BlockSpec((3, BRANCH_W, D_MODEL), lambda b, m: (0, 0, 0)),
                  pl.BlockSpec((D_MODEL, D_MODEL), lambda b, m: (0, 0))],
        out_specs=pl.BlockSpec((1, tm, D_MODEL), lambda b, m: (b, m, 0)),
        out_shape=jax.ShapeDtypeStruct((B, L, D_MODEL), f32),
        compiler_params=_cparams(("arbitrary", "arbitrary")),
        name="merge_out",
    )(x, y_ml, y_hg, y_at, proj, mod, wb, wo)


FFN_SPLITS = ((0, 1536), (1536, FFN_HIDDEN))


def _ffn_kernel(x_ref, mod_ref, g_ref, wi_ref, wd_ref, fg_ref, o_ref, *, final_norm):
    x = x_ref[0]
    hb = (_rms_rows(x, g_ref[...]) * (1.0 + mod_ref[0, 4:5, :]) + mod_ref[0, 3:4, :]).astype(bf16)
    acc = None
    for lo, hi in FFN_SPLITS:
        gate = _dot(hb, wi_ref[:, lo:hi])
        up = _dot(hb, wi_ref[:, FFN_HIDDEN + lo:FFN_HIDDEN + hi])
        part = _dot((_silu(gate) * up).astype(bf16), wd_ref[lo:hi, :])
        acc = part if acc is None else acc + part
    y = x + mod_ref[0, 5:6, :] * acc
    if final_norm:
        y = _rms_rows(y, fg_ref[...])
    o_ref[0] = y


def ffn_call(x, mod, g_row, w_in, w_out, fg_row, shared_mod, final_norm):
    B, L, _ = x.shape
    tm = min(L, 512)
    mod_map = (lambda b, m: (0, 0, 0)) if shared_mod else (lambda b, m: (b, 0, 0))
    kern = functools.partial(_ffn_kernel, final_norm=final_norm)
    resident = pl.Buffered(1)
    return pl.pallas_call(
        kern,
        grid=(B, L // tm),
        in_specs=[pl.BlockSpec((1, tm, D_MODEL), lambda b, m: (b, m, 0)),
                  pl.BlockSpec((1, 8, D_MODEL), mod_map),
                  pl.BlockSpec((1, D_MODEL), lambda b, m: (0, 0)),
                  pl.BlockSpec((D_MODEL, 2 * FFN_HIDDEN), lambda b, m: (0, 0), pipeline_mode=resident),
                  pl.BlockSpec((FFN_HIDDEN, D_MODEL), lambda b, m: (0, 0), pipeline_mode=resident),
                  pl.BlockSpec((1, D_MODEL), lambda b, m: (0, 0))],
        out_specs=pl.BlockSpec((1, tm, D_MODEL), lambda b, m: (b, m, 0)),
        out_shape=jax.ShapeDtypeStruct((B, L, D_MODEL), f32),
        compiler_params=_cparams(("arbitrary", "arbitrary")),
        name="ffn",
    )(x, mod, g_row, w_in, w_out, fg_row)


def _reorder_w_in(w):
    o_mg, o_hq, o_hf, o_hi, o_bg, o_end = 2048, 2064, 2576, 3600, 5392, 8464
    d = w.shape[0]
    sec16 = [w[:, o_bg:o_end], w[:, :o_mg], w[:, o_hq:o_hf], w[:, o_hi:o_bg]]
    n16 = (o_end - o_bg) + o_mg + (o_hf - o_hq) + (o_bg - o_hi)
    sec32 = [w[:, o_hf:o_hi], w[:, o_mg:o_hq]]
    n32 = (o_hi - o_hf) + (o_hq - o_mg)
    return jnp.concatenate(sec16 + [jnp.zeros((d, N16_COLS - n16), w.dtype)] + sec32
                           + [jnp.zeros((d, N32_COLS - n32), w.dtype)], axis=1).astype(bf16)


def _rope_tables(L):
    rows = L // GRID_W
    row = jnp.repeat(jnp.arange(rows, dtype=f32), GRID_W)
    colp = jnp.tile(jnp.arange(GRID_W, dtype=f32), rows)
    n_freq = AT_HD // 4
    inv = ROPE_THETA ** (-jnp.arange(n_freq, dtype=f32) / n_freq)
    ang = jnp.concatenate([row[:, None] * inv, colp[:, None] * inv], axis=-1)
    cos = jnp.repeat(jnp.cos(ang), 2, axis=-1)
    sin = jnp.repeat(jnp.sin(ang), 2, axis=-1)
    sign = jnp.tile(jnp.array([-1.0, 1.0], f32), AT_HD // 2)
    sin = sin * sign
    return jnp.tile(cos, (1, 2)), jnp.tile(sin, (1, 2))


def _mod_rows(ada_rows):
    r = ada_rows.shape[0]
    m = ada_rows.reshape(r, 6, D_MODEL)
    return jnp.concatenate([m, jnp.zeros((r, 2, D_MODEL), f32)], axis=1)


def kernel(x_prompt, x_sample, c, cache_k, cache_v, state_ml_C, state_ml_n, state_ml_m, state_hg_S, c_ctx, w_ada, b_ada, norm1_g, norm2_g, w_in, ml_gate_b, ml_norm_g, hg_lb_logits, hg_norm_g, q_norm_g, k_norm_g, w_branch, w_out, w_ffn_in, w_ffn_out, final_g):
    Bp, Lp, _ = x_prompt.shape
    Bs, Ls, _ = x_sample.shape

    n_rows = 16
    cvec = jnp.concatenate([c_ctx[None, :], c, jnp.zeros((n_rows - 1 - Bs, D_MODEL), f32)], axis=0)
    ada = ada_call(cvec, w_ada, b_ada)
    lb_all = lb_call(hg_lb_logits)
    cos_t, sin_t = _rope_tables(Ls)

    zeros_c = jnp.zeros((Bp, 1, 2, N_HEADS, D_HEAD, D_HEAD), f32)
    zeros_n = jnp.zeros((Bp, 1, 2, N_HEADS, 1, D_HEAD), f32)
    zeros_m = jnp.zeros((Bp * 2 * N_HEADS,), f32)
    n_state = state_ml_n.reshape(Bs, DEPTH, 2, N_HEADS, 1, D_HEAD)

    xp, xs = x_prompt.reshape(1, Bp * Lp, D_MODEL), x_sample
    nk, nv, nC, nn_, nm, nS = [], [], [], [], [], []
    for l in range(DEPTH):
        w_l = _reorder_w_in(w_in[l])
        wb_l = w_branch[l].astype(bf16)
        wo_l = w_out[l].astype(bf16)
        wfi_l = w_ffn_in[l].astype(bf16)
        wfo_l = w_ffn_out[l].astype(bf16)
        g1 = norm1_g[l][None, :]
        g2 = norm2_g[l][None, :]
        mlg = ml_norm_g[l][None, :]
        hgg = hg_norm_g[l][None, :]
        qg = jnp.tile(q_norm_g[l], 2)[None, :]
        kg = jnp.tile(k_norm_g[l], 2)[None, :]
        gate_b = ml_gate_b[l].reshape(-1).astype(f32)
        fg = final_g[None, :]
        last = l == DEPTH - 1

        mod_ctx = _mod_rows(ada[l, 0:1])
        mod_lat = _mod_rows(ada[l, 1:1 + Bs])

        proj_flat, proj32 = in_proj_call(xp, mod_ctx, g1, w_l, True)
        proj = proj_flat.reshape(Bp, Lp, N16_COLS)
        proj32 = proj32.reshape(Bp, Lp, N32_COLS)
        y_ml, c_f, n_f, m_f = mlstm_call(proj, proj32, gate_b, mlg, zeros_c, zeros_n, zeros_m, 0)
        y_hg, s_f = hgrn_call(proj, proj32, lb_all[l], hgg, zeros_c, 0)
        q_h, k_h, v_h, k_n = qkv_prep_call(proj, qg, kg, cos_t[:Lp], sin_t[:Lp], False)
        y_at = attn_call(q_h, k_h, v_h)
        flat = (1, Bp * Lp, BRANCH_W)
        xp = merge_call(xp, y_ml.reshape(flat), y_hg.reshape(flat), y_at.reshape(flat), proj_flat,
                        mod_ctx, wb_l, wo_l, True)
        xp = ffn_call(xp, mod_ctx, g2, wfi_l, wfo_l, fg, True, last)
        nk.append(k_n.reshape(Bp, Lp, AT_KV_HEADS, AT_HD))
        nv.append(proj[:, :, CB_AV * LANES:(CB_AV + 1) * LANES].astype(f32).reshape(Bp, Lp, AT_KV_HEADS, AT_HD))
        nC.append(c_f)
        nn_.append(n_f[:, :, :, 0, :])
        nm.append(m_f[:, :, :, 0, 0])
        nS.append(s_f)

        proj, proj32 = in_proj_call(xs, mod_lat, g1, w_l, False)
        y_ml, _, _, _ = mlstm_call(proj, proj32, gate_b, mlg, state_ml_C, n_state,
                                   state_ml_m[:, l].reshape(-1).astype(f32), l)
        y_hg, _ = hgrn_call(proj, proj32, lb_all[l], hgg, state_hg_S, l)
        q_h, k_h, v_h, _ = qkv_prep_call(proj, qg, kg, cos_t, sin_t, True)
        k_c, v_c = _cache_layouts(cache_k[:, l], cache_v[:, l])
        y_at = attn_call(q_h, k_h, v_h, k_c, v_c)
        xs = merge_call(xs, y_ml, y_hg, y_at, proj, mod_lat, wb_l, wo_l, False)
        xs = ffn_call(xs, mod_lat, g2, wfi_l, wfo_l, fg, False, last)

    return (xp.reshape(Bp, Lp, D_MODEL), xs, jnp.stack(nk, axis=1), jnp.stack(nv, axis=1), jnp.stack(nC, axis=1),
            jnp.stack(nn_, axis=1), jnp.stack(nm, axis=1), jnp.stack(nS, axis=1))
```

```python
import functools
import math

import jax
import jax.numpy as jnp
from jax import lax
from jax.experimental import pallas as pl
from jax.experimental.pallas import tpu as pltpu

f32 = jnp.float32
bf16 = jnp.bfloat16

D_MODEL = 1024
DEPTH = 4
N_HEADS = 4
D_HEAD = 128
AT_HEADS = 8
AT_KV_HEADS = 2
AT_GROUP = AT_HEADS // AT_KV_HEADS
AT_HD = 64
BRANCH_W = 512
FFN_HIDDEN = 2816
GRID_W = 64
ROPE_THETA = 10000.0
EPS = 1e-6

LANES = 128
SUBLANES = 8
VMEM_LIMIT = 52 * 1024 * 1024

CB_BG = 0
CB_MQ = 24
CB_MK = 28
CB_MV = 32
CB_MO = 36
CB_HQ = 40
CB_HI = 44
CB_HG = 48
CB_AQ = 52
CB_AK = 56
CB_AV = 57
N16_BLOCKS = 60
N16_COLS = N16_BLOCKS * LANES
CF_HF = 0
CF_MG = 8
N32_BLOCKS = 10
N32_COLS = N32_BLOCKS * LANES
PROJ_TN = N32_COLS
N16_TILES = N16_COLS // PROJ_TN

ML_CHUNK = 128
ML_EXT = 144
HG_CHUNK = 64
HG_SUB = 8
HG_HALF = 4
HG_LEVELS = (32, 16, 8, 4)
NEG_BIG = -1e30
LOG2E = 1.4426950408889634


def _cparams(sem):
    return pltpu.CompilerParams(dimension_semantics=sem, vmem_limit_bytes=VMEM_LIMIT)


def _dot(a, b):
    return jnp.dot(a, b, preferred_element_type=f32)


def _dot_nt(a, b):
    return lax.dot_general(a, b, (((1,), (1,)), ((), ())), preferred_element_type=f32)


def _dot_tn(a, b):
    return lax.dot_general(a, b, (((0,), (0,)), ((), ())), preferred_element_type=f32)


def _sigmoid(x):
    return 1.0 / (1.0 + jnp.exp(-x))


def _silu(x):
    return x * _sigmoid(x)


def _log_sigmoid(x):
    return jnp.minimum(x, 0.0) - jnp.log1p(jnp.exp(-jnp.abs(x)))


def _split3(x):
    hi = x.astype(bf16)
    r1 = x - hi.astype(f32)
    mid = r1.astype(bf16)
    lo = (r1 - mid.astype(f32)).astype(bf16)
    return hi, mid, lo


def _rms_rows(x, g_row):
    ms = jnp.mean(x * x, axis=-1, keepdims=True)
    return x * lax.rsqrt(ms + EPS) * g_row


def _ada_kernel(c_ref, w_ref, b_ref, o_ref):
    s = _silu(c_ref[...]).astype(bf16)
    o_ref[0] = _dot(s, w_ref[0].astype(bf16)) + b_ref[0]


def ada_call(cvec, w_ada, b_ada):
    rows = cvec.shape[0]
    tn = 1536
    n6 = 6 * D_MODEL
    return pl.pallas_call(
        _ada_kernel,
        grid=(DEPTH, n6 // tn),
        in_specs=[pl.BlockSpec((rows, D_MODEL), lambda l, n: (0, 0)),
                  pl.BlockSpec((1, D_MODEL, tn), lambda l, n: (l, 0, n)),
                  pl.BlockSpec((1, 1, tn), lambda l, n: (l, 0, n))],
        out_specs=pl.BlockSpec((1, rows, tn), lambda l, n: (l, 0, n)),
        out_shape=jax.ShapeDtypeStruct((DEPTH, rows, n6), f32),
        compiler_params=_cparams(("arbitrary", "arbitrary")),
        name="ada",
    )(cvec, w_ada, b_ada.reshape(DEPTH, 1, n6))


def _lb_kernel(x_ref, o_ref):
    xs = [x_ref[l] for l in range(DEPTH)]
    mx = xs[0]
    for l in range(1, DEPTH):
        mx = jnp.maximum(mx, xs[l])
    es = [jnp.exp(x - mx) for x in xs]
    tot = es[0]
    for l in range(1, DEPTH):
        tot = tot + es[l]
    sm = [e / tot for e in es]
    run = sm[0]
    o_ref[0] = run - sm[0]
    for l in range(1, DEPTH):
        run = run + sm[l]
        o_ref[l] = run - sm[0]


def lb_call(logits):
    x = logits.astype(f32).reshape(DEPTH, 2 * N_HEADS, D_HEAD)
    return pl.pallas_call(
        _lb_kernel,
        out_shape=jax.ShapeDtypeStruct((DEPTH, 2 * N_HEADS, D_HEAD), f32),
        name="hg_lower_bounds",
    )(x)


def _in_proj_kernel(x_ref, mod_ref, g_ref, w_ref, o16_ref, o32_ref, h_s):
    n = pl.program_id(2)

    @pl.when(n == 0)
    def _():
        x = x_ref[0]
        sh = mod_ref[0, 0:1, :]
        sc = mod_ref[0, 1:2, :]
        h = _rms_rows(x, g_ref[...]) * (1.0 + sc) + sh
        h_s[...] = h.astype(bf16)

    @pl.when(n < N16_TILES)
    def _():
        o16_ref[0] = _dot(h_s[...], w_ref[...]).astype(bf16)

    @pl.when(n == N16_TILES)
    def _():
        o32_ref[0] = _dot(h_s[...], w_ref[...])


def in_proj_call(x, mod, g_row, w, shared_mod):
    B, L, _ = x.shape
    tm = min(L, 1024)
    tn = PROJ_TN
    mod_map = (lambda b, m, n: (0, 0, 0)) if shared_mod else (lambda b, m, n: (b, 0, 0))
    return pl.pallas_call(
        _in_proj_kernel,
        grid=(B, L // tm, N16_TILES + 1),
        in_specs=[pl.BlockSpec((1, tm, D_MODEL), lambda b, m, n: (b, m, 0)),
                  pl.BlockSpec((1, 8, D_MODEL), mod_map),
                  pl.BlockSpec((1, D_MODEL), lambda b, m, n: (0, 0)),
                  pl.BlockSpec((D_MODEL, tn), lambda b, m, n: (0, n))],
        out_specs=[pl.BlockSpec((1, tm, tn), lambda b, m, n: (b, m, jnp.minimum(n, N16_TILES - 1))),
                   pl.BlockSpec((1, tm, tn), lambda b, m, n: (b, m, 0))],
        out_shape=[jax.ShapeDtypeStruct((B, L, N16_COLS), bf16),
                   jax.ShapeDtypeStruct((B, L, N32_COLS), f32)],
        scratch_shapes=[pltpu.VMEM((tm, D_MODEL), bf16)],
        compiler_params=_cparams(("arbitrary", "arbitrary", "arbitrary")),
        name="in_proj",
    )(x, mod, g_row, w)


def _mlstm_kernel(bias_ref, m0_ref, q_ref, k_ref, v_ref, mo_ref, g_ref, ng_ref, c0_ref, n0_ref,
                  y_ref, cout_ref, nout_ref, mout_ref,
                  gt_s, rw_s, ucb_s, vt_s, ht_s, ct_s, m_s, *, seq_len):
    T = ML_CHUNK
    assert T == LANES
    nc = seq_len // T
    b_idx = pl.program_id(0)
    h = pl.program_id(1)
    scale = D_HEAD ** -0.5
    ext_rows = jnp.concatenate([jnp.ones((1, T), f32), jnp.zeros((ML_EXT - D_HEAD - 1, T), f32)], axis=0)

    def tr_body(j, carry):
        r0 = pl.multiple_of(j * T, T)
        gt_s[:, pl.ds(r0, T)] = g_ref[0, pl.ds(r0, T), :].T
        vt = v_ref[0, pl.ds(r0, T), :].astype(f32).T
        vt_s[:, pl.ds(r0, T)] = jnp.concatenate([vt, ext_rows], axis=0).astype(bf16)
        return carry
    lax.fori_loop(0, nc, tr_body, 0, unroll=2)

    i_f = gt_s[pl.ds(h, 1), :] + bias_ref[h]
    f_f = _log_sigmoid(gt_s[pl.ds(4 + h, 1), :] + bias_ref[4 + h])
    i_b = gt_s[pl.ds(8 + h, 1), :] + bias_ref[8 + h]
    f_b = _log_sigmoid(gt_s[pl.ds(12 + h, 1), :] + bias_ref[12 + h])
    zrow = jnp.zeros_like(i_f)
    rw_s[...] = jnp.concatenate([f_f, i_f, f_b, i_b, zrow, zrow, zrow, zrow], axis=0)

    ui = lax.broadcasted_iota(jnp.int32, (T, T), 0)
    si = lax.broadcasted_iota(jnp.int32, (T, T), 1)
    tri = jnp.concatenate([(ui <= si).astype(bf16), (ui >= si).astype(bf16)], axis=1)
    valid_f = ui <= si
    valid_b = ui >= si

    def cs_body(j, carry):
        r0 = pl.multiple_of(j * T, T)
        rows = rw_s[:, pl.ds(r0, T)]
        hi, mid, lo = _split3(rows)
        cs3 = _dot(jnp.concatenate([hi, mid, lo, jnp.zeros_like(hi)], axis=0), tri)
        cs = cs3[0:8] + cs3[8:16] + cs3[16:24]
        b_f = cs[0:1, 0:T]
        b_b = cs[2:3, T:2 * T]
        rw_s[:, pl.ds(r0, T)] = jnp.concatenate([b_f, rows[1:2], b_b, rows[3:8]], axis=0)
        ucb_s[0, pl.ds(r0, T), :] = jnp.broadcast_to(rows[1:2] - b_f, (T, T)).T
        ucb_s[1, pl.ds(r0, T), :] = jnp.broadcast_to(rows[3:4] - b_b, (T, T)).T
        return carry
    lax.fori_loop(0, nc, cs_body, 0, unroll=min(8, nc))

    for d in range(2):
        ct_s[d] = jnp.concatenate([c0_ref[0, 0, d, 0].T, n0_ref[0, 0, d, 0],
                                   jnp.zeros((ML_EXT - D_HEAD - 1, D_HEAD), f32)], axis=0)
        m_s[d] = jnp.full((1, LANES), m0_ref[b_idx * 2 * N_HEADS + d * N_HEADS + h], f32)

    def step(d, j):
        r0 = pl.multiple_of(j * T, T)
        b_row = rw_s[pl.ds(2 * d, 1), pl.ds(r0, T)]
        b_end = b_row[:, T - 1:T] if d == 0 else b_row[:, 0:1]
        b_end_l = jnp.broadcast_to(b_end, (1, LANES))
        ucb = ucb_s[d, pl.ds(r0, T), :]

        q = q_ref[0, pl.ds(r0, T), :].astype(f32) * scale
        kb = k_ref[0, pl.ds(r0, T), :]
        k = kb.astype(f32)
        qb = q.astype(bf16)
        vt = vt_s[:, pl.ds(r0, T)]
        ct = ct_s[d]
        m_prev = m_s[d]

        d_t = jnp.where(valid_f if d == 0 else valid_b, ucb + b_row, NEG_BIG)
        m_state = b_row + m_prev
        m_t = jnp.maximum(m_state, jnp.max(d_t, axis=0, keepdims=True))
        w_state = jnp.exp(m_state - m_t)
        s_t = _dot_nt(kb, qb) * jnp.exp(d_t - m_t)
        tot = _dot(vt, s_t.astype(bf16)) + w_state * _dot_nt(ct.astype(bf16), qb)
        den = tot[D_HEAD:D_HEAD + 1]
        ht_s[d, :, pl.ds(r0, T)] = tot[:D_HEAD] * (1.0 / jnp.maximum(jnp.abs(den), jnp.exp(-m_t)))

        gcb = ucb + b_end_l
        m_new = jnp.maximum(b_end_l + m_prev, jnp.max(gcb, axis=0, keepdims=True))
        kw = k * jnp.exp(gcb - m_new)
        decay = jnp.exp(b_end_l + m_prev - m_new)
        ct_s[d] = decay * ct + _dot(vt, kw.astype(bf16))
        m_s[d] = m_new

    def loop_body(j, carry):
        step(0, j)
        step(1, nc - 1 - j)
        return carry
    lax.fori_loop(0, nc, loop_body, 0, unroll=min(4, nc))

    ng = ng_ref[...]

    def out_body(j, carry):
        r0 = pl.multiple_of(j * T, T)
        y_t = ht_s[0, :, pl.ds(r0, T)] + ht_s[1, :, pl.ds(r0, T)]
        ms = jnp.mean(y_t * y_t, axis=0, keepdims=True)
        y = (y_t * lax.rsqrt(ms + EPS)).T * ng * _sigmoid(mo_ref[0, pl.ds(r0, T), :].astype(f32))
        y_ref[0, pl.ds(r0, T), :] = y.astype(y_ref.dtype)
        return carry
    lax.fori_loop(0, nc, out_body, 0, unroll=2)

    for d in range(2):
        cout_ref[0, d, 0] = ct_s[d, 0:D_HEAD, :].T
        nout_ref[0, d, 0] = ct_s[d, D_HEAD:D_HEAD + 1, :]
        mout_ref[0, d, 0] = m_s[d]


def mlstm_call(proj, proj32, gate_bias, norm_g, c0, n0, m0, layer):
    B, L, _ = proj.shape

    def col(cb):
        return pl.BlockSpec((1, L, LANES), lambda b, h, cb=cb: (b, 0, cb + h))

    smem = pl.BlockSpec(memory_space=pltpu.SMEM)
    kern = functools.partial(_mlstm_kernel, seq_len=L)
    return pl.pallas_call(
        kern,
        grid=(B, N_HEADS),
        in_specs=[smem, smem,
                  col(CB_MQ), col(CB_MK), col(CB_MV), col(CB_MO),
                  pl.BlockSpec((1, L, LANES), lambda b, h: (b, 0, CF_MG)),
                  pl.BlockSpec((1, LANES), lambda b, h: (0, h)),
                  pl.BlockSpec((1, 1, 2, 1, D_HEAD, D_HEAD), lambda b, h: (b, layer, 0, h, 0, 0)),
                  pl.BlockSpec((1, 1, 2, 1, 1, D_HEAD), lambda b, h: (b, layer, 0, h, 0, 0))],
        out_specs=[pl.BlockSpec((1, L, LANES), lambda b, h: (b, 0, h)),
                   pl.BlockSpec((1, 2, 1, D_HEAD, D_HEAD), lambda b, h: (b, 0, h, 0, 0)),
                   pl.BlockSpec((1, 2, 1, 1, D_HEAD), lambda b, h: (b, 0, h, 0, 0)),
                   pl.BlockSpec((1, 2, 1, 1, LANES), lambda b, h: (b, 0, h, 0, 0))],
        out_shape=[jax.ShapeDtypeStruct((B, L, BRANCH_W), bf16),
                   jax.ShapeDtypeStruct((B, 2, N_HEADS, D_HEAD, D_HEAD), f32),
                   jax.ShapeDtypeStruct((B, 2, N_HEADS, 1, D_HEAD), f32),
                   jax.ShapeDtypeStruct((B, 2, N_HEADS, 1, LANES), f32)],
        scratch_shapes=[pltpu.VMEM((LANES, L), f32),
                        pltpu.VMEM((8, L), f32),
                        pltpu.VMEM((2, L, LANES), f32),
                        pltpu.VMEM((ML_EXT, L), bf16),
                        pltpu.VMEM((2, D_HEAD, L), f32),
                        pltpu.VMEM((2, ML_EXT, D_HEAD), f32),
                        pltpu.VMEM((2, 1, LANES), f32)],
        compiler_params=_cparams(("arbitrary", "arbitrary")),
        name="mlstm",
    )(gate_bias, m0, proj, proj, proj, proj, proj32, norm_g, c0, n0)


def _hgrn_kernel(q_ref, f0_ref, f1_ref, i_ref, og_ref, lb_ref, ng_ref, s0_ref,
                 y_ref, sout_ref, o_s, st_s, qx_s, dec_s, u_s, vt_s, k_s, b_s, q32_s, v32_s, *, seq_len):
    T = HG_CHUNK
    nc = seq_len // T
    h = pl.program_id(1)
    f_refs = (f0_ref, f1_ref)

    lbs, loglb, log1mlb = [], [], []
    for d in range(2):
        lb = lb_ref[pl.ds(d * N_HEADS + h, 1), :]
        lbs.append(lb)
        loglb.append(jnp.log(lb))
        log1mlb.append(jnp.log1p(-lb))
        st_s[d] = s0_ref[0, 0, d, 0].T

    ti = lax.broadcasted_iota(jnp.int32, (T, T), 0)
    ui = lax.broadcasted_iota(jnp.int32, (T, T), 1)
    tri = ((ui <= ti).astype(bf16), (ui >= ti).astype(bf16))
    sub_t = lax.broadcasted_iota(jnp.int32, (HG_SUB, D_HEAD), 0)
    sub_q = sub_t % HG_HALF
    low_half = sub_t < HG_HALF
    row_i = lax.broadcasted_iota(jnp.int32, (T, D_HEAD), 0)
    tt_r = lax.broadcasted_iota(jnp.int32, (T, T), 0)
    tt_c = lax.broadcasted_iota(jnp.int32, (T, T), 1)
    q_rows = {(hh, dd): ((row_i % (2 * hh)) >= hh) if dd == 0 else ((row_i % (2 * hh)) < hh)
              for hh in HG_LEVELS for dd in range(2)}
    same_blk = {hh: (tt_r // (2 * hh)) == (tt_c // (2 * hh)) for hh in HG_LEVELS}

    def gates(d, j, q):
        r0 = pl.multiple_of(j * T, T)
        hf = f_refs[d][0, pl.ds(r0, T), :]
        e = jnp.exp(-jnp.abs(hf))
        sig_neg = jnp.where(hf >= 0.0, e, 1.0) / (1.0 + e)
        logsig = jnp.minimum(hf, 0.0) - jnp.log(1.0 + e)
        cterm = log1mlb[d] + logsig
        amax = jnp.maximum(loglb[d], cterm)
        logf = amax + jnp.log(1.0 + jnp.exp(-jnp.abs(loglb[d] - cterm)))
        k = (1.0 - lbs[d]) * sig_neg
        hi, mid, lo = _split3(logf * LOG2E)
        b = _dot(tri[d], hi) + _dot(tri[d], mid) + _dot(tri[d], lo)
        b_end = b[T - 1:T] if d == 0 else b[0:1]
        k_s[d, pl.ds(r0, T), :] = k
        b_s[d, pl.ds(r0, T), :] = b
        qx_s[d, pl.ds(r0, T), :] = (q * jnp.exp2(b)).astype(bf16)
        dec_s[d, pl.ds(j, 1), :] = jnp.exp2(b_end)

    def gates_body(j, carry):
        r0 = pl.multiple_of(j * T, T)
        q = q_ref[0, pl.ds(r0, T), :].astype(f32)
        v = i_ref[0, pl.ds(r0, T), :].astype(f32)
        q32_s[pl.ds(r0, T), :] = q
        v32_s[pl.ds(r0, T), :] = v
        vt_s[j] = v.T.astype(bf16)
        gates(0, j, q)
        gates(1, j, q)
        return carry
    lax.fori_loop(0, nc, gates_body, 0, unroll=2)

    def intra(d, j):
        r0 = pl.multiple_of(j * T, T)

        def rows(ref2, lo_r, n):
            return ref2[pl.ds(r0 + lo_r, n), :]

        def srows(ref3, lo_r, n):
            return ref3[d, pl.ds(r0 + lo_r, n), :]

        nsub = T // HG_SUB
        o_blk = [None] * nsub

        def add(idx, val):
            o_blk[idx] = val if o_blk[idx] is None else o_blk[idx] + val

        def levels():
            b_all = srows(b_s, 0, T)
            q_all = rows(q32_s, 0, T)
            k_all = srows(k_s, 0, T)
            a_tot = None
            for h in HG_LEVELS:
                blk = 2 * h
                ref_off = h - 1 if d == 0 else h
                ref = jnp.concatenate([jnp.broadcast_to(srows(b_s, base + ref_off, 1), (blk, D_HEAD))
                                       for base in range(0, T, blk)], axis=0)
                is_q = q_rows[(h, d)]
                qx = q_all * jnp.exp2(jnp.where(is_q, b_all - ref, NEG_BIG))
                kx = k_all * jnp.exp2(jnp.where(is_q, NEG_BIG, ref - b_all))
                a = _dot_nt(qx.astype(bf16), kx.astype(bf16))
                if blk < T:
                    a = jnp.where(same_blk[h], a, 0.0)
                a_tot = a if a_tot is None else a_tot + a
            ov = _dot(a_tot.astype(bf16), rows(v32_s, 0, T).astype(bf16))
            for p in range(nsub):
                add(p, ov[p * HG_SUB:(p + 1) * HG_SUB])

        def diag(lo_r):
            bs = srows(b_s, lo_r, HG_SUB)
            qs = rows(q32_s, lo_r, HG_SUB)

            def keyrow(read, s):
                return jnp.where(low_half, read(lo_r + s, 1), read(lo_r + HG_HALF + s, 1))

            acc = None
            for s in range(HG_HALF):
                valid = (sub_q >= s) if d == 0 else (sub_q <= s)
                b_key = keyrow(lambda r, n: srows(b_s, r, n), s)
                ee = jnp.exp2(jnp.where(valid, bs - b_key, NEG_BIG))
                a_col = jnp.sum(qs * ee * keyrow(lambda r, n: srows(k_s, r, n), s), axis=-1, keepdims=True)
                term = a_col * keyrow(lambda r, n: rows(v32_s, r, n), s)
                acc = term if acc is None else acc + term
            add(lo_r // HG_SUB, acc)

        levels()
        for i in range(nsub):
            diag(i * HG_SUB)
        o_s[d, pl.ds(r0, T), :] = jnp.concatenate(o_blk, axis=0)

    def intra_body(j, carry):
        intra(0, j)
        intra(1, j)
        return carry
    lax.fori_loop(0, nc, intra_body, 0, unroll=4)

    def incr_body(j, carry):
        r0 = pl.multiple_of(j * T, T)
        vt = vt_s[j]
        for d in range(2):
            b = b_s[d, pl.ds(r0, T), :]
            b_end = b[T - 1:T] if d == 0 else b[0:1]
            u_s[d, j] = _dot(vt, (k_s[d, pl.ds(r0, T), :] * jnp.exp2(b_end - b)).astype(bf16))
        return carry
    lax.fori_loop(0, nc, incr_body, 0, unroll=4)

    def state_step(d, j):
        r0 = pl.multiple_of(j * T, T)
        st = st_s[d]
        o_s[d, pl.ds(r0, T), :] = o_s[d, pl.ds(r0, T), :] + _dot_nt(qx_s[d, pl.ds(r0, T), :], st.astype(bf16))
        st_s[d] = st * dec_s[d, pl.ds(j, 1), :] + u_s[d, j]

    def state_body(j, carry):
        state_step(0, j)
        state_step(1, nc - 1 - j)
        return carry
    lax.fori_loop(0, nc, state_body, 0, unroll=4)

    ng = ng_ref[...]

    def out_body(j, carry):
        r0 = pl.multiple_of(j * T, T)
        y = o_s[0, pl.ds(r0, T), :] + o_s[1, pl.ds(r0, T), :]
        y = _rms_rows(y, ng) * _silu(og_ref[0, pl.ds(r0, T), :].astype(f32))
        y_ref[0, pl.ds(r0, T), :] = y.astype(y_ref.dtype)
        return carry
    lax.fori_loop(0, nc, out_body, 0, unroll=4)

    for d in range(2):
        sout_ref[0, d, 0] = st_s[d].T


def hgrn_call(proj, proj32, lb_rows, norm_g, s0, layer):
    B, L, _ = proj.shape

    def col(cb):
        return pl.BlockSpec((1, L, LANES), lambda b, h, cb=cb: (b, 0, cb + h))

    kern = functools.partial(_hgrn_kernel, seq_len=L)
    return pl.pallas_call(
        kern,
        grid=(B, N_HEADS),
        in_specs=[col(CB_HQ), col(CF_HF), col(CF_HF + N_HEADS), col(CB_HI), col(CB_HG),
                  pl.BlockSpec((2 * N_HEADS, D_HEAD), lambda b, h: (0, 0)),
                  pl.BlockSpec((1, LANES), lambda b, h: (0, h)),
                  pl.BlockSpec((1, 1, 2, 1, D_HEAD, D_HEAD), lambda b, h: (b, layer, 0, h, 0, 0))],
        out_specs=[pl.BlockSpec((1, L, LANES), lambda b, h: (b, 0, h)),
                   pl.BlockSpec((1, 2, 1, D_HEAD, D_HEAD), lambda b, h: (b, 0, h, 0, 0))],
        out_shape=[jax.ShapeDtypeStruct((B, L, BRANCH_W), bf16),
                   jax.ShapeDtypeStruct((B, 2, N_HEADS, D_HEAD, D_HEAD), f32)],
        scratch_shapes=[pltpu.VMEM((2, L, D_HEAD), f32),
                        pltpu.VMEM((2, D_HEAD, D_HEAD), f32),
                        pltpu.VMEM((2, L, D_HEAD), bf16),
                        pltpu.VMEM((2, L // HG_CHUNK, D_HEAD), f32),
                        pltpu.VMEM((2, L // HG_CHUNK, D_HEAD, D_HEAD), f32),
                        pltpu.VMEM((L // HG_CHUNK, D_HEAD, HG_CHUNK), bf16),
                        pltpu.VMEM((2, L, D_HEAD), f32),
                        pltpu.VMEM((2, L, D_HEAD), f32),
                        pltpu.VMEM((L, D_HEAD), f32),
                        pltpu.VMEM((L, D_HEAD), f32)],
        compiler_params=_cparams(("arbitrary", "arbitrary")),
        name="hgrn",
    )(proj, proj32, proj32, proj, proj, lb_rows, norm_g, s0)


def _pair_swap(x):
    lane = lax.broadcasted_iota(jnp.int32, x.shape, 1)
    return jnp.where((lane % 2) == 0, pltpu.roll(x, LANES - 1, 1), pltpu.roll(x, 1, 1))


def _head_norm_pair(x, g_row):
    gi = lax.broadcasted_iota(jnp.int32, (LANES, LANES), 0) // AT_HD
    gj = lax.broadcasted_iota(jnp.int32, (LANES, LANES), 1) // AT_HD
    same_head = (gi == gj).astype(bf16)
    sq = x * x
    hi = sq.astype(bf16)
    lo = (sq - hi.astype(f32)).astype(bf16)
    ms = (_dot(hi, same_head) + _dot(lo, same_head)) * (1.0 / AT_HD)
    return x * lax.rsqrt(ms + EPS) * g_row


def _qkv_prep_kernel(aq_ref, ak_ref, av_ref, qg_ref, kg_ref, cos_ref, sin_ref,
                     q_ref, k_ref, v_ref, kn_ref, *, use_rope):
    qg = qg_ref[...]
    kg = kg_ref[...]
    if use_rope:
        cos = cos_ref[...]
        sin = sin_ref[...]
    for p in range(AT_HEADS // 2):
        x = _head_norm_pair(aq_ref[0, :, p * LANES:(p + 1) * LANES].astype(f32), qg)
        if use_rope:
            x = x * cos + _pair_swap(x) * sin
        x = (x * (AT_HD ** -0.5)).astype(bf16)
        q_ref[0, 2 * p] = x[:, :AT_HD]
        q_ref[0, 2 * p + 1] = x[:, AT_HD:]
    kn = _head_norm_pair(ak_ref[0].astype(f32), kg)
    kn_ref[0] = kn
    if use_rope:
        kn = kn * cos + _pair_swap(kn) * sin
    kb = kn.astype(bf16)
    for g in range(AT_KV_HEADS):
        k_ref[0, g] = kb[:, g * AT_HD:(g + 1) * AT_HD]
    v = av_ref[0].astype(f32)
    lane = lax.broadcasted_iota(jnp.int32, v.shape, 1)
    tail = jnp.where(lane == AT_HD, 1.0, 0.0)
    v_ref[0, 0] = jnp.where(lane < AT_HD, v, tail).astype(bf16)
    v_ref[0, 1] = jnp.where(lane < AT_HD, pltpu.roll(v, AT_HD, 1), tail).astype(bf16)


def qkv_prep_call(proj, qg_row, kg_row, cos_t, sin_t, use_rope):
    B, L, _ = proj.shape
    tm = min(L, 512)
    kern = functools.partial(_qkv_prep_kernel, use_rope=use_rope)
    return pl.pallas_call(
        kern,
        grid=(B, L // tm),
        in_specs=[pl.BlockSpec((1, tm, 4 * LANES), lambda b, m: (b, m, CB_AQ // 4)),
                  pl.BlockSpec((1, tm, LANES), lambda b, m: (b, m, CB_AK)),
                  pl.BlockSpec((1, tm, LANES), lambda b, m: (b, m, CB_AV)),
                  pl.BlockSpec((1, LANES), lambda b, m: (0, 0)),
                  pl.BlockSpec((1, LANES), lambda b, m: (0, 0)),
                  pl.BlockSpec((tm, LANES), lambda b, m: (m, 0)),
                  pl.BlockSpec((tm, LANES), lambda b, m: (m, 0))],
        out_specs=[pl.BlockSpec((1, AT_HEADS, tm, AT_HD), lambda b, m: (b, 0, m, 0)),
                   pl.BlockSpec((1, AT_KV_HEADS, tm, AT_HD), lambda b, m: (b, 0, m, 0)),
                   pl.BlockSpec((1, AT_KV_HEADS, tm, LANES), lambda b, m: (b, 0, m, 0)),
                   pl.BlockSpec((1, tm, LANES), lambda b, m: (b, m, 0))],
        out_shape=[jax.ShapeDtypeStruct((B, AT_HEADS, L, AT_HD), bf16),
                   jax.ShapeDtypeStruct((B, AT_KV_HEADS, L, AT_HD), bf16),
                   jax.ShapeDtypeStruct((B, AT_KV_HEADS, L, LANES), bf16),
                   jax.ShapeDtypeStruct((B, L, LANES), f32)],
        compiler_params=_cparams(("arbitrary", "arbitrary")),
        name="qkv_prep",
    )(proj, proj, proj, qg_row, kg_row, cos_t, sin_t)


def _attn_kernel(*refs, tq, tk, n_new, n_cache):
    if n_cache:
        q_ref, k_ref, v_ref, kc_ref, vc_ref, o_ref, s_s, m_s = refs
    else:
        q_ref, k_ref, v_ref, o_ref, s_s, m_s = refs
    rows = AT_GROUP * tq
    ncb = tk // LANES
    nkv = n_new + n_cache

    def chunk(new_ref, cache_ref, j):
        if j < n_new:
            return new_ref[0, 0, j * tk:(j + 1) * tk, :]
        return cache_ref[0, 0, 0, (j - n_new) * tk:(j - n_new + 1) * tk, :]

    always = pl.program_id(2) >= 0

    @pl.when(always)
    def _():
        q = q_ref[0].reshape(rows, AT_HD)
        mx = None
        for j in range(nkv):
            s = _dot_nt(q, chunk(k_ref, kc_ref if n_cache else None, j))
            s_s[:, j * tk:(j + 1) * tk] = s
            cm = s[:, 0:LANES]
            for cb in range(1, ncb):
                cm = jnp.maximum(cm, s[:, cb * LANES:(cb + 1) * LANES])
            mx = cm if mx is None else jnp.maximum(mx, cm)
        m_s[...] = jnp.broadcast_to(jnp.max(mx, axis=-1, keepdims=True), (rows, LANES))

    @pl.when(pl.program_id(1) >= 0)
    def _():
        m_b = m_s[...]
        acc = None
        for j in range(nkv):
            ps = []
            for cb in range(ncb):
                off = j * tk + cb * LANES
                ps.append(jnp.exp(s_s[:, off:off + LANES] - m_b).astype(bf16))
            pv = _dot(jnp.concatenate(ps, axis=1), chunk(v_ref, vc_ref if n_cache else None, j))
            acc = pv if acc is None else acc + pv
        o = acc[:, :AT_HD] * (1.0 / acc[:, AT_HD:AT_HD + 1])
        for g in range(AT_GROUP):
            o_ref[0, :, g * AT_HD:(g + 1) * AT_HD] = o[g * tq:(g + 1) * tq].astype(o_ref.dtype)


def attn_call(q, k, v, k_cache=None, v_cache=None, layer=0):
    B, _, Lq, _ = q.shape
    L = k.shape[2]
    P = 0 if k_cache is None else k_cache.shape[3]
    tq = 256
    tk = 512 if (L % 512 == 0 and P % 512 == 0) else L
    assert L % tk == 0 and P % tk == 0
    n_new, n_cache = L // tk, P // tk
    nq = Lq // tq
    kern = functools.partial(_attn_kernel, tq=tq, tk=tk, n_new=n_new, n_cache=n_cache)
    rows = AT_GROUP * tq

    def kv_spec(length, width):
        return pl.BlockSpec((1, 1, length, width), lambda b, g, i: (b, g, 0, 0))

    in_specs = [pl.BlockSpec((1, AT_GROUP, tq, AT_HD), lambda b, g, i: (b, g, i, 0)),
                kv_spec(L, AT_HD), kv_spec(L, LANES)]
    args = [q, k, v]
    if n_cache:
        in_specs += [pl.BlockSpec((1, 1, 1, P, w), lambda b, g, i: (b, layer, g, 0, 0)) for w in (AT_HD, LANES)]
        args += [k_cache, v_cache]
    return pl.pallas_call(
        kern,
        grid=(B, AT_KV_HEADS, nq),
        in_specs=in_specs,
        out_specs=pl.BlockSpec((1, tq, AT_GROUP * AT_HD), lambda b, g, i: (b, i, g)),
        out_shape=jax.ShapeDtypeStruct((B, Lq, AT_HEADS * AT_HD), bf16),
        scratch_shapes=[pltpu.VMEM((rows, L + P), f32),
                        pltpu.VMEM((rows, LANES), f32)],
        compiler_params=_cparams(("arbitrary", "arbitrary", "arbitrary")),
        name="attention",
    )(*args)


def _cache_layouts(cache_k, cache_v):
    kc = jnp.swapaxes(cache_k, 2, 3).astype(bf16)
    vc = jnp.swapaxes(cache_v, 2, 3).astype(bf16)
    ones = jnp.ones(vc.shape[:-1] + (1,), bf16)
    zeros = jnp.zeros(vc.shape[:-1] + (LANES - AT_HD - 1,), bf16)
    return kc, jnp.concatenate([vc, ones, zeros], axis=-1)


def _merge_kernel(x_ref, yml_ref, yhg_ref, yat_ref, bg_ref, mod_ref, wb_ref, wo_ref, o_ref):
    merged = None
    for n, y_ref in enumerate((yml_ref, yhg_ref, yat_ref)):
        pn = _dot(y_ref[0], wb_ref[n])
        gn = _sigmoid(bg_ref[0, :, n * D_MODEL:(n + 1) * D_MODEL].astype(f32))
        merged = gn * pn if merged is None else merged + gn * pn
    out = _dot(merged.astype(bf16), wo_ref[...])
    o_ref[0] = x_ref[0] + mod_ref[0, 2:3, :] * out


def merge_call(x, y_ml, y_hg, y_at, proj, mod, wb, wo, shared_mod):
    B, L, _ = x.shape
    tm = min(L, 512)
    mod_map = (lambda b, m: (0, 0, 0)) if shared_mod else (lambda b, m: (b, 0, 0))
    yspec = pl.BlockSpec((1, tm, BRANCH_W), lambda b, m: (b, m, 0))
    return pl.pallas_call(
        _merge_kernel,
        grid=(B, L // tm),
        in_specs=[pl.BlockSpec((1, tm, D_MODEL), lambda b, m: (b, m, 0)),
                  yspec, yspec, yspec,
                  pl.BlockSpec((1, tm, 3 * D_MODEL), lambda b, m: (b, m, CB_BG)),
                  pl.BlockSpec((1, 8, D_MODEL), mod_map),
                  pl.BlockSpec((3, BRANCH_W, D_MODEL), lambda b, m: (0, 0, 0)),
                  pl.BlockSpec((D_MODEL, D_MODEL), lambda b, m: (0, 0))],
        out_specs=pl.BlockSpec((1, tm, D_MODEL), lambda b, m: (b, m, 0)),
        out_shape=jax.ShapeDtypeStruct((B, L, D_MODEL), f32),
        compiler_params=_cparams(("arbitrary", "arbitrary")),
        name="merge_out",
    )(x, y_ml, y_hg, y_at, proj, mod, wb, wo)


FFN_SPLITS = ((0, 1536), (1536, FFN_HIDDEN))


def _ffn_kernel(x_ref, mod_ref, g_ref, wi_ref, wd_ref, fg_ref, o_ref, *, final_norm):
    x = x_ref[0]
    hb = (_rms_rows(x, g_ref[...]) * (1.0 + mod_ref[0, 4:5, :]) + mod_ref[0, 3:4, :]).astype(bf16)
    acc = None
    for lo, hi in FFN_SPLITS:
        gate = _dot(hb, wi_ref[:, lo:hi])
        up = _dot(hb, wi_ref[:, FFN_HIDDEN + lo:FFN_HIDDEN + hi])
        part = _dot((_silu(gate) * up).astype(bf16), wd_ref[lo:hi, :])
        acc = part if acc is None else acc + part
    y = x + mod_ref[0, 5:6, :] * acc
    if final_norm:
        y = _rms_rows(y, fg_ref[...])
    o_ref[0] = y


def ffn_call(x, mod, g_row, w_in, w_out, fg_row, shared_mod, final_norm):
    B, L, _ = x.shape
    tm = min(L, 512)
    mod_map = (lambda b, m: (0, 0, 0)) if shared_mod else (lambda b, m: (b, 0, 0))
    kern = functools.partial(_ffn_kernel, final_norm=final_norm)
    resident = pl.Buffered(1)
    return pl.pallas_call(
        kern,
        grid=(B, L // tm),
        in_specs=[pl.BlockSpec((1, tm, D_MODEL), lambda b, m: (b, m, 0)),
                  pl.BlockSpec((1, 8, D_MODEL), mod_map),
                  pl.BlockSpec((1, D_MODEL), lambda b, m: (0, 0)),
                  pl.BlockSpec((D_MODEL, 2 * FFN_HIDDEN), lambda b, m: (0, 0), pipeline_mode=resident),
                  pl.BlockSpec((FFN_HIDDEN, D_MODEL), lambda b, m: (0, 0), pipeline_mode=resident),
                  pl.BlockSpec((1, D_MODEL), lambda b, m: (0, 0))],
        out_specs=pl.BlockSpec((1, tm, D_MODEL), lambda b, m: (b, m, 0)),
        out_shape=jax.ShapeDtypeStruct((B, L, D_MODEL), f32),
        compiler_params=_cparams(("arbitrary", "arbitrary")),
        name="ffn",
    )(x, mod, g_row, w_in, w_out, fg_row)


def _reorder_w_in(w):
    o_mg, o_hq, o_hf, o_hi, o_bg, o_end = 2048, 2064, 2576, 3600, 5392, 8464
    d = w.shape[0]
    sec16 = [w[:, o_bg:o_end], w[:, :o_mg], w[:, o_hq:o_hf], w[:, o_hi:o_bg]]
    n16 = (o_end - o_bg) + o_mg + (o_hf - o_hq) + (o_bg - o_hi)
    sec32 = [w[:, o_hf:o_hi], w[:, o_mg:o_hq]]
    n32 = (o_hi - o_hf) + (o_hq - o_mg)
    return jnp.concatenate(sec16 + [jnp.zeros((d, N16_COLS - n16), w.dtype)] + sec32
                           + [jnp.zeros((d, N32_COLS - n32), w.dtype)], axis=1).astype(bf16)


def _rope_tables(L):
    rows = L // GRID_W
    row = jnp.repeat(jnp.arange(rows, dtype=f32), GRID_W)
    colp = jnp.tile(jnp.arange(GRID_W, dtype=f32), rows)
    n_freq = AT_HD // 4
    inv = ROPE_THETA ** (-jnp.arange(n_freq, dtype=f32) / n_freq)
    ang = jnp.concatenate([row[:, None] * inv, colp[:, None] * inv], axis=-1)
    cos = jnp.repeat(jnp.cos(ang), 2, axis=-1)
    sin = jnp.repeat(jnp.sin(ang), 2, axis=-1)
    sign = jnp.tile(jnp.array([-1.0, 1.0], f32), AT_HD // 2)
    sin = sin * sign
    return jnp.tile(cos, (1, 2)), jnp.tile(sin, (1, 2))


def _mod_rows(ada_rows):
    r = ada_rows.shape[0]
    m = ada_rows.reshape(r, 6, D_MODEL)
    return jnp.concatenate([m, jnp.zeros((r, 2, D_MODEL), f32)], axis=1)


def kernel(x_prompt, x_sample, c, cache_k, cache_v, state_ml_C, state_ml_n, state_ml_m, state_hg_S, c_ctx, w_ada, b_ada, norm1_g, norm2_g, w_in, ml_gate_b, ml_norm_g, hg_lb_logits, hg_norm_g, q_norm_g, k_norm_g, w_branch, w_out, w_ffn_in, w_ffn_out, final_g):
    Bp, Lp, _ = x_prompt.shape
    Bs, Ls, _ = x_sample.shape

    n_rows = 16
    cvec = jnp.concatenate([c_ctx[None, :], c, jnp.zeros((n_rows - 1 - Bs, D_MODEL), f32)], axis=0)
    ada = ada_call(cvec, w_ada, b_ada)
    lb_all = lb_call(hg_lb_logits)
    cos_t, sin_t = _rope_tables(Ls)
    k_c, v_c = _cache_layouts(cache_k, cache_v)

    zeros_c = jnp.zeros((Bp, 1, 2, N_HEADS, D_HEAD, D_HEAD), f32)
    zeros_n = jnp.zeros((Bp, 1, 2, N_HEADS, 1, D_HEAD), f32)
    zeros_m = jnp.zeros((Bp * 2 * N_HEADS,), f32)
    n_state = state_ml_n.reshape(Bs, DEPTH, 2, N_HEADS, 1, D_HEAD)

    xp, xs = x_prompt.reshape(1, Bp * Lp, D_MODEL), x_sample
    nk, nv, nC, nn_, nm, nS = [], [], [], [], [], []
    for l in range(DEPTH):
        w_l = _reorder_w_in(w_in[l])
        wb_l = w_branch[l].astype(bf16)
        wo_l = w_out[l].astype(bf16)
        wfi_l = w_ffn_in[l].astype(bf16)
        wfo_l = w_ffn_out[l].astype(bf16)
        g1 = norm1_g[l][None, :]
        g2 = norm2_g[l][None, :]
        mlg = ml_norm_g[l][None, :]
        hgg = hg_norm_g[l][None, :]
        qg = jnp.tile(q_norm_g[l], 2)[None, :]
        kg = jnp.tile(k_norm_g[l], 2)[None, :]
        gate_b = ml_gate_b[l].reshape(-1).astype(f32)
        fg = final_g[None, :]
        last = l == DEPTH - 1

        mod_ctx = _mod_rows(ada[l, 0:1])
        mod_lat = _mod_rows(ada[l, 1:1 + Bs])

        proj_flat, proj32 = in_proj_call(xp, mod_ctx, g1, w_l, True)
        proj = proj_flat.reshape(Bp, Lp, N16_COLS)
        proj32 = proj32.reshape(Bp, Lp, N32_COLS)
        y_ml, c_f, n_f, m_f = mlstm_call(proj, proj32, gate_b, mlg, zeros_c, zeros_n, zeros_m, 0)
        y_hg, s_f = hgrn_call(proj, proj32, lb_all[l], hgg, zeros_c, 0)
        q_h, k_h, v_h, k_n = qkv_prep_call(proj, qg, kg, cos_t[:Lp], sin_t[:Lp], False)
        y_at = attn_call(q_h, k_h, v_h)
        flat = (1, Bp * Lp, BRANCH_W)
        xp = merge_call(xp, y_ml.reshape(flat), y_hg.reshape(flat), y_at.reshape(flat), proj_flat,
                        mod_ctx, wb_l, wo_l, True)
        xp = ffn_call(xp, mod_ctx, g2, wfi_l, wfo_l, fg, True, last)
        nk.append(k_n.reshape(Bp, Lp, AT_KV_HEADS, AT_HD))
        nv.append(proj[:, :, CB_AV * LANES:(CB_AV + 1) * LANES].astype(f32).reshape(Bp, Lp, AT_KV_HEADS, AT_HD))
        nC.append(c_f)
        nn_.append(n_f[:, :, :, 0, :])
        nm.append(m_f[:, :, :, 0, 0])
        nS.append(s_f)

        proj, proj32 = in_proj_call(xs, mod_lat, g1, w_l, False)
        y_ml, _, _, _ = mlstm_call(proj, proj32, gate_b, mlg, state_ml_C, n_state,
                                   state_ml_m[:, l].reshape(-1).astype(f32), l)
        y_hg, _ = hgrn_call(proj, proj32, lb_all[l], hgg, state_hg_S, l)
        q_h, k_h, v_h, _ = qkv_prep_call(proj, qg, kg, cos_t, sin_t, True)
        y_at = attn_call(q_h, k_h, v_h, k_c, v_c, l)
        xs = merge_call(xs, y_ml, y_hg, y_at, proj, mod_lat, wb_l, wo_l, False)
        xs = ffn_call(xs, mod_lat, g2, wfi_l, wfo_l, fg, False, last)

    return (xp.reshape(Bp, Lp, D_MODEL), xs, jnp.stack(nk, axis=1), jnp.stack(nv, axis=1), jnp.stack(nC, axis=1),
            jnp.stack(nn_, axis=1), jnp.stack(nm, axis=1), jnp.stack(nS, axis=1))
```

```python
import functools
import math

import jax
import jax.numpy as jnp
from jax import lax
from jax.experimental import pallas as pl
from jax.experimental.pallas import tpu as pltpu

f32 = jnp.float32
bf16 = jnp.bfloat16

D_MODEL = 1024
DEPTH = 4
N_HEADS = 4
D_HEAD = 128
AT_HEADS = 8
AT_KV_HEADS = 2
AT_GROUP = AT_HEADS // AT_KV_HEADS
AT_HD = 64
BRANCH_W = 512
FFN_HIDDEN = 2816
GRID_W = 64
ROPE_THETA = 10000.0
EPS = 1e-6

LANES = 128
SUBLANES = 8
VMEM_LIMIT = 52 * 1024 * 1024

CB_BG = 0
CB_MQ = 24
CB_MK = 28
CB_MV = 32
CB_MO = 36
CB_HQ = 40
CB_HI = 44
CB_HG = 48
CB_AQ = 52
CB_AK = 56
CB_AV = 57
N16_BLOCKS = 60
N16_COLS = N16_BLOCKS * LANES
CF_HF = 0
CF_MG = 8
N32_BLOCKS = 10
N32_COLS = N32_BLOCKS * LANES
PROJ_TN = N32_COLS
N16_TILES = N16_COLS // PROJ_TN

ML_CHUNK = 128
ML_EXT = 144
HG_CHUNK = 64
HG_SUB = 8
HG_HALF = 4
HG_LEVELS = (32, 16, 8, 4)
NEG_BIG = -1e30
LOG2E = 1.4426950408889634


def _cparams(sem):
    return pltpu.CompilerParams(dimension_semantics=sem, vmem_limit_bytes=VMEM_LIMIT)


def _dot(a, b):
    return jnp.dot(a, b, preferred_element_type=f32)


def _dot_nt(a, b):
    return lax.dot_general(a, b, (((1,), (1,)), ((), ())), preferred_element_type=f32)


def _dot_tn(a, b):
    return lax.dot_general(a, b, (((0,), (0,)), ((), ())), preferred_element_type=f32)


def _sigmoid(x):
    return 1.0 / (1.0 + jnp.exp(-x))


def _silu(x):
    return x * _sigmoid(x)


def _log_sigmoid(x):
    return jnp.minimum(x, 0.0) - jnp.log1p(jnp.exp(-jnp.abs(x)))


def _split3(x):
    hi = x.astype(bf16)
    r1 = x - hi.astype(f32)
    mid = r1.astype(bf16)
    lo = (r1 - mid.astype(f32)).astype(bf16)
    return hi, mid, lo


def _rms_rows(x, g_row):
    ms = jnp.mean(x * x, axis=-1, keepdims=True)
    return x * lax.rsqrt(ms + EPS) * g_row


def _ada_kernel(c_ref, w_ref, b_ref, o_ref):
    s = _silu(c_ref[...]).astype(bf16)
    o_ref[0] = _dot(s, w_ref[0].astype(bf16)) + b_ref[0]


def ada_call(cvec, w_ada, b_ada):
    rows = cvec.shape[0]
    tn = 1536
    n6 = 6 * D_MODEL
    return pl.pallas_call(
        _ada_kernel,
        grid=(DEPTH, n6 // tn),
        in_specs=[pl.BlockSpec((rows, D_MODEL), lambda l, n: (0, 0)),
                  pl.BlockSpec((1, D_MODEL, tn), lambda l, n: (l, 0, n)),
                  pl.BlockSpec((1, 1, tn), lambda l, n: (l, 0, n))],
        out_specs=pl.BlockSpec((1, rows, tn), lambda l, n: (l, 0, n)),
        out_shape=jax.ShapeDtypeStruct((DEPTH, rows, n6), f32),
        compiler_params=_cparams(("arbitrary", "arbitrary")),
        name="ada",
    )(cvec, w_ada, b_ada.reshape(DEPTH, 1, n6))


def _lb_kernel(x_ref, o_ref):
    xs = [x_ref[l] for l in range(DEPTH)]
    mx = xs[0]
    for l in range(1, DEPTH):
        mx = jnp.maximum(mx, xs[l])
    es = [jnp.exp(x - mx) for x in xs]
    tot = es[0]
    for l in range(1, DEPTH):
        tot = tot + es[l]
    sm = [e / tot for e in es]
    run = sm[0]
    o_ref[0] = run - sm[0]
    for l in range(1, DEPTH):
        run = run + sm[l]
        o_ref[l] = run - sm[0]


def lb_call(logits):
    x = logits.astype(f32).reshape(DEPTH, 2 * N_HEADS, D_HEAD)
    return pl.pallas_call(
        _lb_kernel,
        out_shape=jax.ShapeDtypeStruct((DEPTH, 2 * N_HEADS, D_HEAD), f32),
        name="hg_lower_bounds",
    )(x)


def _in_proj_kernel(x_ref, mod_ref, g_ref, w_ref, o16_ref, o32_ref, h_s):
    n = pl.program_id(2)

    @pl.when(n == 0)
    def _():
        x = x_ref[0]
        sh = mod_ref[0, 0:1, :]
        sc = mod_ref[0, 1:2, :]
        h = _rms_rows(x, g_ref[...]) * (1.0 + sc) + sh
        h_s[...] = h.astype(bf16)

    @pl.when(n < N16_TILES)
    def _():
        o16_ref[0] = _dot(h_s[...], w_ref[...]).astype(bf16)

    @pl.when(n == N16_TILES)
    def _():
        o32_ref[0] = _dot(h_s[...], w_ref[...])


def in_proj_call(x, mod, g_row, w, shared_mod):
    B, L, _ = x.shape
    tm = min(L, 1024)
    tn = PROJ_TN
    mod_map = (lambda b, m, n: (0, 0, 0)) if shared_mod else (lambda b, m, n: (b, 0, 0))
    return pl.pallas_call(
        _in_proj_kernel,
        grid=(B, L // tm, N16_TILES + 1),
        in_specs=[pl.BlockSpec((1, tm, D_MODEL), lambda b, m, n: (b, m, 0)),
                  pl.BlockSpec((1, 8, D_MODEL), mod_map),
                  pl.BlockSpec((1, D_MODEL), lambda b, m, n: (0, 0)),
                  pl.BlockSpec((D_MODEL, tn), lambda b, m, n: (0, n))],
        out_specs=[pl.BlockSpec((1, tm, tn), lambda b, m, n: (b, m, jnp.minimum(n, N16_TILES - 1))),
                   pl.BlockSpec((1, tm, tn), lambda b, m, n: (b, m, 0))],
        out_shape=[jax.ShapeDtypeStruct((B, L, N16_COLS), bf16),
                   jax.ShapeDtypeStruct((B, L, N32_COLS), f32)],
        scratch_shapes=[pltpu.VMEM((tm, D_MODEL), bf16)],
        compiler_params=_cparams(("arbitrary", "arbitrary", "arbitrary")),
        name="in_proj",
    )(x, mod, g_row, w)


def _mlstm_kernel(bias_ref, m0_ref, q_ref, k_ref, v_ref, mo_ref, g_ref, ng_ref, c0_ref, n0_ref,
                  y_ref, cout_ref, nout_ref, mout_ref,
                  gt_s, rw_s, ucb_s, vt_s, ht_s, ct_s, m_s, *, seq_len):
    T = ML_CHUNK
    assert T == LANES
    nc = seq_len // T
    b_idx = pl.program_id(0)
    h = pl.program_id(1)
    scale = D_HEAD ** -0.5
    ext_rows = jnp.concatenate([jnp.ones((1, T), f32), jnp.zeros((ML_EXT - D_HEAD - 1, T), f32)], axis=0)

    def tr_body(j, carry):
        r0 = pl.multiple_of(j * T, T)
        gt_s[:, pl.ds(r0, T)] = g_ref[0, pl.ds(r0, T), :].T
        vt = v_ref[0, pl.ds(r0, T), :].astype(f32).T
        vt_s[:, pl.ds(r0, T)] = jnp.concatenate([vt, ext_rows], axis=0).astype(bf16)
        return carry
    lax.fori_loop(0, nc, tr_body, 0, unroll=min(8, nc))

    i_f = gt_s[pl.ds(h, 1), :] + bias_ref[h]
    f_f = _log_sigmoid(gt_s[pl.ds(4 + h, 1), :] + bias_ref[4 + h])
    i_b = gt_s[pl.ds(8 + h, 1), :] + bias_ref[8 + h]
    f_b = _log_sigmoid(gt_s[pl.ds(12 + h, 1), :] + bias_ref[12 + h])
    zrow = jnp.zeros_like(i_f)
    rw_s[...] = jnp.concatenate([f_f, i_f, f_b, i_b, zrow, zrow, zrow, zrow], axis=0)

    ui = lax.broadcasted_iota(jnp.int32, (T, T), 0)
    si = lax.broadcasted_iota(jnp.int32, (T, T), 1)
    tri = jnp.concatenate([(ui <= si).astype(bf16), (ui >= si).astype(bf16)], axis=1)
    valid_f = ui <= si
    valid_b = ui >= si

    def cs_body(j, carry):
        r0 = pl.multiple_of(j * T, T)
        rows = rw_s[:, pl.ds(r0, T)]
        hi, mid, lo = _split3(rows)
        cs3 = _dot(jnp.concatenate([hi, mid, lo, jnp.zeros_like(hi)], axis=0), tri)
        cs = cs3[0:8] + cs3[8:16] + cs3[16:24]
        b_f = cs[0:1, 0:T]
        b_b = cs[2:3, T:2 * T]
        rw_s[:, pl.ds(r0, T)] = jnp.concatenate([b_f, rows[1:2], b_b, rows[3:8]], axis=0)
        ucb_s[0, pl.ds(r0, T), :] = jnp.broadcast_to(rows[1:2] - b_f, (T, T)).T
        ucb_s[1, pl.ds(r0, T), :] = jnp.broadcast_to(rows[3:4] - b_b, (T, T)).T
        return carry
    lax.fori_loop(0, nc, cs_body, 0, unroll=min(16, nc))

    for d in range(2):
        ct_s[d] = jnp.concatenate([c0_ref[0, 0, d, 0].T, n0_ref[0, 0, d, 0],
                                   jnp.zeros((ML_EXT - D_HEAD - 1, D_HEAD), f32)], axis=0)
        m_s[d] = jnp.full((1, LANES), m0_ref[b_idx * 2 * N_HEADS + d * N_HEADS + h], f32)

    def step(d, j):
        r0 = pl.multiple_of(j * T, T)
        b_row = rw_s[pl.ds(2 * d, 1), pl.ds(r0, T)]
        b_end = b_row[:, T - 1:T] if d == 0 else b_row[:, 0:1]
        b_end_l = jnp.broadcast_to(b_end, (1, LANES))
        ucb = ucb_s[d, pl.ds(r0, T), :]

        q = q_ref[0, pl.ds(r0, T), :].astype(f32) * scale
        kb = k_ref[0, pl.ds(r0, T), :]
        k = kb.astype(f32)
        qb = q.astype(bf16)
        vt = vt_s[:, pl.ds(r0, T)]
        ct = ct_s[d]
        m_prev = m_s[d]

        d_t = jnp.where(valid_f if d == 0 else valid_b, ucb + b_row, NEG_BIG)
        m_state = b_row + m_prev
        m_t = jnp.maximum(m_state, jnp.max(d_t, axis=0, keepdims=True))
        w_state = jnp.exp(m_state - m_t)
        s_t = _dot_nt(kb, qb) * jnp.exp(d_t - m_t)
        tot = _dot(vt, s_t.astype(bf16)) + w_state * _dot_nt(ct.astype(bf16), qb)
        den = tot[D_HEAD:D_HEAD + 1]
        ht_s[d, :, pl.ds(r0, T)] = tot[:D_HEAD] * (1.0 / jnp.maximum(jnp.abs(den), jnp.exp(-m_t)))

        gcb = ucb + b_end_l
        m_new = jnp.maximum(b_end_l + m_prev, jnp.max(gcb, axis=0, keepdims=True))
        kw = k * jnp.exp(gcb - m_new)
        decay = jnp.exp(b_end_l + m_prev - m_new)
        ct_s[d] = decay * ct + _dot(vt, kw.astype(bf16))
        m_s[d] = m_new

    def loop_body(j, carry):
        step(0, j)
        step(1, nc - 1 - j)
        return carry
    lax.fori_loop(0, nc, loop_body, 0, unroll=min(8, nc))

    ng = ng_ref[...]

    def out_body(j, carry):
        r0 = pl.multiple_of(j * T, T)
        y_t = ht_s[0, :, pl.ds(r0, T)] + ht_s[1, :, pl.ds(r0, T)]
        ms = jnp.mean(y_t * y_t, axis=0, keepdims=True)
        y = (y_t * lax.rsqrt(ms + EPS)).T * ng * _sigmoid(mo_ref[0, pl.ds(r0, T), :].astype(f32))
        y_ref[0, pl.ds(r0, T), :] = y.astype(y_ref.dtype)
        return carry
    lax.fori_loop(0, nc, out_body, 0, unroll=min(8, nc))

    for d in range(2):
        cout_ref[0, d, 0] = ct_s[d, 0:D_HEAD, :].T
        nout_ref[0, d, 0] = ct_s[d, D_HEAD:D_HEAD + 1, :]
        mout_ref[0, d, 0] = m_s[d]


def mlstm_call(proj, proj32, gate_bias, norm_g, c0, n0, m0, layer):
    B, L, _ = proj.shape

    def col(cb):
        return pl.BlockSpec((1, L, LANES), lambda b, h, cb=cb: (b, 0, cb + h))

    smem = pl.BlockSpec(memory_space=pltpu.SMEM)
    kern = functools.partial(_mlstm_kernel, seq_len=L)
    return pl.pallas_call(
        kern,
        grid=(B, N_HEADS),
        in_specs=[smem, smem,
                  col(CB_MQ), col(CB_MK), col(CB_MV), col(CB_MO),
                  pl.BlockSpec((1, L, LANES), lambda b, h: (b, 0, CF_MG)),
                  pl.BlockSpec((1, LANES), lambda b, h: (0, h)),
                  pl.BlockSpec((1, 1, 2, 1, D_HEAD, D_HEAD), lambda b, h: (b, layer, 0, h, 0, 0)),
                  pl.BlockSpec((1, 1, 2, 1, 1, D_HEAD), lambda b, h: (b, layer, 0, h, 0, 0))],
        out_specs=[pl.BlockSpec((1, L, LANES), lambda b, h: (b, 0, h)),
                   pl.BlockSpec((1, 2, 1, D_HEAD, D_HEAD), lambda b, h: (b, 0, h, 0, 0)),
                   pl.BlockSpec((1, 2, 1, 1, D_HEAD), lambda b, h: (b, 0, h, 0, 0)),
                   pl.BlockSpec((1, 2, 1, 1, LANES), lambda b, h: (b, 0, h, 0, 0))],
        out_shape=[jax.ShapeDtypeStruct((B, L, BRANCH_W), bf16),
                   jax.ShapeDtypeStruct((B, 2, N_HEADS, D_HEAD, D_HEAD), f32),
                   jax.ShapeDtypeStruct((B, 2, N_HEADS, 1, D_HEAD), f32),
                   jax.ShapeDtypeStruct((B, 2, N_HEADS, 1, LANES), f32)],
        scratch_shapes=[pltpu.VMEM((LANES, L), f32),
                        pltpu.VMEM((8, L), f32),
                        pltpu.VMEM((2, L, LANES), f32),
                        pltpu.VMEM((ML_EXT, L), bf16),
                        pltpu.VMEM((2, D_HEAD, L), f32),
                        pltpu.VMEM((2, ML_EXT, D_HEAD), f32),
                        pltpu.VMEM((2, 1, LANES), f32)],
        compiler_params=_cparams(("arbitrary", "arbitrary")),
        name="mlstm",
    )(gate_bias, m0, proj, proj, proj, proj, proj32, norm_g, c0, n0)


def _hgrn_kernel(q_ref, f0_ref, f1_ref, i_ref, og_ref, lb_ref, ng_ref, s0_ref,
                 y_ref, sout_ref, o_s, st_s, qx_s, dec_s, u_s, vt_s, k_s, b_s, q32_s, v32_s, *, seq_len):
    T = HG_CHUNK
    nc = seq_len // T
    h = pl.program_id(1)
    f_refs = (f0_ref, f1_ref)

    lbs, loglb, log1mlb = [], [], []
    for d in range(2):
        lb = lb_ref[pl.ds(d * N_HEADS + h, 1), :]
        lbs.append(lb)
        loglb.append(jnp.log(lb))
        log1mlb.append(jnp.log1p(-lb))
        st_s[d] = s0_ref[0, 0, d, 0].T

    ti = lax.broadcasted_iota(jnp.int32, (T, T), 0)
    ui = lax.broadcasted_iota(jnp.int32, (T, T), 1)
    tri = ((ui <= ti).astype(bf16), (ui >= ti).astype(bf16))
    sub_t = lax.broadcasted_iota(jnp.int32, (HG_SUB, D_HEAD), 0)
    sub_q = sub_t % HG_HALF
    low_half = sub_t < HG_HALF
    row_i = lax.broadcasted_iota(jnp.int32, (T, D_HEAD), 0)
    tt_r = lax.broadcasted_iota(jnp.int32, (T, T), 0)
    tt_c = lax.broadcasted_iota(jnp.int32, (T, T), 1)
    q_rows = {(hh, dd): ((row_i % (2 * hh)) >= hh) if dd == 0 else ((row_i % (2 * hh)) < hh)
              for hh in HG_LEVELS for dd in range(2)}
    same_blk = {hh: (tt_r // (2 * hh)) == (tt_c // (2 * hh)) for hh in HG_LEVELS}

    def gates(d, j, q):
        r0 = pl.multiple_of(j * T, T)
        hf = f_refs[d][0, pl.ds(r0, T), :]
        e = jnp.exp(-jnp.abs(hf))
        sig_neg = jnp.where(hf >= 0.0, e, 1.0) / (1.0 + e)
        logsig = jnp.minimum(hf, 0.0) - jnp.log(1.0 + e)
        cterm = log1mlb[d] + logsig
        amax = jnp.maximum(loglb[d], cterm)
        logf = amax + jnp.log(1.0 + jnp.exp(-jnp.abs(loglb[d] - cterm)))
        k = (1.0 - lbs[d]) * sig_neg
        hi, mid, lo = _split3(logf * LOG2E)
        b = _dot(tri[d], hi) + _dot(tri[d], mid) + _dot(tri[d], lo)
        b_end = b[T - 1:T] if d == 0 else b[0:1]
        k_s[d, pl.ds(r0, T), :] = k
        b_s[d, pl.ds(r0, T), :] = b
        qx_s[d, pl.ds(r0, T), :] = (q * jnp.exp2(b)).astype(bf16)
        dec_s[d, pl.ds(j, 1), :] = jnp.exp2(b_end)

    def gates_body(j, carry):
        r0 = pl.multiple_of(j * T, T)
        q = q_ref[0, pl.ds(r0, T), :].astype(f32)
        v = i_ref[0, pl.ds(r0, T), :].astype(f32)
        q32_s[pl.ds(r0, T), :] = q
        v32_s[pl.ds(r0, T), :] = v
        vt_s[j] = v.T.astype(bf16)
        gates(0, j, q)
        gates(1, j, q)
        return carry
    lax.fori_loop(0, nc, gates_body, 0, unroll=min(8, nc))

    def intra(d, j):
        r0 = pl.multiple_of(j * T, T)

        def rows(ref2, lo_r, n):
            return ref2[pl.ds(r0 + lo_r, n), :]

        def srows(ref3, lo_r, n):
            return ref3[d, pl.ds(r0 + lo_r, n), :]

        nsub = T // HG_SUB
        o_blk = [None] * nsub

        def add(idx, val):
            o_blk[idx] = val if o_blk[idx] is None else o_blk[idx] + val

        def levels():
            b_all = srows(b_s, 0, T)
            q_all = rows(q32_s, 0, T)
            k_all = srows(k_s, 0, T)
            a_tot = None
            for h in HG_LEVELS:
                blk = 2 * h
                ref_off = h - 1 if d == 0 else h
                ref = jnp.concatenate([jnp.broadcast_to(srows(b_s, base + ref_off, 1), (blk, D_HEAD))
                                       for base in range(0, T, blk)], axis=0)
                is_q = q_rows[(h, d)]
                qx = q_all * jnp.exp2(jnp.where(is_q, b_all - ref, NEG_BIG))
                kx = k_all * jnp.exp2(jnp.where(is_q, NEG_BIG, ref - b_all))
                a = _dot_nt(qx.astype(bf16), kx.astype(bf16))
                if blk < T:
                    a = jnp.where(same_blk[h], a, 0.0)
                a_tot = a if a_tot is None else a_tot + a
            ov = _dot(a_tot.astype(bf16), rows(v32_s, 0, T).astype(bf16))
            for p in range(nsub):
                add(p, ov[p * HG_SUB:(p + 1) * HG_SUB])

        def diag(lo_r):
            bs = srows(b_s, lo_r, HG_SUB)
            qs = rows(q32_s, lo_r, HG_SUB)

            def keyrow(read, s):
                return jnp.where(low_half, read(lo_r + s, 1), read(lo_r + HG_HALF + s, 1))

            acc = None
            for s in range(HG_HALF):
                valid = (sub_q >= s) if d == 0 else (sub_q <= s)
                b_key = keyrow(lambda r, n: srows(b_s, r, n), s)
                ee = jnp.exp2(jnp.where(valid, bs - b_key, NEG_BIG))
                a_col = jnp.sum(qs * ee * keyrow(lambda r, n: srows(k_s, r, n), s), axis=-1, keepdims=True)
                term = a_col * keyrow(lambda r, n: rows(v32_s, r, n), s)
                acc = term if acc is None else acc + term
            add(lo_r // HG_SUB, acc)

        levels()
        for i in range(nsub):
            diag(i * HG_SUB)
        o_s[d, pl.ds(r0, T), :] = jnp.concatenate(o_blk, axis=0)

    def intra_body(j, carry):
        intra(0, j)
        intra(1, j)
        return carry
    lax.fori_loop(0, nc, intra_body, 0, unroll=min(8, nc))

    def incr_body(j, carry):
        r0 = pl.multiple_of(j * T, T)
        vt = vt_s[j]
        for d in range(2):
            b = b_s[d, pl.ds(r0, T), :]
            b_end = b[T - 1:T] if d == 0 else b[0:1]
            u_s[d, j] = _dot(vt, (k_s[d, pl.ds(r0, T), :] * jnp.exp2(b_end - b)).astype(bf16))
        return carry
    lax.fori_loop(0, nc, incr_body, 0, unroll=min(8, nc))

    def state_step(d, j):
        r0 = pl.multiple_of(j * T, T)
        st = st_s[d]
        o_s[d, pl.ds(r0, T), :] = o_s[d, pl.ds(r0, T), :] + _dot_nt(qx_s[d, pl.ds(r0, T), :], st.astype(bf16))
        st_s[d] = st * dec_s[d, pl.ds(j, 1), :] + u_s[d, j]

    def state_body(j, carry):
        state_step(0, j)
        state_step(1, nc - 1 - j)
        return carry
    lax.fori_loop(0, nc, state_body, 0, unroll=min(8, nc // 2))

    ng = ng_ref[...]

    def out_body(j, carry):
        r0 = pl.multiple_of(j * T, T)
        y = o_s[0, pl.ds(r0, T), :] + o_s[1, pl.ds(r0, T), :]
        y = _rms_rows(y, ng) * _silu(og_ref[0, pl.ds(r0, T), :].astype(f32))
        y_ref[0, pl.ds(r0, T), :] = y.astype(y_ref.dtype)
        return carry
    lax.fori_loop(0, nc, out_body, 0, unroll=min(8, nc))

    for d in range(2):
        sout_ref[0, d, 0] = st_s[d].T


def hgrn_call(proj, proj32, lb_rows, norm_g, s0, layer):
    B, L, _ = proj.shape

    def col(cb):
        return pl.BlockSpec((1, L, LANES), lambda b, h, cb=cb: (b, 0, cb + h))

    kern = functools.partial(_hgrn_kernel, seq_len=L)
    return pl.pallas_call(
        kern,
        grid=(B, N_HEADS),
        in_specs=[col(CB_HQ), col(CF_HF), col(CF_HF + N_HEADS), col(CB_HI), col(CB_HG),
                  pl.BlockSpec((2 * N_HEADS, D_HEAD), lambda b, h: (0, 0)),
                  pl.BlockSpec((1, LANES), lambda b, h: (0, h)),
                  pl.BlockSpec((1, 1, 2, 1, D_HEAD, D_HEAD), lambda b, h: (b, layer, 0, h, 0, 0))],
        out_specs=[pl.BlockSpec((1, L, LANES), lambda b, h: (b, 0, h)),
                   pl.BlockSpec((1, 2, 1, D_HEAD, D_HEAD), lambda b, h: (b, 0, h, 0, 0))],
        out_shape=[jax.ShapeDtypeStruct((B, L, BRANCH_W), bf16),
                   jax.ShapeDtypeStruct((B, 2, N_HEADS, D_HEAD, D_HEAD), f32)],
        scratch_shapes=[pltpu.VMEM((2, L, D_HEAD), f32),
                        pltpu.VMEM((2, D_HEAD, D_HEAD), f32),
                        pltpu.VMEM((2, L, D_HEAD), bf16),
                        pltpu.VMEM((2, L // HG_CHUNK, D_HEAD), f32),
                        pltpu.VMEM((2, L // HG_CHUNK, D_HEAD, D_HEAD), f32),
                        pltpu.VMEM((L // HG_CHUNK, D_HEAD, HG_CHUNK), bf16),
                        pltpu.VMEM((2, L, D_HEAD), f32),
                        pltpu.VMEM((2, L, D_HEAD), f32),
                        pltpu.VMEM((L, D_HEAD), f32),
                        pltpu.VMEM((L, D_HEAD), f32)],
        compiler_params=_cparams(("arbitrary", "arbitrary")),
        name="hgrn",
    )(proj, proj32, proj32, proj, proj, lb_rows, norm_g, s0)


def _pair_swap(x):
    lane = lax.broadcasted_iota(jnp.int32, x.shape, 1)
    return jnp.where((lane % 2) == 0, pltpu.roll(x, LANES - 1, 1), pltpu.roll(x, 1, 1))


def _head_norm_pair(x, g_row):
    gi = lax.broadcasted_iota(jnp.int32, (LANES, LANES), 0) // AT_HD
    gj = lax.broadcasted_iota(jnp.int32, (LANES, LANES), 1) // AT_HD
    same_head = (gi == gj).astype(bf16)
    sq = x * x
    hi = sq.astype(bf16)
    lo = (sq - hi.astype(f32)).astype(bf16)
    ms = (_dot(hi, same_head) + _dot(lo, same_head)) * (1.0 / AT_HD)
    return x * lax.rsqrt(ms + EPS) * g_row


def _qkv_prep_kernel(aq_ref, ak_ref, av_ref, qg_ref, kg_ref, cos_ref, sin_ref,
                     q_ref, k_ref, v_ref, kn_ref, *, use_rope):
    qg = qg_ref[...]
    kg = kg_ref[...]
    if use_rope:
        cos = cos_ref[...]
        sin = sin_ref[...]
    for p in range(AT_HEADS // 2):
        x = _head_norm_pair(aq_ref[0, :, p * LANES:(p + 1) * LANES].astype(f32), qg)
        if use_rope:
            x = x * cos + _pair_swap(x) * sin
        x = (x * (AT_HD ** -0.5)).astype(bf16)
        q_ref[0, 2 * p] = x[:, :AT_HD]
        q_ref[0, 2 * p + 1] = x[:, AT_HD:]
    kn = _head_norm_pair(ak_ref[0].astype(f32), kg)
    kn_ref[0] = kn
    if use_rope:
        kn = kn * cos + _pair_swap(kn) * sin
    kb = kn.astype(bf16)
    for g in range(AT_KV_HEADS):
        k_ref[0, g] = kb[:, g * AT_HD:(g + 1) * AT_HD]
    v = av_ref[0].astype(f32)
    lane = lax.broadcasted_iota(jnp.int32, v.shape, 1)
    tail = jnp.where(lane == AT_HD, 1.0, 0.0)
    v_ref[0, 0] = jnp.where(lane < AT_HD, v, tail).astype(bf16)
    v_ref[0, 1] = jnp.where(lane < AT_HD, pltpu.roll(v, AT_HD, 1), tail).astype(bf16)


def qkv_prep_call(proj, qg_row, kg_row, cos_t, sin_t, use_rope):
    B, L, _ = proj.shape
    tm = min(L, 512)
    kern = functools.partial(_qkv_prep_kernel, use_rope=use_rope)
    return pl.pallas_call(
        kern,
        grid=(B, L // tm),
        in_specs=[pl.BlockSpec((1, tm, 4 * LANES), lambda b, m: (b, m, CB_AQ // 4)),
                  pl.BlockSpec((1, tm, LANES), lambda b, m: (b, m, CB_AK)),
                  pl.BlockSpec((1, tm, LANES), lambda b, m: (b, m, CB_AV)),
                  pl.BlockSpec((1, LANES), lambda b, m: (0, 0)),
                  pl.BlockSpec((1, LANES), lambda b, m: (0, 0)),
                  pl.BlockSpec((tm, LANES), lambda b, m: (m, 0)),
                  pl.BlockSpec((tm, LANES), lambda b, m: (m, 0))],
        out_specs=[pl.BlockSpec((1, AT_HEADS, tm, AT_HD), lambda b, m: (b, 0, m, 0)),
                   pl.BlockSpec((1, AT_KV_HEADS, tm, AT_HD), lambda b, m: (b, 0, m, 0)),
                   pl.BlockSpec((1, AT_KV_HEADS, tm, LANES), lambda b, m: (b, 0, m, 0)),
                   pl.BlockSpec((1, tm, LANES), lambda b, m: (b, m, 0))],
        out_shape=[jax.ShapeDtypeStruct((B, AT_HEADS, L, AT_HD), bf16),
                   jax.ShapeDtypeStruct((B, AT_KV_HEADS, L, AT_HD), bf16),
                   jax.ShapeDtypeStruct((B, AT_KV_HEADS, L, LANES), bf16),
                   jax.ShapeDtypeStruct((B, L, LANES), f32)],
        compiler_params=_cparams(("arbitrary", "arbitrary")),
        name="qkv_prep",
    )(proj, proj, proj, qg_row, kg_row, cos_t, sin_t)


def _attn_kernel(*refs, tq, tk, n_new, n_cache):
    if n_cache:
        q_ref, k_ref, v_ref, kc_ref, vc_ref, o_ref, s_s, m_s = refs
    else:
        q_ref, k_ref, v_ref, o_ref, s_s, m_s = refs
    rows = AT_GROUP * tq
    ncb = tk // LANES
    nkv = n_new + n_cache

    def chunk(new_ref, cache_ref, j):
        if j < n_new:
            return new_ref[0, 0, j * tk:(j + 1) * tk, :]
        return cache_ref[0, 0, (j - n_new) * tk:(j - n_new + 1) * tk, :]

    always = pl.program_id(2) >= 0

    @pl.when(always)
    def _():
        q = q_ref[0].reshape(rows, AT_HD)
        mx = None
        for j in range(nkv):
            s = _dot_nt(q, chunk(k_ref, kc_ref if n_cache else None, j))
            s_s[:, j * tk:(j + 1) * tk] = s
            cm = s[:, 0:LANES]
            for cb in range(1, ncb):
                cm = jnp.maximum(cm, s[:, cb * LANES:(cb + 1) * LANES])
            mx = cm if mx is None else jnp.maximum(mx, cm)
        m_s[...] = jnp.broadcast_to(jnp.max(mx, axis=-1, keepdims=True), (rows, LANES))

    @pl.when(pl.program_id(1) >= 0)
    def _():
        m_b = m_s[...]
        acc = None
        for j in range(nkv):
            ps = []
            for cb in range(ncb):
                off = j * tk + cb * LANES
                ps.append(jnp.exp(s_s[:, off:off + LANES] - m_b).astype(bf16))
            pv = _dot(jnp.concatenate(ps, axis=1), chunk(v_ref, vc_ref if n_cache else None, j))
            acc = pv if acc is None else acc + pv
        o = acc[:, :AT_HD] * (1.0 / acc[:, AT_HD:AT_HD + 1])
        for g in range(AT_GROUP):
            o_ref[0, :, g * AT_HD:(g + 1) * AT_HD] = o[g * tq:(g + 1) * tq].astype(o_ref.dtype)


def attn_call(q, k, v, k_cache=None, v_cache=None):
    B, _, Lq, _ = q.shape
    L = k.shape[2]
    P = 0 if k_cache is None else k_cache.shape[2]
    tq = 256
    tk = 512 if (L % 512 == 0 and P % 512 == 0) else L
    assert L % tk == 0 and P % tk == 0
    n_new, n_cache = L // tk, P // tk
    nq = Lq // tq
    kern = functools.partial(_attn_kernel, tq=tq, tk=tk, n_new=n_new, n_cache=n_cache)
    rows = AT_GROUP * tq

    def kv_spec(length, width):
        return pl.BlockSpec((1, 1, length, width), lambda b, g, i: (b, g, 0, 0))

    in_specs = [pl.BlockSpec((1, AT_GROUP, tq, AT_HD), lambda b, g, i: (b, g, i, 0)),
                kv_spec(L, AT_HD), kv_spec(L, LANES)]
    args = [q, k, v]
    if n_cache:
        in_specs += [kv_spec(P, AT_HD), kv_spec(P, LANES)]
        args += [k_cache, v_cache]
    return pl.pallas_call(
        kern,
        grid=(B, AT_KV_HEADS, nq),
        in_specs=in_specs,
        out_specs=pl.BlockSpec((1, tq, AT_GROUP * AT_HD), lambda b, g, i: (b, i, g)),
        out_shape=jax.ShapeDtypeStruct((B, Lq, AT_HEADS * AT_HD), bf16),
        scratch_shapes=[pltpu.VMEM((rows, L + P), f32),
                        pltpu.VMEM((rows, LANES), f32)],
        compiler_params=_cparams(("arbitrary", "arbitrary", "arbitrary")),
        name="attention",
    )(*args)


def _cache_layouts(cache_k_l, cache_v_l):
    kc = jnp.swapaxes(cache_k_l, 1, 2).astype(bf16)
    vc = jnp.swapaxes(cache_v_l, 1, 2).astype(bf16)
    ones = jnp.ones(vc.shape[:-1] + (1,), bf16)
    zeros = jnp.zeros(vc.shape[:-1] + (LANES - AT_HD - 1,), bf16)
    return kc, jnp.concatenate([vc, ones, zeros], axis=-1)


def _merge_kernel(x_ref, yml_ref, yhg_ref, yat_ref, bg_ref, mod_ref, wb_ref, wo_ref, o_ref):
    merged = None
    for n, y_ref in enumerate((yml_ref, yhg_ref, yat_ref)):
        pn = _dot(y_ref[0], wb_ref[n])
        gn = _sigmoid(bg_ref[0, :, n * D_MODEL:(n + 1) * D_MODEL].astype(f32))
        merged = gn * pn if merged is None else merged + gn * pn
    out = _dot(merged.astype(bf16), wo_ref[...])
    o_ref[0] = x_ref[0] + mod_ref[0, 2:3, :] * out


def merge_call(x, y_ml, y_hg, y_at, proj, mod, wb, wo, shared_mod):
    B, L, _ = x.shape
    tm = min(L, 512)
    mod_map = (lambda b, m: (0, 0, 0)) if shared_mod else (lambda b, m: (b, 0, 0))
    yspec = pl.BlockSpec((1, tm, BRANCH_W), lambda b, m: (b, m, 0))
    return pl.pallas_call(
        _merge_kernel,
        grid=(B, L // tm),
        in_specs=[pl.BlockSpec((1, tm, D_MODEL), lambda b, m: (b, m, 0)),
                  yspec, yspec, yspec,
                  pl.BlockSpec((1, tm, 3 * D_MODEL), lambda b, m: (b, m, CB_BG)),
                  pl.BlockSpec((1, 8, D_MODEL), mod_map),
                  pl.BlockSpec((3, BRANCH_W, D_MODEL), lambda b, m: (0, 0, 0)),
                  pl.BlockSpec((D_MODEL, D_MODEL), lambda b, m: (0, 0))],
        out_specs=pl.BlockSpec((1, tm, D_MODEL), lambda b, m: (b, m, 0)),
        out_shape=jax.ShapeDtypeStruct((B, L, D_MODEL), f32),
        compiler_params=_cparams(("arbitrary", "arbitrary")),
        name="merge_out",
    )(x, y_ml, y_hg, y_at, proj, mod, wb, wo)


FFN_SPLITS = ((0, 1536), (1536, FFN_HIDDEN))


def _ffn_kernel(x_ref, mod_ref, g_ref, wi_ref, wd_ref, fg_ref, o_ref, *, final_norm):
    x = x_ref[0]
    hb = (_rms_rows(x, g_ref[...]) * (1.0 + mod_ref[0, 4:5, :]) + mod_ref[0, 3:4, :]).astype(bf16)
    acc = None
    for lo, hi in FFN_SPLITS:
        gate = _dot(hb, wi_ref[:, lo:hi])
        up = _dot(hb, wi_ref[:, FFN_HIDDEN + lo:FFN_HIDDEN + hi])
        part = _dot((_silu(gate) * up).astype(bf16), wd_ref[lo:hi, :])
        acc = part if acc is None else acc + part
    y = x + mod_ref[0, 5:6, :] * acc
    if final_norm:
        y = _rms_rows(y, fg_ref[...])
    o_ref[0] = y


def ffn_call(x, mod, g_row, w_in, w_out, fg_row, shared_mod, final_norm):
    B, L, _ = x.shape
    tm = min(L, 512)
    mod_map = (lambda b, m: (0, 0, 0)) if shared_mod else (lambda b, m: (b, 0, 0))
    kern = functools.partial(_ffn_kernel, final_norm=final_norm)
    resident = pl.Buffered(1)
    return pl.pallas_call(
        kern,
        grid=(B, L // tm),
        in_specs=[pl.BlockSpec((1, tm, D_MODEL), lambda b, m: (b, m, 0)),
                  pl.BlockSpec((1, 8, D_MODEL), mod_map),
                  pl.BlockSpec((1, D_MODEL), lambda b, m: (0, 0)),
                  pl.BlockSpec((D_MODEL, 2 * FFN_HIDDEN), lambda b, m: (0, 0), pipeline_mode=resident),
                  pl.BlockSpec((FFN_HIDDEN, D_MODEL), lambda b, m: (0, 0), pipeline_mode=resident),
                  pl.BlockSpec((1, D_MODEL), lambda b, m: (0, 0))],
        out_specs=pl.BlockSpec((1, tm, D_MODEL), lambda b, m: (b, m, 0)),
        out_shape=jax.ShapeDtypeStruct((B, L, D_MODEL), f32),
        compiler_params=_cparams(("arbitrary", "arbitrary")),
        name="ffn",
    )(x, mod, g_row, w_in, w_out, fg_row)


def _reorder_w_in(w):
    o_mg, o_hq, o_hf, o_hi, o_bg, o_end = 2048, 2064, 2576, 3600, 5392, 8464
    d = w.shape[0]
    sec16 = [w[:, o_bg:o_end], w[:, :o_mg], w[:, o_hq:o_hf], w[:, o_hi:o_bg]]
    n16 = (o_end - o_bg) + o_mg + (o_hf - o_hq) + (o_bg - o_hi)
    sec32 = [w[:, o_hf:o_hi], w[:, o_mg:o_hq]]
    n32 = (o_hi - o_hf) + (o_hq - o_mg)
    return jnp.concatenate(sec16 + [jnp.zeros((d, N16_COLS - n16), w.dtype)] + sec32
                           + [jnp.zeros((d, N32_COLS - n32), w.dtype)], axis=1).astype(bf16)


def _rope_tables(L):
    rows = L // GRID_W
    row = jnp.repeat(jnp.arange(rows, dtype=f32), GRID_W)
    colp = jnp.tile(jnp.arange(GRID_W, dtype=f32), rows)
    n_freq = AT_HD // 4
    inv = ROPE_THETA ** (-jnp.arange(n_freq, dtype=f32) / n_freq)
    ang = jnp.concatenate([row[:, None] * inv, colp[:, None] * inv], axis=-1)
    cos = jnp.repeat(jnp.cos(ang), 2, axis=-1)
    sin = jnp.repeat(jnp.sin(ang), 2, axis=-1)
    sign = jnp.tile(jnp.array([-1.0, 1.0], f32), AT_HD // 2)
    sin = sin * sign
    return jnp.tile(cos, (1, 2)), jnp.tile(sin, (1, 2))


def _mod_rows(ada_rows):
    r = ada_rows.shape[0]
    m = ada_rows.reshape(r, 6, D_MODEL)
    return jnp.concatenate([m, jnp.zeros((r, 2, D_MODEL), f32)], axis=1)


def kernel(x_prompt, x_sample, c, cache_k, cache_v, state_ml_C, state_ml_n, state_ml_m, state_hg_S, c_ctx, w_ada, b_ada, norm1_g, norm2_g, w_in, ml_gate_b, ml_norm_g, hg_lb_logits, hg_norm_g, q_norm_g, k_norm_g, w_branch, w_out, w_ffn_in, w_ffn_out, final_g):
    Bp, Lp, _ = x_prompt.shape
    Bs, Ls, _ = x_sample.shape

    n_rows = 16
    cvec = jnp.concatenate([c_ctx[None, :], c, jnp.zeros((n_rows - 1 - Bs, D_MODEL), f32)], axis=0)
    ada = ada_call(cvec, w_ada, b_ada)
    lb_all = lb_call(hg_lb_logits)
    cos_t, sin_t = _rope_tables(Ls)

    zeros_c = jnp.zeros((Bp, 1, 2, N_HEADS, D_HEAD, D_HEAD), f32)
    zeros_n = jnp.zeros((Bp, 1, 2, N_HEADS, 1, D_HEAD), f32)
    zeros_m = jnp.zeros((Bp * 2 * N_HEADS,), f32)
    n_state = state_ml_n.reshape(Bs, DEPTH, 2, N_HEADS, 1, D_HEAD)

    xp, xs = x_prompt.reshape(1, Bp * Lp, D_MODEL), x_sample
    nk, nv, nC, nn_, nm, nS = [], [], [], [], [], []
    for l in range(DEPTH):
        w_l = _reorder_w_in(w_in[l])
        wb_l = w_branch[l].astype(bf16)
        wo_l = w_out[l].astype(bf16)
        wfi_l = w_ffn_in[l].astype(bf16)
        wfo_l = w_ffn_out[l].astype(bf16)
        g1 = norm1_g[l][None, :]
        g2 = norm2_g[l][None, :]
        mlg = ml_norm_g[l][None, :]
        hgg = hg_norm_g[l][None, :]
        qg = jnp.tile(q_norm_g[l], 2)[None, :]
        kg = jnp.tile(k_norm_g[l], 2)[None, :]
        gate_b = ml_gate_b[l].reshape(-1).astype(f32)
        fg = final_g[None, :]
        last = l == DEPTH - 1

        mod_ctx = _mod_rows(ada[l, 0:1])
        mod_lat = _mod_rows(ada[l, 1:1 + Bs])

        proj_flat, proj32 = in_proj_call(xp, mod_ctx, g1, w_l, True)
        proj = proj_flat.reshape(Bp, Lp, N16_COLS)
        proj32 = proj32.reshape(Bp, Lp, N32_COLS)
        y_ml, c_f, n_f, m_f = mlstm_call(proj, proj32, gate_b, mlg, zeros_c, zeros_n, zeros_m, 0)
        y_hg, s_f = hgrn_call(proj, proj32, lb_all[l], hgg, zeros_c, 0)
        q_h, k_h, v_h, k_n = qkv_prep_call(proj, qg, kg, cos_t[:Lp], sin_t[:Lp], False)
        y_at = attn_call(q_h, k_h, v_h)
        flat = (1, Bp * Lp, BRANCH_W)
        xp = merge_call(xp, y_ml.reshape(flat), y_hg.reshape(flat), y_at.reshape(flat), proj_flat,
                        mod_ctx, wb_l, wo_l, True)
        xp = ffn_call(xp, mod_ctx, g2, wfi_l, wfo_l, fg, True, last)
        nk.append(k_n.reshape(Bp, Lp, AT_KV_HEADS, AT_HD))
        nv.append(proj[:, :, CB_AV * LANES:(CB_AV + 1) * LANES].astype(f32).reshape(Bp, Lp, AT_KV_HEADS, AT_HD))
        nC.append(c_f)
        nn_.append(n_f[:, :, :, 0, :])
        nm.append(m_f[:, :, :, 0, 0])
        nS.append(s_f)

        proj, proj32 = in_proj_call(xs, mod_lat, g1, w_l, False)
        y_ml, _, _, _ = mlstm_call(proj, proj32, gate_b, mlg, state_ml_C, n_state,
                                   state_ml_m[:, l].reshape(-1).astype(f32), l)
        y_hg, _ = hgrn_call(proj, proj32, lb_all[l], hgg, state_hg_S, l)
        q_h, k_h, v_h, _ = qkv_prep_call(proj, qg, kg, cos_t, sin_t, True)
        k_c, v_c = _cache_layouts(cache_k[:, l], cache_v[:, l])
        y_at = attn_call(q_h, k_h, v_h, k_c, v_c)
        xs = merge_call(xs, y_ml, y_hg, y_at, proj, mod_lat, wb_l, wo_l, False)
        xs = ffn_call(xs, mod_lat, g2, wfi_l, wfo_l, fg, False, last)

    return (xp.reshape(Bp, Lp, D_MODEL), xs, jnp.stack(nk, axis=1), jnp.stack(nv, axis=1), jnp.stack(nC, axis=1),
            jnp.stack(nn_, axis=1), jnp.stack(nm, axis=1), jnp.stack(nS, axis=1))
```

```python
import functools
import math

import jax
import jax.numpy as jnp
from jax import lax
from jax.experimental import pallas as pl
from jax.experimental.pallas import tpu as pltpu

f32 = jnp.float32
bf16 = jnp.bfloat16

D_MODEL = 1024
DEPTH = 4
N_HEADS = 4
D_HEAD = 128
AT_HEADS = 8
AT_KV_HEADS = 2
AT_GROUP = AT_HEADS // AT_KV_HEADS
AT_HD = 64
BRANCH_W = 512
FFN_HIDDEN = 2816
GRID_W = 64
ROPE_THETA = 10000.0
EPS = 1e-6

LANES = 128
SUBLANES = 8
VMEM_LIMIT = 52 * 1024 * 1024

CB_BG = 0
CB_MQ = 24
CB_MK = 28
CB_MV = 32
CB_MO = 36
CB_HQ = 40
CB_HI = 44
CB_HG = 48
CB_AQ = 52
CB_AK = 56
CB_AV = 57
N16_BLOCKS = 60
N16_COLS = N16_BLOCKS * LANES
CF_HF = 0
CF_MG = 8
N32_BLOCKS = 10
N32_COLS = N32_BLOCKS * LANES
PROJ_TN = N32_COLS
N16_TILES = N16_COLS // PROJ_TN

ML_CHUNK = 128
ML_EXT = 144
HG_CHUNK = 64
HG_SUB = 8
HG_HALF = 4
HG_LEVELS = (32, 16, 8, 4)
NEG_BIG = -1e30
LOG2E = 1.4426950408889634


def _cparams(sem):
    return pltpu.CompilerParams(dimension_semantics=sem, vmem_limit_bytes=VMEM_LIMIT)


def _dot(a, b):
    return jnp.dot(a, b, preferred_element_type=f32)


def _dot_nt(a, b):
    return lax.dot_general(a, b, (((1,), (1,)), ((), ())), preferred_element_type=f32)


def _dot_tn(a, b):
    return lax.dot_general(a, b, (((0,), (0,)), ((), ())), preferred_element_type=f32)


def _sigmoid(x):
    return 1.0 / (1.0 + jnp.exp(-x))


def _silu(x):
    return x * _sigmoid(x)


def _log_sigmoid(x):
    return jnp.minimum(x, 0.0) - jnp.log1p(jnp.exp(-jnp.abs(x)))


def _split3(x):
    hi = x.astype(bf16)
    r1 = x - hi.astype(f32)
    mid = r1.astype(bf16)
    lo = (r1 - mid.astype(f32)).astype(bf16)
    return hi, mid, lo


def _rms_rows(x, g_row):
    ms = jnp.mean(x * x, axis=-1, keepdims=True)
    return x * lax.rsqrt(ms + EPS) * g_row


def _ada_kernel(c_ref, w_ref, b_ref, o_ref):
    s = _silu(c_ref[...]).astype(bf16)
    o_ref[0] = _dot(s, w_ref[0].astype(bf16)) + b_ref[0]


def ada_call(cvec, w_ada, b_ada):
    rows = cvec.shape[0]
    tn = 1536
    n6 = 6 * D_MODEL
    return pl.pallas_call(
        _ada_kernel,
        grid=(DEPTH, n6 // tn),
        in_specs=[pl.BlockSpec((rows, D_MODEL), lambda l, n: (0, 0)),
                  pl.BlockSpec((1, D_MODEL, tn), lambda l, n: (l, 0, n)),
                  pl.BlockSpec((1, 1, tn), lambda l, n: (l, 0, n))],
        out_specs=pl.BlockSpec((1, rows, tn), lambda l, n: (l, 0, n)),
        out_shape=jax.ShapeDtypeStruct((DEPTH, rows, n6), f32),
        compiler_params=_cparams(("arbitrary", "arbitrary")),
        name="ada",
    )(cvec, w_ada, b_ada.reshape(DEPTH, 1, n6))


def _lb_kernel(x_ref, o_ref):
    xs = [x_ref[l] for l in range(DEPTH)]
    mx = xs[0]
    for l in range(1, DEPTH):
        mx = jnp.maximum(mx, xs[l])
    es = [jnp.exp(x - mx) for x in xs]
    tot = es[0]
    for l in range(1, DEPTH):
        tot = tot + es[l]
    sm = [e / tot for e in es]
    run = sm[0]
    o_ref[0] = run - sm[0]
    for l in range(1, DEPTH):
        run = run + sm[l]
        o_ref[l] = run - sm[0]


def lb_call(logits):
    x = logits.astype(f32).reshape(DEPTH, 2 * N_HEADS, D_HEAD)
    return pl.pallas_call(
        _lb_kernel,
        out_shape=jax.ShapeDtypeStruct((DEPTH, 2 * N_HEADS, D_HEAD), f32),
        name="hg_lower_bounds",
    )(x)


def _in_proj_kernel(x_ref, mod_ref, g_ref, w_ref, o16_ref, o32_ref, h_s):
    n = pl.program_id(2)

    @pl.when(n == 0)
    def _():
        x = x_ref[0]
        sh = mod_ref[0, 0:1, :]
        sc = mod_ref[0, 1:2, :]
        h = _rms_rows(x, g_ref[...]) * (1.0 + sc) + sh
        h_s[...] = h.astype(bf16)

    @pl.when(n < N16_TILES)
    def _():
        o16_ref[0] = _dot(h_s[...], w_ref[...]).astype(bf16)

    @pl.when(n == N16_TILES)
    def _():
        o32_ref[0] = _dot(h_s[...], w_ref[...])


def in_proj_call(x, mod, g_row, w, shared_mod):
    B, L, _ = x.shape
    tm = min(L, 1024)
    tn = PROJ_TN
    mod_map = (lambda b, m, n: (0, 0, 0)) if shared_mod else (lambda b, m, n: (b, 0, 0))
    return pl.pallas_call(
        _in_proj_kernel,
        grid=(B, L // tm, N16_TILES + 1),
        in_specs=[pl.BlockSpec((1, tm, D_MODEL), lambda b, m, n: (b, m, 0)),
                  pl.BlockSpec((1, 8, D_MODEL), mod_map),
                  pl.BlockSpec((1, D_MODEL), lambda b, m, n: (0, 0)),
                  pl.BlockSpec((D_MODEL, tn), lambda b, m, n: (0, n))],
        out_specs=[pl.BlockSpec((1, tm, tn), lambda b, m, n: (b, m, jnp.minimum(n, N16_TILES - 1))),
                   pl.BlockSpec((1, tm, tn), lambda b, m, n: (b, m, 0))],
        out_shape=[jax.ShapeDtypeStruct((B, L, N16_COLS), bf16),
                   jax.ShapeDtypeStruct((B, L, N32_COLS), f32)],
        scratch_shapes=[pltpu.VMEM((tm, D_MODEL), bf16)],
        compiler_params=_cparams(("arbitrary", "arbitrary", "arbitrary")),
        name="in_proj",
    )(x, mod, g_row, w)


def _mlstm_kernel(bias_ref, m0_ref, q_ref, k_ref, v_ref, mo_ref, g_ref, ng_ref, c0_ref, n0_ref,
                  y_ref, cout_ref, nout_ref, mout_ref,
                  gt_s, rw_s, ucb_s, vt_s, ht_s, ct_s, m_s, *, seq_len):
    T = ML_CHUNK
    assert T == LANES
    nc = seq_len // T
    b_idx = pl.program_id(0)
    h = pl.program_id(1)
    scale = D_HEAD ** -0.5
    ext_rows = jnp.concatenate([jnp.ones((1, T), f32), jnp.zeros((ML_EXT - D_HEAD - 1, T), f32)], axis=0)

    def tr_body(j, carry):
        r0 = pl.multiple_of(j * T, T)
        gt_s[:, pl.ds(r0, T)] = g_ref[0, pl.ds(r0, T), :].T
        vt = v_ref[0, pl.ds(r0, T), :].astype(f32).T
        vt_s[:, pl.ds(r0, T)] = jnp.concatenate([vt, ext_rows], axis=0).astype(bf16)
        return carry
    lax.fori_loop(0, nc, tr_body, 0, unroll=min(8, nc))

    i_f = gt_s[pl.ds(h, 1), :] + bias_ref[h]
    f_f = _log_sigmoid(gt_s[pl.ds(4 + h, 1), :] + bias_ref[4 + h])
    i_b = gt_s[pl.ds(8 + h, 1), :] + bias_ref[8 + h]
    f_b = _log_sigmoid(gt_s[pl.ds(12 + h, 1), :] + bias_ref[12 + h])
    zrow = jnp.zeros_like(i_f)
    rw_s[...] = jnp.concatenate([f_f, i_f, f_b, i_b, zrow, zrow, zrow, zrow], axis=0)

    ui = lax.broadcasted_iota(jnp.int32, (T, T), 0)
    si = lax.broadcasted_iota(jnp.int32, (T, T), 1)
    tri = jnp.concatenate([(ui <= si).astype(bf16), (ui >= si).astype(bf16)], axis=1)
    valid_f = ui <= si
    valid_b = ui >= si

    def cs_body(j, carry):
        r0 = pl.multiple_of(j * T, T)
        rows = rw_s[:, pl.ds(r0, T)]
        hi, mid, lo = _split3(rows)
        cs3 = _dot(jnp.concatenate([hi, mid, lo, jnp.zeros_like(hi)], axis=0), tri)
        cs = cs3[0:8] + cs3[8:16] + cs3[16:24]
        b_f = cs[0:1, 0:T]
        b_b = cs[2:3, T:2 * T]
        rw_s[:, pl.ds(r0, T)] = jnp.concatenate([b_f, rows[1:2], b_b, rows[3:8]], axis=0)
        ucb_s[0, pl.ds(r0, T), :] = jnp.broadcast_to(rows[1:2] - b_f, (T, T)).T
        ucb_s[1, pl.ds(r0, T), :] = jnp.broadcast_to(rows[3:4] - b_b, (T, T)).T
        return carry
    lax.fori_loop(0, nc, cs_body, 0, unroll=min(16, nc))

    for d in range(2):
        ct_s[d] = jnp.concatenate([c0_ref[0, 0, d, 0].T, n0_ref[0, 0, d, 0],
                                   jnp.zeros((ML_EXT - D_HEAD - 1, D_HEAD), f32)], axis=0)
        m_s[d] = jnp.full((1, LANES), m0_ref[b_idx * 2 * N_HEADS + d * N_HEADS + h], f32)

    def step(d, j):
        r0 = pl.multiple_of(j * T, T)
        b_row = rw_s[pl.ds(2 * d, 1), pl.ds(r0, T)]
        b_end = b_row[:, T - 1:T] if d == 0 else b_row[:, 0:1]
        b_end_l = jnp.broadcast_to(b_end, (1, LANES))
        ucb = ucb_s[d, pl.ds(r0, T), :]

        q = q_ref[0, pl.ds(r0, T), :].astype(f32) * scale
        kb = k_ref[0, pl.ds(r0, T), :]
        k = kb.astype(f32)
        qb = q.astype(bf16)
        vt = vt_s[:, pl.ds(r0, T)]
        ct = ct_s[d]
        m_prev = m_s[d]

        d_t = jnp.where(valid_f if d == 0 else valid_b, ucb + b_row, NEG_BIG)
        m_state = b_row + m_prev
        m_t = jnp.maximum(m_state, jnp.max(d_t, axis=0, keepdims=True))
        w_state = jnp.exp(m_state - m_t)
        s_t = _dot_nt(kb, qb) * jnp.exp(d_t - m_t)
        tot = _dot(vt, s_t.astype(bf16)) + w_state * _dot_nt(ct.astype(bf16), qb)
        den = tot[D_HEAD:D_HEAD + 1]
        ht_s[d, :, pl.ds(r0, T)] = tot[:D_HEAD] * (1.0 / jnp.maximum(jnp.abs(den), jnp.exp(-m_t)))

        gcb = ucb + b_end_l
        m_new = jnp.maximum(b_end_l + m_prev, jnp.max(gcb, axis=0, keepdims=True))
        kw = k * jnp.exp(gcb - m_new)
        decay = jnp.exp(b_end_l + m_prev - m_new)
        ct_s[d] = decay * ct + _dot(vt, kw.astype(bf16))
        m_s[d] = m_new

    def loop_body(j, carry):
        step(0, j)
        step(1, nc - 1 - j)
        return carry
    lax.fori_loop(0, nc, loop_body, 0, unroll=min(16, nc))

    ng = ng_ref[...]

    def out_body(j, carry):
        r0 = pl.multiple_of(j * T, T)
        y_t = ht_s[0, :, pl.ds(r0, T)] + ht_s[1, :, pl.ds(r0, T)]
        ms = jnp.mean(y_t * y_t, axis=0, keepdims=True)
        y = (y_t * lax.rsqrt(ms + EPS)).T * ng * _sigmoid(mo_ref[0, pl.ds(r0, T), :].astype(f32))
        y_ref[0, pl.ds(r0, T), :] = y.astype(y_ref.dtype)
        return carry
    lax.fori_loop(0, nc, out_body, 0, unroll=min(8, nc))

    for d in range(2):
        cout_ref[0, d, 0] = ct_s[d, 0:D_HEAD, :].T
        nout_ref[0, d, 0] = ct_s[d, D_HEAD:D_HEAD + 1, :]
        mout_ref[0, d, 0] = m_s[d]


def mlstm_call(proj, proj32, gate_bias, norm_g, c0, n0, m0, layer):
    B, L, _ = proj.shape

    def col(cb):
        return pl.BlockSpec((1, L, LANES), lambda b, h, cb=cb: (b, 0, cb + h))

    smem = pl.BlockSpec(memory_space=pltpu.SMEM)
    kern = functools.partial(_mlstm_kernel, seq_len=L)
    return pl.pallas_call(
        kern,
        grid=(B, N_HEADS),
        in_specs=[smem, smem,
                  col(CB_MQ), col(CB_MK), col(CB_MV), col(CB_MO),
                  pl.BlockSpec((1, L, LANES), lambda b, h: (b, 0, CF_MG)),
                  pl.BlockSpec((1, LANES), lambda b, h: (0, h)),
                  pl.BlockSpec((1, 1, 2, 1, D_HEAD, D_HEAD), lambda b, h: (b, layer, 0, h, 0, 0)),
                  pl.BlockSpec((1, 1, 2, 1, 1, D_HEAD), lambda b, h: (b, layer, 0, h, 0, 0))],
        out_specs=[pl.BlockSpec((1, L, LANES), lambda b, h: (b, 0, h)),
                   pl.BlockSpec((1, 2, 1, D_HEAD, D_HEAD), lambda b, h: (b, 0, h, 0, 0)),
                   pl.BlockSpec((1, 2, 1, 1, D_HEAD), lambda b, h: (b, 0, h, 0, 0)),
                   pl.BlockSpec((1, 2, 1, 1, LANES), lambda b, h: (b, 0, h, 0, 0))],
        out_shape=[jax.ShapeDtypeStruct((B, L, BRANCH_W), bf16),
                   jax.ShapeDtypeStruct((B, 2, N_HEADS, D_HEAD, D_HEAD), f32),
                   jax.ShapeDtypeStruct((B, 2, N_HEADS, 1, D_HEAD), f32),
                   jax.ShapeDtypeStruct((B, 2, N_HEADS, 1, LANES), f32)],
        scratch_shapes=[pltpu.VMEM((LANES, L), f32),
                        pltpu.VMEM((8, L), f32),
                        pltpu.VMEM((2, L, LANES), f32),
                        pltpu.VMEM((ML_EXT, L), bf16),
                        pltpu.VMEM((2, D_HEAD, L), f32),
                        pltpu.VMEM((2, ML_EXT, D_HEAD), f32),
                        pltpu.VMEM((2, 1, LANES), f32)],
        compiler_params=_cparams(("arbitrary", "arbitrary")),
        name="mlstm",
    )(gate_bias, m0, proj, proj, proj, proj, proj32, norm_g, c0, n0)


def _hgrn_kernel(q_ref, f0_ref, f1_ref, i_ref, og_ref, lb_ref, ng_ref, s0_ref,
                 y_ref, sout_ref, o_s, st_s, qx_s, dec_s, u_s, vt_s, k_s, b_s, q32_s, v32_s, *, seq_len):
    T = HG_CHUNK
    nc = seq_len // T
    h = pl.program_id(1)
    f_refs = (f0_ref, f1_ref)

    lbs, loglb, log1mlb = [], [], []
    for d in range(2):
        lb = lb_ref[pl.ds(d * N_HEADS + h, 1), :]
        lbs.append(lb)
        loglb.append(jnp.log(lb))
        log1mlb.append(jnp.log1p(-lb))
        st_s[d] = s0_ref[0, 0, d, 0].T

    ti = lax.broadcasted_iota(jnp.int32, (T, T), 0)
    ui = lax.broadcasted_iota(jnp.int32, (T, T), 1)
    tri = ((ui <= ti).astype(bf16), (ui >= ti).astype(bf16))
    sub_t = lax.broadcasted_iota(jnp.int32, (HG_SUB, D_HEAD), 0)
    sub_q = sub_t % HG_HALF
    low_half = sub_t < HG_HALF
    row_i = lax.broadcasted_iota(jnp.int32, (T, D_HEAD), 0)
    tt_r = lax.broadcasted_iota(jnp.int32, (T, T), 0)
    tt_c = lax.broadcasted_iota(jnp.int32, (T, T), 1)
    q_rows = {(hh, dd): ((row_i % (2 * hh)) >= hh) if dd == 0 else ((row_i % (2 * hh)) < hh)
              for hh in HG_LEVELS for dd in range(2)}
    same_blk = {hh: (tt_r // (2 * hh)) == (tt_c // (2 * hh)) for hh in HG_LEVELS}

    def gates(d, j, q):
        r0 = pl.multiple_of(j * T, T)
        hf = f_refs[d][0, pl.ds(r0, T), :]
        e = jnp.exp(-jnp.abs(hf))
        sig_neg = jnp.where(hf >= 0.0, e, 1.0) / (1.0 + e)
        logsig = jnp.minimum(hf, 0.0) - jnp.log(1.0 + e)
        cterm = log1mlb[d] + logsig
        amax = jnp.maximum(loglb[d], cterm)
        logf = amax + jnp.log(1.0 + jnp.exp(-jnp.abs(loglb[d] - cterm)))
        k = (1.0 - lbs[d]) * sig_neg
        hi, mid, lo = _split3(logf * LOG2E)
        b = _dot(tri[d], hi) + _dot(tri[d], mid) + _dot(tri[d], lo)
        b_end = b[T - 1:T] if d == 0 else b[0:1]
        k_s[d, pl.ds(r0, T), :] = k
        b_s[d, pl.ds(r0, T), :] = b
        qx_s[d, pl.ds(r0, T), :] = (q * jnp.exp2(b)).astype(bf16)
        dec_s[d, pl.ds(j, 1), :] = jnp.exp2(b_end)

    def gates_body(j, carry):
        r0 = pl.multiple_of(j * T, T)
        q = q_ref[0, pl.ds(r0, T), :].astype(f32)
        v = i_ref[0, pl.ds(r0, T), :].astype(f32)
        q32_s[pl.ds(r0, T), :] = q
        v32_s[pl.ds(r0, T), :] = v
        vt_s[j] = v.T.astype(bf16)
        gates(0, j, q)
        gates(1, j, q)
        return carry
    lax.fori_loop(0, nc, gates_body, 0, unroll=min(8, nc))

    def intra(d, j):
        r0 = pl.multiple_of(j * T, T)

        def rows(ref2, lo_r, n):
            return ref2[pl.ds(r0 + lo_r, n), :]

        def srows(ref3, lo_r, n):
            return ref3[d, pl.ds(r0 + lo_r, n), :]

        nsub = T // HG_SUB
        o_blk = [None] * nsub

        def add(idx, val):
            o_blk[idx] = val if o_blk[idx] is None else o_blk[idx] + val

        def levels():
            b_all = srows(b_s, 0, T)
            q_all = rows(q32_s, 0, T)
            k_all = srows(k_s, 0, T)
            a_tot = None
            for h in HG_LEVELS:
                blk = 2 * h
                ref_off = h - 1 if d == 0 else h
                ref = jnp.concatenate([jnp.broadcast_to(srows(b_s, base + ref_off, 1), (blk, D_HEAD))
                                       for base in range(0, T, blk)], axis=0)
                is_q = q_rows[(h, d)]
                qx = q_all * jnp.exp2(jnp.where(is_q, b_all - ref, NEG_BIG))
                kx = k_all * jnp.exp2(jnp.where(is_q, NEG_BIG, ref - b_all))
                a = _dot_nt(qx.astype(bf16), kx.astype(bf16))
                if blk < T:
                    a = jnp.where(same_blk[h], a, 0.0)
                a_tot = a if a_tot is None else a_tot + a
            ov = _dot(a_tot.astype(bf16), rows(v32_s, 0, T).astype(bf16))
            for p in range(nsub):
                add(p, ov[p * HG_SUB:(p + 1) * HG_SUB])

        def diag(lo_r):
            bs = srows(b_s, lo_r, HG_SUB)
            qs = rows(q32_s, lo_r, HG_SUB)

            def keyrow(read, s):
                return jnp.where(low_half, read(lo_r + s, 1), read(lo_r + HG_HALF + s, 1))

            acc = None
            for s in range(HG_HALF):
                valid = (sub_q >= s) if d == 0 else (sub_q <= s)
                b_key = keyrow(lambda r, n: srows(b_s, r, n), s)
                ee = jnp.exp2(jnp.where(valid, bs - b_key, NEG_BIG))
                a_col = jnp.sum(qs * ee * keyrow(lambda r, n: srows(k_s, r, n), s), axis=-1, keepdims=True)
                term = a_col * keyrow(lambda r, n: rows(v32_s, r, n), s)
                acc = term if acc is None else acc + term
            add(lo_r // HG_SUB, acc)

        levels()
        for i in range(nsub):
            diag(i * HG_SUB)
        o_s[d, pl.ds(r0, T), :] = jnp.concatenate(o_blk, axis=0)

    def intra_body(j, carry):
        intra(0, j)
        intra(1, j)
        return carry
    lax.fori_loop(0, nc, intra_body, 0, unroll=min(16, nc))

    def incr_body(j, carry):
        r0 = pl.multiple_of(j * T, T)
        vt = vt_s[j]
        for d in range(2):
            b = b_s[d, pl.ds(r0, T), :]
            b_end = b[T - 1:T] if d == 0 else b[0:1]
            u_s[d, j] = _dot(vt, (k_s[d, pl.ds(r0, T), :] * jnp.exp2(b_end - b)).astype(bf16))
        return carry
    lax.fori_loop(0, nc, incr_body, 0, unroll=min(8, nc))

    def state_step(d, j):
        r0 = pl.multiple_of(j * T, T)
        st = st_s[d]
        o_s[d, pl.ds(r0, T), :] = o_s[d, pl.ds(r0, T), :] + _dot_nt(qx_s[d, pl.ds(r0, T), :], st.astype(bf16))
        st_s[d] = st * dec_s[d, pl.ds(j, 1), :] + u_s[d, j]

    def state_body(j, carry):
        state_step(0, j)
        state_step(1, nc - 1 - j)
        return carry
    lax.fori_loop(0, nc, state_body, 0, unroll=min(16, nc // 2))

    ng = ng_ref[...]

    def out_body(j, carry):
        r0 = pl.multiple_of(j * T, T)
        y = o_s[0, pl.ds(r0, T), :] + o_s[1, pl.ds(r0, T), :]
        y = _rms_rows(y, ng) * _silu(og_ref[0, pl.ds(r0, T), :].astype(f32))
        y_ref[0, pl.ds(r0, T), :] = y.astype(y_ref.dtype)
        return carry
    lax.fori_loop(0, nc, out_body, 0, unroll=min(8, nc))

    for d in range(2):
        sout_ref[0, d, 0] = st_s[d].T


def hgrn_call(proj, proj32, lb_rows, norm_g, s0, layer):
    B, L, _ = proj.shape

    def col(cb):
        return pl.BlockSpec((1, L, LANES), lambda b, h, cb=cb: (b, 0, cb + h))

    kern = functools.partial(_hgrn_kernel, seq_len=L)
    return pl.pallas_call(
        kern,
        grid=(B, N_HEADS),
        in_specs=[col(CB_HQ), col(CF_HF), col(CF_HF + N_HEADS), col(CB_HI), col(CB_HG),
                  pl.BlockSpec((2 * N_HEADS, D_HEAD), lambda b, h: (0, 0)),
                  pl.BlockSpec((1, LANES), lambda b, h: (0, h)),
                  pl.BlockSpec((1, 1, 2, 1, D_HEAD, D_HEAD), lambda b, h: (b, layer, 0, h, 0, 0))],
        out_specs=[pl.BlockSpec((1, L, LANES), lambda b, h: (b, 0, h)),
                   pl.BlockSpec((1, 2, 1, D_HEAD, D_HEAD), lambda b, h: (b, 0, h, 0, 0))],
        out_shape=[jax.ShapeDtypeStruct((B, L, BRANCH_W), bf16),
                   jax.ShapeDtypeStruct((B, 2, N_HEADS, D_HEAD, D_HEAD), f32)],
        scratch_shapes=[pltpu.VMEM((2, L, D_HEAD), f32),
                        pltpu.VMEM((2, D_HEAD, D_HEAD), f32),
                        pltpu.VMEM((2, L, D_HEAD), bf16),
                        pltpu.VMEM((2, L // HG_CHUNK, D_HEAD), f32),
                        pltpu.VMEM((2, L // HG_CHUNK, D_HEAD, D_HEAD), f32),
                        pltpu.VMEM((L // HG_CHUNK, D_HEAD, HG_CHUNK), bf16),
                        pltpu.VMEM((2, L, D_HEAD), f32),
                        pltpu.VMEM((2, L, D_HEAD), f32),
                        pltpu.VMEM((L, D_HEAD), f32),
                        pltpu.VMEM((L, D_HEAD), f32)],
        compiler_params=_cparams(("arbitrary", "arbitrary")),
        name="hgrn",
    )(proj, proj32, proj32, proj, proj, lb_rows, norm_g, s0)


def _pair_swap(x):
    lane = lax.broadcasted_iota(jnp.int32, x.shape, 1)
    return jnp.where((lane % 2) == 0, pltpu.roll(x, LANES - 1, 1), pltpu.roll(x, 1, 1))


def _head_norm_pair(x, g_row):
    gi = lax.broadcasted_iota(jnp.int32, (LANES, LANES), 0) // AT_HD
    gj = lax.broadcasted_iota(jnp.int32, (LANES, LANES), 1) // AT_HD
    same_head = (gi == gj).astype(bf16)
    sq = x * x
    hi = sq.astype(bf16)
    lo = (sq - hi.astype(f32)).astype(bf16)
    ms = (_dot(hi, same_head) + _dot(lo, same_head)) * (1.0 / AT_HD)
    return x * lax.rsqrt(ms + EPS) * g_row


def _qkv_prep_kernel(aq_ref, ak_ref, av_ref, qg_ref, kg_ref, cos_ref, sin_ref,
                     q_ref, k_ref, v_ref, kn_ref, *, use_rope):
    qg = qg_ref[...]
    kg = kg_ref[...]
    if use_rope:
        cos = cos_ref[...]
        sin = sin_ref[...]
    for p in range(AT_HEADS // 2):
        x = _head_norm_pair(aq_ref[0, :, p * LANES:(p + 1) * LANES].astype(f32), qg)
        if use_rope:
            x = x * cos + _pair_swap(x) * sin
        x = (x * (AT_HD ** -0.5)).astype(bf16)
        q_ref[0, 2 * p] = x[:, :AT_HD]
        q_ref[0, 2 * p + 1] = x[:, AT_HD:]
    kn = _head_norm_pair(ak_ref[0].astype(f32), kg)
    kn_ref[0] = kn
    if use_rope:
        kn = kn * cos + _pair_swap(kn) * sin
    kb = kn.astype(bf16)
    for g in range(AT_KV_HEADS):
        k_ref[0, g] = kb[:, g * AT_HD:(g + 1) * AT_HD]
    v = av_ref[0].astype(f32)
    lane = lax.broadcasted_iota(jnp.int32, v.shape, 1)
    tail = jnp.where(lane == AT_HD, 1.0, 0.0)
    v_ref[0, 0] = jnp.where(lane < AT_HD, v, tail).astype(bf16)
    v_ref[0, 1] = jnp.where(lane < AT_HD, pltpu.roll(v, AT_HD, 1), tail).astype(bf16)


def qkv_prep_call(proj, qg_row, kg_row, cos_t, sin_t, use_rope):
    B, L, _ = proj.shape
    tm = min(L, 512)
    kern = functools.partial(_qkv_prep_kernel, use_rope=use_rope)
    return pl.pallas_call(
        kern,
        grid=(B, L // tm),
        in_specs=[pl.BlockSpec((1, tm, 4 * LANES), lambda b, m: (b, m, CB_AQ // 4)),
                  pl.BlockSpec((1, tm, LANES), lambda b, m: (b, m, CB_AK)),
                  pl.BlockSpec((1, tm, LANES), lambda b, m: (b, m, CB_AV)),
                  pl.BlockSpec((1, LANES), lambda b, m: (0, 0)),
                  pl.BlockSpec((1, LANES), lambda b, m: (0, 0)),
                  pl.BlockSpec((tm, LANES), lambda b, m: (m, 0)),
                  pl.BlockSpec((tm, LANES), lambda b, m: (m, 0))],
        out_specs=[pl.BlockSpec((1, AT_HEADS, tm, AT_HD), lambda b, m: (b, 0, m, 0)),
                   pl.BlockSpec((1, AT_KV_HEADS, tm, AT_HD), lambda b, m: (b, 0, m, 0)),
                   pl.BlockSpec((1, AT_KV_HEADS, tm, LANES), lambda b, m: (b, 0, m, 0)),
                   pl.BlockSpec((1, tm, LANES), lambda b, m: (b, m, 0))],
        out_shape=[jax.ShapeDtypeStruct((B, AT_HEADS, L, AT_HD), bf16),
                   jax.ShapeDtypeStruct((B, AT_KV_HEADS, L, AT_HD), bf16),
                   jax.ShapeDtypeStruct((B, AT_KV_HEADS, L, LANES), bf16),
                   jax.ShapeDtypeStruct((B, L, LANES), f32)],
        compiler_params=_cparams(("arbitrary", "arbitrary")),
        name="qkv_prep",
    )(proj, proj, proj, qg_row, kg_row, cos_t, sin_t)


def _attn_kernel(*refs, tq, tk, n_new, n_cache):
    if n_cache:
        q_ref, k_ref, v_ref, kc_ref, vc_ref, o_ref, s_s, m_s = refs
    else:
        q_ref, k_ref, v_ref, o_ref, s_s, m_s = refs
    rows = AT_GROUP * tq
    ncb = tk // LANES
    nkv = n_new + n_cache

    def chunk(new_ref, cache_ref, j):
        if j < n_new:
            return new_ref[0, 0, j * tk:(j + 1) * tk, :]
        return cache_ref[0, 0, (j - n_new) * tk:(j - n_new + 1) * tk, :]

    always = pl.program_id(2) >= 0

    @pl.when(always)
    def _():
        q = q_ref[0].reshape(rows, AT_HD)
        mx = None
        for j in range(nkv):
            s = _dot_nt(q, chunk(k_ref, kc_ref if n_cache else None, j))
            s_s[:, j * tk:(j + 1) * tk] = s
            cm = s[:, 0:LANES]
            for cb in range(1, ncb):
                cm = jnp.maximum(cm, s[:, cb * LANES:(cb + 1) * LANES])
            mx = cm if mx is None else jnp.maximum(mx, cm)
        m_s[...] = jnp.broadcast_to(jnp.max(mx, axis=-1, keepdims=True), (rows, LANES))

    @pl.when(pl.program_id(1) >= 0)
    def _():
        m_b = m_s[...]
        acc = None
        for j in range(nkv):
            ps = []
            for cb in range(ncb):
                off = j * tk + cb * LANES
                ps.append(jnp.exp(s_s[:, off:off + LANES] - m_b).astype(bf16))
            pv = _dot(jnp.concatenate(ps, axis=1), chunk(v_ref, vc_ref if n_cache else None, j))
            acc = pv if acc is None else acc + pv
        o = acc[:, :AT_HD] * (1.0 / acc[:, AT_HD:AT_HD + 1])
        for g in range(AT_GROUP):
            o_ref[0, :, g * AT_HD:(g + 1) * AT_HD] = o[g * tq:(g + 1) * tq].astype(o_ref.dtype)


def attn_call(q, k, v, k_cache=None, v_cache=None):
    B, _, Lq, _ = q.shape
    L = k.shape[2]
    P = 0 if k_cache is None else k_cache.shape[2]
    tq = 256
    tk = 512 if (L % 512 == 0 and P % 512 == 0) else L
    assert L % tk == 0 and P % tk == 0
    n_new, n_cache = L // tk, P // tk
    nq = Lq // tq
    kern = functools.partial(_attn_kernel, tq=tq, tk=tk, n_new=n_new, n_cache=n_cache)
    rows = AT_GROUP * tq

    def kv_spec(length, width):
        return pl.BlockSpec((1, 1, length, width), lambda b, g, i: (b, g, 0, 0))

    in_specs = [pl.BlockSpec((1, AT_GROUP, tq, AT_HD), lambda b, g, i: (b, g, i, 0)),
                kv_spec(L, AT_HD), kv_spec(L, LANES)]
    args = [q, k, v]
    if n_cache:
        in_specs += [kv_spec(P, AT_HD), kv_spec(P, LANES)]
        args += [k_cache, v_cache]
    return pl.pallas_call(
        kern,
        grid=(B, AT_KV_HEADS, nq),
        in_specs=in_specs,
        out_specs=pl.BlockSpec((1, tq, AT_GROUP * AT_HD), lambda b, g, i: (b, i, g)),
        out_shape=jax.ShapeDtypeStruct((B, Lq, AT_HEADS * AT_HD), bf16),
        scratch_shapes=[pltpu.VMEM((rows, L + P), f32),
                        pltpu.VMEM((rows, LANES), f32)],
        compiler_params=_cparams(("arbitrary", "arbitrary", "arbitrary")),
        name="attention",
    )(*args)


def _cache_layouts(cache_k_l, cache_v_l):
    kc = jnp.swapaxes(cache_k_l, 1, 2).astype(bf16)
    vc = jnp.swapaxes(cache_v_l, 1, 2).astype(bf16)
    ones = jnp.ones(vc.shape[:-1] + (1,), bf16)
    zeros = jnp.zeros(vc.shape[:-1] + (LANES - AT_HD - 1,), bf16)
    return kc, jnp.concatenate([vc, ones, zeros], axis=-1)


def _merge_kernel(x_ref, yml_ref, yhg_ref, yat_ref, bg_ref, mod_ref, wb_ref, wo_ref, o_ref):
    merged = None
    for n, y_ref in enumerate((yml_ref, yhg_ref, yat_ref)):
        pn = _dot(y_ref[0], wb_ref[n])
        gn = _sigmoid(bg_ref[0, :, n * D_MODEL:(n + 1) * D_MODEL].astype(f32))
        merged = gn * pn if merged is None else merged + gn * pn
    out = _dot(merged.astype(bf16), wo_ref[...])
    o_ref[0] = x_ref[0] + mod_ref[0, 2:3, :] * out


def merge_call(x, y_ml, y_hg, y_at, proj, mod, wb, wo, shared_mod):
    B, L, _ = x.shape
    tm = min(L, 512)
    mod_map = (lambda b, m: (0, 0, 0)) if shared_mod else (lambda b, m: (b, 0, 0))
    yspec = pl.BlockSpec((1, tm, BRANCH_W), lambda b, m: (b, m, 0))
    return pl.pallas_call(
        _merge_kernel,
        grid=(B, L // tm),
        in_specs=[pl.BlockSpec((1, tm, D_MODEL), lambda b, m: (b, m, 0)),
                  yspec, yspec, yspec,
                  pl.BlockSpec((1, tm, 3 * D_MODEL), lambda b, m: (b, m, CB_BG)),
                  pl.BlockSpec((1, 8, D_MODEL), mod_map),
                  pl.BlockSpec((3, BRANCH_W, D_MODEL), lambda b, m: (0, 0, 0)),
                  pl.BlockSpec((D_MODEL, D_MODEL), lambda b, m: (0, 0))],
        out_specs=pl.BlockSpec((1, tm, D_MODEL), lambda b, m: (b, m, 0)),
        out_shape=jax.ShapeDtypeStruct((B, L, D_MODEL), f32),
        compiler_params=_cparams(("arbitrary", "arbitrary")),
        name="merge_out",
    )(x, y_ml, y_hg, y_at, proj, mod, wb, wo)


FFN_SPLITS = ((0, 1536), (1536, FFN_HIDDEN))


def _ffn_kernel(x_ref, mod_ref, g_ref, wi_ref, wd_ref, fg_ref, o_ref, *, final_norm):
    x = x_ref[0]
    hb = (_rms_rows(x, g_ref[...]) * (1.0 + mod_ref[0, 4:5, :]) + mod_ref[0, 3:4, :]).astype(bf16)
    acc = None
    for lo, hi in FFN_SPLITS:
        gate = _dot(hb, wi_ref[:, lo:hi])
        up = _dot(hb, wi_ref[:, FFN_HIDDEN + lo:FFN_HIDDEN + hi])
        part = _dot((_silu(gate) * up).astype(bf16), wd_ref[lo:hi, :])
        acc = part if acc is None else acc + part
    y = x + mod_ref[0, 5:6, :] * acc
    if final_norm:
        y = _rms_rows(y, fg_ref[...])
    o_ref[0] = y


def ffn_call(x, mod, g_row, w_in, w_out, fg_row, shared_mod, final_norm):
    B, L, _ = x.shape
    tm = min(L, 512)
    mod_map = (lambda b, m: (0, 0, 0)) if shared_mod else (lambda b, m: (b, 0, 0))
    kern = functools.partial(_ffn_kernel, final_norm=final_norm)
    resident = pl.Buffered(1)
    return pl.pallas_call(
        kern,
        grid=(B, L // tm),
        in_specs=[pl.BlockSpec((1, tm, D_MODEL), lambda b, m: (b, m, 0)),
                  pl.BlockSpec((1, 8, D_MODEL), mod_map),
                  pl.BlockSpec((1, D_MODEL), lambda b, m: (0, 0)),
                  pl.BlockSpec((D_MODEL, 2 * FFN_HIDDEN), lambda b, m: (0, 0), pipeline_mode=resident),
                  pl.BlockSpec((FFN_HIDDEN, D_MODEL), lambda b, m: (0, 0), pipeline_mode=resident),
                  pl.BlockSpec((1, D_MODEL), lambda b, m: (0, 0))],
        out_specs=pl.BlockSpec((1, tm, D_MODEL), lambda b, m: (b, m, 0)),
        out_shape=jax.ShapeDtypeStruct((B, L, D_MODEL), f32),
        compiler_params=_cparams(("arbitrary", "arbitrary")),
        name="ffn",
    )(x, mod, g_row, w_in, w_out, fg_row)


def _reorder_w_in(w):
    o_mg, o_hq, o_hf, o_hi, o_bg, o_end = 2048, 2064, 2576, 3600, 5392, 8464
    d = w.shape[0]
    sec16 = [w[:, o_bg:o_end], w[:, :o_mg], w[:, o_hq:o_hf], w[:, o_hi:o_bg]]
    n16 = (o_end - o_bg) + o_mg + (o_hf - o_hq) + (o_bg - o_hi)
    sec32 = [w[:, o_hf:o_hi], w[:, o_mg:o_hq]]
    n32 = (o_hi - o_hf) + (o_hq - o_mg)
    return jnp.concatenate(sec16 + [jnp.zeros((d, N16_COLS - n16), w.dtype)] + sec32
                           + [jnp.zeros((d, N32_COLS - n32), w.dtype)], axis=1).astype(bf16)


def _rope_tables(L):
    rows = L // GRID_W
    row = jnp.repeat(jnp.arange(rows, dtype=f32), GRID_W)
    colp = jnp.tile(jnp.arange(GRID_W, dtype=f32), rows)
    n_freq = AT_HD // 4
    inv = ROPE_THETA ** (-jnp.arange(n_freq, dtype=f32) / n_freq)
    ang = jnp.concatenate([row[:, None] * inv, colp[:, None] * inv], axis=-1)
    cos = jnp.repeat(jnp.cos(ang), 2, axis=-1)
    sin = jnp.repeat(jnp.sin(ang), 2, axis=-1)
    sign = jnp.tile(jnp.array([-1.0, 1.0], f32), AT_HD // 2)
    sin = sin * sign
    return jnp.tile(cos, (1, 2)), jnp.tile(sin, (1, 2))


def _mod_rows(ada_rows):
    r = ada_rows.shape[0]
    m = ada_rows.reshape(r, 6, D_MODEL)
    return jnp.concatenate([m, jnp.zeros((r, 2, D_MODEL), f32)], axis=1)


def kernel(x_prompt, x_sample, c, cache_k, cache_v, state_ml_C, state_ml_n, state_ml_m, state_hg_S, c_ctx, w_ada, b_ada, norm1_g, norm2_g, w_in, ml_gate_b, ml_norm_g, hg_lb_logits, hg_norm_g, q_norm_g, k_norm_g, w_branch, w_out, w_ffn_in, w_ffn_out, final_g):
    Bp, Lp, _ = x_prompt.shape
    Bs, Ls, _ = x_sample.shape

    n_rows = 16
    cvec = jnp.concatenate([c_ctx[None, :], c, jnp.zeros((n_rows - 1 - Bs, D_MODEL), f32)], axis=0)
    ada = ada_call(cvec, w_ada, b_ada)
    lb_all = lb_call(hg_lb_logits)
    cos_t, sin_t = _rope_tables(Ls)

    zeros_c = jnp.zeros((Bp, 1, 2, N_HEADS, D_HEAD, D_HEAD), f32)
    zeros_n = jnp.zeros((Bp, 1, 2, N_HEADS, 1, D_HEAD), f32)
    zeros_m = jnp.zeros((Bp * 2 * N_HEADS,), f32)
    n_state = state_ml_n.reshape(Bs, DEPTH, 2, N_HEADS, 1, D_HEAD)

    xp, xs = x_prompt.reshape(1, Bp * Lp, D_MODEL), x_sample
    nk, nv, nC, nn_, nm, nS = [], [], [], [], [], []
    for l in range(DEPTH):
        w_l = _reorder_w_in(w_in[l])
        wb_l = w_branch[l].astype(bf16)
        wo_l = w_out[l].astype(bf16)
        wfi_l = w_ffn_in[l].astype(bf16)
        wfo_l = w_ffn_out[l].astype(bf16)
        g1 = norm1_g[l][None, :]
        g2 = norm2_g[l][None, :]
        mlg = ml_norm_g[l][None, :]
        hgg = hg_norm_g[l][None, :]
        qg = jnp.tile(q_norm_g[l], 2)[None, :]
        kg = jnp.tile(k_norm_g[l], 2)[None, :]
        gate_b = ml_gate_b[l].reshape(-1).astype(f32)
        fg = final_g[None, :]
        last = l == DEPTH - 1

        mod_ctx = _mod_rows(ada[l, 0:1])
        mod_lat = _mod_rows(ada[l, 1:1 + Bs])

        proj_flat, proj32 = in_proj_call(xp, mod_ctx, g1, w_l, True)
        proj = proj_flat.reshape(Bp, Lp, N16_COLS)
        proj32 = proj32.reshape(Bp, Lp, N32_COLS)
        y_ml, c_f, n_f, m_f = mlstm_call(proj, proj32, gate_b, mlg, zeros_c, zeros_n, zeros_m, 0)
        y_hg, s_f = hgrn_call(proj, proj32, lb_all[l], hgg, zeros_c, 0)
        q_h, k_h, v_h, k_n = qkv_prep_call(proj, qg, kg, cos_t[:Lp], sin_t[:Lp], False)
        y_at = attn_call(q_h, k_h, v_h)
        flat = (1, Bp * Lp, BRANCH_W)
        xp = merge_call(xp, y_ml.reshape(flat), y_hg.reshape(flat), y_at.reshape(flat), proj_flat,
                        mod_ctx, wb_l, wo_l, True)
        xp = ffn_call(xp, mod_ctx, g2, wfi_l, wfo_l, fg, True, last)
        nk.append(k_n.reshape(Bp, Lp, AT_KV_HEADS, AT_HD))
        nv.append(proj[:, :, CB_AV * LANES:(CB_AV + 1) * LANES].astype(f32).reshape(Bp, Lp, AT_KV_HEADS, AT_HD))
        nC.append(c_f)
        nn_.append(n_f[:, :, :, 0, :])
        nm.append(m_f[:, :, :, 0, 0])
        nS.append(s_f)

        proj, proj32 = in_proj_call(xs, mod_lat, g1, w_l, False)
        y_ml, _, _, _ = mlstm_call(proj, proj32, gate_b, mlg, state_ml_C, n_state,
                                   state_ml_m[:, l].reshape(-1).astype(f32), l)
        y_hg, _ = hgrn_call(proj, proj32, lb_all[l], hgg, state_hg_S, l)
        q_h, k_h, v_h, _ = qkv_prep_call(proj, qg, kg, cos_t, sin_t, True)
        k_c, v_c = _cache_layouts(cache_k[:, l], cache_v[:, l])
        y_at = attn_call(q_h, k_h, v_h, k_c, v_c)
        xs = merge_call(xs, y_ml, y_hg, y_at, proj, mod_lat, wb_l, wo_l, False)
        xs = ffn_call(xs, mod_lat, g2, wfi_l, wfo_l, fg, False, last)

    return (xp.reshape(Bp, Lp, D_MODEL), xs, jnp.stack(nk, axis=1), jnp.stack(nv, axis=1), jnp.stack(nC, axis=1),
            jnp.stack(nn_, axis=1), jnp.stack(nm, axis=1), jnp.stack(nS, axis=1))
```

```python
import functools
import math

import jax
import jax.numpy as jnp
from jax import lax
from jax.experimental import pallas as pl
from jax.experimental.pallas import tpu as pltpu

f32 = jnp.float32
bf16 = jnp.bfloat16

D_MODEL = 1024
DEPTH = 4
N_HEADS = 4
D_HEAD = 128
AT_HEADS = 8
AT_KV_HEADS = 2
AT_GROUP = AT_HEADS // AT_KV_HEADS
AT_HD = 64
BRANCH_W = 512
FFN_HIDDEN = 2816
GRID_W = 64
ROPE_THETA = 10000.0
EPS = 1e-6

LANES = 128
SUBLANES = 8
VMEM_LIMIT = 52 * 1024 * 1024

CB_BG = 0
CB_MQ = 24
CB_MK = 28
CB_MV = 32
CB_MO = 36
CB_HQ = 40
CB_HI = 44
CB_HG = 48
CB_AQ = 52
CB_AK = 56
CB_AV = 57
N16_BLOCKS = 60
N16_COLS = N16_BLOCKS * LANES
CF_HF = 0
CF_MG = 8
N32_BLOCKS = 10
N32_COLS = N32_BLOCKS * LANES
PROJ_TN = N32_COLS
N16_TILES = N16_COLS // PROJ_TN

ML_CHUNK = 128
ML_EXT = 144
HG_CHUNK = 64
HG_SUB = 8
HG_HALF = 4
HG_LEVELS = (32, 16, 8, 4)
NEG_BIG = -1e30
LOG2E = 1.4426950408889634


def _cparams(sem):
    return pltpu.CompilerParams(dimension_semantics=sem, vmem_limit_bytes=VMEM_LIMIT)


def _dot(a, b):
    return jnp.dot(a, b, preferred_element_type=f32)


def _dot_nt(a, b):
    return lax.dot_general(a, b, (((1,), (1,)), ((), ())), preferred_element_type=f32)


def _dot_tn(a, b):
    return lax.dot_general(a, b, (((0,), (0,)), ((), ())), preferred_element_type=f32)


def _sigmoid(x):
    return 1.0 / (1.0 + jnp.exp(-x))


def _silu(x):
    return x * _sigmoid(x)


def _log_sigmoid(x):
    return jnp.minimum(x, 0.0) - jnp.log1p(jnp.exp(-jnp.abs(x)))


def _split3(x):
    hi = x.astype(bf16)
    r1 = x - hi.astype(f32)
    mid = r1.astype(bf16)
    lo = (r1 - mid.astype(f32)).astype(bf16)
    return hi, mid, lo


def _rms_rows(x, g_row):
    ms = jnp.mean(x * x, axis=-1, keepdims=True)
    return x * lax.rsqrt(ms + EPS) * g_row


def _ada_kernel(c_ref, w_ref, b_ref, o_ref):
    s = _silu(c_ref[...]).astype(bf16)
    o_ref[0] = _dot(s, w_ref[0].astype(bf16)) + b_ref[0]


def ada_call(cvec, w_ada, b_ada):
    rows = cvec.shape[0]
    tn = 1536
    n6 = 6 * D_MODEL
    return pl.pallas_call(
        _ada_kernel,
        grid=(DEPTH, n6 // tn),
        in_specs=[pl.BlockSpec((rows, D_MODEL), lambda l, n: (0, 0)),
                  pl.BlockSpec((1, D_MODEL, tn), lambda l, n: (l, 0, n)),
                  pl.BlockSpec((1, 1, tn), lambda l, n: (l, 0, n))],
        out_specs=pl.BlockSpec((1, rows, tn), lambda l, n: (l, 0, n)),
        out_shape=jax.ShapeDtypeStruct((DEPTH, rows, n6), f32),
        compiler_params=_cparams(("arbitrary", "arbitrary")),
        name="ada",
    )(cvec, w_ada, b_ada.reshape(DEPTH, 1, n6))


def _lb_kernel(x_ref, o_ref):
    xs = [x_ref[l] for l in range(DEPTH)]
    mx = xs[0]
    for l in range(1, DEPTH):
        mx = jnp.maximum(mx, xs[l])
    es = [jnp.exp(x - mx) for x in xs]
    tot = es[0]
    for l in range(1, DEPTH):
        tot = tot + es[l]
    sm = [e / tot for e in es]
    run = sm[0]
    o_ref[0] = run - sm[0]
    for l in range(1, DEPTH):
        run = run + sm[l]
        o_ref[l] = run - sm[0]


def lb_call(logits):
    x = logits.astype(f32).reshape(DEPTH, 2 * N_HEADS, D_HEAD)
    return pl.pallas_call(
        _lb_kernel,
        out_shape=jax.ShapeDtypeStruct((DEPTH, 2 * N_HEADS, D_HEAD), f32),
        name="hg_lower_bounds",
    )(x)


def _in_proj_kernel(x_ref, mod_ref, g_ref, w_ref, o16_ref, o32_ref, h_s):
    n = pl.program_id(2)

    @pl.when(n == 0)
    def _():
        x = x_ref[0]
        sh = mod_ref[0, 0:1, :]
        sc = mod_ref[0, 1:2, :]
        h = _rms_rows(x, g_ref[...]) * (1.0 + sc) + sh
        h_s[...] = h.astype(bf16)

    @pl.when(n < N16_TILES)
    def _():
        o16_ref[0] = _dot(h_s[...], w_ref[...]).astype(bf16)

    @pl.when(n == N16_TILES)
    def _():
        o32_ref[0] = _dot(h_s[...], w_ref[...])


def in_proj_call(x, mod, g_row, w, shared_mod):
    B, L, _ = x.shape
    tm = min(L, 1024)
    tn = PROJ_TN
    mod_map = (lambda b, m, n: (0, 0, 0)) if shared_mod else (lambda b, m, n: (b, 0, 0))
    return pl.pallas_call(
        _in_proj_kernel,
        grid=(B, L // tm, N16_TILES + 1),
        in_specs=[pl.BlockSpec((1, tm, D_MODEL), lambda b, m, n: (b, m, 0)),
                  pl.BlockSpec((1, 8, D_MODEL), mod_map),
                  pl.BlockSpec((1, D_MODEL), lambda b, m, n: (0, 0)),
                  pl.BlockSpec((D_MODEL, tn), lambda b, m, n: (0, n))],
        out_specs=[pl.BlockSpec((1, tm, tn), lambda b, m, n: (b, m, jnp.minimum(n, N16_TILES - 1))),
                   pl.BlockSpec((1, tm, tn), lambda b, m, n: (b, m, 0))],
        out_shape=[jax.ShapeDtypeStruct((B, L, N16_COLS), bf16),
                   jax.ShapeDtypeStruct((B, L, N32_COLS), f32)],
        scratch_shapes=[pltpu.VMEM((tm, D_MODEL), bf16)],
        compiler_params=_cparams(("arbitrary", "arbitrary", "arbitrary")),
        name="in_proj",
    )(x, mod, g_row, w)


def _mlstm_kernel(bias_ref, m0_ref, q_ref, k_ref, v_ref, mo_ref, g_ref, ng_ref, c0_ref, n0_ref,
                  y_ref, cout_ref, nout_ref, mout_ref,
                  gt_s, rw_s, ucb_s, vt_s, ht_s, ct_s, m_s, *, seq_len):
    T = ML_CHUNK
    assert T == LANES
    nc = seq_len // T
    b_idx = pl.program_id(0)
    h = pl.program_id(1)
    scale = D_HEAD ** -0.5
    ext_rows = jnp.concatenate([jnp.ones((1, T), f32), jnp.zeros((ML_EXT - D_HEAD - 1, T), f32)], axis=0)

    def tr_body(j, carry):
        r0 = pl.multiple_of(j * T, T)
        gt_s[:, pl.ds(r0, T)] = g_ref[0, pl.ds(r0, T), :].T
        vt = v_ref[0, pl.ds(r0, T), :].astype(f32).T
        vt_s[:, pl.ds(r0, T)] = jnp.concatenate([vt, ext_rows], axis=0).astype(bf16)
        return carry
    lax.fori_loop(0, nc, tr_body, 0, unroll=min(16, nc))

    i_f = gt_s[pl.ds(h, 1), :] + bias_ref[h]
    f_f = _log_sigmoid(gt_s[pl.ds(4 + h, 1), :] + bias_ref[4 + h])
    i_b = gt_s[pl.ds(8 + h, 1), :] + bias_ref[8 + h]
    f_b = _log_sigmoid(gt_s[pl.ds(12 + h, 1), :] + bias_ref[12 + h])
    zrow = jnp.zeros_like(i_f)
    rw_s[...] = jnp.concatenate([f_f, i_f, f_b, i_b, zrow, zrow, zrow, zrow], axis=0)

    ui = lax.broadcasted_iota(jnp.int32, (T, T), 0)
    si = lax.broadcasted_iota(jnp.int32, (T, T), 1)
    tri = jnp.concatenate([(ui <= si).astype(bf16), (ui >= si).astype(bf16)], axis=1)
    valid_f = ui <= si
    valid_b = ui >= si

    def cs_body(j, carry):
        r0 = pl.multiple_of(j * T, T)
        rows = rw_s[:, pl.ds(r0, T)]
        hi, mid, lo = _split3(rows)
        cs3 = _dot(jnp.concatenate([hi, mid, lo, jnp.zeros_like(hi)], axis=0), tri)
        cs = cs3[0:8] + cs3[8:16] + cs3[16:24]
        b_f = cs[0:1, 0:T]
        b_b = cs[2:3, T:2 * T]
        rw_s[:, pl.ds(r0, T)] = jnp.concatenate([b_f, rows[1:2], b_b, rows[3:8]], axis=0)
        ucb_s[0, pl.ds(r0, T), :] = jnp.broadcast_to(rows[1:2] - b_f, (T, T)).T
        ucb_s[1, pl.ds(r0, T), :] = jnp.broadcast_to(rows[3:4] - b_b, (T, T)).T
        return carry
    lax.fori_loop(0, nc, cs_body, 0, unroll=min(16, nc))

    for d in range(2):
        ct_s[d] = jnp.concatenate([c0_ref[0, 0, d, 0].T, n0_ref[0, 0, d, 0],
                                   jnp.zeros((ML_EXT - D_HEAD - 1, D_HEAD), f32)], axis=0)
        m_s[d] = jnp.full((1, LANES), m0_ref[b_idx * 2 * N_HEADS + d * N_HEADS + h], f32)

    def step(d, j):
        r0 = pl.multiple_of(j * T, T)
        b_row = rw_s[pl.ds(2 * d, 1), pl.ds(r0, T)]
        b_end = b_row[:, T - 1:T] if d == 0 else b_row[:, 0:1]
        b_end_l = jnp.broadcast_to(b_end, (1, LANES))
        ucb = ucb_s[d, pl.ds(r0, T), :]

        q = q_ref[0, pl.ds(r0, T), :].astype(f32) * scale
        kb = k_ref[0, pl.ds(r0, T), :]
        k = kb.astype(f32)
        qb = q.astype(bf16)
        vt = vt_s[:, pl.ds(r0, T)]
        ct = ct_s[d]
        m_prev = m_s[d]

        d_t = jnp.where(valid_f if d == 0 else valid_b, ucb + b_row, NEG_BIG)
        m_state = b_row + m_prev
        m_t = jnp.maximum(m_state, jnp.max(d_t, axis=0, keepdims=True))
        w_state = jnp.exp(m_state - m_t)
        s_t = _dot_nt(kb, qb) * jnp.exp(d_t - m_t)
        tot = _dot(vt, s_t.astype(bf16)) + w_state * _dot_nt(ct.astype(bf16), qb)
        den = tot[D_HEAD:D_HEAD + 1]
        ht_s[d, :, pl.ds(r0, T)] = tot[:D_HEAD] * (1.0 / jnp.maximum(jnp.abs(den), jnp.exp(-m_t)))

        gcb = ucb + b_end_l
        m_new = jnp.maximum(b_end_l + m_prev, jnp.max(gcb, axis=0, keepdims=True))
        kw = k * jnp.exp(gcb - m_new)
        decay = jnp.exp(b_end_l + m_prev - m_new)
        ct_s[d] = decay * ct + _dot(vt, kw.astype(bf16))
        m_s[d] = m_new

    def loop_body(j, carry):
        step(0, j)
        step(1, nc - 1 - j)
        return carry
    lax.fori_loop(0, nc, loop_body, 0, unroll=min(16, nc))

    ng = ng_ref[...]

    def out_body(j, carry):
        r0 = pl.multiple_of(j * T, T)
        y_t = ht_s[0, :, pl.ds(r0, T)] + ht_s[1, :, pl.ds(r0, T)]
        ms = jnp.mean(y_t * y_t, axis=0, keepdims=True)
        y = (y_t * lax.rsqrt(ms + EPS)).T * ng * _sigmoid(mo_ref[0, pl.ds(r0, T), :].astype(f32))
        y_ref[0, pl.ds(r0, T), :] = y.astype(y_ref.dtype)
        return carry
    lax.fori_loop(0, nc, out_body, 0, unroll=min(8, nc))

    for d in range(2):
        cout_ref[0, d, 0] = ct_s[d, 0:D_HEAD, :].T
        nout_ref[0, d, 0] = ct_s[d, D_HEAD:D_HEAD + 1, :]
        mout_ref[0, d, 0] = m_s[d]


def mlstm_call(proj, proj32, gate_bias, norm_g, c0, n0, m0, layer):
    B, L, _ = proj.shape

    def col(cb):
        return pl.BlockSpec((1, L, LANES), lambda b, h, cb=cb: (b, 0, cb + h))

    smem = pl.BlockSpec(memory_space=pltpu.SMEM)
    kern = functools.partial(_mlstm_kernel, seq_len=L)
    return pl.pallas_call(
        kern,
        grid=(B, N_HEADS),
        in_specs=[smem, smem,
                  col(CB_MQ), col(CB_MK), col(CB_MV), col(CB_MO),
                  pl.BlockSpec((1, L, LANES), lambda b, h: (b, 0, CF_MG)),
                  pl.BlockSpec((1, LANES), lambda b, h: (0, h)),
                  pl.BlockSpec((1, 1, 2, 1, D_HEAD, D_HEAD), lambda b, h: (b, layer, 0, h, 0, 0)),
                  pl.BlockSpec((1, 1, 2, 1, 1, D_HEAD), lambda b, h: (b, layer, 0, h, 0, 0))],
        out_specs=[pl.BlockSpec((1, L, LANES), lambda b, h: (b, 0, h)),
                   pl.BlockSpec((1, 2, 1, D_HEAD, D_HEAD), lambda b, h: (b, 0, h, 0, 0)),
                   pl.BlockSpec((1, 2, 1, 1, D_HEAD), lambda b, h: (b, 0, h, 0, 0)),
                   pl.BlockSpec((1, 2, 1, 1, LANES), lambda b, h: (b, 0, h, 0, 0))],
        out_shape=[jax.ShapeDtypeStruct((B, L, BRANCH_W), bf16),
                   jax.ShapeDtypeStruct((B, 2, N_HEADS, D_HEAD, D_HEAD), f32),
                   jax.ShapeDtypeStruct((B, 2, N_HEADS, 1, D_HEAD), f32),
                   jax.ShapeDtypeStruct((B, 2, N_HEADS, 1, LANES), f32)],
        scratch_shapes=[pltpu.VMEM((LANES, L), f32),
                        pltpu.VMEM((8, L), f32),
                        pltpu.VMEM((2, L, LANES), f32),
                        pltpu.VMEM((ML_EXT, L), bf16),
                        pltpu.VMEM((2, D_HEAD, L), f32),
                        pltpu.VMEM((2, ML_EXT, D_HEAD), f32),
                        pltpu.VMEM((2, 1, LANES), f32)],
        compiler_params=_cparams(("arbitrary", "arbitrary")),
        name="mlstm",
    )(gate_bias, m0, proj, proj, proj, proj, proj32, norm_g, c0, n0)


def _hgrn_kernel(q_ref, f0_ref, f1_ref, i_ref, og_ref, lb_ref, ng_ref, s0_ref,
                 y_ref, sout_ref, o_s, st_s, qx_s, dec_s, u_s, vt_s, k_s, b_s, q32_s, v32_s, *, seq_len):
    T = HG_CHUNK
    nc = seq_len // T
    h = pl.program_id(1)
    f_refs = (f0_ref, f1_ref)

    lbs, loglb, log1mlb = [], [], []
    for d in range(2):
        lb = lb_ref[pl.ds(d * N_HEADS + h, 1), :]
        lbs.append(lb)
        loglb.append(jnp.log(lb))
        log1mlb.append(jnp.log1p(-lb))
        st_s[d] = s0_ref[0, 0, d, 0].T

    ti = lax.broadcasted_iota(jnp.int32, (T, T), 0)
    ui = lax.broadcasted_iota(jnp.int32, (T, T), 1)
    tri = ((ui <= ti).astype(bf16), (ui >= ti).astype(bf16))
    sub_t = lax.broadcasted_iota(jnp.int32, (HG_SUB, D_HEAD), 0)
    sub_q = sub_t % HG_HALF
    low_half = sub_t < HG_HALF
    row_i = lax.broadcasted_iota(jnp.int32, (T, D_HEAD), 0)
    tt_r = lax.broadcasted_iota(jnp.int32, (T, T), 0)
    tt_c = lax.broadcasted_iota(jnp.int32, (T, T), 1)
    q_rows = {(hh, dd): ((row_i % (2 * hh)) >= hh) if dd == 0 else ((row_i % (2 * hh)) < hh)
              for hh in HG_LEVELS for dd in range(2)}
    same_blk = {hh: (tt_r // (2 * hh)) == (tt_c // (2 * hh)) for hh in HG_LEVELS}

    def gates(d, j, q):
        r0 = pl.multiple_of(j * T, T)
        hf = f_refs[d][0, pl.ds(r0, T), :]
        e = jnp.exp(-jnp.abs(hf))
        sig_neg = jnp.where(hf >= 0.0, e, 1.0) / (1.0 + e)
        logsig = jnp.minimum(hf, 0.0) - jnp.log(1.0 + e)
        cterm = log1mlb[d] + logsig
        amax = jnp.maximum(loglb[d], cterm)
        logf = amax + jnp.log(1.0 + jnp.exp(-jnp.abs(loglb[d] - cterm)))
        k = (1.0 - lbs[d]) * sig_neg
        hi, mid, lo = _split3(logf * LOG2E)
        b = _dot(tri[d], hi) + _dot(tri[d], mid) + _dot(tri[d], lo)
        b_end = b[T - 1:T] if d == 0 else b[0:1]
        k_s[d, pl.ds(r0, T), :] = k
        b_s[d, pl.ds(r0, T), :] = b
        qx_s[d, pl.ds(r0, T), :] = (q * jnp.exp2(b)).astype(bf16)
        dec_s[d, pl.ds(j, 1), :] = jnp.exp2(b_end)

    def gates_body(j, carry):
        r0 = pl.multiple_of(j * T, T)
        q = q_ref[0, pl.ds(r0, T), :].astype(f32)
        v = i_ref[0, pl.ds(r0, T), :].astype(f32)
        q32_s[pl.ds(r0, T), :] = q
        v32_s[pl.ds(r0, T), :] = v
        vt_s[j] = v.T.astype(bf16)
        gates(0, j, q)
        gates(1, j, q)
        return carry
    lax.fori_loop(0, nc, gates_body, 0, unroll=min(16, nc))

    def intra(d, j):
        r0 = pl.multiple_of(j * T, T)

        def rows(ref2, lo_r, n):
            return ref2[pl.ds(r0 + lo_r, n), :]

        def srows(ref3, lo_r, n):
            return ref3[d, pl.ds(r0 + lo_r, n), :]

        nsub = T // HG_SUB
        o_blk = [None] * nsub

        def add(idx, val):
            o_blk[idx] = val if o_blk[idx] is None else o_blk[idx] + val

        def levels():
            b_all = srows(b_s, 0, T)
            q_all = rows(q32_s, 0, T)
            k_all = srows(k_s, 0, T)
            a_tot = None
            for h in HG_LEVELS:
                blk = 2 * h
                ref_off = h - 1 if d == 0 else h
                ref = jnp.concatenate([jnp.broadcast_to(srows(b_s, base + ref_off, 1), (blk, D_HEAD))
                                       for base in range(0, T, blk)], axis=0)
                is_q = q_rows[(h, d)]
                qx = q_all * jnp.exp2(jnp.where(is_q, b_all - ref, NEG_BIG))
                kx = k_all * jnp.exp2(jnp.where(is_q, NEG_BIG, ref - b_all))
                a = _dot_nt(qx.astype(bf16), kx.astype(bf16))
                if blk < T:
                    a = jnp.where(same_blk[h], a, 0.0)
                a_tot = a if a_tot is None else a_tot + a
            ov = _dot(a_tot.astype(bf16), rows(v32_s, 0, T).astype(bf16))
            for p in range(nsub):
                add(p, ov[p * HG_SUB:(p + 1) * HG_SUB])

        def diag(lo_r):
            bs = srows(b_s, lo_r, HG_SUB)
            qs = rows(q32_s, lo_r, HG_SUB)

            def keyrow(read, s):
                return jnp.where(low_half, read(lo_r + s, 1), read(lo_r + HG_HALF + s, 1))

            acc = None
            for s in range(HG_HALF):
                valid = (sub_q >= s) if d == 0 else (sub_q <= s)
                b_key = keyrow(lambda r, n: srows(b_s, r, n), s)
                ee = jnp.exp2(jnp.where(valid, bs - b_key, NEG_BIG))
                a_col = jnp.sum(qs * ee * keyrow(lambda r, n: srows(k_s, r, n), s), axis=-1, keepdims=True)
                term = a_col * keyrow(lambda r, n: rows(v32_s, r, n), s)
                acc = term if acc is None else acc + term
            add(lo_r // HG_SUB, acc)

        levels()
        for i in range(nsub):
            diag(i * HG_SUB)
        o_s[d, pl.ds(r0, T), :] = jnp.concatenate(o_blk, axis=0)

    def intra_body(j, carry):
        intra(0, j)
        intra(1, j)
        return carry
    lax.fori_loop(0, nc, intra_body, 0, unroll=min(16, nc))

    def incr_body(j, carry):
        r0 = pl.multiple_of(j * T, T)
        vt = vt_s[j]
        for d in range(2):
            b = b_s[d, pl.ds(r0, T), :]
            b_end = b[T - 1:T] if d == 0 else b[0:1]
            u_s[d, j] = _dot(vt, (k_s[d, pl.ds(r0, T), :] * jnp.exp2(b_end - b)).astype(bf16))
        return carry
    lax.fori_loop(0, nc, incr_body, 0, unroll=min(16, nc))

    def state_step(d, j):
        r0 = pl.multiple_of(j * T, T)
        st = st_s[d]
        o_s[d, pl.ds(r0, T), :] = o_s[d, pl.ds(r0, T), :] + _dot_nt(qx_s[d, pl.ds(r0, T), :], st.astype(bf16))
        st_s[d] = st * dec_s[d, pl.ds(j, 1), :] + u_s[d, j]

    def state_body(j, carry):
        state_step(0, j)
        state_step(1, nc - 1 - j)
        return carry
    lax.fori_loop(0, nc, state_body, 0, unroll=min(16, nc // 2))

    ng = ng_ref[...]

    def out_body(j, carry):
        r0 = pl.multiple_of(j * T, T)
        y = o_s[0, pl.ds(r0, T), :] + o_s[1, pl.ds(r0, T), :]
        y = _rms_rows(y, ng) * _silu(og_ref[0, pl.ds(r0, T), :].astype(f32))
        y_ref[0, pl.ds(r0, T), :] = y.astype(y_ref.dtype)
        return carry
    lax.fori_loop(0, nc, out_body, 0, unroll=min(8, nc))

    for d in range(2):
        sout_ref[0, d, 0] = st_s[d].T


def hgrn_call(proj, proj32, lb_rows, norm_g, s0, layer):
    B, L, _ = proj.shape

    def col(cb):
        return pl.BlockSpec((1, L, LANES), lambda b, h, cb=cb: (b, 0, cb + h))

    kern = functools.partial(_hgrn_kernel, seq_len=L)
    return pl.pallas_call(
        kern,
        grid=(B, N_HEADS),
        in_specs=[col(CB_HQ), col(CF_HF), col(CF_HF + N_HEADS), col(CB_HI), col(CB_HG),
                  pl.BlockSpec((2 * N_HEADS, D_HEAD), lambda b, h: (0, 0)),
                  pl.BlockSpec((1, LANES), lambda b, h: (0, h)),
                  pl.BlockSpec((1, 1, 2, 1, D_HEAD, D_HEAD), lambda b, h: (b, layer, 0, h, 0, 0))],
        out_specs=[pl.BlockSpec((1, L, LANES), lambda b, h: (b, 0, h)),
                   pl.BlockSpec((1, 2, 1, D_HEAD, D_HEAD), lambda b, h: (b, 0, h, 0, 0))],
        out_shape=[jax.ShapeDtypeStruct((B, L, BRANCH_W), bf16),
                   jax.ShapeDtypeStruct((B, 2, N_HEADS, D_HEAD, D_HEAD), f32)],
        scratch_shapes=[pltpu.VMEM((2, L, D_HEAD), f32),
                        pltpu.VMEM((2, D_HEAD, D_HEAD), f32),
                        pltpu.VMEM((2, L, D_HEAD), bf16),
                        pltpu.VMEM((2, L // HG_CHUNK, D_HEAD), f32),
                        pltpu.VMEM((2, L // HG_CHUNK, D_HEAD, D_HEAD), f32),
                        pltpu.VMEM((L // HG_CHUNK, D_HEAD, HG_CHUNK), bf16),
                        pltpu.VMEM((2, L, D_HEAD), f32),
                        pltpu.VMEM((2, L, D_HEAD), f32),
                        pltpu.VMEM((L, D_HEAD), f32),
                        pltpu.VMEM((L, D_HEAD), f32)],
        compiler_params=_cparams(("arbitrary", "arbitrary")),
        name="hgrn",
    )(proj, proj32, proj32, proj, proj, lb_rows, norm_g, s0)


def _pair_swap(x):
    lane = lax.broadcasted_iota(jnp.int32, x.shape, 1)
    return jnp.where((lane % 2) == 0, pltpu.roll(x, LANES - 1, 1), pltpu.roll(x, 1, 1))


def _head_norm_pair(x, g_row):
    gi = lax.broadcasted_iota(jnp.int32, (LANES, LANES), 0) // AT_HD
    gj = lax.broadcasted_iota(jnp.int32, (LANES, LANES), 1) // AT_HD
    same_head = (gi == gj).astype(bf16)
    sq = x * x
    hi = sq.astype(bf16)
    lo = (sq - hi.astype(f32)).astype(bf16)
    ms = (_dot(hi, same_head) + _dot(lo, same_head)) * (1.0 / AT_HD)
    return x * lax.rsqrt(ms + EPS) * g_row


def _qkv_prep_kernel(aq_ref, ak_ref, av_ref, qg_ref, kg_ref, cos_ref, sin_ref,
                     q_ref, k_ref, v_ref, kn_ref, *, use_rope):
    qg = qg_ref[...]
    kg = kg_ref[...]
    if use_rope:
        cos = cos_ref[...]
        sin = sin_ref[...]
    for p in range(AT_HEADS // 2):
        x = _head_norm_pair(aq_ref[0, :, p * LANES:(p + 1) * LANES].astype(f32), qg)
        if use_rope:
            x = x * cos + _pair_swap(x) * sin
        x = (x * (AT_HD ** -0.5)).astype(bf16)
        q_ref[0, 2 * p] = x[:, :AT_HD]
        q_ref[0, 2 * p + 1] = x[:, AT_HD:]
    kn = _head_norm_pair(ak_ref[0].astype(f32), kg)
    kn_ref[0] = kn
    if use_rope:
        kn = kn * cos + _pair_swap(kn) * sin
    kb = kn.astype(bf16)
    for g in range(AT_KV_HEADS):
        k_ref[0, g] = kb[:, g * AT_HD:(g + 1) * AT_HD]
    v = av_ref[0].astype(f32)
    lane = lax.broadcasted_iota(jnp.int32, v.shape, 1)
    tail = jnp.where(lane == AT_HD, 1.0, 0.0)
    v_ref[0, 0] = jnp.where(lane < AT_HD, v, tail).astype(bf16)
    v_ref[0, 1] = jnp.where(lane < AT_HD, pltpu.roll(v, AT_HD, 1), tail).astype(bf16)


def qkv_prep_call(proj, qg_row, kg_row, cos_t, sin_t, use_rope):
    B, L, _ = proj.shape
    tm = min(L, 512)
    kern = functools.partial(_qkv_prep_kernel, use_rope=use_rope)
    return pl.pallas_call(
        kern,
        grid=(B, L // tm),
        in_specs=[pl.BlockSpec((1, tm, 4 * LANES), lambda b, m: (b, m, CB_AQ // 4)),
                  pl.BlockSpec((1, tm, LANES), lambda b, m: (b, m, CB_AK)),
                  pl.BlockSpec((1, tm, LANES), lambda b, m: (b, m, CB_AV)),
                  pl.BlockSpec((1, LANES), lambda b, m: (0, 0)),
                  pl.BlockSpec((1, LANES), lambda b, m: (0, 0)),
                  pl.BlockSpec((tm, LANES), lambda b, m: (m, 0)),
                  pl.BlockSpec((tm, LANES), lambda b, m: (m, 0))],
        out_specs=[pl.BlockSpec((1, AT_HEADS, tm, AT_HD), lambda b, m: (b, 0, m, 0)),
                   pl.BlockSpec((1, AT_KV_HEADS, tm, AT_HD), lambda b, m: (b, 0, m, 0)),
                   pl.BlockSpec((1, AT_KV_HEADS, tm, LANES), lambda b, m: (b, 0, m, 0)),
                   pl.BlockSpec((1, tm, LANES), lambda b, m: (b, m, 0))],
        out_shape=[jax.ShapeDtypeStruct((B, AT_HEADS, L, AT_HD), bf16),
                   jax.ShapeDtypeStruct((B, AT_KV_HEADS, L, AT_HD), bf16),
                   jax.ShapeDtypeStruct((B, AT_KV_HEADS, L, LANES), bf16),
                   jax.ShapeDtypeStruct((B, L, LANES), f32)],
        compiler_params=_cparams(("arbitrary", "arbitrary")),
        name="qkv_prep",
    )(proj, proj, proj, qg_row, kg_row, cos_t, sin_t)


def _attn_kernel(*refs, tq, tk, n_new, n_cache):
    if n_cache:
        q_ref, k_ref, v_ref, kc_ref, vc_ref, o_ref, s_s, m_s = refs
    else:
        q_ref, k_ref, v_ref, o_ref, s_s, m_s = refs
    rows = AT_GROUP * tq
    ncb = tk // LANES
    nkv = n_new + n_cache

    def chunk(new_ref, cache_ref, j):
        if j < n_new:
            return new_ref[0, 0, j * tk:(j + 1) * tk, :]
        return cache_ref[0, 0, (j - n_new) * tk:(j - n_new + 1) * tk, :]

    always = pl.program_id(2) >= 0

    @pl.when(always)
    def _():
        q = q_ref[0].reshape(rows, AT_HD)
        mx = None
        for j in range(nkv):
            s = _dot_nt(q, chunk(k_ref, kc_ref if n_cache else None, j))
            s_s[:, j * tk:(j + 1) * tk] = s
            cm = s[:, 0:LANES]
            for cb in range(1, ncb):
                cm = jnp.maximum(cm, s[:, cb * LANES:(cb + 1) * LANES])
            mx = cm if mx is None else jnp.maximum(mx, cm)
        m_s[...] = jnp.broadcast_to(jnp.max(mx, axis=-1, keepdims=True), (rows, LANES))

    @pl.when(pl.program_id(1) >= 0)
    def _():
        m_b = m_s[...]
        acc = None
        for j in range(nkv):
            ps = []
            for cb in range(ncb):
                off = j * tk + cb * LANES
                ps.append(jnp.exp(s_s[:, off:off + LANES] - m_b).astype(bf16))
            pv = _dot(jnp.concatenate(ps, axis=1), chunk(v_ref, vc_ref if n_cache else None, j))
            acc = pv if acc is None else acc + pv
        o = acc[:, :AT_HD] * (1.0 / acc[:, AT_HD:AT_HD + 1])
        for g in range(AT_GROUP):
            o_ref[0, :, g * AT_HD:(g + 1) * AT_HD] = o[g * tq:(g + 1) * tq].astype(o_ref.dtype)


def attn_call(q, k, v, k_cache=None, v_cache=None):
    B, _, Lq, _ = q.shape
    L = k.shape[2]
    P = 0 if k_cache is None else k_cache.shape[2]
    tq = 256
    tk = 512 if (L % 512 == 0 and P % 512 == 0) else L
    assert L % tk == 0 and P % tk == 0
    n_new, n_cache = L // tk, P // tk
    nq = Lq // tq
    kern = functools.partial(_attn_kernel, tq=tq, tk=tk, n_new=n_new, n_cache=n_cache)
    rows = AT_GROUP * tq

    def kv_spec(length, width):
        return pl.BlockSpec((1, 1, length, width), lambda b, g, i: (b, g, 0, 0))

    in_specs = [pl.BlockSpec((1, AT_GROUP, tq, AT_HD), lambda b, g, i: (b, g, i, 0)),
                kv_spec(L, AT_HD), kv_spec(L, LANES)]
    args = [q, k, v]
    if n_cache:
        in_specs += [kv_spec(P, AT_HD), kv_spec(P, LANES)]
        args += [k_cache, v_cache]
    return pl.pallas_call(
        kern,
        grid=(B, AT_KV_HEADS, nq),
        in_specs=in_specs,
        out_specs=pl.BlockSpec((1, tq, AT_GROUP * AT_HD), lambda b, g, i: (b, i, g)),
        out_shape=jax.ShapeDtypeStruct((B, Lq, AT_HEADS * AT_HD), bf16),
        scratch_shapes=[pltpu.VMEM((rows, L + P), f32),
                        pltpu.VMEM((rows, LANES), f32)],
        compiler_params=_cparams(("arbitrary", "arbitrary", "arbitrary")),
        name="attention",
    )(*args)


def _cache_layouts(cache_k_l, cache_v_l):
    kc = jnp.swapaxes(cache_k_l, 1, 2).astype(bf16)
    vc = jnp.swapaxes(cache_v_l, 1, 2).astype(bf16)
    ones = jnp.ones(vc.shape[:-1] + (1,), bf16)
    zeros = jnp.zeros(vc.shape[:-1] + (LANES - AT_HD - 1,), bf16)
    return kc, jnp.concatenate([vc, ones, zeros], axis=-1)


def _merge_kernel(x_ref, yml_ref, yhg_ref, yat_ref, bg_ref, mod_ref, wb_ref, wo_ref, o_ref):
    merged = None
    for n, y_ref in enumerate((yml_ref, yhg_ref, yat_ref)):
        pn = _dot(y_ref[0], wb_ref[n])
        gn = _sigmoid(bg_ref[0, :, n * D_MODEL:(n + 1) * D_MODEL].astype(f32))
        merged = gn * pn if merged is None else merged + gn * pn
    out = _dot(merged.astype(bf16), wo_ref[...])
    o_ref[0] = x_ref[0] + mod_ref[0, 2:3, :] * out


def merge_call(x, y_ml, y_hg, y_at, proj, mod, wb, wo, shared_mod):
    B, L, _ = x.shape
    tm = min(L, 512)
    mod_map = (lambda b, m: (0, 0, 0)) if shared_mod else (lambda b, m: (b, 0, 0))
    yspec = pl.BlockSpec((1, tm, BRANCH_W), lambda b, m: (b, m, 0))
    return pl.pallas_call(
        _merge_kernel,
        grid=(B, L // tm),
        in_specs=[pl.BlockSpec((1, tm, D_MODEL), lambda b, m: (b, m, 0)),
                  yspec, yspec, yspec,
                  pl.BlockSpec((1, tm, 3 * D_MODEL), lambda b, m: (b, m, CB_BG)),
                  pl.BlockSpec((1, 8, D_MODEL), mod_map),
                  pl.BlockSpec((3, BRANCH_W, D_MODEL), lambda b, m: (0, 0, 0)),
                  pl.BlockSpec((D_MODEL, D_MODEL), lambda b, m: (0, 0))],
        out_specs=pl.BlockSpec((1, tm, D_MODEL), lambda b, m: (b, m, 0)),
        out_shape=jax.ShapeDtypeStruct((B, L, D_MODEL), f32),
        compiler_params=_cparams(("arbitrary", "arbitrary")),
        name="merge_out",
    )(x, y_ml, y_hg, y_at, proj, mod, wb, wo)


FFN_SPLITS = ((0, 1536), (1536, FFN_HIDDEN))


def _ffn_kernel(x_ref, mod_ref, g_ref, wi_ref, wd_ref, fg_ref, o_ref, *, final_norm):
    x = x_ref[0]
    hb = (_rms_rows(x, g_ref[...]) * (1.0 + mod_ref[0, 4:5, :]) + mod_ref[0, 3:4, :]).astype(bf16)
    acc = None
    for lo, hi in FFN_SPLITS:
        gate = _dot(hb, wi_ref[:, lo:hi])
        up = _dot(hb, wi_ref[:, FFN_HIDDEN + lo:FFN_HIDDEN + hi])
        part = _dot((_silu(gate) * up).astype(bf16), wd_ref[lo:hi, :])
        acc = part if acc is None else acc + part
    y = x + mod_ref[0, 5:6, :] * acc
    if final_norm:
        y = _rms_rows(y, fg_ref[...])
    o_ref[0] = y


def ffn_call(x, mod, g_row, w_in, w_out, fg_row, shared_mod, final_norm):
    B, L, _ = x.shape
    tm = min(L, 512)
    mod_map = (lambda b, m: (0, 0, 0)) if shared_mod else (lambda b, m: (b, 0, 0))
    kern = functools.partial(_ffn_kernel, final_norm=final_norm)
    resident = pl.Buffered(1)
    return pl.pallas_call(
        kern,
        grid=(B, L // tm),
        in_specs=[pl.BlockSpec((1, tm, D_MODEL), lambda b, m: (b, m, 0)),
                  pl.BlockSpec((1, 8, D_MODEL), mod_map),
                  pl.BlockSpec((1, D_MODEL), lambda b, m: (0, 0)),
                  pl.BlockSpec((D_MODEL, 2 * FFN_HIDDEN), lambda b, m: (0, 0), pipeline_mode=resident),
                  pl.BlockSpec((FFN_HIDDEN, D_MODEL), lambda b, m: (0, 0), pipeline_mode=resident),
                  pl.BlockSpec((1, D_MODEL), lambda b, m: (0, 0))],
        out_specs=pl.BlockSpec((1, tm, D_MODEL), lambda b, m: (b, m, 0)),
        out_shape=jax.ShapeDtypeStruct((B, L, D_MODEL), f32),
        compiler_params=_cparams(("arbitrary", "arbitrary")),
        name="ffn",
    )(x, mod, g_row, w_in, w_out, fg_row)


def _reorder_w_in(w):
    o_mg, o_hq, o_hf, o_hi, o_bg, o_end = 2048, 2064, 2576, 3600, 5392, 8464
    d = w.shape[0]
    sec16 = [w[:, o_bg:o_end], w[:, :o_mg], w[:, o_hq:o_hf], w[:, o_hi:o_bg]]
    n16 = (o_end - o_bg) + o_mg + (o_hf - o_hq) + (o_bg - o_hi)
    sec32 = [w[:, o_hf:o_hi], w[:, o_mg:o_hq]]
    n32 = (o_hi - o_hf) + (o_hq - o_mg)
    return jnp.concatenate(sec16 + [jnp.zeros((d, N16_COLS - n16), w.dtype)] + sec32
                           + [jnp.zeros((d, N32_COLS - n32), w.dtype)], axis=1).astype(bf16)


def _rope_tables(L):
    rows = L // GRID_W
    row = jnp.repeat(jnp.arange(rows, dtype=f32), GRID_W)
    colp = jnp.tile(jnp.arange(GRID_W, dtype=f32), rows)
    n_freq = AT_HD // 4
    inv = ROPE_THETA ** (-jnp.arange(n_freq, dtype=f32) / n_freq)
    ang = jnp.concatenate([row[:, None] * inv, colp[:, None] * inv], axis=-1)
    cos = jnp.repeat(jnp.cos(ang), 2, axis=-1)
    sin = jnp.repeat(jnp.sin(ang), 2, axis=-1)
    sign = jnp.tile(jnp.array([-1.0, 1.0], f32), AT_HD // 2)
    sin = sin * sign
    return jnp.tile(cos, (1, 2)), jnp.tile(sin, (1, 2))


def _mod_rows(ada_rows):
    r = ada_rows.shape[0]
    m = ada_rows.reshape(r, 6, D_MODEL)
    return jnp.concatenate([m, jnp.zeros((r, 2, D_MODEL), f32)], axis=1)


def kernel(x_prompt, x_sample, c, cache_k, cache_v, state_ml_C, state_ml_n, state_ml_m, state_hg_S, c_ctx, w_ada, b_ada, norm1_g, norm2_g, w_in, ml_gate_b, ml_norm_g, hg_lb_logits, hg_norm_g, q_norm_g, k_norm_g, w_branch, w_out, w_ffn_in, w_ffn_out, final_g):
    Bp, Lp, _ = x_prompt.shape
    Bs, Ls, _ = x_sample.shape

    n_rows = 16
    cvec = jnp.concatenate([c_ctx[None, :], c, jnp.zeros((n_rows - 1 - Bs, D_MODEL), f32)], axis=0)
    ada = ada_call(cvec, w_ada, b_ada)
    lb_all = lb_call(hg_lb_logits)
    cos_t, sin_t = _rope_tables(Ls)

    zeros_c = jnp.zeros((Bp, 1, 2, N_HEADS, D_HEAD, D_HEAD), f32)
    zeros_n = jnp.zeros((Bp, 1, 2, N_HEADS, 1, D_HEAD), f32)
    zeros_m = jnp.zeros((Bp * 2 * N_HEADS,), f32)
    n_state = state_ml_n.reshape(Bs, DEPTH, 2, N_HEADS, 1, D_HEAD)

    xp, xs = x_prompt.reshape(1, Bp * Lp, D_MODEL), x_sample
    nk, nv, nC, nn_, nm, nS = [], [], [], [], [], []
    for l in range(DEPTH):
        w_l = _reorder_w_in(w_in[l])
        wb_l = w_branch[l].astype(bf16)
        wo_l = w_out[l].astype(bf16)
        wfi_l = w_ffn_in[l].astype(bf16)
        wfo_l = w_ffn_out[l].astype(bf16)
        g1 = norm1_g[l][None, :]
        g2 = norm2_g[l][None, :]
        mlg = ml_norm_g[l][None, :]
        hgg = hg_norm_g[l][None, :]
        qg = jnp.tile(q_norm_g[l], 2)[None, :]
        kg = jnp.tile(k_norm_g[l], 2)[None, :]
        gate_b = ml_gate_b[l].reshape(-1).astype(f32)
        fg = final_g[None, :]
        last = l == DEPTH - 1

        mod_ctx = _mod_rows(ada[l, 0:1])
        mod_lat = _mod_rows(ada[l, 1:1 + Bs])

        proj_flat, proj32 = in_proj_call(xp, mod_ctx, g1, w_l, True)
        proj = proj_flat.reshape(Bp, Lp, N16_COLS)
        proj32 = proj32.reshape(Bp, Lp, N32_COLS)
        y_ml, c_f, n_f, m_f = mlstm_call(proj, proj32, gate_b, mlg, zeros_c, zeros_n, zeros_m, 0)
        y_hg, s_f = hgrn_call(proj, proj32, lb_all[l], hgg, zeros_c, 0)
        q_h, k_h, v_h, k_n = qkv_prep_call(proj, qg, kg, cos_t[:Lp], sin_t[:Lp], False)
        y_at = attn_call(q_h, k_h, v_h)
        flat = (1, Bp * Lp, BRANCH_W)
        xp = merge_call(xp, y_ml.reshape(flat), y_hg.reshape(flat), y_at.reshape(flat), proj_flat,
                        mod_ctx, wb_l, wo_l, True)
        xp = ffn_call(xp, mod_ctx, g2, wfi_l, wfo_l, fg, True, last)
        nk.append(k_n.reshape(Bp, Lp, AT_KV_HEADS, AT_HD))
        nv.append(proj[:, :, CB_AV * LANES:(CB_AV + 1) * LANES].astype(f32).reshape(Bp, Lp, AT_KV_HEADS, AT_HD))
        nC.append(c_f)
        nn_.append(n_f[:, :, :, 0, :])
        nm.append(m_f[:, :, :, 0, 0])
        nS.append(s_f)

        proj, proj32 = in_proj_call(xs, mod_lat, g1, w_l, False)
        y_ml, _, _, _ = mlstm_call(proj, proj32, gate_b, mlg, state_ml_C, n_state,
                                   state_ml_m[:, l].reshape(-1).astype(f32), l)
        y_hg, _ = hgrn_call(proj, proj32, lb_all[l], hgg, state_hg_S, l)
        q_h, k_h, v_h, _ = qkv_prep_call(proj, qg, kg, cos_t, sin_t, True)
        k_c, v_c = _cache_layouts(cache_k[:, l], cache_v[:, l])
        y_at = attn_call(q_h, k_h, v_h, k_c, v_c)
        xs = merge_call(xs, y_ml, y_hg, y_at, proj, mod_lat, wb_l, wo_l, False)
        xs = ffn_call(xs, mod_lat, g2, wfi_l, wfo_l, fg, False, last)

    return (xp.reshape(Bp, Lp, D_MODEL), xs, jnp.stack(nk, axis=1), jnp.stack(nv, axis=1), jnp.stack(nC, axis=1),
            jnp.stack(nn_, axis=1), jnp.stack(nm, axis=1), jnp.stack(nS, axis=1))
```

```python
import functools
import math

import jax
import jax.numpy as jnp
from jax import lax
from jax.experimental import pallas as pl
from jax.experimental.pallas import tpu as pltpu

f32 = jnp.float32
bf16 = jnp.bfloat16

D_MODEL = 1024
DEPTH = 4
N_HEADS = 4
D_HEAD = 128
AT_HEADS = 8
AT_KV_HEADS = 2
AT_GROUP = AT_HEADS // AT_KV_HEADS
AT_HD = 64
BRANCH_W = 512
FFN_HIDDEN = 2816
GRID_W = 64
ROPE_THETA = 10000.0
EPS = 1e-6

LANES = 128
SUBLANES = 8
VMEM_LIMIT = 52 * 1024 * 1024

CB_BG = 0
CB_MQ = 24
CB_MK = 28
CB_MV = 32
CB_MO = 36
CB_HQ = 40
CB_HI = 44
CB_HG = 48
CB_AQ = 52
CB_AK = 56
CB_AV = 57
N16_BLOCKS = 60
N16_COLS = N16_BLOCKS * LANES
CF_HF = 0
CF_MG = 8
N32_BLOCKS = 10
N32_COLS = N32_BLOCKS * LANES
PROJ_TN = N32_COLS
N16_TILES = N16_COLS // PROJ_TN

ML_CHUNK = 128
ML_EXT = 144
HG_CHUNK = 64
HG_SUB = 8
HG_HALF = 4
HG_LEVELS = (32, 16, 8, 4)
NEG_BIG = -1e30
LOG2E = 1.4426950408889634


def _cparams(sem):
    return pltpu.CompilerParams(dimension_semantics=sem, vmem_limit_bytes=VMEM_LIMIT)


def _dot(a, b):
    return jnp.dot(a, b, preferred_element_type=f32)


def _dot_nt(a, b):
    return lax.dot_general(a, b, (((1,), (1,)), ((), ())), preferred_element_type=f32)


def _dot_tn(a, b):
    return lax.dot_general(a, b, (((0,), (0,)), ((), ())), preferred_element_type=f32)


def _sigmoid(x):
    return 1.0 / (1.0 + jnp.exp(-x))


def _silu(x):
    return x * _sigmoid(x)


def _log_sigmoid(x):
    return jnp.minimum(x, 0.0) - jnp.log1p(jnp.exp(-jnp.abs(x)))


def _split3(x):
    hi = x.astype(bf16)
    r1 = x - hi.astype(f32)
    mid = r1.astype(bf16)
    lo = (r1 - mid.astype(f32)).astype(bf16)
    return hi, mid, lo


def _rms_rows(x, g_row):
    ms = jnp.mean(x * x, axis=-1, keepdims=True)
    return x * lax.rsqrt(ms + EPS) * g_row


def _ada_kernel(c_ref, w_ref, b_ref, o_ref):
    s = _silu(c_ref[...]).astype(bf16)
    o_ref[0] = _dot(s, w_ref[0].astype(bf16)) + b_ref[0]


def ada_call(cvec, w_ada, b_ada):
    rows = cvec.shape[0]
    tn = 1536
    n6 = 6 * D_MODEL
    return pl.pallas_call(
        _ada_kernel,
        grid=(DEPTH, n6 // tn),
        in_specs=[pl.BlockSpec((rows, D_MODEL), lambda l, n: (0, 0)),
                  pl.BlockSpec((1, D_MODEL, tn), lambda l, n: (l, 0, n)),
                  pl.BlockSpec((1, 1, tn), lambda l, n: (l, 0, n))],
        out_specs=pl.BlockSpec((1, rows, tn), lambda l, n: (l, 0, n)),
        out_shape=jax.ShapeDtypeStruct((DEPTH, rows, n6), f32),
        compiler_params=_cparams(("arbitrary", "arbitrary")),
        name="ada",
    )(cvec, w_ada, b_ada.reshape(DEPTH, 1, n6))


def _lb_kernel(x_ref, o_ref):
    xs = [x_ref[l] for l in range(DEPTH)]
    mx = xs[0]
    for l in range(1, DEPTH):
        mx = jnp.maximum(mx, xs[l])
    es = [jnp.exp(x - mx) for x in xs]
    tot = es[0]
    for l in range(1, DEPTH):
        tot = tot + es[l]
    sm = [e / tot for e in es]
    run = sm[0]
    o_ref[0] = run - sm[0]
    for l in range(1, DEPTH):
        run = run + sm[l]
        o_ref[l] = run - sm[0]


def lb_call(logits):
    x = logits.astype(f32).reshape(DEPTH, 2 * N_HEADS, D_HEAD)
    return pl.pallas_call(
        _lb_kernel,
        out_shape=jax.ShapeDtypeStruct((DEPTH, 2 * N_HEADS, D_HEAD), f32),
        name="hg_lower_bounds",
    )(x)


def _in_proj_kernel(x_ref, mod_ref, g_ref, w_ref, o16_ref, o32_ref, h_s):
    n = pl.program_id(2)

    @pl.when(n == 0)
    def _():
        x = x_ref[0]
        sh = mod_ref[0, 0:1, :]
        sc = mod_ref[0, 1:2, :]
        h = _rms_rows(x, g_ref[...]) * (1.0 + sc) + sh
        h_s[...] = h.astype(bf16)

    @pl.when(n < N16_TILES)
    def _():
        o16_ref[0] = _dot(h_s[...], w_ref[...]).astype(bf16)

    @pl.when(n == N16_TILES)
    def _():
        o32_ref[0] = _dot(h_s[...], w_ref[...])


def in_proj_call(x, mod, g_row, w, shared_mod):
    B, L, _ = x.shape
    tm = min(L, 1024)
    tn = PROJ_TN
    mod_map = (lambda b, m, n: (0, 0, 0)) if shared_mod else (lambda b, m, n: (b, 0, 0))
    return pl.pallas_call(
        _in_proj_kernel,
        grid=(B, L // tm, N16_TILES + 1),
        in_specs=[pl.BlockSpec((1, tm, D_MODEL), lambda b, m, n: (b, m, 0)),
                  pl.BlockSpec((1, 8, D_MODEL), mod_map),
                  pl.BlockSpec((1, D_MODEL), lambda b, m, n: (0, 0)),
                  pl.BlockSpec((D_MODEL, tn), lambda b, m, n: (0, n))],
        out_specs=[pl.BlockSpec((1, tm, tn), lambda b, m, n: (b, m, jnp.minimum(n, N16_TILES - 1))),
                   pl.BlockSpec((1, tm, tn), lambda b, m, n: (b, m, 0))],
        out_shape=[jax.ShapeDtypeStruct((B, L, N16_COLS), bf16),
                   jax.ShapeDtypeStruct((B, L, N32_COLS), f32)],
        scratch_shapes=[pltpu.VMEM((tm, D_MODEL), bf16)],
        compiler_params=_cparams(("arbitrary", "arbitrary", "arbitrary")),
        name="in_proj",
    )(x, mod, g_row, w)


def _mlstm_kernel(bias_ref, m0_ref, q_ref, k_ref, v_ref, mo_ref, g_ref, ng_ref, c0_ref, n0_ref,
                  y_ref, cout_ref, nout_ref, mout_ref,
                  gt_s, rw_s, cr_s, ucb_s, vt_s, ht_s, ct_s, m_s, *, seq_len):
    T = ML_CHUNK
    assert T == LANES
    nc = seq_len // T
    b_idx = pl.program_id(0)
    h = pl.program_id(1)
    scale = D_HEAD ** -0.5
    ext_rows = jnp.concatenate([jnp.ones((1, T), f32), jnp.zeros((ML_EXT - D_HEAD - 1, T), f32)], axis=0)

    def tr_body(j, carry):
        r0 = pl.multiple_of(j * T, T)
        gt_s[:, pl.ds(r0, T)] = g_ref[0, pl.ds(r0, T), :].T
        vt = v_ref[0, pl.ds(r0, T), :].astype(f32).T
        vt_s[:, pl.ds(r0, T)] = jnp.concatenate([vt, ext_rows], axis=0).astype(bf16)
        return carry
    lax.fori_loop(0, nc, tr_body, 0, unroll=min(16, nc))

    i_f = gt_s[pl.ds(h, 1), :] + bias_ref[h]
    f_f = _log_sigmoid(gt_s[pl.ds(4 + h, 1), :] + bias_ref[4 + h])
    i_b = gt_s[pl.ds(8 + h, 1), :] + bias_ref[8 + h]
    f_b = _log_sigmoid(gt_s[pl.ds(12 + h, 1), :] + bias_ref[12 + h])
    zrow = jnp.zeros_like(i_f)
    rw_s[...] = jnp.concatenate([f_f, i_f, f_b, i_b, zrow, zrow, zrow, zrow], axis=0)

    ui = lax.broadcasted_iota(jnp.int32, (T, T), 0)
    si = lax.broadcasted_iota(jnp.int32, (T, T), 1)
    tri = jnp.concatenate([(ui <= si).astype(bf16), (ui >= si).astype(bf16)], axis=1)
    valid_f = ui <= si
    valid_b = ui >= si

    def cs_body(j, carry):
        r0 = pl.multiple_of(j * T, T)
        rows = rw_s[:, pl.ds(r0, T)]
        hi, mid, lo = _split3(rows)
        cs3 = _dot(jnp.concatenate([hi, mid, lo, jnp.zeros_like(hi)], axis=0), tri)
        cs = cs3[0:8] + cs3[8:16] + cs3[16:24]
        b_f = cs[0:1, 0:T]
        b_b = cs[2:3, T:2 * T]
        cr_s[:, pl.ds(r0, T)] = jnp.concatenate([b_f, rows[1:2], b_b, rows[3:4],
                                                 rows[1:2] - b_f, rows[3:4] - b_b, rows[6:8]], axis=0)
        return carry
    lax.fori_loop(0, nc, cs_body, 0, unroll=min(16, nc))

    def ucb_body(j, carry):
        r0 = pl.multiple_of(j * T, T)
        for d in range(2):
            ucb_s[d, pl.ds(r0, T), :] = jnp.broadcast_to(cr_s[pl.ds(4 + d, 1), pl.ds(r0, T)], (T, T)).T
        return carry
    lax.fori_loop(0, nc, ucb_body, 0, unroll=min(16, nc))

    for d in range(2):
        ct_s[d] = jnp.concatenate([c0_ref[0, 0, d, 0].T, n0_ref[0, 0, d, 0],
                                   jnp.zeros((ML_EXT - D_HEAD - 1, D_HEAD), f32)], axis=0)
        m_s[d] = jnp.full((1, LANES), m0_ref[b_idx * 2 * N_HEADS + d * N_HEADS + h], f32)

    def step(d, j):
        r0 = pl.multiple_of(j * T, T)
        b_row = cr_s[pl.ds(2 * d, 1), pl.ds(r0, T)]
        b_end = b_row[:, T - 1:T] if d == 0 else b_row[:, 0:1]
        b_end_l = jnp.broadcast_to(b_end, (1, LANES))
        ucb = ucb_s[d, pl.ds(r0, T), :]

        q = q_ref[0, pl.ds(r0, T), :].astype(f32) * scale
        kb = k_ref[0, pl.ds(r0, T), :]
        k = kb.astype(f32)
        qb = q.astype(bf16)
        vt = vt_s[:, pl.ds(r0, T)]
        ct = ct_s[d]
        m_prev = m_s[d]

        d_t = jnp.where(valid_f if d == 0 else valid_b, ucb + b_row, NEG_BIG)
        m_state = b_row + m_prev
        m_t = jnp.maximum(m_state, jnp.max(d_t, axis=0, keepdims=True))
        w_state = jnp.exp(m_state - m_t)
        s_t = _dot_nt(kb, qb) * jnp.exp(d_t - m_t)
        tot = _dot(vt, s_t.astype(bf16)) + w_state * _dot_nt(ct.astype(bf16), qb)
        den = tot[D_HEAD:D_HEAD + 1]
        ht_s[d, :, pl.ds(r0, T)] = tot[:D_HEAD] * (1.0 / jnp.maximum(jnp.abs(den), jnp.exp(-m_t)))

        gcb = ucb + b_end_l
        m_new = jnp.maximum(b_end_l + m_prev, jnp.max(gcb, axis=0, keepdims=True))
        kw = k * jnp.exp(gcb - m_new)
        decay = jnp.exp(b_end_l + m_prev - m_new)
        ct_s[d] = decay * ct + _dot(vt, kw.astype(bf16))
        m_s[d] = m_new

    def loop_body(j, carry):
        step(0, j)
        step(1, nc - 1 - j)
        return carry
    lax.fori_loop(0, nc, loop_body, 0, unroll=min(16, nc))

    ng = ng_ref[...]

    def out_body(j, carry):
        r0 = pl.multiple_of(j * T, T)
        y_t = ht_s[0, :, pl.ds(r0, T)] + ht_s[1, :, pl.ds(r0, T)]
        ms = jnp.mean(y_t * y_t, axis=0, keepdims=True)
        y = (y_t * lax.rsqrt(ms + EPS)).T * ng * _sigmoid(mo_ref[0, pl.ds(r0, T), :].astype(f32))
        y_ref[0, pl.ds(r0, T), :] = y.astype(y_ref.dtype)
        return carry
    lax.fori_loop(0, nc, out_body, 0, unroll=min(8, nc))

    for d in range(2):
        cout_ref[0, d, 0] = ct_s[d, 0:D_HEAD, :].T
        nout_ref[0, d, 0] = ct_s[d, D_HEAD:D_HEAD + 1, :]
        mout_ref[0, d, 0] = m_s[d]


def mlstm_call(proj, proj32, gate_bias, norm_g, c0, n0, m0, layer):
    B, L, _ = proj.shape

    def col(cb):
        return pl.BlockSpec((1, L, LANES), lambda b, h, cb=cb: (b, 0, cb + h))

    smem = pl.BlockSpec(memory_space=pltpu.SMEM)
    kern = functools.partial(_mlstm_kernel, seq_len=L)
    return pl.pallas_call(
        kern,
        grid=(B, N_HEADS),
        in_specs=[smem, smem,
                  col(CB_MQ), col(CB_MK), col(CB_MV), col(CB_MO),
                  pl.BlockSpec((1, L, LANES), lambda b, h: (b, 0, CF_MG)),
                  pl.BlockSpec((1, LANES), lambda b, h: (0, h)),
                  pl.BlockSpec((1, 1, 2, 1, D_HEAD, D_HEAD), lambda b, h: (b, layer, 0, h, 0, 0)),
                  pl.BlockSpec((1, 1, 2, 1, 1, D_HEAD), lambda b, h: (b, layer, 0, h, 0, 0))],
        out_specs=[pl.BlockSpec((1, L, LANES), lambda b, h: (b, 0, h)),
                   pl.BlockSpec((1, 2, 1, D_HEAD, D_HEAD), lambda b, h: (b, 0, h, 0, 0)),
                   pl.BlockSpec((1, 2, 1, 1, D_HEAD), lambda b, h: (b, 0, h, 0, 0)),
                   pl.BlockSpec((1, 2, 1, 1, LANES), lambda b, h: (b, 0, h, 0, 0))],
        out_shape=[jax.ShapeDtypeStruct((B, L, BRANCH_W), bf16),
                   jax.ShapeDtypeStruct((B, 2, N_HEADS, D_HEAD, D_HEAD), f32),
                   jax.ShapeDtypeStruct((B, 2, N_HEADS, 1, D_HEAD), f32),
                   jax.ShapeDtypeStruct((B, 2, N_HEADS, 1, LANES), f32)],
        scratch_shapes=[pltpu.VMEM((LANES, L), f32),
                        pltpu.VMEM((8, L), f32),
                        pltpu.VMEM((8, L), f32),
                        pltpu.VMEM((2, L, LANES), f32),
                        pltpu.VMEM((ML_EXT, L), bf16),
                        pltpu.VMEM((2, D_HEAD, L), f32),
                        pltpu.VMEM((2, ML_EXT, D_HEAD), f32),
                        pltpu.VMEM((2, 1, LANES), f32)],
        compiler_params=_cparams(("arbitrary", "arbitrary")),
        name="mlstm",
    )(gate_bias, m0, proj, proj, proj, proj, proj32, norm_g, c0, n0)


def _hgrn_kernel(q_ref, f0_ref, f1_ref, i_ref, og_ref, lb_ref, ng_ref, s0_ref,
                 y_ref, sout_ref, o_s, st_s, qx_s, dec_s, u_s, vt_s, k_s, b_s, q32_s, v32_s, *, seq_len):
    T = HG_CHUNK
    nc = seq_len // T
    h = pl.program_id(1)
    f_refs = (f0_ref, f1_ref)

    lbs, loglb, log1mlb = [], [], []
    for d in range(2):
        lb = lb_ref[pl.ds(d * N_HEADS + h, 1), :]
        lbs.append(lb)
        loglb.append(jnp.log(lb))
        log1mlb.append(jnp.log1p(-lb))
        st_s[d] = s0_ref[0, 0, d, 0].T

    ti = lax.broadcasted_iota(jnp.int32, (T, T), 0)
    ui = lax.broadcasted_iota(jnp.int32, (T, T), 1)
    tri = ((ui <= ti).astype(bf16), (ui >= ti).astype(bf16))
    sub_t = lax.broadcasted_iota(jnp.int32, (HG_SUB, D_HEAD), 0)
    sub_q = sub_t % HG_HALF
    low_half = sub_t < HG_HALF
    row_i = lax.broadcasted_iota(jnp.int32, (T, D_HEAD), 0)
    tt_r = lax.broadcasted_iota(jnp.int32, (T, T), 0)
    tt_c = lax.broadcasted_iota(jnp.int32, (T, T), 1)
    q_rows = {(hh, dd): ((row_i % (2 * hh)) >= hh) if dd == 0 else ((row_i % (2 * hh)) < hh)
              for hh in HG_LEVELS for dd in range(2)}
    same_blk = {hh: (tt_r // (2 * hh)) == (tt_c // (2 * hh)) for hh in HG_LEVELS}

    def gates(d, j, q):
        r0 = pl.multiple_of(j * T, T)
        hf = f_refs[d][0, pl.ds(r0, T), :]
        e = jnp.exp(-jnp.abs(hf))
        sig_neg = jnp.where(hf >= 0.0, e, 1.0) / (1.0 + e)
        logsig = jnp.minimum(hf, 0.0) - jnp.log(1.0 + e)
        cterm = log1mlb[d] + logsig
        amax = jnp.maximum(loglb[d], cterm)
        logf = amax + jnp.log(1.0 + jnp.exp(-jnp.abs(loglb[d] - cterm)))
        k = (1.0 - lbs[d]) * sig_neg
        hi, mid, lo = _split3(logf * LOG2E)
        b = _dot(tri[d], hi) + _dot(tri[d], mid) + _dot(tri[d], lo)
        b_end = b[T - 1:T] if d == 0 else b[0:1]
        k_s[d, pl.ds(r0, T), :] = k
        b_s[d, pl.ds(r0, T), :] = b
        qx_s[d, pl.ds(r0, T), :] = (q * jnp.exp2(b)).astype(bf16)
        dec_s[d, pl.ds(j, 1), :] = jnp.exp2(b_end)

    def gates_body(j, carry):
        r0 = pl.multiple_of(j * T, T)
        q = q_ref[0, pl.ds(r0, T), :].astype(f32)
        v = i_ref[0, pl.ds(r0, T), :].astype(f32)
        q32_s[pl.ds(r0, T), :] = q
        v32_s[pl.ds(r0, T), :] = v
        vt_s[j] = v.T.astype(bf16)
        gates(0, j, q)
        gates(1, j, q)
        return carry
    lax.fori_loop(0, nc, gates_body, 0, unroll=min(16, nc))

    def intra(d, j):
        r0 = pl.multiple_of(j * T, T)

        def rows(ref2, lo_r, n):
            return ref2[pl.ds(r0 + lo_r, n), :]

        def srows(ref3, lo_r, n):
            return ref3[d, pl.ds(r0 + lo_r, n), :]

        nsub = T // HG_SUB
        o_blk = [None] * nsub

        def add(idx, val):
            o_blk[idx] = val if o_blk[idx] is None else o_blk[idx] + val

        def levels():
            b_all = srows(b_s, 0, T)
            q_all = rows(q32_s, 0, T)
            k_all = srows(k_s, 0, T)
            a_tot = None
            for h in HG_LEVELS:
                blk = 2 * h
                ref_off = h - 1 if d == 0 else h
                ref = jnp.concatenate([jnp.broadcast_to(srows(b_s, base + ref_off, 1), (blk, D_HEAD))
                                       for base in range(0, T, blk)], axis=0)
                is_q = q_rows[(h, d)]
                qx = q_all * jnp.exp2(jnp.where(is_q, b_all - ref, NEG_BIG))
                kx = k_all * jnp.exp2(jnp.where(is_q, NEG_BIG, ref - b_all))
                a = _dot_nt(qx.astype(bf16), kx.astype(bf16))
                if blk < T:
                    a = jnp.where(same_blk[h], a, 0.0)
                a_tot = a if a_tot is None else a_tot + a
            ov = _dot(a_tot.astype(bf16), rows(v32_s, 0, T).astype(bf16))
            for p in range(nsub):
                add(p, ov[p * HG_SUB:(p + 1) * HG_SUB])

        def diag(lo_r):
            bs = srows(b_s, lo_r, HG_SUB)
            qs = rows(q32_s, lo_r, HG_SUB)

            def keyrow(read, s):
                return jnp.where(low_half, read(lo_r + s, 1), read(lo_r + HG_HALF + s, 1))

            acc = None
            for s in range(HG_HALF):
                valid = (sub_q >= s) if d == 0 else (sub_q <= s)
                b_key = keyrow(lambda r, n: srows(b_s, r, n), s)
                ee = jnp.exp2(jnp.where(valid, bs - b_key, NEG_BIG))
                a_col = jnp.sum(qs * ee * keyrow(lambda r, n: srows(k_s, r, n), s), axis=-1, keepdims=True)
                term = a_col * keyrow(lambda r, n: rows(v32_s, r, n), s)
                acc = term if acc is None else acc + term
            add(lo_r // HG_SUB, acc)

        levels()
        for i in range(nsub):
            diag(i * HG_SUB)
        o_s[d, pl.ds(r0, T), :] = jnp.concatenate(o_blk, axis=0)

    def intra_body(j, carry):
        intra(0, j)
        intra(1, j)
        return carry
    lax.fori_loop(0, nc, intra_body, 0, unroll=min(16, nc))

    def incr_body(j, carry):
        r0 = pl.multiple_of(j * T, T)
        vt = vt_s[j]
        for d in range(2):
            b = b_s[d, pl.ds(r0, T), :]
            b_end = b[T - 1:T] if d == 0 else b[0:1]
            u_s[d, j] = _dot(vt, (k_s[d, pl.ds(r0, T), :] * jnp.exp2(b_end - b)).astype(bf16))
        return carry
    lax.fori_loop(0, nc, incr_body, 0, unroll=min(16, nc))

    def state_step(d, j):
        r0 = pl.multiple_of(j * T, T)
        st = st_s[d]
        o_s[d, pl.ds(r0, T), :] = o_s[d, pl.ds(r0, T), :] + _dot_nt(qx_s[d, pl.ds(r0, T), :], st.astype(bf16))
        st_s[d] = st * dec_s[d, pl.ds(j, 1), :] + u_s[d, j]

    def state_body(j, carry):
        state_step(0, j)
        state_step(1, nc - 1 - j)
        return carry
    lax.fori_loop(0, nc, state_body, 0, unroll=min(16, nc // 2))

    ng = ng_ref[...]

    def out_body(j, carry):
        r0 = pl.multiple_of(j * T, T)
        y = o_s[0, pl.ds(r0, T), :] + o_s[1, pl.ds(r0, T), :]
        y = _rms_rows(y, ng) * _silu(og_ref[0, pl.ds(r0, T), :].astype(f32))
        y_ref[0, pl.ds(r0, T), :] = y.astype(y_ref.dtype)
        return carry
    lax.fori_loop(0, nc, out_body, 0, unroll=min(8, nc))

    for d in range(2):
        sout_ref[0, d, 0] = st_s[d].T


def hgrn_call(proj, proj32, lb_rows, norm_g, s0, layer):
    B, L, _ = proj.shape

    def col(cb):
        return pl.BlockSpec((1, L, LANES), lambda b, h, cb=cb: (b, 0, cb + h))

    kern = functools.partial(_hgrn_kernel, seq_len=L)
    return pl.pallas_call(
        kern,
        grid=(B, N_HEADS),
        in_specs=[col(CB_HQ), col(CF_HF), col(CF_HF + N_HEADS), col(CB_HI), col(CB_HG),
                  pl.BlockSpec((2 * N_HEADS, D_HEAD), lambda b, h: (0, 0)),
                  pl.BlockSpec((1, LANES), lambda b, h: (0, h)),
                  pl.BlockSpec((1, 1, 2, 1, D_HEAD, D_HEAD), lambda b, h: (b, layer, 0, h, 0, 0))],
        out_specs=[pl.BlockSpec((1, L, LANES), lambda b, h: (b, 0, h)),
                   pl.BlockSpec((1, 2, 1, D_HEAD, D_HEAD), lambda b, h: (b, 0, h, 0, 0))],
        out_shape=[jax.ShapeDtypeStruct((B, L, BRANCH_W), bf16),
                   jax.ShapeDtypeStruct((B, 2, N_HEADS, D_HEAD, D_HEAD), f32)],
        scratch_shapes=[pltpu.VMEM((2, L, D_HEAD), f32),
                        pltpu.VMEM((2, D_HEAD, D_HEAD), f32),
                        pltpu.VMEM((2, L, D_HEAD), bf16),
                        pltpu.VMEM((2, L // HG_CHUNK, D_HEAD), f32),
                        pltpu.VMEM((2, L // HG_CHUNK, D_HEAD, D_HEAD), f32),
                        pltpu.VMEM((L // HG_CHUNK, D_HEAD, HG_CHUNK), bf16),
                        pltpu.VMEM((2, L, D_HEAD), f32),
                        pltpu.VMEM((2, L, D_HEAD), f32),
                        pltpu.VMEM((L, D_HEAD), f32),
                        pltpu.VMEM((L, D_HEAD), f32)],
        compiler_params=_cparams(("arbitrary", "arbitrary")),
        name="hgrn",
    )(proj, proj32, proj32, proj, proj, lb_rows, norm_g, s0)


def _pair_swap(x):
    lane = lax.broadcasted_iota(jnp.int32, x.shape, 1)
    return jnp.where((lane % 2) == 0, pltpu.roll(x, LANES - 1, 1), pltpu.roll(x, 1, 1))


def _head_norm_pair(x, g_row):
    gi = lax.broadcasted_iota(jnp.int32, (LANES, LANES), 0) // AT_HD
    gj = lax.broadcasted_iota(jnp.int32, (LANES, LANES), 1) // AT_HD
    same_head = (gi == gj).astype(bf16)
    sq = x * x
    hi = sq.astype(bf16)
    lo = (sq - hi.astype(f32)).astype(bf16)
    ms = (_dot(hi, same_head) + _dot(lo, same_head)) * (1.0 / AT_HD)
    return x * lax.rsqrt(ms + EPS) * g_row


def _qkv_prep_kernel(aq_ref, ak_ref, av_ref, qg_ref, kg_ref, cos_ref, sin_ref,
                     q_ref, k_ref, v_ref, kn_ref, *, use_rope):
    qg = qg_ref[...]
    kg = kg_ref[...]
    if use_rope:
        cos = cos_ref[...]
        sin = sin_ref[...]
    for p in range(AT_HEADS // 2):
        x = _head_norm_pair(aq_ref[0, :, p * LANES:(p + 1) * LANES].astype(f32), qg)
        if use_rope:
            x = x * cos + _pair_swap(x) * sin
        x = (x * (AT_HD ** -0.5)).astype(bf16)
        q_ref[0, 2 * p] = x[:, :AT_HD]
        q_ref[0, 2 * p + 1] = x[:, AT_HD:]
    kn = _head_norm_pair(ak_ref[0].astype(f32), kg)
    kn_ref[0] = kn
    if use_rope:
        kn = kn * cos + _pair_swap(kn) * sin
    kb = kn.astype(bf16)
    for g in range(AT_KV_HEADS):
        k_ref[0, g] = kb[:, g * AT_HD:(g + 1) * AT_HD]
    v = av_ref[0].astype(f32)
    lane = lax.broadcasted_iota(jnp.int32, v.shape, 1)
    tail = jnp.where(lane == AT_HD, 1.0, 0.0)
    v_ref[0, 0] = jnp.where(lane < AT_HD, v, tail).astype(bf16)
    v_ref[0, 1] = jnp.where(lane < AT_HD, pltpu.roll(v, AT_HD, 1), tail).astype(bf16)


def qkv_prep_call(proj, qg_row, kg_row, cos_t, sin_t, use_rope):
    B, L, _ = proj.shape
    tm = min(L, 512)
    kern = functools.partial(_qkv_prep_kernel, use_rope=use_rope)
    return pl.pallas_call(
        kern,
        grid=(B, L // tm),
        in_specs=[pl.BlockSpec((1, tm, 4 * LANES), lambda b, m: (b, m, CB_AQ // 4)),
                  pl.BlockSpec((1, tm, LANES), lambda b, m: (b, m, CB_AK)),
                  pl.BlockSpec((1, tm, LANES), lambda b, m: (b, m, CB_AV)),
                  pl.BlockSpec((1, LANES), lambda b, m: (0, 0)),
                  pl.BlockSpec((1, LANES), lambda b, m: (0, 0)),
                  pl.BlockSpec((tm, LANES), lambda b, m: (m, 0)),
                  pl.BlockSpec((tm, LANES), lambda b, m: (m, 0))],
        out_specs=[pl.BlockSpec((1, AT_HEADS, tm, AT_HD), lambda b, m: (b, 0, m, 0)),
                   pl.BlockSpec((1, AT_KV_HEADS, tm, AT_HD), lambda b, m: (b, 0, m, 0)),
                   pl.BlockSpec((1, AT_KV_HEADS, tm, LANES), lambda b, m: (b, 0, m, 0)),
                   pl.BlockSpec((1, tm, LANES), lambda b, m: (b, m, 0))],
        out_shape=[jax.ShapeDtypeStruct((B, AT_HEADS, L, AT_HD), bf16),
                   jax.ShapeDtypeStruct((B, AT_KV_HEADS, L, AT_HD), bf16),
                   jax.ShapeDtypeStruct((B, AT_KV_HEADS, L, LANES), bf16),
                   jax.ShapeDtypeStruct((B, L, LANES), f32)],
        compiler_params=_cparams(("arbitrary", "arbitrary")),
        name="qkv_prep",
    )(proj, proj, proj, qg_row, kg_row, cos_t, sin_t)


def _attn_kernel(*refs, tq, tk, n_new, n_cache):
    if n_cache:
        q_ref, k_ref, v_ref, kc_ref, vc_ref, o_ref, s_s, m_s = refs
    else:
        q_ref, k_ref, v_ref, o_ref, s_s, m_s = refs
    rows = AT_GROUP * tq
    ncb = tk // LANES
    nkv = n_new + n_cache

    def chunk(new_ref, cache_ref, j):
        if j < n_new:
            return new_ref[0, 0, j * tk:(j + 1) * tk, :]
        return cache_ref[0, 0, (j - n_new) * tk:(j - n_new + 1) * tk, :]

    always = pl.program_id(2) >= 0

    @pl.when(always)
    def _():
        q = q_ref[0].reshape(rows, AT_HD)
        mx = None
        for j in range(nkv):
            s = _dot_nt(q, chunk(k_ref, kc_ref if n_cache else None, j))
            s_s[:, j * tk:(j + 1) * tk] = s
            cm = s[:, 0:LANES]
            for cb in range(1, ncb):
                cm = jnp.maximum(cm, s[:, cb * LANES:(cb + 1) * LANES])
            mx = cm if mx is None else jnp.maximum(mx, cm)
        m_s[...] = jnp.broadcast_to(jnp.max(mx, axis=-1, keepdims=True), (rows, LANES))

    @pl.when(pl.program_id(1) >= 0)
    def _():
        m_b = m_s[...]
        acc = None
        for j in range(nkv):
            ps = []
            for cb in range(ncb):
                off = j * tk + cb * LANES
                ps.append(jnp.exp(s_s[:, off:off + LANES] - m_b).astype(bf16))
            pv = _dot(jnp.concatenate(ps, axis=1), chunk(v_ref, vc_ref if n_cache else None, j))
            acc = pv if acc is None else acc + pv
        o = acc[:, :AT_HD] * (1.0 / acc[:, AT_HD:AT_HD + 1])
        for g in range(AT_GROUP):
            o_ref[0, :, g * AT_HD:(g + 1) * AT_HD] = o[g * tq:(g + 1) * tq].astype(o_ref.dtype)


def attn_call(q, k, v, k_cache=None, v_cache=None):
    B, _, Lq, _ = q.shape
    L = k.shape[2]
    P = 0 if k_cache is None else k_cache.shape[2]
    tq = 256
    tk = 512 if (L % 512 == 0 and P % 512 == 0) else L
    assert L % tk == 0 and P % tk == 0
    n_new, n_cache = L // tk, P // tk
    nq = Lq // tq
    kern = functools.partial(_attn_kernel, tq=tq, tk=tk, n_new=n_new, n_cache=n_cache)
    rows = AT_GROUP * tq

    def kv_spec(length, width):
        return pl.BlockSpec((1, 1, length, width), lambda b, g, i: (b, g, 0, 0))

    in_specs = [pl.BlockSpec((1, AT_GROUP, tq, AT_HD), lambda b, g, i: (b, g, i, 0)),
                kv_spec(L, AT_HD), kv_spec(L, LANES)]
    args = [q, k, v]
    if n_cache:
        in_specs += [kv_spec(P, AT_HD), kv_spec(P, LANES)]
        args += [k_cache, v_cache]
    return pl.pallas_call(
        kern,
        grid=(B, AT_KV_HEADS, nq),
        in_specs=in_specs,
        out_specs=pl.BlockSpec((1, tq, AT_GROUP * AT_HD), lambda b, g, i: (b, i, g)),
        out_shape=jax.ShapeDtypeStruct((B, Lq, AT_HEADS * AT_HD), bf16),
        scratch_shapes=[pltpu.VMEM((rows, L + P), f32),
                        pltpu.VMEM((rows, LANES), f32)],
        compiler_params=_cparams(("arbitrary", "arbitrary", "arbitrary")),
        name="attention",
    )(*args)


def _cache_layouts(cache_k_l, cache_v_l):
    kc = jnp.swapaxes(cache_k_l, 1, 2).astype(bf16)
    vc = jnp.swapaxes(cache_v_l, 1, 2).astype(bf16)
    ones = jnp.ones(vc.shape[:-1] + (1,), bf16)
    zeros = jnp.zeros(vc.shape[:-1] + (LANES - AT_HD - 1,), bf16)
    return kc, jnp.concatenate([vc, ones, zeros], axis=-1)


def _merge_kernel(x_ref, yml_ref, yhg_ref, yat_ref, bg_ref, mod_ref, wb_ref, wo_ref, o_ref):
    merged = None
    for n, y_ref in enumerate((yml_ref, yhg_ref, yat_ref)):
        pn = _dot(y_ref[0], wb_ref[n])
        gn = _sigmoid(bg_ref[0, :, n * D_MODEL:(n + 1) * D_MODEL].astype(f32))
        merged = gn * pn if merged is None else merged + gn * pn
    out = _dot(merged.astype(bf16), wo_ref[...])
    o_ref[0] = x_ref[0] + mod_ref[0, 2:3, :] * out


def merge_call(x, y_ml, y_hg, y_at, proj, mod, wb, wo, shared_mod):
    B, L, _ = x.shape
    tm = min(L, 512)
    mod_map = (lambda b, m: (0, 0, 0)) if shared_mod else (lambda b, m: (b, 0, 0))
    yspec = pl.BlockSpec((1, tm, BRANCH_W), lambda b, m: (b, m, 0))
    return pl.pallas_call(
        _merge_kernel,
        grid=(B, L // tm),
        in_specs=[pl.BlockSpec((1, tm, D_MODEL), lambda b, m: (b, m, 0)),
                  yspec, yspec, yspec,
                  pl.BlockSpec((1, tm, 3 * D_MODEL), lambda b, m: (b, m, CB_BG)),
                  pl.BlockSpec((1, 8, D_MODEL), mod_map),
                  pl.BlockSpec((3, BRANCH_W, D_MODEL), lambda b, m: (0, 0, 0)),
                  pl.BlockSpec((D_MODEL, D_MODEL), lambda b, m: (0, 0))],
        out_specs=pl.BlockSpec((1, tm, D_MODEL), lambda b, m: (b, m, 0)),
        out_shape=jax.ShapeDtypeStruct((B, L, D_MODEL), f32),
        compiler_params=_cparams(("arbitrary", "arbitrary")),
        name="merge_out",
    )(x, y_ml, y_hg, y_at, proj, mod, wb, wo)


FFN_SPLITS = ((0, 1536), (1536, FFN_HIDDEN))


def _ffn_kernel(x_ref, mod_ref, g_ref, wi_ref, wd_ref, fg_ref, o_ref, *, final_norm):
    x = x_ref[0]
    hb = (_rms_rows(x, g_ref[...]) * (1.0 + mod_ref[0, 4:5, :]) + mod_ref[0, 3:4, :]).astype(bf16)
    acc = None
    for lo, hi in FFN_SPLITS:
        gate = _dot(hb, wi_ref[:, lo:hi])
        up = _dot(hb, wi_ref[:, FFN_HIDDEN + lo:FFN_HIDDEN + hi])
        part = _dot((_silu(gate) * up).astype(bf16), wd_ref[lo:hi, :])
        acc = part if acc is None else acc + part
    y = x + mod_ref[0, 5:6, :] * acc
    if final_norm:
        y = _rms_rows(y, fg_ref[...])
    o_ref[0] = y


def ffn_call(x, mod, g_row, w_in, w_out, fg_row, shared_mod, final_norm):
    B, L, _ = x.shape
    tm = min(L, 512)
    mod_map = (lambda b, m: (0, 0, 0)) if shared_mod else (lambda b, m: (b, 0, 0))
    kern = functools.partial(_ffn_kernel, final_norm=final_norm)
    resident = pl.Buffered(1)
    return pl.pallas_call(
        kern,
        grid=(B, L // tm),
        in_specs=[pl.BlockSpec((1, tm, D_MODEL), lambda b, m: (b, m, 0)),
                  pl.BlockSpec((1, 8, D_MODEL), mod_map),
                  pl.BlockSpec((1, D_MODEL), lambda b, m: (0, 0)),
                  pl.BlockSpec((D_MODEL, 2 * FFN_HIDDEN), lambda b, m: (0, 0), pipeline_mode=resident),
                  pl.BlockSpec((FFN_HIDDEN, D_MODEL), lambda b, m: (0, 0), pipeline_mode=resident),
                  pl.BlockSpec((1, D_MODEL), lambda b, m: (0, 0))],
        out_specs=pl.BlockSpec((1, tm, D_MODEL), lambda b, m: (b, m, 0)),
        out_shape=jax.ShapeDtypeStruct((B, L, D_MODEL), f32),
        compiler_params=_cparams(("arbitrary", "arbitrary")),
        name="ffn",
    )(x, mod, g_row, w_in, w_out, fg_row)


def _reorder_w_in(w):
    o_mg, o_hq, o_hf, o_hi, o_bg, o_end = 2048, 2064, 2576, 3600, 5392, 8464
    d = w.shape[0]
    sec16 = [w[:, o_bg:o_end], w[:, :o_mg], w[:, o_hq:o_hf], w[:, o_hi:o_bg]]
    n16 = (o_end - o_bg) + o_mg + (o_hf - o_hq) + (o_bg - o_hi)
    sec32 = [w[:, o_hf:o_hi], w[:, o_mg:o_hq]]
    n32 = (o_hi - o_hf) + (o_hq - o_mg)
    return jnp.concatenate(sec16 + [jnp.zeros((d, N16_COLS - n16), w.dtype)] + sec32
                           + [jnp.zeros((d, N32_COLS - n32), w.dtype)], axis=1).astype(bf16)


def _rope_tables(L):
    rows = L // GRID_W
    row = jnp.repeat(jnp.arange(rows, dtype=f32), GRID_W)
    colp = jnp.tile(jnp.arange(GRID_W, dtype=f32), rows)
    n_freq = AT_HD // 4
    inv = ROPE_THETA ** (-jnp.arange(n_freq, dtype=f32) / n_freq)
    ang = jnp.concatenate([row[:, None] * inv, colp[:, None] * inv], axis=-1)
    cos = jnp.repeat(jnp.cos(ang), 2, axis=-1)
    sin = jnp.repeat(jnp.sin(ang), 2, axis=-1)
    sign = jnp.tile(jnp.array([-1.0, 1.0], f32), AT_HD // 2)
    sin = sin * sign
    return jnp.tile(cos, (1, 2)), jnp.tile(sin, (1, 2))


def _mod_rows(ada_rows):
    r = ada_rows.shape[0]
    m = ada_rows.reshape(r, 6, D_MODEL)
    return jnp.concatenate([m, jnp.zeros((r, 2, D_MODEL), f32)], axis=1)


def kernel(x_prompt, x_sample, c, cache_k, cache_v, state_ml_C, state_ml_n, state_ml_m, state_hg_S, c_ctx, w_ada, b_ada, norm1_g, norm2_g, w_in, ml_gate_b, ml_norm_g, hg_lb_logits, hg_norm_g, q_norm_g, k_norm_g, w_branch, w_out, w_ffn_in, w_ffn_out, final_g):
    Bp, Lp, _ = x_prompt.shape
    Bs, Ls, _ = x_sample.shape

    n_rows = 16
    cvec = jnp.concatenate([c_ctx[None, :], c, jnp.zeros((n_rows - 1 - Bs, D_MODEL), f32)], axis=0)
    ada = ada_call(cvec, w_ada, b_ada)
    lb_all = lb_call(hg_lb_logits)
    cos_t, sin_t = _rope_tables(Ls)

    zeros_c = jnp.zeros((Bp, 1, 2, N_HEADS, D_HEAD, D_HEAD), f32)
    zeros_n = jnp.zeros((Bp, 1, 2, N_HEADS, 1, D_HEAD), f32)
    zeros_m = jnp.zeros((Bp * 2 * N_HEADS,), f32)
    n_state = state_ml_n.reshape(Bs, DEPTH, 2, N_HEADS, 1, D_HEAD)

    xp, xs = x_prompt.reshape(1, Bp * Lp, D_MODEL), x_sample
    nk, nv, nC, nn_, nm, nS = [], [], [], [], [], []
    for l in range(DEPTH):
        w_l = _reorder_w_in(w_in[l])
        wb_l = w_branch[l].astype(bf16)
        wo_l = w_out[l].astype(bf16)
        wfi_l = w_ffn_in[l].astype(bf16)
        wfo_l = w_ffn_out[l].astype(bf16)
        g1 = norm1_g[l][None, :]
        g2 = norm2_g[l][None, :]
        mlg = ml_norm_g[l][None, :]
        hgg = hg_norm_g[l][None, :]
        qg = jnp.tile(q_norm_g[l], 2)[None, :]
        kg = jnp.tile(k_norm_g[l], 2)[None, :]
        gate_b = ml_gate_b[l].reshape(-1).astype(f32)
        fg = final_g[None, :]
        last = l == DEPTH - 1

        mod_ctx = _mod_rows(ada[l, 0:1])
        mod_lat = _mod_rows(ada[l, 1:1 + Bs])

        proj_flat, proj32 = in_proj_call(xp, mod_ctx, g1, w_l, True)
        proj = proj_flat.reshape(Bp, Lp, N16_COLS)
        proj32 = proj32.reshape(Bp, Lp, N32_COLS)
        y_ml, c_f, n_f, m_f = mlstm_call(proj, proj32, gate_b, mlg, zeros_c, zeros_n, zeros_m, 0)
        y_hg, s_f = hgrn_call(proj, proj32, lb_all[l], hgg, zeros_c, 0)
        q_h, k_h, v_h, k_n = qkv_prep_call(proj, qg, kg, cos_t[:Lp], sin_t[:Lp], False)
        y_at = attn_call(q_h, k_h, v_h)
        flat = (1, Bp * Lp, BRANCH_W)
        xp = merge_call(xp, y_ml.reshape(flat), y_hg.reshape(flat), y_at.reshape(flat), proj_flat,
                        mod_ctx, wb_l, wo_l, True)
        xp = ffn_call(xp, mod_ctx, g2, wfi_l, wfo_l, fg, True, last)
        nk.append(k_n.reshape(Bp, Lp, AT_KV_HEADS, AT_HD))
        nv.append(proj[:, :, CB_AV * LANES:(CB_AV + 1) * LANES].astype(f32).reshape(Bp, Lp, AT_KV_HEADS, AT_HD))
        nC.append(c_f)
        nn_.append(n_f[:, :, :, 0, :])
        nm.append(m_f[:, :, :, 0, 0])
        nS.append(s_f)

        proj, proj32 = in_proj_call(xs, mod_lat, g1, w_l, False)
        y_ml, _, _, _ = mlstm_call(proj, proj32, gate_b, mlg, state_ml_C, n_state,
                                   state_ml_m[:, l].reshape(-1).astype(f32), l)
        y_hg, _ = hgrn_call(proj, proj32, lb_all[l], hgg, state_hg_S, l)
        q_h, k_h, v_h, _ = qkv_prep_call(proj, qg, kg, cos_t, sin_t, True)
        k_c, v_c = _cache_layouts(cache_k[:, l], cache_v[:, l])
        y_at = attn_call(q_h, k_h, v_h, k_c, v_c)
        xs = merge_call(xs, y_ml, y_hg, y_at, proj, mod_lat, wb_l, wo_l, False)
        xs = ffn_call(xs, mod_lat, g2, wfi_l, wfo_l, fg, False, last)

    return (xp.reshape(Bp, Lp, D_MODEL), xs, jnp.stack(nk, axis=1), jnp.stack(nv, axis=1), jnp.stack(nC, axis=1),
            jnp.stack(nn_, axis=1), jnp.stack(nm, axis=1), jnp.stack(nS, axis=1))
```

```python
import functools
import math

import jax
import jax.numpy as jnp
from jax import lax
from jax.experimental import pallas as pl
from jax.experimental.pallas import tpu as pltpu

f32 = jnp.float32
bf16 = jnp.bfloat16

D_MODEL = 1024
DEPTH = 4
N_HEADS = 4
D_HEAD = 128
AT_HEADS = 8
AT_KV_HEADS = 2
AT_GROUP = AT_HEADS // AT_KV_HEADS
AT_HD = 64
BRANCH_W = 512
FFN_HIDDEN = 2816
GRID_W = 64
ROPE_THETA = 10000.0
EPS = 1e-6

LANES = 128
SUBLANES = 8
VMEM_LIMIT = 52 * 1024 * 1024

CB_BG = 0
CB_MQ = 24
CB_MK = 28
CB_MV = 32
CB_MO = 36
CB_HQ = 40
CB_HI = 44
CB_HG = 48
CB_AQ = 52
CB_AK = 56
CB_AV = 57
N16_BLOCKS = 60
N16_COLS = N16_BLOCKS * LANES
CF_HF = 0
CF_MG = 8
N32_BLOCKS = 10
N32_COLS = N32_BLOCKS * LANES
PROJ_TN = N32_COLS
N16_TILES = N16_COLS // PROJ_TN

ML_CHUNK = 128
ML_EXT = 144
HG_CHUNK = 64
HG_SUB = 8
HG_HALF = 4
HG_LEVELS = (32, 16, 8, 4)
NEG_BIG = -1e30
LOG2E = 1.4426950408889634


def _cparams(sem):
    return pltpu.CompilerParams(dimension_semantics=sem, vmem_limit_bytes=VMEM_LIMIT)


def _dot(a, b):
    return jnp.dot(a, b, preferred_element_type=f32)


def _dot_nt(a, b):
    return lax.dot_general(a, b, (((1,), (1,)), ((), ())), preferred_element_type=f32)


def _dot_tn(a, b):
    return lax.dot_general(a, b, (((0,), (0,)), ((), ())), preferred_element_type=f32)


def _sigmoid(x):
    return 1.0 / (1.0 + jnp.exp(-x))


def _silu(x):
    return x * _sigmoid(x)


def _log_sigmoid(x):
    return jnp.minimum(x, 0.0) - jnp.log1p(jnp.exp(-jnp.abs(x)))


def _split3(x):
    hi = x.astype(bf16)
    r1 = x - hi.astype(f32)
    mid = r1.astype(bf16)
    lo = (r1 - mid.astype(f32)).astype(bf16)
    return hi, mid, lo


def _rms_rows(x, g_row):
    ms = jnp.mean(x * x, axis=-1, keepdims=True)
    return x * lax.rsqrt(ms + EPS) * g_row


def _ada_kernel(c_ref, w_ref, b_ref, o_ref):
    s = _silu(c_ref[...]).astype(bf16)
    o_ref[0] = _dot(s, w_ref[0].astype(bf16)) + b_ref[0]


def ada_call(cvec, w_ada, b_ada):
    rows = cvec.shape[0]
    tn = 1536
    n6 = 6 * D_MODEL
    return pl.pallas_call(
        _ada_kernel,
        grid=(DEPTH, n6 // tn),
        in_specs=[pl.BlockSpec((rows, D_MODEL), lambda l, n: (0, 0)),
                  pl.BlockSpec((1, D_MODEL, tn), lambda l, n: (l, 0, n)),
                  pl.BlockSpec((1, 1, tn), lambda l, n: (l, 0, n))],
        out_specs=pl.BlockSpec((1, rows, tn), lambda l, n: (l, 0, n)),
        out_shape=jax.ShapeDtypeStruct((DEPTH, rows, n6), f32),
        compiler_params=_cparams(("arbitrary", "arbitrary")),
        name="ada",
    )(cvec, w_ada, b_ada.reshape(DEPTH, 1, n6))


def _lb_kernel(x_ref, o_ref):
    xs = [x_ref[l] for l in range(DEPTH)]
    mx = xs[0]
    for l in range(1, DEPTH):
        mx = jnp.maximum(mx, xs[l])
    es = [jnp.exp(x - mx) for x in xs]
    tot = es[0]
    for l in range(1, DEPTH):
        tot = tot + es[l]
    sm = [e / tot for e in es]
    run = sm[0]
    o_ref[0] = run - sm[0]
    for l in range(1, DEPTH):
        run = run + sm[l]
        o_ref[l] = run - sm[0]


def lb_call(logits):
    x = logits.astype(f32).reshape(DEPTH, 2 * N_HEADS, D_HEAD)
    return pl.pallas_call(
        _lb_kernel,
        out_shape=jax.ShapeDtypeStruct((DEPTH, 2 * N_HEADS, D_HEAD), f32),
        name="hg_lower_bounds",
    )(x)


def _in_proj_kernel(x_ref, mod_ref, g_ref, w_ref, o16_ref, o32_ref, h_s):
    n = pl.program_id(2)

    @pl.when(n == 0)
    def _():
        x = x_ref[0]
        sh = mod_ref[0, 0:1, :]
        sc = mod_ref[0, 1:2, :]
        h = _rms_rows(x, g_ref[...]) * (1.0 + sc) + sh
        h_s[...] = h.astype(bf16)

    @pl.when(n < N16_TILES)
    def _():
        o16_ref[0] = _dot(h_s[...], w_ref[...]).astype(bf16)

    @pl.when(n == N16_TILES)
    def _():
        o32_ref[0] = _dot(h_s[...], w_ref[...])


def in_proj_call(x, mod, g_row, w, shared_mod):
    B, L, _ = x.shape
    tm = min(L, 1024)
    tn = PROJ_TN
    mod_map = (lambda b, m, n: (0, 0, 0)) if shared_mod else (lambda b, m, n: (b, 0, 0))
    return pl.pallas_call(
        _in_proj_kernel,
        grid=(B, L // tm, N16_TILES + 1),
        in_specs=[pl.BlockSpec((1, tm, D_MODEL), lambda b, m, n: (b, m, 0)),
                  pl.BlockSpec((1, 8, D_MODEL), mod_map),
                  pl.BlockSpec((1, D_MODEL), lambda b, m, n: (0, 0)),
                  pl.BlockSpec((D_MODEL, tn), lambda b, m, n: (0, n))],
        out_specs=[pl.BlockSpec((1, tm, tn), lambda b, m, n: (b, m, jnp.minimum(n, N16_TILES - 1))),
                   pl.BlockSpec((1, tm, tn), lambda b, m, n: (b, m, 0))],
        out_shape=[jax.ShapeDtypeStruct((B, L, N16_COLS), bf16),
                   jax.ShapeDtypeStruct((B, L, N32_COLS), f32)],
        scratch_shapes=[pltpu.VMEM((tm, D_MODEL), bf16)],
        compiler_params=_cparams(("arbitrary", "arbitrary", "arbitrary")),
        name="in_proj",
    )(x, mod, g_row, w)


def _mlstm_kernel(bias_ref, m0_ref, q_ref, k_ref, v_ref, mo_ref, g_ref, ng_ref, c0_ref, n0_ref,
                  y_ref, cout_ref, nout_ref, mout_ref,
                  gt_s, rw_s, cr_s, vt_s, ht_s, ct_s, m_s, *, seq_len):
    T = ML_CHUNK
    assert T == LANES
    nc = seq_len // T
    b_idx = pl.program_id(0)
    h = pl.program_id(1)
    scale = D_HEAD ** -0.5
    ext_rows = jnp.concatenate([jnp.ones((1, T), f32), jnp.zeros((ML_EXT - D_HEAD - 1, T), f32)], axis=0)

    def tr_body(j, carry):
        r0 = pl.multiple_of(j * T, T)
        gt_s[:, pl.ds(r0, T)] = g_ref[0, pl.ds(r0, T), :].T
        vt = v_ref[0, pl.ds(r0, T), :].astype(f32).T
        vt_s[:, pl.ds(r0, T)] = jnp.concatenate([vt, ext_rows], axis=0).astype(bf16)
        return carry
    lax.fori_loop(0, nc, tr_body, 0, unroll=min(16, nc))

    i_f = gt_s[pl.ds(h, 1), :] + bias_ref[h]
    f_f = _log_sigmoid(gt_s[pl.ds(4 + h, 1), :] + bias_ref[4 + h])
    i_b = gt_s[pl.ds(8 + h, 1), :] + bias_ref[8 + h]
    f_b = _log_sigmoid(gt_s[pl.ds(12 + h, 1), :] + bias_ref[12 + h])
    zrow = jnp.zeros_like(i_f)
    rw_s[...] = jnp.concatenate([f_f, i_f, f_b, i_b, zrow, zrow, zrow, zrow], axis=0)

    ui = lax.broadcasted_iota(jnp.int32, (T, T), 0)
    si = lax.broadcasted_iota(jnp.int32, (T, T), 1)
    tri = jnp.concatenate([(ui <= si).astype(bf16), (ui >= si).astype(bf16)], axis=1)
    valid_f = ui <= si
    valid_b = ui >= si

    def cs_body(j, carry):
        r0 = pl.multiple_of(j * T, T)
        rows = rw_s[:, pl.ds(r0, T)]
        hi, mid, lo = _split3(rows)
        cs3 = _dot(jnp.concatenate([hi, mid, lo, jnp.zeros_like(hi)], axis=0), tri)
        cs = cs3[0:8] + cs3[8:16] + cs3[16:24]
        b_f = cs[0:1, 0:T]
        b_b = cs[2:3, T:2 * T]
        cr_s[:, pl.ds(r0, T)] = jnp.concatenate([b_f, rows[1:2], b_b, rows[3:4],
                                                 rows[1:2] - b_f, rows[3:4] - b_b, rows[6:8]], axis=0)
        return carry
    lax.fori_loop(0, nc, cs_body, 0, unroll=min(16, nc))

    for d in range(2):
        ct_s[d] = jnp.concatenate([c0_ref[0, 0, d, 0].T, n0_ref[0, 0, d, 0],
                                   jnp.zeros((ML_EXT - D_HEAD - 1, D_HEAD), f32)], axis=0)
        m_s[d] = jnp.full((1, LANES), m0_ref[b_idx * 2 * N_HEADS + d * N_HEADS + h], f32)

    def step(d, j):
        r0 = pl.multiple_of(j * T, T)
        b_row = cr_s[pl.ds(2 * d, 1), pl.ds(r0, T)]
        b_end = b_row[:, T - 1:T] if d == 0 else b_row[:, 0:1]
        b_end_l = jnp.broadcast_to(b_end, (1, LANES))
        ucb = jnp.broadcast_to(cr_s[pl.ds(4 + d, 1), pl.ds(r0, T)], (T, T)).T

        q = q_ref[0, pl.ds(r0, T), :].astype(f32) * scale
        kb = k_ref[0, pl.ds(r0, T), :]
        k = kb.astype(f32)
        qb = q.astype(bf16)
        vt = vt_s[:, pl.ds(r0, T)]
        ct = ct_s[d]
        m_prev = m_s[d]

        d_t = jnp.where(valid_f if d == 0 else valid_b, ucb + b_row, NEG_BIG)
        m_state = b_row + m_prev
        m_t = jnp.maximum(m_state, jnp.max(d_t, axis=0, keepdims=True))
        w_state = jnp.exp(m_state - m_t)
        s_t = _dot_nt(kb, qb) * jnp.exp(d_t - m_t)
        tot = _dot(vt, s_t.astype(bf16)) + w_state * _dot_nt(ct.astype(bf16), qb)
        den = tot[D_HEAD:D_HEAD + 1]
        ht_s[d, :, pl.ds(r0, T)] = tot[:D_HEAD] * (1.0 / jnp.maximum(jnp.abs(den), jnp.exp(-m_t)))

        gcb = ucb + b_end_l
        m_new = jnp.maximum(b_end_l + m_prev, jnp.max(gcb, axis=0, keepdims=True))
        kw = k * jnp.exp(gcb - m_new)
        decay = jnp.exp(b_end_l + m_prev - m_new)
        ct_s[d] = decay * ct + _dot(vt, kw.astype(bf16))
        m_s[d] = m_new

    def loop_body(j, carry):
        step(0, j)
        step(1, nc - 1 - j)
        return carry
    lax.fori_loop(0, nc, loop_body, 0, unroll=min(16, nc))

    ng = ng_ref[...]

    def out_body(j, carry):
        r0 = pl.multiple_of(j * T, T)
        y_t = ht_s[0, :, pl.ds(r0, T)] + ht_s[1, :, pl.ds(r0, T)]
        ms = jnp.mean(y_t * y_t, axis=0, keepdims=True)
        y = (y_t * lax.rsqrt(ms + EPS)).T * ng * _sigmoid(mo_ref[0, pl.ds(r0, T), :].astype(f32))
        y_ref[0, pl.ds(r0, T), :] = y.astype(y_ref.dtype)
        return carry
    lax.fori_loop(0, nc, out_body, 0, unroll=min(8, nc))

    for d in range(2):
        cout_ref[0, d, 0] = ct_s[d, 0:D_HEAD, :].T
        nout_ref[0, d, 0] = ct_s[d, D_HEAD:D_HEAD + 1, :]
        mout_ref[0, d, 0] = m_s[d]


def mlstm_call(proj, proj32, gate_bias, norm_g, c0, n0, m0, layer):
    B, L, _ = proj.shape

    def col(cb):
        return pl.BlockSpec((1, L, LANES), lambda b, h, cb=cb: (b, 0, cb + h))

    smem = pl.BlockSpec(memory_space=pltpu.SMEM)
    kern = functools.partial(_mlstm_kernel, seq_len=L)
    return pl.pallas_call(
        kern,
        grid=(B, N_HEADS),
        in_specs=[smem, smem,
                  col(CB_MQ), col(CB_MK), col(CB_MV), col(CB_MO),
                  pl.BlockSpec((1, L, LANES), lambda b, h: (b, 0, CF_MG)),
                  pl.BlockSpec((1, LANES), lambda b, h: (0, h)),
                  pl.BlockSpec((1, 1, 2, 1, D_HEAD, D_HEAD), lambda b, h: (b, layer, 0, h, 0, 0)),
                  pl.BlockSpec((1, 1, 2, 1, 1, D_HEAD), lambda b, h: (b, layer, 0, h, 0, 0))],
        out_specs=[pl.BlockSpec((1, L, LANES), lambda b, h: (b, 0, h)),
                   pl.BlockSpec((1, 2, 1, D_HEAD, D_HEAD), lambda b, h: (b, 0, h, 0, 0)),
                   pl.BlockSpec((1, 2, 1, 1, D_HEAD), lambda b, h: (b, 0, h, 0, 0)),
                   pl.BlockSpec((1, 2, 1, 1, LANES), lambda b, h: (b, 0, h, 0, 0))],
        out_shape=[jax.ShapeDtypeStruct((B, L, BRANCH_W), bf16),
                   jax.ShapeDtypeStruct((B, 2, N_HEADS, D_HEAD, D_HEAD), f32),
                   jax.ShapeDtypeStruct((B, 2, N_HEADS, 1, D_HEAD), f32),
                   jax.ShapeDtypeStruct((B, 2, N_HEADS, 1, LANES), f32)],
        scratch_shapes=[pltpu.VMEM((LANES, L), f32),
                        pltpu.VMEM((8, L), f32),
                        pltpu.VMEM((8, L), f32),
                        pltpu.VMEM((ML_EXT, L), bf16),
                        pltpu.VMEM((2, D_HEAD, L), f32),
                        pltpu.VMEM((2, ML_EXT, D_HEAD), f32),
                        pltpu.VMEM((2, 1, LANES), f32)],
        compiler_params=_cparams(("arbitrary", "arbitrary")),
        name="mlstm",
    )(gate_bias, m0, proj, proj, proj, proj, proj32, norm_g, c0, n0)


def _hgrn_kernel(q_ref, f0_ref, f1_ref, i_ref, og_ref, lb_ref, ng_ref, s0_ref,
                 y_ref, sout_ref, o_s, st_s, qx_s, dec_s, u_s, vt_s, k_s, b_s, q32_s, v32_s, *, seq_len):
    T = HG_CHUNK
    nc = seq_len // T
    h = pl.program_id(1)
    f_refs = (f0_ref, f1_ref)

    lbs, loglb, log1mlb = [], [], []
    for d in range(2):
        lb = lb_ref[pl.ds(d * N_HEADS + h, 1), :]
        lbs.append(lb)
        loglb.append(jnp.log(lb))
        log1mlb.append(jnp.log1p(-lb))
        st_s[d] = s0_ref[0, 0, d, 0].T

    ti = lax.broadcasted_iota(jnp.int32, (T, T), 0)
    ui = lax.broadcasted_iota(jnp.int32, (T, T), 1)
    tri = ((ui <= ti).astype(bf16), (ui >= ti).astype(bf16))
    sub_t = lax.broadcasted_iota(jnp.int32, (HG_SUB, D_HEAD), 0)
    sub_q = sub_t % HG_HALF
    low_half = sub_t < HG_HALF
    row_i = lax.broadcasted_iota(jnp.int32, (T, D_HEAD), 0)
    tt_r = lax.broadcasted_iota(jnp.int32, (T, T), 0)
    tt_c = lax.broadcasted_iota(jnp.int32, (T, T), 1)
    q_rows = {(hh, dd): ((row_i % (2 * hh)) >= hh) if dd == 0 else ((row_i % (2 * hh)) < hh)
              for hh in HG_LEVELS for dd in range(2)}
    same_blk = {hh: (tt_r // (2 * hh)) == (tt_c // (2 * hh)) for hh in HG_LEVELS}

    def gates(d, j, q):
        r0 = pl.multiple_of(j * T, T)
        hf = f_refs[d][0, pl.ds(r0, T), :]
        e = jnp.exp(-jnp.abs(hf))
        sig_neg = jnp.where(hf >= 0.0, e, 1.0) / (1.0 + e)
        logsig = jnp.minimum(hf, 0.0) - jnp.log(1.0 + e)
        cterm = log1mlb[d] + logsig
        amax = jnp.maximum(loglb[d], cterm)
        logf = amax + jnp.log(1.0 + jnp.exp(-jnp.abs(loglb[d] - cterm)))
        k = (1.0 - lbs[d]) * sig_neg
        hi, mid, lo = _split3(logf * LOG2E)
        b = _dot(tri[d], hi) + _dot(tri[d], mid) + _dot(tri[d], lo)
        b_end = b[T - 1:T] if d == 0 else b[0:1]
        k_s[d, pl.ds(r0, T), :] = k
        b_s[d, pl.ds(r0, T), :] = b
        qx_s[d, pl.ds(r0, T), :] = (q * jnp.exp2(b)).astype(bf16)
        dec_s[d, pl.ds(j, 1), :] = jnp.exp2(b_end)

    def gates_body(j, carry):
        r0 = pl.multiple_of(j * T, T)
        q = q_ref[0, pl.ds(r0, T), :].astype(f32)
        v = i_ref[0, pl.ds(r0, T), :].astype(f32)
        q32_s[pl.ds(r0, T), :] = q
        v32_s[pl.ds(r0, T), :] = v
        vt_s[j] = v.T.astype(bf16)
        gates(0, j, q)
        gates(1, j, q)
        return carry
    lax.fori_loop(0, nc, gates_body, 0, unroll=min(16, nc))

    def intra(d, j):
        r0 = pl.multiple_of(j * T, T)

        def rows(ref2, lo_r, n):
            return ref2[pl.ds(r0 + lo_r, n), :]

        def srows(ref3, lo_r, n):
            return ref3[d, pl.ds(r0 + lo_r, n), :]

        nsub = T // HG_SUB
        o_blk = [None] * nsub

        def add(idx, val):
            o_blk[idx] = val if o_blk[idx] is None else o_blk[idx] + val

        def levels():
            b_all = srows(b_s, 0, T)
            q_all = rows(q32_s, 0, T)
            k_all = srows(k_s, 0, T)
            a_tot = None
            for h in HG_LEVELS:
                blk = 2 * h
                ref_off = h - 1 if d == 0 else h
                ref = jnp.concatenate([jnp.broadcast_to(srows(b_s, base + ref_off, 1), (blk, D_HEAD))
                                       for base in range(0, T, blk)], axis=0)
                is_q = q_rows[(h, d)]
                qx = q_all * jnp.exp2(jnp.where(is_q, b_all - ref, NEG_BIG))
                kx = k_all * jnp.exp2(jnp.where(is_q, NEG_BIG, ref - b_all))
                a = _dot_nt(qx.astype(bf16), kx.astype(bf16))
                if blk < T:
                    a = jnp.where(same_blk[h], a, 0.0)
                a_tot = a if a_tot is None else a_tot + a
            ov = _dot(a_tot.astype(bf16), rows(v32_s, 0, T).astype(bf16))
            for p in range(nsub):
                add(p, ov[p * HG_SUB:(p + 1) * HG_SUB])

        def diag(lo_r):
            bs = srows(b_s, lo_r, HG_SUB)
            qs = rows(q32_s, lo_r, HG_SUB)

            def keyrow(read, s):
                return jnp.where(low_half, read(lo_r + s, 1), read(lo_r + HG_HALF + s, 1))

            acc = None
            for s in range(HG_HALF):
                valid = (sub_q >= s) if d == 0 else (sub_q <= s)
                b_key = keyrow(lambda r, n: srows(b_s, r, n), s)
                ee = jnp.exp2(jnp.where(valid, bs - b_key, NEG_BIG))
                a_col = jnp.sum(qs * ee * keyrow(lambda r, n: srows(k_s, r, n), s), axis=-1, keepdims=True)
                term = a_col * keyrow(lambda r, n: rows(v32_s, r, n), s)
                acc = term if acc is None else acc + term
            add(lo_r // HG_SUB, acc)

        levels()
        for i in range(nsub):
            diag(i * HG_SUB)
        o_s[d, pl.ds(r0, T), :] = jnp.concatenate(o_blk, axis=0)

    def intra_body(j, carry):
        intra(0, j)
        intra(1, j)
        return carry
    lax.fori_loop(0, nc, intra_body, 0, unroll=min(16, nc))

    def incr_body(j, carry):
        r0 = pl.multiple_of(j * T, T)
        vt = vt_s[j]
        for d in range(2):
            b = b_s[d, pl.ds(r0, T), :]
            b_end = b[T - 1:T] if d == 0 else b[0:1]
            u_s[d, j] = _dot(vt, (k_s[d, pl.ds(r0, T), :] * jnp.exp2(b_end - b)).astype(bf16))
        return carry
    lax.fori_loop(0, nc, incr_body, 0, unroll=min(16, nc))

    def state_step(d, j):
        r0 = pl.multiple_of(j * T, T)
        st = st_s[d]
        o_s[d, pl.ds(r0, T), :] = o_s[d, pl.ds(r0, T), :] + _dot_nt(qx_s[d, pl.ds(r0, T), :], st.astype(bf16))
        st_s[d] = st * dec_s[d, pl.ds(j, 1), :] + u_s[d, j]

    def state_body(j, carry):
        state_step(0, j)
        state_step(1, nc - 1 - j)
        return carry
    lax.fori_loop(0, nc, state_body, 0, unroll=min(16, nc // 2))

    ng = ng_ref[...]

    def out_body(j, carry):
        r0 = pl.multiple_of(j * T, T)
        y = o_s[0, pl.ds(r0, T), :] + o_s[1, pl.ds(r0, T), :]
        y = _rms_rows(y, ng) * _silu(og_ref[0, pl.ds(r0, T), :].astype(f32))
        y_ref[0, pl.ds(r0, T), :] = y.astype(y_ref.dtype)
        return carry
    lax.fori_loop(0, nc, out_body, 0, unroll=min(8, nc))

    for d in range(2):
        sout_ref[0, d, 0] = st_s[d].T


def hgrn_call(proj, proj32, lb_rows, norm_g, s0, layer):
    B, L, _ = proj.shape

    def col(cb):
        return pl.BlockSpec((1, L, LANES), lambda b, h, cb=cb: (b, 0, cb + h))

    kern = functools.partial(_hgrn_kernel, seq_len=L)
    return pl.pallas_call(
        kern,
        grid=(B, N_HEADS),
        in_specs=[col(CB_HQ), col(CF_HF), col(CF_HF + N_HEADS), col(CB_HI), col(CB_HG),
                  pl.BlockSpec((2 * N_HEADS, D_HEAD), lambda b, h: (0, 0)),
                  pl.BlockSpec((1, LANES), lambda b, h: (0, h)),
                  pl.BlockSpec((1, 1, 2, 1, D_HEAD, D_HEAD), lambda b, h: (b, layer, 0, h, 0, 0))],
        out_specs=[pl.BlockSpec((1, L, LANES), lambda b, h: (b, 0, h)),
                   pl.BlockSpec((1, 2, 1, D_HEAD, D_HEAD), lambda b, h: (b, 0, h, 0, 0))],
        out_shape=[jax.ShapeDtypeStruct((B, L, BRANCH_W), bf16),
                   jax.ShapeDtypeStruct((B, 2, N_HEADS, D_HEAD, D_HEAD), f32)],
        scratch_shapes=[pltpu.VMEM((2, L, D_HEAD), f32),
                        pltpu.VMEM((2, D_HEAD, D_HEAD), f32),
                        pltpu.VMEM((2, L, D_HEAD), bf16),
                        pltpu.VMEM((2, L // HG_CHUNK, D_HEAD), f32),
                        pltpu.VMEM((2, L // HG_CHUNK, D_HEAD, D_HEAD), f32),
                        pltpu.VMEM((L // HG_CHUNK, D_HEAD, HG_CHUNK), bf16),
                        pltpu.VMEM((2, L, D_HEAD), f32),
                        pltpu.VMEM((2, L, D_HEAD), f32),
                        pltpu.VMEM((L, D_HEAD), f32),
                        pltpu.VMEM((L, D_HEAD), f32)],
        compiler_params=_cparams(("arbitrary", "arbitrary")),
        name="hgrn",
    )(proj, proj32, proj32, proj, proj, lb_rows, norm_g, s0)


def _pair_swap(x):
    lane = lax.broadcasted_iota(jnp.int32, x.shape, 1)
    return jnp.where((lane % 2) == 0, pltpu.roll(x, LANES - 1, 1), pltpu.roll(x, 1, 1))


def _head_norm_pair(x, g_row):
    gi = lax.broadcasted_iota(jnp.int32, (LANES, LANES), 0) // AT_HD
    gj = lax.broadcasted_iota(jnp.int32, (LANES, LANES), 1) // AT_HD
    same_head = (gi == gj).astype(bf16)
    sq = x * x
    hi = sq.astype(bf16)
    lo = (sq - hi.astype(f32)).astype(bf16)
    ms = (_dot(hi, same_head) + _dot(lo, same_head)) * (1.0 / AT_HD)
    return x * lax.rsqrt(ms + EPS) * g_row


def _qkv_prep_kernel(aq_ref, ak_ref, av_ref, qg_ref, kg_ref, cos_ref, sin_ref,
                     q_ref, k_ref, v_ref, kn_ref, *, use_rope):
    qg = qg_ref[...]
    kg = kg_ref[...]
    if use_rope:
        cos = cos_ref[...]
        sin = sin_ref[...]
    for p in range(AT_HEADS // 2):
        x = _head_norm_pair(aq_ref[0, :, p * LANES:(p + 1) * LANES].astype(f32), qg)
        if use_rope:
            x = x * cos + _pair_swap(x) * sin
        x = (x * (AT_HD ** -0.5)).astype(bf16)
        q_ref[0, 2 * p] = x[:, :AT_HD]
        q_ref[0, 2 * p + 1] = x[:, AT_HD:]
    kn = _head_norm_pair(ak_ref[0].astype(f32), kg)
    kn_ref[0] = kn
    if use_rope:
        kn = kn * cos + _pair_swap(kn) * sin
    kb = kn.astype(bf16)
    for g in range(AT_KV_HEADS):
        k_ref[0, g] = kb[:, g * AT_HD:(g + 1) * AT_HD]
    v = av_ref[0].astype(f32)
    lane = lax.broadcasted_iota(jnp.int32, v.shape, 1)
    tail = jnp.where(lane == AT_HD, 1.0, 0.0)
    v_ref[0, 0] = jnp.where(lane < AT_HD, v, tail).astype(bf16)
    v_ref[0, 1] = jnp.where(lane < AT_HD, pltpu.roll(v, AT_HD, 1), tail).astype(bf16)


def qkv_prep_call(proj, qg_row, kg_row, cos_t, sin_t, use_rope):
    B, L, _ = proj.shape
    tm = min(L, 512)
    kern = functools.partial(_qkv_prep_kernel, use_rope=use_rope)
    return pl.pallas_call(
        kern,
        grid=(B, L // tm),
        in_specs=[pl.BlockSpec((1, tm, 4 * LANES), lambda b, m: (b, m, CB_AQ // 4)),
                  pl.BlockSpec((1, tm, LANES), lambda b, m: (b, m, CB_AK)),
                  pl.BlockSpec((1, tm, LANES), lambda b, m: (b, m, CB_AV)),
                  pl.BlockSpec((1, LANES), lambda b, m: (0, 0)),
                  pl.BlockSpec((1, LANES), lambda b, m: (0, 0)),
                  pl.BlockSpec((tm, LANES), lambda b, m: (m, 0)),
                  pl.BlockSpec((tm, LANES), lambda b, m: (m, 0))],
        out_specs=[pl.BlockSpec((1, AT_HEADS, tm, AT_HD), lambda b, m: (b, 0, m, 0)),
                   pl.BlockSpec((1, AT_KV_HEADS, tm, AT_HD), lambda b, m: (b, 0, m, 0)),
                   pl.BlockSpec((1, AT_KV_HEADS, tm, LANES), lambda b, m: (b, 0, m, 0)),
                   pl.BlockSpec((1, tm, LANES), lambda b, m: (b, m, 0))],
        out_shape=[jax.ShapeDtypeStruct((B, AT_HEADS, L, AT_HD), bf16),
                   jax.ShapeDtypeStruct((B, AT_KV_HEADS, L, AT_HD), bf16),
                   jax.ShapeDtypeStruct((B, AT_KV_HEADS, L, LANES), bf16),
                   jax.ShapeDtypeStruct((B, L, LANES), f32)],
        compiler_params=_cparams(("arbitrary", "arbitrary")),
        name="qkv_prep",
    )(proj, proj, proj, qg_row, kg_row, cos_t, sin_t)


def _attn_kernel(*refs, tq, tk, n_new, n_cache):
    if n_cache:
        q_ref, k_ref, v_ref, kc_ref, vc_ref, o_ref, s_s, m_s = refs
    else:
        q_ref, k_ref, v_ref, o_ref, s_s, m_s = refs
    rows = AT_GROUP * tq
    ncb = tk // LANES
    nkv = n_new + n_cache

    def chunk(new_ref, cache_ref, j):
        if j < n_new:
            return new_ref[0, 0, j * tk:(j + 1) * tk, :]
        return cache_ref[0, 0, (j - n_new) * tk:(j - n_new + 1) * tk, :]

    always = pl.program_id(2) >= 0

    @pl.when(always)
    def _():
        q = q_ref[0].reshape(rows, AT_HD)
        mx = None
        for j in range(nkv):
            s = _dot_nt(q, chunk(k_ref, kc_ref if n_cache else None, j))
            s_s[:, j * tk:(j + 1) * tk] = s
            cm = s[:, 0:LANES]
            for cb in range(1, ncb):
                cm = jnp.maximum(cm, s[:, cb * LANES:(cb + 1) * LANES])
            mx = cm if mx is None else jnp.maximum(mx, cm)
        m_s[...] = jnp.broadcast_to(jnp.max(mx, axis=-1, keepdims=True), (rows, LANES))

    @pl.when(pl.program_id(1) >= 0)
    def _():
        m_b = m_s[...]
        acc = None
        for j in range(nkv):
            ps = []
            for cb in range(ncb):
                off = j * tk + cb * LANES
                ps.append(jnp.exp(s_s[:, off:off + LANES] - m_b).astype(bf16))
            pv = _dot(jnp.concatenate(ps, axis=1), chunk(v_ref, vc_ref if n_cache else None, j))
            acc = pv if acc is None else acc + pv
        o = acc[:, :AT_HD] * (1.0 / acc[:, AT_HD:AT_HD + 1])
        for g in range(AT_GROUP):
            o_ref[0, :, g * AT_HD:(g + 1) * AT_HD] = o[g * tq:(g + 1) * tq].astype(o_ref.dtype)


def attn_call(q, k, v, k_cache=None, v_cache=None):
    B, _, Lq, _ = q.shape
    L = k.shape[2]
    P = 0 if k_cache is None else k_cache.shape[2]
    tq = 256
    tk = 512 if (L % 512 == 0 and P % 512 == 0) else L
    assert L % tk == 0 and P % tk == 0
    n_new, n_cache = L // tk, P // tk
    nq = Lq // tq
    kern = functools.partial(_attn_kernel, tq=tq, tk=tk, n_new=n_new, n_cache=n_cache)
    rows = AT_GROUP * tq

    def kv_spec(length, width):
        return pl.BlockSpec((1, 1, length, width), lambda b, g, i: (b, g, 0, 0))

    in_specs = [pl.BlockSpec((1, AT_GROUP, tq, AT_HD), lambda b, g, i: (b, g, i, 0)),
                kv_spec(L, AT_HD), kv_spec(L, LANES)]
    args = [q, k, v]
    if n_cache:
        in_specs += [kv_spec(P, AT_HD), kv_spec(P, LANES)]
        args += [k_cache, v_cache]
    return pl.pallas_call(
        kern,
        grid=(B, AT_KV_HEADS, nq),
        in_specs=in_specs,
        out_specs=pl.BlockSpec((1, tq, AT_GROUP * AT_HD), lambda b, g, i: (b, i, g)),
        out_shape=jax.ShapeDtypeStruct((B, Lq, AT_HEADS * AT_HD), bf16),
        scratch_shapes=[pltpu.VMEM((rows, L + P), f32),
                        pltpu.VMEM((rows, LANES), f32)],
        compiler_params=_cparams(("arbitrary", "arbitrary", "arbitrary")),
        name="attention",
    )(*args)


def _cache_layouts(cache_k_l, cache_v_l):
    kc = jnp.swapaxes(cache_k_l, 1, 2).astype(bf16)
    vc = jnp.swapaxes(cache_v_l, 1, 2).astype(bf16)
    ones = jnp.ones(vc.shape[:-1] + (1,), bf16)
    zeros = jnp.zeros(vc.shape[:-1] + (LANES - AT_HD - 1,), bf16)
    return kc, jnp.concatenate([vc, ones, zeros], axis=-1)


def _merge_kernel(x_ref, yml_ref, yhg_ref, yat_ref, bg_ref, mod_ref, wb_ref, wo_ref, o_ref):
    merged = None
    for n, y_ref in enumerate((yml_ref, yhg_ref, yat_ref)):
        pn = _dot(y_ref[0], wb_ref[n])
        gn = _sigmoid(bg_ref[0, :, n * D_MODEL:(n + 1) * D_MODEL].astype(f32))
        merged = gn * pn if merged is None else merged + gn * pn
    out = _dot(merged.astype(bf16), wo_ref[...])
    o_ref[0] = x_ref[0] + mod_ref[0, 2:3, :] * out


def merge_call(x, y_ml, y_hg, y_at, proj, mod, wb, wo, shared_mod):
    B, L, _ = x.shape
    tm = min(L, 512)
    mod_map = (lambda b, m: (0, 0, 0)) if shared_mod else (lambda b, m: (b, 0, 0))
    yspec = pl.BlockSpec((1, tm, BRANCH_W), lambda b, m: (b, m, 0))
    return pl.pallas_call(
        _merge_kernel,
        grid=(B, L // tm),
        in_specs=[pl.BlockSpec((1, tm, D_MODEL), lambda b, m: (b, m, 0)),
                  yspec, yspec, yspec,
                  pl.BlockSpec((1, tm, 3 * D_MODEL), lambda b, m: (b, m, CB_BG)),
                  pl.BlockSpec((1, 8, D_MODEL), mod_map),
                  pl.BlockSpec((3, BRANCH_W, D_MODEL), lambda b, m: (0, 0, 0)),
                  pl.BlockSpec((D_MODEL, D_MODEL), lambda b, m: (0, 0))],
        out_specs=pl.BlockSpec((1, tm, D_MODEL), lambda b, m: (b, m, 0)),
        out_shape=jax.ShapeDtypeStruct((B, L, D_MODEL), f32),
        compiler_params=_cparams(("arbitrary", "arbitrary")),
        name="merge_out",
    )(x, y_ml, y_hg, y_at, proj, mod, wb, wo)


FFN_SPLITS = ((0, 1536), (1536, FFN_HIDDEN))


def _ffn_kernel(x_ref, mod_ref, g_ref, wi_ref, wd_ref, fg_ref, o_ref, *, final_norm):
    x = x_ref[0]
    hb = (_rms_rows(x, g_ref[...]) * (1.0 + mod_ref[0, 4:5, :]) + mod_ref[0, 3:4, :]).astype(bf16)
    acc = None
    for lo, hi in FFN_SPLITS:
        gate = _dot(hb, wi_ref[:, lo:hi])
        up = _dot(hb, wi_ref[:, FFN_HIDDEN + lo:FFN_HIDDEN + hi])
        part = _dot((_silu(gate) * up).astype(bf16), wd_ref[lo:hi, :])
        acc = part if acc is None else acc + part
    y = x + mod_ref[0, 5:6, :] * acc
    if final_norm:
        y = _rms_rows(y, fg_ref[...])
    o_ref[0] = y


def ffn_call(x, mod, g_row, w_in, w_out, fg_row, shared_mod, final_norm):
    B, L, _ = x.shape
    tm = min(L, 512)
    mod_map = (lambda b, m: (0, 0, 0)) if shared_mod else (lambda b, m: (b, 0, 0))
    kern = functools.partial(_ffn_kernel, final_norm=final_norm)
    resident = pl.Buffered(1)
    return pl.pallas_call(
        kern,
        grid=(B, L // tm),
        in_specs=[pl.BlockSpec((1, tm, D_MODEL), lambda b, m: (b, m, 0)),
                  pl.BlockSpec((1, 8, D_MODEL), mod_map),
                  pl.BlockSpec((1, D_MODEL), lambda b, m: (0, 0)),
                  pl.BlockSpec((D_MODEL, 2 * FFN_HIDDEN), lambda b, m: (0, 0), pipeline_mode=resident),
                  pl.BlockSpec((FFN_HIDDEN, D_MODEL), lambda b, m: (0, 0), pipeline_mode=resident),
                  pl.BlockSpec((1, D_MODEL), lambda b, m: (0, 0))],
        out_specs=pl.BlockSpec((1, tm, D_MODEL), lambda b, m: (b, m, 0)),
        out_shape=jax.ShapeDtypeStruct((B, L, D_MODEL), f32),
        compiler_params=_cparams(("arbitrary", "arbitrary")),
        name="ffn",
    )(x, mod, g_row, w_in, w_out, fg_row)


def _reorder_w_in(w):
    o_mg, o_hq, o_hf, o_hi, o_bg, o_end = 2048, 2064, 2576, 3600, 5392, 8464
    d = w.shape[0]
    sec16 = [w[:, o_bg:o_end], w[:, :o_mg], w[:, o_hq:o_hf], w[:, o_hi:o_bg]]
    n16 = (o_end - o_bg) + o_mg + (o_hf - o_hq) + (o_bg - o_hi)
    sec32 = [w[:, o_hf:o_hi], w[:, o_mg:o_hq]]
    n32 = (o_hi - o_hf) + (o_hq - o_mg)
    return jnp.concatenate(sec16 + [jnp.zeros((d, N16_COLS - n16), w.dtype)] + sec32
                           + [jnp.zeros((d, N32_COLS - n32), w.dtype)], axis=1).astype(bf16)


def _rope_tables(L):
    rows = L // GRID_W
    row = jnp.repeat(jnp.arange(rows, dtype=f32), GRID_W)
    colp = jnp.tile(jnp.arange(GRID_W, dtype=f32), rows)
    n_freq = AT_HD // 4
    inv = ROPE_THETA ** (-jnp.arange(n_freq, dtype=f32) / n_freq)
    ang = jnp.concatenate([row[:, None] * inv, colp[:, None] * inv], axis=-1)
    cos = jnp.repeat(jnp.cos(ang), 2, axis=-1)
    sin = jnp.repeat(jnp.sin(ang), 2, axis=-1)
    sign = jnp.tile(jnp.array([-1.0, 1.0], f32), AT_HD // 2)
    sin = sin * sign
    return jnp.tile(cos, (1, 2)), jnp.tile(sin, (1, 2))


def _mod_rows(ada_rows):
    r = ada_rows.shape[0]
    m = ada_rows.reshape(r, 6, D_MODEL)
    return jnp.concatenate([m, jnp.zeros((r, 2, D_MODEL), f32)], axis=1)


def kernel(x_prompt, x_sample, c, cache_k, cache_v, state_ml_C, state_ml_n, state_ml_m, state_hg_S, c_ctx, w_ada, b_ada, norm1_g, norm2_g, w_in, ml_gate_b, ml_norm_g, hg_lb_logits, hg_norm_g, q_norm_g, k_norm_g, w_branch, w_out, w_ffn_in, w_ffn_out, final_g):
    Bp, Lp, _ = x_prompt.shape
    Bs, Ls, _ = x_sample.shape

    n_rows = 16
    cvec = jnp.concatenate([c_ctx[None, :], c, jnp.zeros((n_rows - 1 - Bs, D_MODEL), f32)], axis=0)
    ada = ada_call(cvec, w_ada, b_ada)
    lb_all = lb_call(hg_lb_logits)
    cos_t, sin_t = _rope_tables(Ls)

    zeros_c = jnp.zeros((Bp, 1, 2, N_HEADS, D_HEAD, D_HEAD), f32)
    zeros_n = jnp.zeros((Bp, 1, 2, N_HEADS, 1, D_HEAD), f32)
    zeros_m = jnp.zeros((Bp * 2 * N_HEADS,), f32)
    n_state = state_ml_n.reshape(Bs, DEPTH, 2, N_HEADS, 1, D_HEAD)

    xp, xs = x_prompt.reshape(1, Bp * Lp, D_MODEL), x_sample
    nk, nv, nC, nn_, nm, nS = [], [], [], [], [], []
    for l in range(DEPTH):
        w_l = _reorder_w_in(w_in[l])
        wb_l = w_branch[l].astype(bf16)
        wo_l = w_out[l].astype(bf16)
        wfi_l = w_ffn_in[l].astype(bf16)
        wfo_l = w_ffn_out[l].astype(bf16)
        g1 = norm1_g[l][None, :]
        g2 = norm2_g[l][None, :]
        mlg = ml_norm_g[l][None, :]
        hgg = hg_norm_g[l][None, :]
        qg = jnp.tile(q_norm_g[l], 2)[None, :]
        kg = jnp.tile(k_norm_g[l], 2)[None, :]
        gate_b = ml_gate_b[l].reshape(-1).astype(f32)
        fg = final_g[None, :]
        last = l == DEPTH - 1

        mod_ctx = _mod_rows(ada[l, 0:1])
        mod_lat = _mod_rows(ada[l, 1:1 + Bs])

        proj_flat, proj32 = in_proj_call(xp, mod_ctx, g1, w_l, True)
        proj = proj_flat.reshape(Bp, Lp, N16_COLS)
        proj32 = proj32.reshape(Bp, Lp, N32_COLS)
        y_ml, c_f, n_f, m_f = mlstm_call(proj, proj32, gate_b, mlg, zeros_c, zeros_n, zeros_m, 0)
        y_hg, s_f = hgrn_call(proj, proj32, lb_all[l], hgg, zeros_c, 0)
        q_h, k_h, v_h, k_n = qkv_prep_call(proj, qg, kg, cos_t[:Lp], sin_t[:Lp], False)
        y_at = attn_call(q_h, k_h, v_h)
        flat = (1, Bp * Lp, BRANCH_W)
        xp = merge_call(xp, y_ml.reshape(flat), y_hg.reshape(flat), y_at.reshape(flat), proj_flat,
                        mod_ctx, wb_l, wo_l, True)
        xp = ffn_call(xp, mod_ctx, g2, wfi_l, wfo_l, fg, True, last)
        nk.append(k_n.reshape(Bp, Lp, AT_KV_HEADS, AT_HD))
        nv.append(proj[:, :, CB_AV * LANES:(CB_AV + 1) * LANES].astype(f32).reshape(Bp, Lp, AT_KV_HEADS, AT_HD))
        nC.append(c_f)
        nn_.append(n_f[:, :, :, 0, :])
        nm.append(m_f[:, :, :, 0, 0])
        nS.append(s_f)

        proj, proj32 = in_proj_call(xs, mod_lat, g1, w_l, False)
        y_ml, _, _, _ = mlstm_call(proj, proj32, gate_b, mlg, state_ml_C, n_state,
                                   state_ml_m[:, l].reshape(-1).astype(f32), l)
        y_hg, _ = hgrn_call(proj, proj32, lb_all[l], hgg, state_hg_S, l)
        q_h, k_h, v_h, _ = qkv_prep_call(proj, qg, kg, cos_t, sin_t, True)
        k_c, v_c = _cache_layouts(cache_k[:, l], cache_v[:, l])
        y_at = attn_call(q_h, k_h, v_h, k_c, v_c)
        xs = merge_call(xs, y_ml, y_hg, y_at, proj, mod_lat, wb_l, wo_l, False)
        xs = ffn_call(xs, mod_lat, g2, wfi_l, wfo_l, fg, False, last)

    return (xp.reshape(Bp, Lp, D_MODEL), xs, jnp.stack(nk, axis=1), jnp.stack(nv, axis=1), jnp.stack(nC, axis=1),
            jnp.stack(nn_, axis=1), jnp.stack(nm, axis=1), jnp.stack(nS, axis=1))
```
